```python
import math
import jax, jax.numpy as jnp
from jax import lax
import numpy as np

D_MODEL = 1024
BATCH = 32
SEQ = 256
DEPTH = 2
DEC_BATCH = 8
DEC_SEQ = 1024
PAST_LEN = 256

GRID_W = 64
N_EVEN = (DEPTH + 1) // 2
N_ODD = DEPTH // 2
H_A = 4
DK_A = D_MODEL // 16
DV_A = D_MODEL // 8
GK_RANK = 16
GATE_TEMP = 16.0
H_B = 4
DK_B = D_MODEL // 16
DV_B = D_MODEL // 8
CHUNK = 64
REC_SIZES = (H_A * DK_A, H_A * DK_A, H_A * DV_A, H_A * DV_A, H_B * DK_B, H_B * DK_B, H_B * DV_B, H_B * DV_B, 2 * GK_RANK, 4 * H_B)
REC_IN = sum(REC_SIZES)
REC_OUT = H_A * DV_A + H_B * DV_B
HY_ORDER = 2
HY_EMB = 33
HY_BANDS = (HY_EMB - 1) // 2
HY_FFN = 64
HY_TARGET = 1e-2
HY_FAST_PCT = 0.3
HY_SLOW_PCT = 1.5
HY_MAX_DECAY = abs(math.log(HY_TARGET)) / HY_FAST_PCT
HY_MIN_DECAY = abs(math.log(HY_TARGET)) / HY_SLOW_PCT
N_GROUPS = 4
EXP_PER_GROUP = 4
N_EXPERTS = N_GROUPS * EXP_PER_GROUP
TOP_K = 2
D_EXPERT = D_MODEL // 2
N_MOD = 6
POS_THETA = 10000.0
EPS = 1e-6

kernel_name = 'hybrid_prefix_diffusion_gla_mlstm_hyena_hmoe'


def _rmsnorm(x, w):
    x32 = x.astype(jnp.float32)
    y = x32 * lax.rsqrt(jnp.mean(x32 * x32, axis=-1, keepdims=True) + EPS)
    return (y * w.astype(jnp.float32)).astype(x.dtype)


def _modulate(x, w, shift, scale):
    return _rmsnorm(x, w) * (1.0 + scale) + shift


def _ada(cvec, w, b):
    m = jax.nn.silu(cvec) @ w + b
    return jnp.split(m[:, None, :], N_MOD, axis=-1)


def _grid_pos_embed(L):
    f32 = jnp.float32
    rows = L // GRID_W
    r, cl = jnp.meshgrid(jnp.arange(rows, dtype=f32), jnp.arange(GRID_W, dtype=f32), indexing='ij')
    quarter = D_MODEL // 4
    omega = POS_THETA ** (-jnp.arange(quarter, dtype=f32) / quarter)
    def enc(pos):
        a = pos.reshape(-1, 1) * omega[None, :]
        return jnp.concatenate([jnp.sin(a), jnp.cos(a)], axis=-1)
    return jnp.concatenate([enc(r), enc(cl)], axis=-1)


def _to_chunks(t):
    b, l = t.shape[0], t.shape[1]
    return jnp.moveaxis(t.reshape((b, l // CHUNK, CHUNK) + t.shape[2:]), 1, 0)


def _from_chunks(t):
    t = jnp.moveaxis(t, 0, 1)
    return t.reshape((t.shape[0], t.shape[1] * t.shape[2]) + t.shape[3:])


def _gla_scan(q, k, v, g, s0):
    mask = jnp.tril(jnp.ones((CHUNK, CHUNK), dtype=bool))
    def step(S, inp):
        qc, kc, vc, gc = inp
        b = jnp.cumsum(gc, axis=1)
        qe = qc * jnp.exp(b)
        ke = kc * jnp.exp(-b)
        A = jnp.where(mask, jnp.einsum('bthk,bshk->bhts', qe, ke), 0.0)
        o = jnp.einsum('bhts,bshv->bthv', A, vc) + jnp.einsum('bthk,bhkv->bthv', qe, S)
        b_end = b[:, -1]
        S = jnp.exp(b_end)[..., None] * S + jnp.einsum('bshk,bshv->bhkv', kc * jnp.exp(b_end[:, None] - b), vc)
        return S, o
    s_fin, o = lax.scan(step, s0.astype(jnp.float32), (_to_chunks(q), _to_chunks(k), _to_chunks(v), _to_chunks(g)))
    return _from_chunks(o), s_fin


def _mlstm_scan(q, k, v, ig, lf, c0, n0, m0):
    mask = jnp.tril(jnp.ones((CHUNK, CHUNK), dtype=bool))
    def step(carry, inp):
        Cm, n, m = carry
        qc, kc, vc, ic, fc = inp
        b = jnp.cumsum(fc, axis=1).transpose(0, 2, 1)
        it = ic.transpose(0, 2, 1)
        dlog = jnp.where(mask, b[..., :, None] - b[..., None, :] + it[..., None, :], -jnp.inf)
        inter = b + m[..., None]
        m_t = jnp.maximum(inter, jnp.max(dlog, axis=-1))
        w = jnp.exp(dlog - m_t[..., None])
        a_in = jnp.exp(inter - m_t).transpose(0, 2, 1)[..., None]
        s = jnp.einsum('bthk,bshk->bhts', qc, kc) * w
        num = jnp.einsum('bhts,bshv->bthv', s, vc) + a_in * jnp.einsum('bthk,bhkv->bthv', qc, Cm)
        den = jnp.sum(s, axis=-1).transpose(0, 2, 1)[..., None] + a_in * jnp.einsum('bthk,bhk->bth', qc, n)[..., None]
        h = num / jnp.maximum(jnp.abs(den), jnp.exp(-m_t).transpose(0, 2, 1)[..., None])
        b_end = b[..., -1]
        dend = b_end[..., None] - b + it
        m_new = jnp.maximum(b_end + m, jnp.max(dend, axis=-1))
        w_end = jnp.exp(dend - m_new[..., None])
        decay = jnp.exp(b_end + m - m_new)
        Cm = decay[..., None, None] * Cm + jnp.einsum('bhs,bshk,bshv->bhkv', w_end, kc, vc)
        n = decay[..., None] * n + jnp.einsum('bhs,bshk->bhk', w_end, kc)
        return (Cm, n, m_new), h
    f32 = jnp.float32
    fin, h = lax.scan(step, (c0.astype(f32), n0.astype(f32), m0.astype(f32)),
                      (_to_chunks(q), _to_chunks(k), _to_chunks(v), _to_chunks(ig), _to_chunks(lf)))
    return _from_chunks(h), fin


def _rec_mixer(u, st_gla, st_c, st_n, st_m, p, j):
    f32 = jnp.float32
    bsz, L, _ = u.shape
    proj = (u @ p['rec_w_in'][j]).astype(f32)
    offs = np.cumsum(REC_SIZES)[:-1].tolist()
    qa, ka, va, ra, qb, kb, vb, ob, gk_lr, mg = jnp.split(proj, offs, axis=-1)
    qa = qa.reshape(bsz, L, H_A, DK_A) * (DK_A ** -0.5)
    ka = ka.reshape(bsz, L, H_A, DK_A)
    va = va.reshape(bsz, L, H_A, DV_A)
    glog = jax.nn.log_sigmoid(jnp.einsum('bldr,drk->bldk', gk_lr.reshape(bsz, L, 2, GK_RANK), p['gla_gk_w'][j].astype(f32))
                              + p['gla_gk_b'][j].astype(f32)) / GATE_TEMP
    glog = glog.reshape(bsz, L, 2, H_A, DK_A)
    qb = qb.reshape(bsz, L, H_B, DK_B) * (DK_B ** -0.5)
    kb = kb.reshape(bsz, L, H_B, DK_B)
    vb = vb.reshape(bsz, L, H_B, DV_B)
    mg = mg.reshape(bsz, L, 2, 2, H_B) + p['mlstm_gate_b'][j].astype(f32)
    ig = mg[:, :, :, 0]
    lf = jax.nn.log_sigmoid(mg[:, :, :, 1])
    rev = lambda t: t[:, ::-1]
    oa_f, sa_f = _gla_scan(qa, ka, va, glog[:, :, 0], st_gla[:, 0])
    oa_b, sa_b = _gla_scan(rev(qa), rev(ka), rev(va), rev(glog[:, :, 1]), st_gla[:, 1])
    hb_f, (c_f, n_f, m_f) = _mlstm_scan(qb, kb, vb, ig[:, :, 0], lf[:, :, 0], st_c[:, 0], st_n[:, 0], st_m[:, 0])
    hb_b, (c_b, n_b, m_b) = _mlstm_scan(rev(qb), rev(kb), rev(vb), rev(ig[:, :, 1]), rev(lf[:, :, 1]), st_c[:, 1], st_n[:, 1], st_m[:, 1])
    out_a = _rmsnorm(oa_f + rev(oa_b), p['gla_norm_w'][j]).reshape(bsz, L, H_A * DV_A) * jax.nn.silu(ra)
    out_b = _rmsnorm(hb_f + rev(hb_b), p['mlstm_norm_w'][j]).reshape(bsz, L, H_B * DV_B) * jax.nn.sigmoid(ob)
    mix = jnp.concatenate([out_a, out_b], axis=-1).astype(u.dtype) @ p['rec_w_out'][j]
    states = (jnp.stack([sa_f, sa_b], axis=1), jnp.stack([c_f, c_b], axis=1),
              jnp.stack([n_f, n_b], axis=1), jnp.stack([m_f, m_b], axis=1))
    return mix, states


def _hyena_filters(L, p, j):
    f32 = jnp.float32
    t = jnp.linspace(0.0, 1.0, L, dtype=f32)[:, None]
    w = 2.0 * math.pi * jnp.arange(L, dtype=f32)[:, None] / L
    f = jnp.linspace(1e-4, HY_BANDS - 1, HY_BANDS, dtype=f32)[None, :]
    emb = jnp.concatenate([t, jnp.cos(f * w), -jnp.sin(f * w)], axis=-1)
    h = jnp.sin(p['hy_f_freq1'][j].astype(f32) * (emb @ p['hy_f_w1'][j].astype(f32) + p['hy_f_b1'][j].astype(f32)))
    h = jnp.sin(p['hy_f_freq2'][j].astype(f32) * (h @ p['hy_f_w2'][j].astype(f32) + p['hy_f_b2'][j].astype(f32)))
    h = (h @ p['hy_f_w3'][j].astype(f32)).reshape(L, HY_ORDER, 2, D_MODEL)
    decay = jnp.exp(-t * jnp.linspace(HY_MIN_DECAY, HY_MAX_DECAY, D_MODEL, dtype=f32)[None, :])
    h = h * decay[:, None, None, :]
    h_fwd, h_bwd = h[:, :, 0], h[:, :, 1]
    k = jnp.concatenate([h_fwd, jnp.zeros((1, HY_ORDER, D_MODEL), f32), h_bwd[1:][::-1]], axis=0)
    return jnp.fft.rfft(k, axis=0)


def _hyena_mixer(u, p, j):
    f32 = jnp.float32
    L = u.shape[1]
    z = u @ p['hy_w_in'][j]
    z = lax.conv_general_dilated(z, p['hy_conv_w'][j][:, None, :], window_strides=(1,), padding='SAME',
                                 dimension_numbers=('NWC', 'WIO', 'NWC'), feature_group_count=3 * D_MODEL)
    v, x1, x2 = jnp.split(z.astype(f32), 3, axis=-1)
    kf = _hyena_filters(L, p, j)
    bias = p['hy_f_bias'][j].astype(f32)
    def long_conv(s, o):
        y = jnp.fft.irfft(jnp.fft.rfft(s, n=2 * L, axis=1) * kf[None, :, o], n=2 * L, axis=1)[:, :L]
        return y + bias[o] * s
    y = x2 * long_conv(x1 * long_conv(v, 0), 1)
    return y.astype(u.dtype) @ p['hy_w_out'][j]


def _hier_moe(u, p, l):
    f32 = jnp.float32
    g_logits = (u @ p['moe_w_group'][l] + p['moe_b_group'][l]).astype(f32)
    g_idx = jnp.argmax(g_logits, axis=-1)
    g_oh = jax.nn.one_hot(g_idx, N_GROUPS, dtype=f32)
    p_group = jnp.sum(jax.nn.softmax(g_logits, axis=-1) * g_oh, axis=-1)
    e_logits = (u @ p['moe_w_router'][l] + p['moe_b_router'][l]).astype(f32)
    e_logits = e_logits.reshape(u.shape[:2] + (N_GROUPS, EXP_PER_GROUP))
    e_in = jnp.einsum('blge,blg->ble', e_logits, g_oh)
    top_v, top_i = lax.top_k(e_in, TOP_K)
    w = jax.nn.softmax(top_v, axis=-1) * p_group[..., None]
    ids = g_idx[..., None] * EXP_PER_GROUP + top_i
    gates = jnp.einsum('blke,blk->ble', jax.nn.one_hot(ids, N_EXPERTS, dtype=f32), w)
    hg = jnp.einsum('bld,edf->blef', u, p['moe_w_gate'][l])
    hu = jnp.einsum('bld,edf->blef', u, p['moe_w_up'][l])
    h = jax.nn.silu(hg) * hu * gates.astype(u.dtype)[..., None]
    return jnp.einsum('blef,efd->bld', h, p['moe_w_down'][l])


def _trunk(x, cvec, init_states, p):
    finals = []
    for l in range(DEPTH):
        j = l // 2
        sh1, sc1, g1, sh2, sc2, g2 = _ada(cvec, p['ada_w'][l], p['ada_b'][l])
        h = _modulate(x, p['norm1_w'][l], sh1, sc1)
        if l % 2 == 0:
            mix, st = _rec_mixer(h, *init_states[j], p, j)
            finals.append(st)
        else:
            mix = _hyena_mixer(h, p, j)
        x = x + g1 * mix
        h = _modulate(x, p['norm2_w'][l], sh2, sc2)
        x = x + g2 * _hier_moe(h, p, l)
    return _rmsnorm(x, p['norm_f_w']), finals


def setup_inputs(seed: int = 0) -> dict:
    key = jax.random.key(seed)
    ks = iter(jax.random.split(key, 48))
    f32 = jnp.float32
    def nrm(shape, scale):
        return jax.random.normal(next(ks), shape, f32) * scale
    gate_b = nrm((N_EVEN, 2, 2, H_B), 0.1) + jnp.array([0.0, 3.0], f32)[None, None, :, None]
    return {
        'x_prompt': nrm((BATCH, SEQ, D_MODEL), 1.0),
        'x_sample': nrm((DEC_BATCH, DEC_SEQ, D_MODEL), 1.0),
        'state_gla': nrm((DEC_BATCH, N_EVEN, 2, H_A, DK_A, DV_A), 0.5),
        'state_mlstm_c': nrm((DEC_BATCH, N_EVEN, 2, H_B, DK_B, DV_B), 0.5),
        'state_mlstm_n': nrm((DEC_BATCH, N_EVEN, 2, H_B, DK_B), 0.5),
        'state_mlstm_m': nrm((DEC_BATCH, N_EVEN, 2, H_B), 1.0),
        'c': nrm((DEC_BATCH, D_MODEL), 1.0),
        'c_ctx': nrm((D_MODEL,), 1.0),
        'norm1_w': 1.0 + nrm((DEPTH, D_MODEL), 0.02),
        'norm2_w': 1.0 + nrm((DEPTH, D_MODEL), 0.02),
        'norm_f_w': 1.0 + nrm((D_MODEL,), 0.02),
        'ada_w': nrm((DEPTH, D_MODEL, N_MOD * D_MODEL), D_MODEL ** -0.5),
        'ada_b': nrm((DEPTH, N_MOD * D_MODEL), 0.02),
        'rec_w_in': nrm((N_EVEN, D_MODEL, REC_IN), D_MODEL ** -0.5),
        'gla_gk_w': nrm((N_EVEN, 2, GK_RANK, H_A * DK_A), GK_RANK ** -0.5),
        'gla_gk_b': nrm((N_EVEN, 2, H_A * DK_A), 0.1),
        'mlstm_gate_b': gate_b,
        'gla_norm_w': 1.0 + nrm((N_EVEN, DV_A), 0.02),
        'mlstm_norm_w': 1.0 + nrm((N_EVEN, DV_B), 0.02),
        'rec_w_out': nrm((N_EVEN, REC_OUT, D_MODEL), REC_OUT ** -0.5),
        'hy_w_in': nrm((N_ODD, D_MODEL, 3 * D_MODEL), D_MODEL ** -0.5),
        'hy_conv_w': nrm((N_ODD, 3, 3 * D_MODEL), 3 ** -0.5),
        'hy_f_w1': nrm((N_ODD, HY_EMB, HY_FFN), HY_EMB ** -0.5),
        'hy_f_b1': nrm((N_ODD, HY_FFN), 0.1),
        'hy_f_freq1': 1.0 + nrm((N_ODD, HY_FFN), 0.1),
        'hy_f_w2': nrm((N_ODD, HY_FFN, HY_FFN), HY_FFN ** -0.5),
        'hy_f_b2': nrm((N_ODD, HY_FFN), 0.1),
        'hy_f_freq2': 1.0 + nrm((N_ODD, HY_FFN), 0.1),
        'hy_f_w3': nrm((N_ODD, HY_FFN, HY_ORDER * 2 * D_MODEL), 0.01),
        'hy_f_bias': nrm((N_ODD, HY_ORDER, D_MODEL), 0.5),
        'hy_w_out': nrm((N_ODD, D_MODEL, D_MODEL), D_MODEL ** -0.5),
        'moe_w_group': nrm((DEPTH, D_MODEL, N_GROUPS), D_MODEL ** -0.5),
        'moe_b_group': nrm((DEPTH, N_GROUPS), 0.01),
        'moe_w_router': nrm((DEPTH, D_MODEL, N_EXPERTS), D_MODEL ** -0.5),
        'moe_b_router': nrm((DEPTH, N_EXPERTS), 0.01),
        'moe_w_gate': nrm((DEPTH, N_EXPERTS, D_MODEL, D_EXPERT), D_MODEL ** -0.5),
        'moe_w_up': nrm((DEPTH, N_EXPERTS, D_MODEL, D_EXPERT), D_MODEL ** -0.5),
        'moe_w_down': nrm((DEPTH, N_EXPERTS, D_EXPERT, D_MODEL), D_EXPERT ** -0.5),
    }


def reference(x_prompt, x_sample, state_gla, state_mlstm_c, state_mlstm_n, state_mlstm_m, c, c_ctx,
              norm1_w, norm2_w, norm_f_w, ada_w, ada_b, rec_w_in, gla_gk_w, gla_gk_b, mlstm_gate_b,
              gla_norm_w, mlstm_norm_w, rec_w_out, hy_w_in, hy_conv_w, hy_f_w1, hy_f_b1, hy_f_freq1,
              hy_f_w2, hy_f_b2, hy_f_freq2, hy_f_w3, hy_f_bias, hy_w_out, moe_w_group, moe_b_group,
              moe_w_router, moe_b_router, moe_w_gate, moe_w_up, moe_w_down):
    p = dict(norm1_w=norm1_w, norm2_w=norm2_w, norm_f_w=norm_f_w, ada_w=ada_w, ada_b=ada_b,
             rec_w_in=rec_w_in, gla_gk_w=gla_gk_w, gla_gk_b=gla_gk_b, mlstm_gate_b=mlstm_gate_b,
             gla_norm_w=gla_norm_w, mlstm_norm_w=mlstm_norm_w, rec_w_out=rec_w_out,
             hy_w_in=hy_w_in, hy_conv_w=hy_conv_w, hy_f_w1=hy_f_w1, hy_f_b1=hy_f_b1, hy_f_freq1=hy_f_freq1,
             hy_f_w2=hy_f_w2, hy_f_b2=hy_f_b2, hy_f_freq2=hy_f_freq2, hy_f_w3=hy_f_w3, hy_f_bias=hy_f_bias,
             hy_w_out=hy_w_out, moe_w_group=moe_w_group, moe_b_group=moe_b_group,
             moe_w_router=moe_w_router, moe_b_router=moe_b_router, moe_w_gate=moe_w_gate,
             moe_w_up=moe_w_up, moe_w_down=moe_w_down)
    f32 = jnp.float32
    bp = x_prompt.shape[0]
    zero_states = [(jnp.zeros((bp, 2, H_A, DK_A, DV_A), f32), jnp.zeros((bp, 2, H_B, DK_B, DV_B), f32),
                    jnp.zeros((bp, 2, H_B, DK_B), f32), jnp.zeros((bp, 2, H_B), f32)) for _ in range(N_EVEN)]
    y_prompt, ctx_states = _trunk(x_prompt, c_ctx[None, :], zero_states, p)
    cached = [(state_gla[:, j], state_mlstm_c[:, j], state_mlstm_n[:, j], state_mlstm_m[:, j]) for j in range(N_EVEN)]
    x_lat = x_sample + _grid_pos_embed(x_sample.shape[1]).astype(x_sample.dtype)
    y_sample, _ = _trunk(x_lat, c, cached, p)
    new_state_gla = jnp.stack([s[0] for s in ctx_states], axis=1)
    new_state_mlstm_c = jnp.stack([s[1] for s in ctx_states], axis=1)
    new_state_mlstm_n = jnp.stack([s[2] for s in ctx_states], axis=1)
    new_state_mlstm_m = jnp.stack([s[3] for s in ctx_states], axis=1)
    return (y_prompt, y_sample, new_state_gla, new_state_mlstm_c, new_state_mlstm_n, new_state_mlstm_m)
```

```python
import functools
import math

import jax
import jax.numpy as jnp
from jax import lax
from jax.experimental import pallas as pl
from jax.experimental.pallas import tpu as pltpu

F32 = jnp.float32
BF16 = jnp.bfloat16

D_MODEL = 1024
GRID_W = 64
H_A = 4
DK_A = D_MODEL // 16
DV_A = D_MODEL // 8
GK_RANK = 16
GATE_TEMP = 16.0
H_B = 4
DK_B = D_MODEL // 16
DV_B = D_MODEL // 8
CHUNK = 64
HY_ORDER = 2
HY_EMB = 33
HY_BANDS = (HY_EMB - 1) // 2
HY_FFN = 64
HY_TARGET = 1e-2
HY_MAX_DECAY = abs(math.log(HY_TARGET)) / 0.3
HY_MIN_DECAY = abs(math.log(HY_TARGET)) / 1.5
N_GROUPS = 4
EXP_PER_GROUP = 4
N_EXPERTS = N_GROUPS * EXP_PER_GROUP
D_EXPERT = D_MODEL // 2
N_MOD = 6
POS_THETA = 10000.0
EPS = 1e-6

_QA, _KA, _VA, _RA = 0, 256, 512, 1024
_QB, _KB, _VB, _OB = 1536, 1792, 2048, 2560
_REC_MAIN = 3072
_REC_GATE = 48
_MG0 = 2 * GK_RANK

LANES = 128
TM = 256
TM_MOE = 1024
VMEM_LIMIT = 56 * 1024 * 1024


def _cparams(*sem):
    return pltpu.CompilerParams(dimension_semantics=sem, vmem_limit_bytes=VMEM_LIMIT)


def _bf(x):
    return x.astype(BF16)


def _dot(a, b):
    return jnp.dot(a, b, preferred_element_type=F32)


def _dot_nt(a, b):
    return lax.dot_general(a, b, (((1,), (1,)), ((), ())), preferred_element_type=F32)


def _dot_tn(a, b):
    return lax.dot_general(a, b, (((0,), (0,)), ((), ())), preferred_element_type=F32)


def _split2(x):
    hi = _bf(x)
    return hi, _bf(x - hi.astype(F32))


def _split3(x):
    hi = _bf(x)
    r = x - hi.astype(F32)
    mid = _bf(r)
    return hi, mid, _bf(r - mid.astype(F32))


def _dot_hp(a, b, dot=_dot):
    ah, al = _split2(a)
    bh, bl = _split2(b)
    return dot(ah, bh) + (dot(ah, bl) + dot(al, bh))


def _dot_mask_l(m, x):
    x1, x2, x3 = _split3(x)
    return _dot(m, x1) + (_dot(m, x2) + _dot(m, x3))


def _dot_mask_r(x, m):
    x1, x2, x3 = _split3(x)
    return _dot(x1, m) + (_dot(x2, m) + _dot(x3, m))


def _rms(x, w):
    return x * lax.rsqrt(jnp.mean(x * x, axis=-1, keepdims=True) + EPS) * w


def _mod_slices(mod):
    return [mod[:, k * D_MODEL:(k + 1) * D_MODEL] for k in range(N_MOD)]


def _ada_kernel(cv_ref, w_ref, b_ref, o_ref):
    a = cv_ref[...]
    a = a * jax.nn.sigmoid(a)
    o_ref[...] = _dot_hp(a, w_ref[...]) + b_ref[...]


def _ada(cv, ada_w, ada_b):
    depth, d, n = ada_w.shape
    rows = cv.shape[0]
    tn = 768
    return pl.pallas_call(
        _ada_kernel,
        out_shape=jax.ShapeDtypeStruct((depth, rows, n), F32),
        grid=(depth, n // tn),
        in_specs=[
            pl.BlockSpec((rows, d), lambda l, j: (0, 0)),
            pl.BlockSpec((None, d, tn), lambda l, j: (l, 0, j)),
            pl.BlockSpec((None, 1, tn), lambda l, j: (l, 0, j)),
        ],
        out_specs=pl.BlockSpec((None, rows, tn), lambda l, j: (l, 0, j)),
        compiler_params=_cparams("arbitrary", "arbitrary"),
        name="ada_mod",
    )(cv, ada_w, ada_b.reshape(depth, 1, n))


class _Streams:
    def __init__(self, bp, lp, bs, ls):
        self.bp, self.lp, self.bs, self.ls = bp, lp, bs, ls
        self.tp, self.ts = bp * lp, bs * ls
        self.t = self.tp + self.ts
        assert lp % TM == 0 and ls % TM == 0 and self.tp % ls == 0
        assert self.tp % TM_MOE == 0 and ls % TM_MOE == 0

    def mod_row(self, i, tm):
        ncb = self.tp // tm
        return jnp.where(i < ncb, 0, 1 + (i - ncb) // (self.ls // tm))


def _inproj_kernel(xp_ref, xs_ref, pos_ref, mod_ref, nw_ref, w_ref, wg_ref, wgt_ref,
                   main_ref, g_ref, mt_ref, *, ncb):
    i = pl.program_id(0)
    x = jnp.where(i < ncb, xp_ref[...], xs_ref[...] + pos_ref[...])
    sh1, sc1 = _mod_slices(mod_ref[...])[:2]
    h = _rms(x, nw_ref[...]) * (1.0 + sc1) + sh1
    main_ref[...] = _bf(_dot(_bf(h), w_ref[...]))
    g_ref[...] = _dot_hp(h, wg_ref[...])
    mt = _dot_hp(wgt_ref[...], h, dot=_dot_nt)
    for c in range(TM // CHUNK):
        mt_ref[c] = mt[:, c * CHUNK:(c + 1) * CHUNK]


def _inproj(st, xp, xs, pos, mods, nw, w_main, w_gate, w_gate_t):
    ncb = st.tp // TM
    bps = st.ls // TM
    t = st.t
    return pl.pallas_call(
        functools.partial(_inproj_kernel, ncb=ncb),
        out_shape=(jax.ShapeDtypeStruct((t, _REC_MAIN), BF16),
                   jax.ShapeDtypeStruct((t, LANES), F32),
                   jax.ShapeDtypeStruct((t // CHUNK, 16, CHUNK), F32)),
        grid=(t // TM,),
        in_specs=[
            pl.BlockSpec((TM, D_MODEL), lambda i: (jnp.minimum(i, ncb - 1), 0)),
            pl.BlockSpec((TM, D_MODEL), lambda i: (jnp.maximum(i - ncb, 0), 0)),
            pl.BlockSpec((TM, D_MODEL), lambda i: (jnp.maximum(i - ncb, 0) % bps, 0)),
            pl.BlockSpec((None, 1, N_MOD * D_MODEL), lambda i: (st.mod_row(i, TM), 0, 0)),
            pl.BlockSpec((1, D_MODEL), lambda i: (0, 0)),
            pl.BlockSpec((D_MODEL, _REC_MAIN), lambda i: (0, 0)),
            pl.BlockSpec((D_MODEL, LANES), lambda i: (0, 0)),
            pl.BlockSpec((16, D_MODEL), lambda i: (0, 0)),
        ],
        out_specs=(pl.BlockSpec((TM, _REC_MAIN), lambda i: (i, 0)),
                   pl.BlockSpec((TM, LANES), lambda i: (i, 0)),
                   pl.BlockSpec((TM // CHUNK, 16, CHUNK), lambda i: (i, 0, 0))),
        compiler_params=_cparams("arbitrary"),
        name="rec_inproj",
    )(xp, xs, pos, mods, nw, w_main, w_gate, w_gate_t)


def _scan_kernel(*refs, seq, add_pos):
    if add_pos:
        (main_ref, g_ref, mt_ref, x_ref, pos_ref, mod_ref, sg_ref, sc_ref, sn_ref, sm_ref,
         gkw_ref, gkb_ref, gbrow_ref, gbcol_ref, gnw_ref, mnw_ref, wout_ref, n2w_ref, _x1_in, _h2_in,
         x1_ref, h2_ref, og_ref, oc_ref, on_ref, om_ref, oa_scr, ob_scr) = refs
    else:
        (main_ref, g_ref, mt_ref, x_ref, mod_ref, sg_ref, sc_ref, sn_ref, sm_ref,
         gkw_ref, gkb_ref, gbrow_ref, gbcol_ref, gnw_ref, mnw_ref, wout_ref, n2w_ref,
         x1_ref, h2_ref, og_ref, oc_ref, on_ref, om_ref, oa_scr, ob_scr) = refs
        pos_ref = None
    c = CHUNK
    nchunks = seq // c

    og_ref[...] = sg_ref[...]
    oc_ref[...] = sc_ref[...]
    on_ref[...] = sn_ref[...]
    om_ref[...] = sm_ref[...]
    oa_scr[...] = jnp.zeros_like(oa_scr)
    ob_scr[...] = jnp.zeros_like(ob_scr)

    row = lax.broadcasted_iota(jnp.int32, (c, c), 0)
    col = lax.broadcasted_iota(jnp.int32, (c, c), 1)
    live = (col <= row, col >= row)
    tri = tuple(jnp.where(m, 1.0, 0.0).astype(BF16) for m in live)
    tri_t = (tri[1], tri[0])
    lane = lax.broadcasted_iota(jnp.int32, (1, LANES), 1)
    f_lane = (lane >= _MG0) & (lane < _MG0 + 16) & (((lane - _MG0) % 8) >= 4)
    grow = lax.broadcasted_iota(jnp.int32, (16, 1), 0)
    f_row = (grow % 8) >= 4
    scale_q = DK_A ** -0.5

    def chunk_step(i, carry):
        for d in (0, 1):
            ci = i if d == 0 else nchunks - 1 - i
            r0 = pl.multiple_of(ci * c, c)
            rows = pl.ds(r0, c)
            end = c - 1 if d == 0 else 0
            g_in = g_ref[rows, :]
            glin = _dot_hp(g_in, gkw_ref[d]) + gkb_ref[d]
            glog = jax.nn.log_sigmoid(glin) / GATE_TEMP
            bc = _dot_mask_l(tri[d], glog)
            b_end = bc[end:end + 1, :]
            qa = main_ref[rows, _QA:_QA + 256].astype(F32) * scale_q
            ka = main_ref[rows, _KA:_KA + 256].astype(F32)
            va = main_ref[rows, _VA:_VA + 512]
            qe = _bf(qa * jnp.exp(bc))
            ke = _bf(ka * jnp.exp(-bc))
            kd = _bf(ka * jnp.exp(b_end - bc))
            eb_end = jnp.exp(b_end)
            for h in range(H_A):
                ks = slice(h * DK_A, (h + 1) * DK_A)
                vs = slice(h * DV_A, (h + 1) * DV_A)
                a = jnp.where(live[d], _dot_nt(qe[:, ks], ke[:, ks]), 0.0)
                s_t = og_ref[d, h]
                o = _dot(_bf(a), va[:, vs]) + _dot_nt(qe[:, ks], _bf(s_t))
                oa_scr[rows, vs] += o
                og_ref[d, h] = s_t * eb_end[:, ks] + _dot_tn(va[:, vs], kd[:, ks])
            gm = g_in + gbrow_ref[...]
            gm = jnp.where(f_lane, jax.nn.log_sigmoid(gm), gm)
            cum = _dot_mask_l(tri[d], gm)
            mt = mt_ref[ci] + gbcol_ref[...]
            mt = jnp.where(f_row, jax.nn.log_sigmoid(mt), mt)
            cum_t = _dot_mask_r(mt, tri_t[d])
            qb = main_ref[rows, _QB:_QB + 256] * jnp.asarray(DK_B ** -0.5, BF16)
            kb = main_ref[rows, _KB:_KB + 256]
            vb = main_ref[rows, _VB:_VB + 512]
            for h in range(H_B):
                ks = slice(h * DK_B, (h + 1) * DK_B)
                vs = slice(h * DV_B, (h + 1) * DV_B)
                r = d * H_B + h
                li = _MG0 + d * 8 + h
                b_col = cum[:, li + 4:li + 5]
                i_col = gm[:, li:li + 1]
                b_row = cum_t[d * 8 + h + 4:d * 8 + h + 5, :]
                i_row = mt[d * 8 + h:d * 8 + h + 1, :]
                m_old = om_ref[:, r:r + 1]
                dlog = jnp.where(live[d], b_col - b_row + i_row, -jnp.inf)
                inter = b_col + m_old
                m_t = jnp.maximum(inter, jnp.max(dlog, axis=-1, keepdims=True))
                w = jnp.exp(dlog - m_t)
                a_in = jnp.exp(inter - m_t)
                qh, kh, vh = qb[:, ks], kb[:, ks], vb[:, vs]
                s = _dot_nt(qh, kh) * w
                c_old = oc_ref[d, h]
                n_old = on_ref[r:r + 1, :]
                num = _dot(_bf(s), vh) + a_in * _dot(qh, _bf(c_old))
                den = (jnp.sum(s, axis=-1, keepdims=True)
                       + a_in * jnp.sum(qh.astype(F32) * n_old, axis=-1, keepdims=True))
                ob_scr[rows, vs] += num / jnp.maximum(jnp.abs(den), jnp.exp(-m_t))
                b_last = b_col[end:end + 1, :]
                dend = b_last - b_col + i_col
                m_new = jnp.maximum(b_last + m_old, jnp.max(dend, axis=0, keepdims=True))
                decay = jnp.exp(b_last + m_old - m_new)
                kw = kh.astype(F32) * jnp.exp(dend - m_new)
                oc_ref[d, h] = decay * c_old + _dot_tn(_bf(kw), vh)
                on_ref[r:r + 1, :] = decay * n_old + jnp.sum(kw, axis=0, keepdims=True)
                om_ref[:, r:r + 1] = m_new
        return carry

    lax.fori_loop(0, nchunks, chunk_step, 0)

    mod = _mod_slices(mod_ref[...])
    g1, sh2, sc2 = mod[2], mod[3], mod[4]

    def out_step(j, carry):
        r0 = pl.multiple_of(j * TM, TM)
        rows = pl.ds(r0, TM)
        parts = []
        for h in range(H_A):
            vs = slice(h * DV_A, (h + 1) * DV_A)
            ra = main_ref[rows, _RA + h * DV_A:_RA + (h + 1) * DV_A].astype(F32)
            parts.append(_rms(oa_scr[rows, vs], gnw_ref[...]) * (ra * jax.nn.sigmoid(ra)))
        for h in range(H_B):
            vs = slice(h * DV_B, (h + 1) * DV_B)
            ob = main_ref[rows, _OB + h * DV_B:_OB + (h + 1) * DV_B].astype(F32)
            parts.append(_rms(ob_scr[rows, vs], mnw_ref[...]) * jax.nn.sigmoid(ob))
        mix = _dot(_bf(jnp.concatenate(parts, axis=-1)), wout_ref[...])
        x = x_ref[rows, :]
        if add_pos:
            x = x + pos_ref[rows, :]
        x1 = x + g1 * mix
        x1_ref[rows, :] = x1
        h2_ref[rows, :] = _bf(_rms(x1, n2w_ref[...]) * (1.0 + sc2) + sh2)
        return carry

    lax.fori_loop(0, seq // TM, out_step, 0)


def _scan(st, ctx, main, g, mt, x, pos, mods, states, consts, prev):
    seq = st.lp if ctx else st.ls
    nb = st.bp if ctx else st.bs
    off = 0 if ctx else st.tp // st.ls
    add_pos = not ctx
    sg, sc, sn, sm = states
    full = lambda shape: pl.BlockSpec(shape, lambda b: (0,) * len(shape))
    in_specs = [
        pl.BlockSpec((seq, _REC_MAIN), lambda b: (off + b, 0)),
        pl.BlockSpec((seq, LANES), lambda b: (off + b, 0)),
        pl.BlockSpec((seq // CHUNK, 16, CHUNK), lambda b: (off + b, 0, 0)),
        pl.BlockSpec((seq, D_MODEL), lambda b: (b, 0)),
    ]
    args = [main, g, mt, x]
    if add_pos:
        in_specs.append(full((seq, D_MODEL)))
        args.append(pos)
    mod_row = (lambda b: (0, 0, 0)) if ctx else (lambda b: (1 + b, 0, 0))
    in_specs += [
        pl.BlockSpec((None, 1, N_MOD * D_MODEL), mod_row),
        pl.BlockSpec((None, 2, H_A, DV_A, DK_A), lambda b: (b, 0, 0, 0, 0)),
        pl.BlockSpec((None, 2, H_B, DK_B, DV_B), lambda b: (b, 0, 0, 0, 0)),
        pl.BlockSpec((None, 2 * H_B, DK_B), lambda b: (b, 0, 0)),
        pl.BlockSpec((None, 1, 2 * H_B), lambda b: (b, 0, 0)),
    ]
    args += [mods, sg, sc, sn, sm]
    for a in consts:
        in_specs.append(full(a.shape))
        args.append(a)
    aliases = {}
    if prev is not None:
        aliases = {len(args): 0, len(args) + 1: 1}
        in_specs += [pl.BlockSpec(memory_space=pl.ANY), pl.BlockSpec(memory_space=pl.ANY)]
        args += list(prev)
    out_shape = (jax.ShapeDtypeStruct((st.t, D_MODEL), F32),
                 jax.ShapeDtypeStruct((st.t, D_MODEL), BF16),
                 jax.ShapeDtypeStruct(sg.shape, F32), jax.ShapeDtypeStruct(sc.shape, F32),
                 jax.ShapeDtypeStruct(sn.shape, F32), jax.ShapeDtypeStruct(sm.shape, F32))
    out_specs = (pl.BlockSpec((seq, D_MODEL), lambda b: (off + b, 0)),
                 pl.BlockSpec((seq, D_MODEL), lambda b: (off + b, 0)),
                 pl.BlockSpec((None, 2, H_A, DV_A, DK_A), lambda b: (b, 0, 0, 0, 0)),
                 pl.BlockSpec((None, 2, H_B, DK_B, DV_B), lambda b: (b, 0, 0, 0, 0)),
                 pl.BlockSpec((None, 2 * H_B, DK_B), lambda b: (b, 0, 0)),
                 pl.BlockSpec((None, 1, 2 * H_B), lambda b: (b, 0, 0)))
    return pl.pallas_call(
        functools.partial(_scan_kernel, seq=seq, add_pos=add_pos),
        out_shape=out_shape,
        grid=(nb,),
        in_specs=in_specs,
        out_specs=out_specs,
        scratch_shapes=[pltpu.VMEM((seq, H_A * DV_A), F32), pltpu.VMEM((seq, H_B * DV_B), F32)],
        input_output_aliases=aliases,
        compiler_params=_cparams("arbitrary"),
        name="rec_scan_ctx" if ctx else "rec_scan_lat",
    )(*args)


def _route(logits):
    lane = lax.broadcasted_iota(jnp.int32, logits.shape, 1)
    big = jnp.int32(1 << 20)
    neg = -jnp.inf
    is_g = (lane >= N_EXPERTS) & (lane < N_EXPERTS + N_GROUPS)
    gl = jnp.where(is_g, logits, neg)
    gmax = jnp.max(gl, axis=-1, keepdims=True)
    gidx = jnp.min(jnp.where(gl == gmax, lane - N_EXPERTS, big), axis=-1, keepdims=True)
    p_group = 1.0 / jnp.sum(jnp.exp(gl - gmax), axis=-1, keepdims=True)
    in_grp = (lane >= gidx * EXP_PER_GROUP) & (lane < (gidx + 1) * EXP_PER_GROUP)
    ev = jnp.where(in_grp, logits, neg)
    v1 = jnp.max(ev, axis=-1, keepdims=True)
    i1 = jnp.min(jnp.where(ev == v1, lane, big), axis=-1, keepdims=True)
    ev2 = jnp.where(lane == i1, neg, ev)
    v2 = jnp.max(ev2, axis=-1, keepdims=True)
    i2 = jnp.min(jnp.where(ev2 == v2, lane, big), axis=-1, keepdims=True)
    e2 = jnp.exp(v2 - v1)
    w1 = p_group / (1.0 + e2)
    w2 = p_group * e2 / (1.0 + e2)
    return jnp.where(lane == i1, w1, 0.0) + jnp.where(lane == i2, w2, 0.0)


def _moe_kernel(h_ref, x_ref, mod_ref, wr_ref, br_ref, wg_ref, wu_ref, wd_ref, nf_ref, o_ref,
                acc_scr, gate_scr, *, final_norm):
    e = pl.program_id(1)

    @pl.when(e == 0)
    def _():
        logits = _dot_hp(h_ref[...].astype(F32), wr_ref[...]) + br_ref[...]
        gate_scr[...] = _route(logits)
        acc_scr[...] = jnp.zeros_like(acc_scr)

    h = h_ref[...]
    lane = lax.broadcasted_iota(jnp.int32, gate_scr.shape, 1)
    ge = jnp.sum(jnp.where(lane == e, gate_scr[...], 0.0), axis=-1, keepdims=True)
    hg = _dot(h, wg_ref[...])
    hu = _dot(h, wu_ref[...])
    act = hg * jax.nn.sigmoid(hg) * hu * ge
    acc_scr[...] += _dot(_bf(act), wd_ref[...])

    @pl.when(e == N_EXPERTS - 1)
    def _():
        g2 = _mod_slices(mod_ref[...])[5]
        y = x_ref[...] + g2 * acc_scr[...]
        if final_norm:
            y = _rms(y, nf_ref[...])
        o_ref[...] = y


def _moe(st, h, x, mods, w_route, b_route, wg, wu, wd, nf, final_norm):
    tm = TM_MOE
    return pl.pallas_call(
        functools.partial(_moe_kernel, final_norm=final_norm),
        out_shape=jax.ShapeDtypeStruct((st.t, D_MODEL), F32),
        grid=(st.t // tm, N_EXPERTS),
        in_specs=[
            pl.BlockSpec((tm, D_MODEL), lambda i, e: (i, 0)),
            pl.BlockSpec((tm, D_MODEL), lambda i, e: (i, 0)),
            pl.BlockSpec((None, 1, N_MOD * D_MODEL), lambda i, e: (st.mod_row(i, tm), 0, 0)),
            pl.BlockSpec((D_MODEL, LANES), lambda i, e: (0, 0)),
            pl.BlockSpec((1, LANES), lambda i, e: (0, 0)),
            pl.BlockSpec((None, D_MODEL, D_EXPERT), lambda i, e: (e, 0, 0)),
            pl.BlockSpec((None, D_MODEL, D_EXPERT), lambda i, e: (e, 0, 0)),
            pl.BlockSpec((None, D_EXPERT, D_MODEL), lambda i, e: (e, 0, 0)),
            pl.BlockSpec((1, D_MODEL), lambda i, e: (0, 0)),
        ],
        out_specs=pl.BlockSpec((tm, D_MODEL), lambda i, e: (i, 0)),
        scratch_shapes=[pltpu.VMEM((tm, D_MODEL), F32), pltpu.VMEM((tm, LANES), F32)],
        compiler_params=_cparams("arbitrary", "arbitrary"),
        name="moe_final" if final_norm else "moe",
    )(h, x, mods, w_route, b_route, wg, wu, wd, nf)


def _hy_inproj_kernel(x_ref, mod_ref, nw_ref, w_ref, z_ref):
    sh1, sc1 = _mod_slices(mod_ref[...])[:2]
    h = _rms(x_ref[...], nw_ref[...]) * (1.0 + sc1) + sh1
    z_ref[...] = _bf(_dot(_bf(h), w_ref[...]))


def _hy_inproj(st, x, mods, nw, w):
    n = w.shape[1]
    return pl.pallas_call(
        _hy_inproj_kernel,
        out_shape=jax.ShapeDtypeStruct((st.t, n), BF16),
        grid=(st.t // TM,),
        in_specs=[
            pl.BlockSpec((TM, D_MODEL), lambda i: (i, 0)),
            pl.BlockSpec((None, 1, N_MOD * D_MODEL), lambda i: (st.mod_row(i, TM), 0, 0)),
            pl.BlockSpec((1, D_MODEL), lambda i: (0, 0)),
            pl.BlockSpec((D_MODEL, n), lambda i: (0, 0)),
        ],
        out_specs=pl.BlockSpec((TM, n), lambda i: (i, 0)),
        compiler_params=_cparams("arbitrary"),
        name="hy_inproj",
    )(x, mods, nw, w)


def _dft_tables(seq):
    n2 = 2 * seq
    k = jnp.arange(seq, dtype=jnp.int32)
    kn = (k[:, None] * k[None, :]) % n2
    ang = kn.astype(F32) * (2.0 * math.pi / n2)
    cos, sin = jnp.cos(ang), jnp.sin(ang)
    alt = jnp.where(k % 2 == 0, 1.0, -1.0).astype(F32)
    f_re = cos
    f_im = jnp.where(k[:, None] == 0, alt[None, :], -sin)
    wk = jnp.where(k == 0, 1.0, 2.0).astype(F32) / n2
    b_re = cos * wk[None, :]
    b_im = jnp.where(k[None, :] == 0, alt[:, None] / n2, -sin * wk[None, :])
    return f_re, f_im, b_re, b_im


def _hy_filter_kernel(emb_ref, dec_ref, w1_ref, b1_ref, f1_ref, w2_ref, b2_ref, f2_ref, w3_ref,
                      fre_ref, fim_ref, kr_ref, ki_ref, kr2_ref, *, seq):
    h = jnp.sin(f1_ref[...] * (_dot_hp(emb_ref[...], w1_ref[...]) + b1_ref[...]))
    h = jnp.sin(f2_ref[...] * (_dot_hp(h, w2_ref[...]) + b2_ref[...]))
    dec = dec_ref[...]
    row0 = lax.broadcasted_iota(jnp.int32, (seq, 1), 0) == 0
    f_re, f_im = fre_ref[...], fim_ref[...]
    for o in range(HY_ORDER):
        h_f = _dot_hp(h, w3_ref[:, 2 * o, :]) * dec
        h_b = jnp.where(row0, 0.0, _dot_hp(h, w3_ref[:, 2 * o + 1, :]) * dec)
        k_re = _dot_hp(f_re, h_f + h_b)
        p = _dot_hp(f_im, h_f)
        q = _dot_hp(f_im, h_b)
        kr_ref[o] = k_re
        ki_ref[o] = jnp.where(row0, 0.0, p - q)
        kr2_ref[o] = jnp.where(row0, p + q, k_re)


def _hy_filter(seq, w1, b1, f1, w2, b2, f2, w3, f_re, f_im):
    t = jnp.linspace(0.0, 1.0, seq, dtype=F32)[:, None]
    w = 2.0 * math.pi * jnp.arange(seq, dtype=F32)[:, None] / seq
    f = jnp.linspace(1e-4, HY_BANDS - 1, HY_BANDS, dtype=F32)[None, :]
    emb = jnp.concatenate([t, jnp.cos(f * w), -jnp.sin(f * w), jnp.zeros((seq, LANES - HY_EMB), F32)], axis=-1)
    decay = jnp.exp(-t * jnp.linspace(HY_MIN_DECAY, HY_MAX_DECAY, D_MODEL, dtype=F32)[None, :])
    w1p = jnp.concatenate([w1, jnp.zeros((LANES - HY_EMB, HY_FFN), F32)], axis=0)
    dblk = 256
    out = jax.ShapeDtypeStruct((HY_ORDER, seq, D_MODEL), F32)
    full = lambda shape: pl.BlockSpec(shape, lambda j: (0,) * len(shape))
    ospec = pl.BlockSpec((HY_ORDER, seq, dblk), lambda j: (0, 0, j))
    return pl.pallas_call(
        functools.partial(_hy_filter_kernel, seq=seq),
        out_shape=(out, out, out),
        grid=(D_MODEL // dblk,),
        in_specs=[
            full((seq, LANES)),
            pl.BlockSpec((seq, dblk), lambda j: (0, j)),
            full((LANES, HY_FFN)), full((1, HY_FFN)), full((1, HY_FFN)),
            full((HY_FFN, HY_FFN)), full((1, HY_FFN)), full((1, HY_FFN)),
            pl.BlockSpec((HY_FFN, 2 * HY_ORDER, dblk), lambda j: (0, 0, j)),
            full((seq, seq)), full((seq, seq)),
        ],
        out_specs=(ospec, ospec, ospec),
        compiler_params=_cparams("arbitrary"),
        name=f"hy_filter_{seq}",
    )(emb, decay, w1p, b1[None, :], f1[None, :], w2, b2[None, :], f2[None, :],
      w3.reshape(HY_FFN, 2 * HY_ORDER, D_MODEL), f_re, f_im)


def _hy_conv_kernel(*refs, seq, has_prev):
    (zv_ref, z1_ref, z2_ref, cv_ref, c1_ref, c2_ref, kr_ref, ki_ref, kr2_ref, bias_ref,
     fre_ref, fim_ref, bre_ref, bim_ref) = refs[:14]
    y_ref = refs[-1]
    t = lax.broadcasted_iota(jnp.int32, (seq, 1), 0)
    first, last = t == 0, t == seq - 1

    def short_conv(z_ref, c_ref):
        z = z_ref[...].astype(F32)
        prev = jnp.where(first, 0.0, pltpu.roll(z, 1, 0))
        nxt = jnp.where(last, 0.0, pltpu.roll(z, seq - 1, 0))
        return c_ref[0:1, :] * prev + c_ref[1:2, :] * z + c_ref[2:3, :] * nxt

    def long_conv(s, o):
        sb = _bf(s)
        x_re = _dot(fre_ref[...], sb)
        x_im = _dot(fim_ref[...], sb)
        k_re, k_im, k_re2 = kr_ref[o], ki_ref[o], kr2_ref[o]
        y_re = x_re * k_re - x_im * k_im
        y_im = x_re * k_im + x_im * k_re2
        y = _dot(bre_ref[...], _bf(y_re)) + _dot(bim_ref[...], _bf(y_im))
        return y + bias_ref[o] * s

    v = short_conv(zv_ref, cv_ref)
    x1 = short_conv(z1_ref, c1_ref)
    x2 = short_conv(z2_ref, c2_ref)
    y_ref[...] = _bf(x2 * long_conv(x1 * long_conv(v, 0), 1))


def _hy_conv(st, ctx, z, conv_w, filt, bias, tables, prev):
    seq = st.lp if ctx else st.ls
    nb = st.bp if ctx else st.bs
    off = 0 if ctx else st.tp // st.ls
    dblk = 256
    nd = D_MODEL // dblk
    kr, ki, kr2 = filt
    f_re, f_im, b_re, b_im = tables
    full = lambda shape: pl.BlockSpec(shape, lambda j, b: (0,) * len(shape))
    kspec = pl.BlockSpec((HY_ORDER, seq, dblk), lambda j, b: (0, 0, j))
    in_specs = [
        pl.BlockSpec((seq, dblk), lambda j, b: (off + b, j)),
        pl.BlockSpec((seq, dblk), lambda j, b: (off + b, nd + j)),
        pl.BlockSpec((seq, dblk), lambda j, b: (off + b, 2 * nd + j)),
        pl.BlockSpec((3, dblk), lambda j, b: (0, j)),
        pl.BlockSpec((3, dblk), lambda j, b: (0, nd + j)),
        pl.BlockSpec((3, dblk), lambda j, b: (0, 2 * nd + j)),
        kspec, kspec, kspec,
        pl.BlockSpec((HY_ORDER, 1, dblk), lambda j, b: (0, 0, j)),
        full((seq, seq)), full((seq, seq)), full((seq, seq)), full((seq, seq)),
    ]
    args = [z, z, z, conv_w, conv_w, conv_w, kr, ki, kr2, bias.reshape(HY_ORDER, 1, D_MODEL),
            f_re, f_im, b_re, b_im]
    aliases = {}
    if prev is not None:
        aliases = {len(args): 0}
        in_specs.append(pl.BlockSpec(memory_space=pl.ANY))
        args.append(prev)
    return pl.pallas_call(
        functools.partial(_hy_conv_kernel, seq=seq, has_prev=prev is not None),
        out_shape=jax.ShapeDtypeStruct((st.t, D_MODEL), BF16),
        grid=(nd, nb),
        in_specs=in_specs,
        out_specs=pl.BlockSpec((seq, dblk), lambda j, b: (off + b, j)),
        input_output_aliases=aliases,
        compiler_params=_cparams("arbitrary", "arbitrary"),
        name="hy_conv_ctx" if ctx else "hy_conv_lat",
    )(*args)


def _hy_outproj_kernel(y_ref, x_ref, mod_ref, w_ref, n2w_ref, x3_ref, h2_ref):
    mod = _mod_slices(mod_ref[...])
    g1, sh2, sc2 = mod[2], mod[3], mod[4]
    x3 = x_ref[...] + g1 * _dot(y_ref[...], w_ref[...])
    x3_ref[...] = x3
    h2_ref[...] = _bf(_rms(x3, n2w_ref[...]) * (1.0 + sc2) + sh2)


def _hy_outproj(st, y, x, mods, w, n2w):
    return pl.pallas_call(
        _hy_outproj_kernel,
        out_shape=(jax.ShapeDtypeStruct((st.t, D_MODEL), F32), jax.ShapeDtypeStruct((st.t, D_MODEL), BF16)),
        grid=(st.t // TM,),
        in_specs=[
            pl.BlockSpec((TM, D_MODEL), lambda i: (i, 0)),
            pl.BlockSpec((TM, D_MODEL), lambda i: (i, 0)),
            pl.BlockSpec((None, 1, N_MOD * D_MODEL), lambda i: (st.mod_row(i, TM), 0, 0)),
            pl.BlockSpec((D_MODEL, D_MODEL), lambda i: (0, 0)),
            pl.BlockSpec((1, D_MODEL), lambda i: (0, 0)),
        ],
        out_specs=(pl.BlockSpec((TM, D_MODEL), lambda i: (i, 0)), pl.BlockSpec((TM, D_MODEL), lambda i: (i, 0))),
        compiler_params=_cparams("arbitrary"),
        name="hy_outproj",
    )(y, x, mods, w, n2w)


def _grid_pos_table(seq):
    rows = seq // GRID_W
    r, cl = jnp.meshgrid(jnp.arange(rows, dtype=F32), jnp.arange(GRID_W, dtype=F32), indexing='ij')
    quarter = D_MODEL // 4
    omega = POS_THETA ** (-jnp.arange(quarter, dtype=F32) / quarter)

    def enc(pos):
        a = pos.reshape(-1, 1) * omega[None, :]
        return jnp.concatenate([jnp.sin(a), jnp.cos(a)], axis=-1)

    return jnp.concatenate([enc(r), enc(cl)], axis=-1)


def _router_weights(w_group, b_group, w_router, b_router):
    pad = LANES - N_EXPERTS - N_GROUPS
    w = jnp.concatenate([w_router, w_group, jnp.zeros((D_MODEL, pad), F32)], axis=1)
    b = jnp.concatenate([b_router, b_group, jnp.zeros((pad,), F32)])[None, :]
    return w, b


def kernel(x_prompt, x_sample, state_gla, state_mlstm_c, state_mlstm_n, state_mlstm_m, c, c_ctx, norm1_w, norm2_w, norm_f_w, ada_w, ada_b, rec_w_in, gla_gk_w, gla_gk_b, mlstm_gate_b, gla_norm_w, mlstm_norm_w, rec_w_out, hy_w_in, hy_conv_w, hy_f_w1, hy_f_b1, hy_f_freq1, hy_f_w2, hy_f_b2, hy_f_freq2, hy_f_w3, hy_f_bias, hy_w_out, moe_w_group, moe_b_group, moe_w_router, moe_b_router, moe_w_gate, moe_w_up, moe_w_down):
    bp, lp, _ = x_prompt.shape
    bs, ls, _ = x_sample.shape
    st = _Streams(bp, lp, bs, ls)
    xp = x_prompt.reshape(st.tp, D_MODEL)
    xs = x_sample.reshape(st.ts, D_MODEL)

    nrow = -(-(1 + bs) // 8) * 8
    cv = jnp.concatenate([c_ctx[None, :], c, jnp.zeros((nrow - 1 - bs, D_MODEL), F32)], axis=0)
    mods = _ada(cv, ada_w, ada_b)
    mods0 = mods[0].reshape(nrow, 1, N_MOD * D_MODEL)
    mods1 = mods[1].reshape(nrow, 1, N_MOD * D_MODEL)

    pos = _grid_pos_table(ls)

    w_in = rec_w_in[0]
    w_main = _bf(w_in[:, :_REC_MAIN])
    w_gate = jnp.concatenate([w_in[:, _REC_MAIN:], jnp.zeros((D_MODEL, LANES - _REC_GATE), F32)], axis=1)
    w_gate_t = w_in[:, _REC_MAIN + _MG0:].T
    main, g, mt = _inproj(st, xp, xs, pos, mods0, norm1_w[0][None, :], w_main, w_gate, w_gate_t)

    gkw = jnp.zeros((2, LANES, H_A * DK_A), F32)
    gkw = gkw.at[0, :GK_RANK].set(gla_gk_w[0, 0]).at[1, GK_RANK:2 * GK_RANK].set(gla_gk_w[0, 1])
    gkb = gla_gk_b[0][:, None, :]
    gb = mlstm_gate_b[0].reshape(16)
    gbrow = jnp.zeros((1, LANES), F32).at[0, _MG0:_MG0 + 16].set(gb)
    gbcol = gb[:, None]
    consts = [gkw, gkb, gbrow, gbcol, gla_norm_w[0][None, :], mlstm_norm_w[0][None, :],
              _bf(rec_w_out[0]), norm2_w[0][None, :]]

    def scan_states(sg, sc, sn, sm):
        nb = sg.shape[0]
        return (jnp.swapaxes(sg, -1, -2), sc, sn.reshape(nb, 2 * H_B, DK_B), sm.reshape(nb, 1, 2 * H_B))

    zero = (jnp.zeros((bp, 2, H_A, DK_A, DV_A), F32), jnp.zeros((bp, 2, H_B, DK_B, DV_B), F32),
            jnp.zeros((bp, 2, H_B, DK_B), F32), jnp.zeros((bp, 2, H_B), F32))
    cached = (state_gla[:, 0], state_mlstm_c[:, 0], state_mlstm_n[:, 0], state_mlstm_m[:, 0])
    x1, h2, fg, fc, fn, fm = _scan(st, True, main, g, mt, xp, None, mods0, scan_states(*zero), consts, None)
    x1, h2 = _scan(st, False, main, g, mt, xs, pos, mods0, scan_states(*cached), consts, (x1, h2))[:2]

    new_gla = jnp.swapaxes(fg, -1, -2)[:, None]
    new_c = fc[:, None]
    new_n = fn.reshape(bp, 1, 2, H_B, DK_B)
    new_m = fm.reshape(bp, 1, 2, H_B)

    wr0, br0 = _router_weights(moe_w_group[0], moe_b_group[0], moe_w_router[0], moe_b_router[0])
    x2 = _moe(st, h2, x1, mods0, wr0, br0, _bf(moe_w_gate[0]), _bf(moe_w_up[0]), _bf(moe_w_down[0]),
              norm_f_w[None, :], False)

    z = _hy_inproj(st, x2, mods1, norm1_w[1][None, :], _bf(hy_w_in[0]))
    y = None
    for ctx in (True, False):
        seq = lp if ctx else ls
        tables = _dft_tables(seq)
        filt = _hy_filter(seq, hy_f_w1[0], hy_f_b1[0], hy_f_freq1[0], hy_f_w2[0], hy_f_b2[0], hy_f_freq2[0],
                          hy_f_w3[0], tables[0], tables[1])
        y = _hy_conv(st, ctx, z, hy_conv_w[0], filt, hy_f_bias[0], tuple(_bf(a) for a in tables), y)
    x3, h4 = _hy_outproj(st, y, x2, mods1, _bf(hy_w_out[0]), norm2_w[1][None, :])

    wr1, br1 = _router_weights(moe_w_group[1], moe_b_group[1], moe_w_router[1], moe_b_router[1])
    out = _moe(st, h4, x3, mods1, wr1, br1, _bf(moe_w_gate[1]), _bf(moe_w_up[1]), _bf(moe_w_down[1]),
               norm_f_w[None, :], True)
    y_prompt = out[:st.tp].reshape(bp, lp, D_MODEL)
    y_sample = out[st.tp:].reshape(bs, ls, D_MODEL)
    return (y_prompt, y_sample, new_gla, new_c, new_n, new_m)
```

```python
import functools
import math

import jax
import jax.numpy as jnp
from jax import lax
from jax.experimental import pallas as pl
from jax.experimental.pallas import tpu as pltpu

F32 = jnp.float32
BF16 = jnp.bfloat16

D_MODEL = 1024
GRID_W = 64
H_A = 4
DK_A = D_MODEL // 16
DV_A = D_MODEL // 8
GK_RANK = 16
GATE_TEMP = 16.0
H_B = 4
DK_B = D_MODEL // 16
DV_B = D_MODEL // 8
CHUNK = 64
HY_ORDER = 2
HY_EMB = 33
HY_BANDS = (HY_EMB - 1) // 2
HY_FFN = 64
HY_TARGET = 1e-2
HY_MAX_DECAY = abs(math.log(HY_TARGET)) / 0.3
HY_MIN_DECAY = abs(math.log(HY_TARGET)) / 1.5
N_GROUPS = 4
EXP_PER_GROUP = 4
N_EXPERTS = N_GROUPS * EXP_PER_GROUP
D_EXPERT = D_MODEL // 2
N_MOD = 6
POS_THETA = 10000.0
EPS = 1e-6

_QA, _KA, _VA, _RA = 0, 256, 512, 1024
_QB, _KB, _VB, _OB = 1536, 1792, 2048, 2560
_REC_MAIN = 3072
_REC_GATE = 48
_MG0 = 2 * GK_RANK

LANES = 128
SUBLANES = 8
_RPV = D_MODEL // LANES
assert _RPV == SUBLANES
TM = 256
SB_MOE = 2048
ROW_PAD = 16
LG_ROWS = 32
VMEM_LIMIT = 56 * 1024 * 1024


def _cparams(*sem):
    return pltpu.CompilerParams(dimension_semantics=sem, vmem_limit_bytes=VMEM_LIMIT)


def _bf(x):
    return x.astype(BF16)


def _dot(a, b):
    return jnp.dot(a, b, preferred_element_type=F32)


def _dot_nt(a, b):
    return lax.dot_general(a, b, (((1,), (1,)), ((), ())), preferred_element_type=F32)


def _dot_tn(a, b):
    return lax.dot_general(a, b, (((0,), (0,)), ((), ())), preferred_element_type=F32)


def _split2(x):
    hi = _bf(x)
    return hi, _bf(x - hi.astype(F32))


def _split3(x):
    hi = _bf(x)
    r = x - hi.astype(F32)
    mid = _bf(r)
    return hi, mid, _bf(r - mid.astype(F32))


def _dot_hp(a, b, dot=_dot):
    ah, al = _split2(a)
    bh, bl = _split2(b)
    return dot(ah, bh) + (dot(ah, bl) + dot(al, bh))


def _dot_mask_l(m, x):
    x1, x2, x3 = _split3(x)
    return _dot(m, x1) + (_dot(m, x2) + _dot(m, x3))


def _dot_mask_r(x, m):
    x1, x2, x3 = _split3(x)
    return _dot(x1, m) + (_dot(x2, m) + _dot(x3, m))


def _rms(x, w):
    return x * lax.rsqrt(jnp.mean(x * x, axis=-1, keepdims=True) + EPS) * w


def _mod_slices(mod):
    return [mod[:, k * D_MODEL:(k + 1) * D_MODEL] for k in range(N_MOD)]


def _store_rows_per_vreg(dst, h):
    rows = h.shape[0]
    for j in range(_RPV):
        dst[pl.ds(j, rows, stride=_RPV), :] = h[:, j * LANES:(j + 1) * LANES]


def _load_rows_per_vreg(src, rows):
    return jnp.concatenate([src[pl.ds(j, rows, stride=_RPV), :] for j in range(_RPV)], axis=-1)


def _router_logits(h, wrt_ref, brt_ref):
    return _dot_hp(wrt_ref[...], h, dot=_dot_nt) + brt_ref[...]


def _ada_kernel(cv_ref, w_ref, b_ref, o_ref):
    a = cv_ref[...]
    a = a * jax.nn.sigmoid(a)
    o_ref[...] = _dot_hp(a, w_ref[...]) + b_ref[...]


def _ada(cv, ada_w, ada_b):
    depth, d, n = ada_w.shape
    rows = cv.shape[0]
    tn = 768
    return pl.pallas_call(
        _ada_kernel,
        out_shape=jax.ShapeDtypeStruct((depth, rows, n), F32),
        grid=(depth, n // tn),
        in_specs=[
            pl.BlockSpec((rows, d), lambda l, j: (0, 0)),
            pl.BlockSpec((None, d, tn), lambda l, j: (l, 0, j)),
            pl.BlockSpec((None, 1, tn), lambda l, j: (l, 0, j)),
        ],
        out_specs=pl.BlockSpec((None, rows, tn), lambda l, j: (l, 0, j)),
        compiler_params=_cparams("arbitrary", "arbitrary"),
        name="ada_mod",
    )(cv, ada_w, ada_b.reshape(depth, 1, n))


class _Streams:
    def __init__(self, bp, lp, bs, ls):
        self.bp, self.lp, self.bs, self.ls = bp, lp, bs, ls
        self.tp, self.ts = bp * lp, bs * ls
        self.t = self.tp + self.ts
        assert lp % TM == 0 and ls % TM == 0 and self.tp % ls == 0
        self.sb = min(SB_MOE, math.gcd(self.tp, self.ts))
        assert self.t % self.sb == 0 and self.sb % TM == 0

    def mod_row(self, i, tm):
        ncb = self.tp // tm
        return jnp.where(i < ncb, 0, 1 + (i - ncb) // (self.ls // tm))


def _inproj_kernel(xp_ref, xs_ref, pos_ref, mod_ref, nw_ref, w_ref, wg_ref, wgt_ref,
                   main_ref, g_ref, mt_ref, *, ncb):
    i = pl.program_id(0)
    x = jnp.where(i < ncb, xp_ref[...], xs_ref[...] + pos_ref[...])
    sh1, sc1 = _mod_slices(mod_ref[...])[:2]
    h = _rms(x, nw_ref[...]) * (1.0 + sc1) + sh1
    main_ref[...] = _bf(_dot(_bf(h), w_ref[...]))
    g_ref[...] = _dot_hp(h, wg_ref[...])
    mt = _dot_hp(wgt_ref[...], h, dot=_dot_nt)
    for c in range(TM // CHUNK):
        mt_ref[c] = mt[:, c * CHUNK:(c + 1) * CHUNK]


def _inproj(st, xp, xs, pos, mods, nw, w_main, w_gate, w_gate_t):
    ncb = st.tp // TM
    bps = st.ls // TM
    t = st.t
    return pl.pallas_call(
        functools.partial(_inproj_kernel, ncb=ncb),
        out_shape=(jax.ShapeDtypeStruct((t, _REC_MAIN), BF16),
                   jax.ShapeDtypeStruct((t, LANES), F32),
                   jax.ShapeDtypeStruct((t // CHUNK, 16, CHUNK), F32)),
        grid=(t // TM,),
        in_specs=[
            pl.BlockSpec((TM, D_MODEL), lambda i: (jnp.minimum(i, ncb - 1), 0)),
            pl.BlockSpec((TM, D_MODEL), lambda i: (jnp.maximum(i - ncb, 0), 0)),
            pl.BlockSpec((TM, D_MODEL), lambda i: (jnp.maximum(i - ncb, 0) % bps, 0)),
            pl.BlockSpec((None, 1, N_MOD * D_MODEL), lambda i: (st.mod_row(i, TM), 0, 0)),
            pl.BlockSpec((1, D_MODEL), lambda i: (0, 0)),
            pl.BlockSpec((D_MODEL, _REC_MAIN), lambda i: (0, 0)),
            pl.BlockSpec((D_MODEL, LANES), lambda i: (0, 0)),
            pl.BlockSpec((16, D_MODEL), lambda i: (0, 0)),
        ],
        out_specs=(pl.BlockSpec((TM, _REC_MAIN), lambda i: (i, 0)),
                   pl.BlockSpec((TM, LANES), lambda i: (i, 0)),
                   pl.BlockSpec((TM // CHUNK, 16, CHUNK), lambda i: (i, 0, 0))),
        compiler_params=_cparams("arbitrary"),
        name="rec_inproj",
    )(xp, xs, pos, mods, nw, w_main, w_gate, w_gate_t)


def _scan_kernel(*refs, seq, add_pos):
    if add_pos:
        (main_ref, g_ref, mt_ref, x_ref, pos_ref, mod_ref, sg_ref, sc_ref, sn_ref, sm_ref,
         gkw_ref, gkb_ref, gbrow_ref, gbcol_ref, gnw_ref, mnw_ref, wout_ref, n2w_ref, wrt_ref, brt_ref,
         _x1_in, _h2_in, _lg_in,
         x1_ref, h2_ref, lg_ref, og_ref, oc_ref, on_ref, om_ref, oa_scr, ob_scr) = refs
    else:
        (main_ref, g_ref, mt_ref, x_ref, mod_ref, sg_ref, sc_ref, sn_ref, sm_ref,
         gkw_ref, gkb_ref, gbrow_ref, gbcol_ref, gnw_ref, mnw_ref, wout_ref, n2w_ref, wrt_ref, brt_ref,
         x1_ref, h2_ref, lg_ref, og_ref, oc_ref, on_ref, om_ref, oa_scr, ob_scr) = refs
        pos_ref = None
    c = CHUNK
    nchunks = seq // c

    og_ref[...] = sg_ref[...]
    oc_ref[...] = sc_ref[...]
    on_ref[...] = sn_ref[...]
    om_ref[...] = sm_ref[...]
    oa_scr[...] = jnp.zeros_like(oa_scr)
    ob_scr[...] = jnp.zeros_like(ob_scr)

    row = lax.broadcasted_iota(jnp.int32, (c, c), 0)
    col = lax.broadcasted_iota(jnp.int32, (c, c), 1)
    live = (col <= row, col >= row)
    tri = tuple(jnp.where(m, 1.0, 0.0).astype(BF16) for m in live)
    tri_t = (tri[1], tri[0])
    lane = lax.broadcasted_iota(jnp.int32, (1, LANES), 1)
    f_lane = (lane >= _MG0) & (lane < _MG0 + 16) & (((lane - _MG0) % 8) >= 4)
    grow = lax.broadcasted_iota(jnp.int32, (16, 1), 0)
    f_row = (grow % 8) >= 4
    scale_q = DK_A ** -0.5

    def chunk_step(i, carry):
        for d in (0, 1):
            ci = i if d == 0 else nchunks - 1 - i
            r0 = pl.multiple_of(ci * c, c)
            rows = pl.ds(r0, c)
            end = c - 1 if d == 0 else 0
            g_in = g_ref[rows, :]
            glin = _dot_hp(g_in, gkw_ref[d]) + gkb_ref[d]
            glog = jax.nn.log_sigmoid(glin) / GATE_TEMP
            bc = _dot_mask_l(tri[d], glog)
            b_end = bc[end:end + 1, :]
            qa = main_ref[rows, _QA:_QA + 256].astype(F32) * scale_q
            ka = main_ref[rows, _KA:_KA + 256].astype(F32)
            va = main_ref[rows, _VA:_VA + 512]
            qe = _bf(qa * jnp.exp(bc))
            ke = _bf(ka * jnp.exp(-bc))
            kd = _bf(ka * jnp.exp(b_end - bc))
            eb_end = jnp.exp(b_end)
            for h in range(H_A):
                ks = slice(h * DK_A, (h + 1) * DK_A)
                vs = slice(h * DV_A, (h + 1) * DV_A)
                a = jnp.where(live[d], _dot_nt(qe[:, ks], ke[:, ks]), 0.0)
                s_t = og_ref[d, h]
                o = _dot(_bf(a), va[:, vs]) + _dot_nt(qe[:, ks], _bf(s_t))
                oa_scr[rows, vs] += o
                og_ref[d, h] = s_t * eb_end[:, ks] + _dot_tn(va[:, vs], kd[:, ks])
            gm = g_in + gbrow_ref[...]
            gm = jnp.where(f_lane, jax.nn.log_sigmoid(gm), gm)
            cum = _dot_mask_l(tri[d], gm)
            mt = mt_ref[ci] + gbcol_ref[...]
            mt = jnp.where(f_row, jax.nn.log_sigmoid(mt), mt)
            cum_t = _dot_mask_r(mt, tri_t[d])
            qb = main_ref[rows, _QB:_QB + 256] * jnp.asarray(DK_B ** -0.5, BF16)
            kb = main_ref[rows, _KB:_KB + 256]
            vb = main_ref[rows, _VB:_VB + 512]
            for h in range(H_B):
                ks = slice(h * DK_B, (h + 1) * DK_B)
                vs = slice(h * DV_B, (h + 1) * DV_B)
                r = d * H_B + h
                li = _MG0 + d * 8 + h
                b_col = cum[:, li + 4:li + 5]
                i_col = gm[:, li:li + 1]
                b_row = cum_t[d * 8 + h + 4:d * 8 + h + 5, :]
                i_row = mt[d * 8 + h:d * 8 + h + 1, :]
                m_old = om_ref[:, r:r + 1]
                dlog = jnp.where(live[d], b_col - b_row + i_row, -jnp.inf)
                inter = b_col + m_old
                m_t = jnp.maximum(inter, jnp.max(dlog, axis=-1, keepdims=True))
                w = jnp.exp(dlog - m_t)
                a_in = jnp.exp(inter - m_t)
                qh, kh, vh = qb[:, ks], kb[:, ks], vb[:, vs]
                s = _dot_nt(qh, kh) * w
                c_old = oc_ref[d, h]
                n_old = on_ref[r:r + 1, :]
                num = _dot(_bf(s), vh) + a_in * _dot(qh, _bf(c_old))
                den = (jnp.sum(s, axis=-1, keepdims=True)
                       + a_in * jnp.sum(qh.astype(F32) * n_old, axis=-1, keepdims=True))
                ob_scr[rows, vs] += num / jnp.maximum(jnp.abs(den), jnp.exp(-m_t))
                b_last = b_col[end:end + 1, :]
                dend = b_last - b_col + i_col
                m_new = jnp.maximum(b_last + m_old, jnp.max(dend, axis=0, keepdims=True))
                decay = jnp.exp(b_last + m_old - m_new)
                kw = kh.astype(F32) * jnp.exp(dend - m_new)
                oc_ref[d, h] = decay * c_old + _dot_tn(_bf(kw), vh)
                on_ref[r:r + 1, :] = decay * n_old + jnp.sum(kw, axis=0, keepdims=True)
                om_ref[:, r:r + 1] = m_new
        return carry

    lax.fori_loop(0, nchunks, chunk_step, 0)

    mod = _mod_slices(mod_ref[...])
    g1, sh2, sc2 = mod[2], mod[3], mod[4]

    def out_step(j, carry):
        r0 = pl.multiple_of(j * TM, TM)
        rows = pl.ds(r0, TM)
        parts = []
        for h in range(H_A):
            vs = slice(h * DV_A, (h + 1) * DV_A)
            ra = main_ref[rows, _RA + h * DV_A:_RA + (h + 1) * DV_A].astype(F32)
            parts.append(_rms(oa_scr[rows, vs], gnw_ref[...]) * (ra * jax.nn.sigmoid(ra)))
        for h in range(H_B):
            vs = slice(h * DV_B, (h + 1) * DV_B)
            ob = main_ref[rows, _OB + h * DV_B:_OB + (h + 1) * DV_B].astype(F32)
            parts.append(_rms(ob_scr[rows, vs], mnw_ref[...]) * jax.nn.sigmoid(ob))
        mix = _dot(_bf(jnp.concatenate(parts, axis=-1)), wout_ref[...])
        x = x_ref[rows, :]
        if add_pos:
            x = x + pos_ref[rows, :]
        x1 = x + g1 * mix
        x1_ref[rows, :] = x1
        h2 = _rms(x1, n2w_ref[...]) * (1.0 + sc2) + sh2
        _store_rows_per_vreg(h2_ref.at[pl.ds(pl.multiple_of(r0 * _RPV, TM * _RPV), TM * _RPV)], h2)
        lg_ref[j] = _router_logits(h2, wrt_ref, brt_ref)
        return carry

    lax.fori_loop(0, seq // TM, out_step, 0)


def _scan(st, ctx, main, g, mt, x, pos, mods, states, consts, prev):
    seq = st.lp if ctx else st.ls
    nb = st.bp if ctx else st.bs
    off = 0 if ctx else st.tp // st.ls
    add_pos = not ctx
    sg, sc, sn, sm = states
    full = lambda shape: pl.BlockSpec(shape, lambda b: (0,) * len(shape))
    in_specs = [
        pl.BlockSpec((seq, _REC_MAIN), lambda b: (off + b, 0)),
        pl.BlockSpec((seq, LANES), lambda b: (off + b, 0)),
        pl.BlockSpec((seq // CHUNK, 16, CHUNK), lambda b: (off + b, 0, 0)),
        pl.BlockSpec((seq, D_MODEL), lambda b: (b, 0)),
    ]
    args = [main, g, mt, x]
    if add_pos:
        in_specs.append(full((seq, D_MODEL)))
        args.append(pos)
    mod_row = (lambda b: (0, 0, 0)) if ctx else (lambda b: (1 + b, 0, 0))
    in_specs += [
        pl.BlockSpec((None, 1, N_MOD * D_MODEL), mod_row),
        pl.BlockSpec((None, 2, H_A, DV_A, DK_A), lambda b: (b, 0, 0, 0, 0)),
        pl.BlockSpec((None, 2, H_B, DK_B, DV_B), lambda b: (b, 0, 0, 0, 0)),
        pl.BlockSpec((None, 2 * H_B, DK_B), lambda b: (b, 0, 0)),
        pl.BlockSpec((None, 1, 2 * H_B), lambda b: (b, 0, 0)),
    ]
    args += [mods, sg, sc, sn, sm]
    for a in consts:
        in_specs.append(full(a.shape))
        args.append(a)
    aliases = {}
    if prev is not None:
        aliases = {len(args) + k: k for k in range(3)}
        in_specs += [pl.BlockSpec(memory_space=pl.ANY)] * 3
        args += list(prev)
    out_shape = (jax.ShapeDtypeStruct((st.t, D_MODEL), F32),
                 jax.ShapeDtypeStruct((st.t * _RPV, LANES), F32),
                 jax.ShapeDtypeStruct((st.t // TM, LG_ROWS, TM), F32),
                 jax.ShapeDtypeStruct(sg.shape, F32), jax.ShapeDtypeStruct(sc.shape, F32),
                 jax.ShapeDtypeStruct(sn.shape, F32), jax.ShapeDtypeStruct(sm.shape, F32))
    out_specs = (pl.BlockSpec((seq, D_MODEL), lambda b: (off + b, 0)),
                 pl.BlockSpec((seq * _RPV, LANES), lambda b: (off + b, 0)),
                 pl.BlockSpec((seq // TM, LG_ROWS, TM), lambda b: (off + b, 0, 0)),
                 pl.BlockSpec((None, 2, H_A, DV_A, DK_A), lambda b: (b, 0, 0, 0, 0)),
                 pl.BlockSpec((None, 2, H_B, DK_B, DV_B), lambda b: (b, 0, 0, 0, 0)),
                 pl.BlockSpec((None, 2 * H_B, DK_B), lambda b: (b, 0, 0)),
                 pl.BlockSpec((None, 1, 2 * H_B), lambda b: (b, 0, 0)))
    return pl.pallas_call(
        functools.partial(_scan_kernel, seq=seq, add_pos=add_pos),
        out_shape=out_shape,
        grid=(nb,),
        in_specs=in_specs,
        out_specs=out_specs,
        scratch_shapes=[pltpu.VMEM((seq, H_A * DV_A), F32), pltpu.VMEM((seq, H_B * DV_B), F32)],
        input_output_aliases=aliases,
        compiler_params=_cparams("arbitrary"),
        name="rec_scan_ctx" if ctx else "rec_scan_lat",
    )(*args)


def _first_max(rows):
    m = rows[0]
    for r in rows[1:]:
        m = jnp.maximum(m, r)
    idx = jnp.full(m.shape, len(rows) - 1, jnp.int32)
    for k in range(len(rows) - 2, -1, -1):
        idx = jnp.where(rows[k] == m, k, idx)
    return m, idx


def _route_kernel(lg_ref, pos1_ref, pos2_ref, w1_ref, w2_ref, tab_ref, *, sb):
    lg = jnp.concatenate([lg_ref[b] for b in range(sb // TM)], axis=1)
    rows = [lg[k:k + 1, :] for k in range(N_EXPERTS + N_GROUPS)]
    grp = rows[N_EXPERTS:]
    gmax, gidx = _first_max(grp)
    p_group = 1.0 / sum(jnp.exp(r - gmax) for r in grp)
    e_in = []
    for k in range(EXP_PER_GROUP):
        v = rows[(N_GROUPS - 1) * EXP_PER_GROUP + k]
        for g in range(N_GROUPS - 2, -1, -1):
            v = jnp.where(gidx == g, rows[g * EXP_PER_GROUP + k], v)
        e_in.append(v)
    v1, i1 = _first_max(e_in)
    v2, i2 = _first_max([jnp.where(i1 == k, -jnp.inf, e_in[k]) for k in range(EXP_PER_GROUP)])
    ex = jnp.exp(v2 - v1)
    w1_ref[...] = p_group / (1.0 + ex)
    w2_ref[...] = p_group * ex / (1.0 + ex)
    x1 = gidx * EXP_PER_GROUP + i1
    x2 = gidx * EXP_PER_GROUP + i2

    eid = lax.broadcasted_iota(jnp.int32, (N_EXPERTS, sb), 0)
    sel = jnp.where((eid == x1) | (eid == x2), 1.0, 0.0)
    r_i = lax.broadcasted_iota(jnp.int32, (TM, TM), 0)
    c_i = lax.broadcasted_iota(jnp.int32, (TM, TM), 1)
    before = jnp.where(r_i < c_i, 1.0, 0.0).astype(BF16)
    carry = jnp.zeros((N_EXPERTS, 1), F32)
    ranks = []
    for b in range(sb // TM):
        s_b = sel[:, b * TM:(b + 1) * TM]
        ranks.append(_dot(_bf(s_b), before) + carry)
        carry = carry + jnp.sum(s_b, axis=1, keepdims=True)
    rank = jnp.concatenate(ranks, axis=1)
    shift = ROW_PAD.bit_length() - 1
    npad = jnp.left_shift(jnp.right_shift(carry.astype(jnp.int32) + (ROW_PAD - 1), shift), shift)

    lane = lax.broadcasted_iota(jnp.int32, (1, LANES), 1)
    tab = jnp.zeros((1, LANES), jnp.int32)
    pos1 = jnp.zeros((1, sb), F32)
    pos2 = jnp.zeros((1, sb), F32)
    off = jnp.zeros((1, 1), jnp.int32)
    for e in range(N_EXPERTS):
        n_e = npad[e:e + 1, :]
        tab = jnp.where(lane == e, off, tab)
        tab = jnp.where(lane == N_EXPERTS + e, n_e, tab)
        row = off.astype(F32) + rank[e:e + 1, :]
        pos1 = jnp.where(x1 == e, row, pos1)
        pos2 = jnp.where(x2 == e, row, pos2)
        off = off + n_e
    pos1_ref[...] = pos1.astype(jnp.int32)
    pos2_ref[...] = pos2.astype(jnp.int32)
    tab_ref[...] = tab


def _route(st, lg):
    sb = st.sb
    nsb = st.t // sb
    row_i = jax.ShapeDtypeStruct((nsb, 1, sb), jnp.int32)
    row_f = jax.ShapeDtypeStruct((nsb, 1, sb), F32)
    rspec = pl.BlockSpec((None, 1, sb), lambda s: (s, 0, 0))
    return pl.pallas_call(
        functools.partial(_route_kernel, sb=sb),
        out_shape=(row_i, row_i, row_f, row_f, jax.ShapeDtypeStruct((nsb, 1, LANES), jnp.int32)),
        grid=(nsb,),
        in_specs=[pl.BlockSpec((sb // TM, LG_ROWS, TM), lambda s: (s, 0, 0))],
        out_specs=(rspec, rspec, rspec, rspec, pl.BlockSpec((None, 1, LANES), lambda s: (s, 0, 0))),
        compiler_params=_cparams("arbitrary"),
        name="moe_route",
    )(lg)


def _moe_rows(sb):
    return 2 * sb + N_EXPERTS * ROW_PAD


def _moe_kernel(pos1_ref, pos2_ref, w1_ref, w2_ref, tab_ref, xr_ref, wg_ref, wu_ref, wd_ref, o_ref,
                rows_scr, stage_scr, *, sb):
    s = pl.program_id(0)
    e = pl.program_id(1)

    def tile_at(ref, r):
        return ref.at[pl.ds(pl.multiple_of(r * _RPV, _RPV), _RPV)]

    @pl.when((s == 0) & (e == 0))
    def _():
        rows_scr[...] = jnp.zeros_like(rows_scr)

    @pl.when(e == 0)
    def _():
        def dispatch(t, carry):
            v = tile_at(xr_ref, t)[...]
            tile_at(rows_scr, pos1_ref[0, t])[...] = v
            tile_at(rows_scr, pos2_ref[0, t])[...] = v
            return carry

        lax.fori_loop(0, sb, dispatch, 0, unroll=8)

    def ffn_tile(r0, m):
        win = rows_scr.at[pl.ds(pl.multiple_of(r0 * _RPV, ROW_PAD * _RPV), m * _RPV)]
        x = _bf(_load_rows_per_vreg(win, m))
        hg = _dot(x, wg_ref[...])
        hu = _dot(x, wu_ref[...])
        y = _dot(_bf(hg * jax.nn.sigmoid(hg) * hu), wd_ref[...])
        _store_rows_per_vreg(win, y)

    off = tab_ref[0, e]
    npad = tab_ref[0, N_EXPERTS + e]
    nfull = npad // LANES

    def full_tile(i, carry):
        ffn_tile(off + i * LANES, LANES)
        return carry

    lax.fori_loop(0, nfull, full_tile, 0)
    r = off + nfull * LANES
    m = LANES // 2
    while m >= ROW_PAD:
        @pl.when((npad & m) != 0)
        def _(r=r, m=m):
            ffn_tile(r, m)
        r = r + (npad & m)
        m //= 2

    @pl.when(e == N_EXPERTS - 1)
    def _():
        for c in range(sb // TM):
            def combine(t, carry, c=c):
                tt = c * TM + t
                y = (w1_ref[0, tt] * tile_at(rows_scr, pos1_ref[0, tt])[...]
                     + w2_ref[0, tt] * tile_at(rows_scr, pos2_ref[0, tt])[...])
                tile_at(stage_scr, t)[...] = y
                return carry

            lax.fori_loop(0, TM, combine, 0, unroll=8)
            o_ref[c * TM:(c + 1) * TM, :] = _bf(_load_rows_per_vreg(stage_scr, TM))


def _moe(st, xr, route, wg, wu, wd):
    sb = st.sb
    smem = lambda n: pl.BlockSpec((None, 1, n), lambda s, e: (s, 0, 0), memory_space=pltpu.SMEM)
    return pl.pallas_call(
        functools.partial(_moe_kernel, sb=sb),
        out_shape=jax.ShapeDtypeStruct((st.t, D_MODEL), BF16),
        grid=(st.t // sb, N_EXPERTS),
        in_specs=[
            smem(sb), smem(sb), smem(sb), smem(sb), smem(LANES),
            pl.BlockSpec((sb * _RPV, LANES), lambda s, e: (s, 0)),
            pl.BlockSpec((None, D_MODEL, D_EXPERT), lambda s, e: (e, 0, 0)),
            pl.BlockSpec((None, D_MODEL, D_EXPERT), lambda s, e: (e, 0, 0)),
            pl.BlockSpec((None, D_EXPERT, D_MODEL), lambda s, e: (e, 0, 0)),
        ],
        out_specs=pl.BlockSpec((sb, D_MODEL), lambda s, e: (s, 0)),
        scratch_shapes=[pltpu.VMEM((_moe_rows(sb) * _RPV, LANES), F32), pltpu.VMEM((TM * _RPV, LANES), F32)],
        compiler_params=_cparams("arbitrary", "arbitrary"),
        name="moe_ffn",
    )(*route, xr, wg, wu, wd)


def _hy_inproj_kernel(x_ref, m_ref, mod0_ref, mod_ref, nw_ref, w_ref, x2_ref, z_ref):
    g2 = _mod_slices(mod0_ref[...])[5]
    x2 = x_ref[...] + g2 * m_ref[...].astype(F32)
    x2_ref[...] = x2
    sh1, sc1 = _mod_slices(mod_ref[...])[:2]
    h = _rms(x2, nw_ref[...]) * (1.0 + sc1) + sh1
    z_ref[...] = _bf(_dot(_bf(h), w_ref[...]))


def _hy_inproj(st, x, moe, mods_prev, mods, nw, w):
    n = w.shape[1]
    mspec = pl.BlockSpec((None, 1, N_MOD * D_MODEL), lambda i: (st.mod_row(i, TM), 0, 0))
    return pl.pallas_call(
        _hy_inproj_kernel,
        out_shape=(jax.ShapeDtypeStruct((st.t, D_MODEL), F32), jax.ShapeDtypeStruct((st.t, n), BF16)),
        grid=(st.t // TM,),
        in_specs=[
            pl.BlockSpec((TM, D_MODEL), lambda i: (i, 0)),
            pl.BlockSpec((TM, D_MODEL), lambda i: (i, 0)),
            mspec, mspec,
            pl.BlockSpec((1, D_MODEL), lambda i: (0, 0)),
            pl.BlockSpec((D_MODEL, n), lambda i: (0, 0)),
        ],
        out_specs=(pl.BlockSpec((TM, D_MODEL), lambda i: (i, 0)), pl.BlockSpec((TM, n), lambda i: (i, 0))),
        compiler_params=_cparams("arbitrary"),
        name="hy_inproj",
    )(x, moe, mods_prev, mods, nw, w)


def _dft_tables(seq):
    n2 = 2 * seq
    k = jnp.arange(seq, dtype=jnp.int32)
    kn = (k[:, None] * k[None, :]) % n2
    ang = kn.astype(F32) * (2.0 * math.pi / n2)
    cos, sin = jnp.cos(ang), jnp.sin(ang)
    alt = jnp.where(k % 2 == 0, 1.0, -1.0).astype(F32)
    f_re = cos
    f_im = jnp.where(k[:, None] == 0, alt[None, :], -sin)
    wk = jnp.where(k == 0, 1.0, 2.0).astype(F32) / n2
    b_re = cos * wk[None, :]
    b_im = jnp.where(k[None, :] == 0, alt[:, None] / n2, -sin * wk[None, :])
    return f_re, f_im, b_re, b_im


def _hy_filter_kernel(emb_ref, dec_ref, w1_ref, b1_ref, f1_ref, w2_ref, b2_ref, f2_ref, w3_ref,
                      fre_ref, fim_ref, kr_ref, ki_ref, kr2_ref, *, seq):
    h = jnp.sin(f1_ref[...] * (_dot_hp(emb_ref[...], w1_ref[...]) + b1_ref[...]))
    h = jnp.sin(f2_ref[...] * (_dot_hp(h, w2_ref[...]) + b2_ref[...]))
    dec = dec_ref[...]
    row0 = lax.broadcasted_iota(jnp.int32, (seq, 1), 0) == 0
    f_re, f_im = fre_ref[...], fim_ref[...]
    for o in range(HY_ORDER):
        h_f = _dot_hp(h, w3_ref[:, 2 * o, :]) * dec
        h_b = jnp.where(row0, 0.0, _dot_hp(h, w3_ref[:, 2 * o + 1, :]) * dec)
        k_re = _dot_hp(f_re, h_f + h_b)
        p = _dot_hp(f_im, h_f)
        q = _dot_hp(f_im, h_b)
        kr_ref[o] = k_re
        ki_ref[o] = jnp.where(row0, 0.0, p - q)
        kr2_ref[o] = jnp.where(row0, p + q, k_re)


def _hy_filter(seq, w1, b1, f1, w2, b2, f2, w3, f_re, f_im):
    t = jnp.linspace(0.0, 1.0, seq, dtype=F32)[:, None]
    w = 2.0 * math.pi * jnp.arange(seq, dtype=F32)[:, None] / seq
    f = jnp.linspace(1e-4, HY_BANDS - 1, HY_BANDS, dtype=F32)[None, :]
    emb = jnp.concatenate([t, jnp.cos(f * w), -jnp.sin(f * w), jnp.zeros((seq, LANES - HY_EMB), F32)], axis=-1)
    decay = jnp.exp(-t * jnp.linspace(HY_MIN_DECAY, HY_MAX_DECAY, D_MODEL, dtype=F32)[None, :])
    w1p = jnp.concatenate([w1, jnp.zeros((LANES - HY_EMB, HY_FFN), F32)], axis=0)
    dblk = 256
    out = jax.ShapeDtypeStruct((HY_ORDER, seq, D_MODEL), F32)
    full = lambda shape: pl.BlockSpec(shape, lambda j: (0,) * len(shape))
    ospec = pl.BlockSpec((HY_ORDER, seq, dblk), lambda j: (0, 0, j))
    return pl.pallas_call(
        functools.partial(_hy_filter_kernel, seq=seq),
        out_shape=(out, out, out),
        grid=(D_MODEL // dblk,),
        in_specs=[
            full((seq, LANES)),
            pl.BlockSpec((seq, dblk), lambda j: (0, j)),
            full((LANES, HY_FFN)), full((1, HY_FFN)), full((1, HY_FFN)),
            full((HY_FFN, HY_FFN)), full((1, HY_FFN)), full((1, HY_FFN)),
            pl.BlockSpec((HY_FFN, 2 * HY_ORDER, dblk), lambda j: (0, 0, j)),
            full((seq, seq)), full((seq, seq)),
        ],
        out_specs=(ospec, ospec, ospec),
        compiler_params=_cparams("arbitrary"),
        name=f"hy_filter_{seq}",
    )(emb, decay, w1p, b1[None, :], f1[None, :], w2, b2[None, :], f2[None, :],
      w3.reshape(HY_FFN, 2 * HY_ORDER, D_MODEL), f_re, f_im)


def _hy_conv_kernel(*refs, seq, has_prev):
    (zv_ref, z1_ref, z2_ref, cv_ref, c1_ref, c2_ref, kr_ref, ki_ref, kr2_ref, bias_ref,
     fre_ref, fim_ref, bre_ref, bim_ref) = refs[:14]
    y_ref = refs[-1]
    t = lax.broadcasted_iota(jnp.int32, (seq, 1), 0)
    first, last = t == 0, t == seq - 1

    def short_conv(z_ref, c_ref):
        z = z_ref[...].astype(F32)
        prev = jnp.where(first, 0.0, pltpu.roll(z, 1, 0))
        nxt = jnp.where(last, 0.0, pltpu.roll(z, seq - 1, 0))
        return c_ref[0:1, :] * prev + c_ref[1:2, :] * z + c_ref[2:3, :] * nxt

    def long_conv(s, o):
        sb = _bf(s)
        x_re = _dot(fre_ref[...], sb)
        x_im = _dot(fim_ref[...], sb)
        k_re, k_im, k_re2 = kr_ref[o], ki_ref[o], kr2_ref[o]
        y_re = x_re * k_re - x_im * k_im
        y_im = x_re * k_im + x_im * k_re2
        y = _dot(bre_ref[...], _bf(y_re)) + _dot(bim_ref[...], _bf(y_im))
        return y + bias_ref[o] * s

    v = short_conv(zv_ref, cv_ref)
    x1 = short_conv(z1_ref, c1_ref)
    x2 = short_conv(z2_ref, c2_ref)
    y_ref[...] = _bf(x2 * long_conv(x1 * long_conv(v, 0), 1))


def _hy_conv(st, ctx, z, conv_w, filt, bias, tables, prev):
    seq = st.lp if ctx else st.ls
    nb = st.bp if ctx else st.bs
    off = 0 if ctx else st.tp // st.ls
    dblk = 256
    nd = D_MODEL // dblk
    kr, ki, kr2 = filt
    f_re, f_im, b_re, b_im = tables
    full = lambda shape: pl.BlockSpec(shape, lambda j, b: (0,) * len(shape))
    kspec = pl.BlockSpec((HY_ORDER, seq, dblk), lambda j, b: (0, 0, j))
    in_specs = [
        pl.BlockSpec((seq, dblk), lambda j, b: (off + b, j)),
        pl.BlockSpec((seq, dblk), lambda j, b: (off + b, nd + j)),
        pl.BlockSpec((seq, dblk), lambda j, b: (off + b, 2 * nd + j)),
        pl.BlockSpec((3, dblk), lambda j, b: (0, j)),
        pl.BlockSpec((3, dblk), lambda j, b: (0, nd + j)),
        pl.BlockSpec((3, dblk), lambda j, b: (0, 2 * nd + j)),
        kspec, kspec, kspec,
        pl.BlockSpec((HY_ORDER, 1, dblk), lambda j, b: (0, 0, j)),
        full((seq, seq)), full((seq, seq)), full((seq, seq)), full((seq, seq)),
    ]
    args = [z, z, z, conv_w, conv_w, conv_w, kr, ki, kr2, bias.reshape(HY_ORDER, 1, D_MODEL),
            f_re, f_im, b_re, b_im]
    aliases = {}
    if prev is not None:
        aliases = {len(args): 0}
        in_specs.append(pl.BlockSpec(memory_space=pl.ANY))
        args.append(prev)
    return pl.pallas_call(
        functools.partial(_hy_conv_kernel, seq=seq, has_prev=prev is not None),
        out_shape=jax.ShapeDtypeStruct((st.t, D_MODEL), BF16),
        grid=(nd, nb),
        in_specs=in_specs,
        out_specs=pl.BlockSpec((seq, dblk), lambda j, b: (off + b, j)),
        input_output_aliases=aliases,
        compiler_params=_cparams("arbitrary", "arbitrary"),
        name="hy_conv_ctx" if ctx else "hy_conv_lat",
    )(*args)


def _hy_outproj_kernel(y_ref, x_ref, mod_ref, w_ref, n2w_ref, wrt_ref, brt_ref, x3_ref, h2_ref, lg_ref):
    mod = _mod_slices(mod_ref[...])
    g1, sh2, sc2 = mod[2], mod[3], mod[4]
    x3 = x_ref[...] + g1 * _dot(y_ref[...], w_ref[...])
    x3_ref[...] = x3
    h2 = _rms(x3, n2w_ref[...]) * (1.0 + sc2) + sh2
    _store_rows_per_vreg(h2_ref, h2)
    lg_ref[...] = _router_logits(h2, wrt_ref, brt_ref)


def _hy_outproj(st, y, x, mods, w, n2w, wrt, brt):
    return pl.pallas_call(
        _hy_outproj_kernel,
        out_shape=(jax.ShapeDtypeStruct((st.t, D_MODEL), F32),
                   jax.ShapeDtypeStruct((st.t * _RPV, LANES), F32),
                   jax.ShapeDtypeStruct((st.t // TM, LG_ROWS, TM), F32)),
        grid=(st.t // TM,),
        in_specs=[
            pl.BlockSpec((TM, D_MODEL), lambda i: (i, 0)),
            pl.BlockSpec((TM, D_MODEL), lambda i: (i, 0)),
            pl.BlockSpec((None, 1, N_MOD * D_MODEL), lambda i: (st.mod_row(i, TM), 0, 0)),
            pl.BlockSpec((D_MODEL, D_MODEL), lambda i: (0, 0)),
            pl.BlockSpec((1, D_MODEL), lambda i: (0, 0)),
            pl.BlockSpec((LG_ROWS, D_MODEL), lambda i: (0, 0)),
            pl.BlockSpec((LG_ROWS, 1), lambda i: (0, 0)),
        ],
        out_specs=(pl.BlockSpec((TM, D_MODEL), lambda i: (i, 0)),
                   pl.BlockSpec((TM * _RPV, LANES), lambda i: (i, 0)),
                   pl.BlockSpec((None, LG_ROWS, TM), lambda i: (i, 0, 0))),
        compiler_params=_cparams("arbitrary"),
        name="hy_outproj",
    )(y, x, mods, w, n2w, wrt, brt)


def _final_kernel(x_ref, m_ref, mod_ref, nf_ref, yp_ref, ys_ref, *, ncb):
    i = pl.program_id(0)
    g2 = _mod_slices(mod_ref[...])[5]
    y = _rms(x_ref[...] + g2 * m_ref[...].astype(F32), nf_ref[...])

    @pl.when(i < ncb)
    def _():
        yp_ref[...] = y

    @pl.when(i >= ncb)
    def _():
        ys_ref[...] = y


def _final(st, x, moe, mods, nf):
    ncb = st.tp // TM
    return pl.pallas_call(
        functools.partial(_final_kernel, ncb=ncb),
        out_shape=(jax.ShapeDtypeStruct((st.tp, D_MODEL), F32), jax.ShapeDtypeStruct((st.ts, D_MODEL), F32)),
        grid=(st.t // TM,),
        in_specs=[
            pl.BlockSpec((TM, D_MODEL), lambda i: (i, 0)),
            pl.BlockSpec((TM, D_MODEL), lambda i: (i, 0)),
            pl.BlockSpec((None, 1, N_MOD * D_MODEL), lambda i: (st.mod_row(i, TM), 0, 0)),
            pl.BlockSpec((1, D_MODEL), lambda i: (0, 0)),
        ],
        out_specs=(pl.BlockSpec((TM, D_MODEL), lambda i: (jnp.minimum(i, ncb - 1), 0)),
                   pl.BlockSpec((TM, D_MODEL), lambda i: (jnp.maximum(i - ncb, 0), 0))),
        compiler_params=_cparams("arbitrary"),
        name="final_norm",
    )(x, moe, mods, nf)


def _grid_pos_table(seq):
    rows = seq // GRID_W
    r, cl = jnp.meshgrid(jnp.arange(rows, dtype=F32), jnp.arange(GRID_W, dtype=F32), indexing='ij')
    quarter = D_MODEL // 4
    omega = POS_THETA ** (-jnp.arange(quarter, dtype=F32) / quarter)

    def enc(pos):
        a = pos.reshape(-1, 1) * omega[None, :]
        return jnp.concatenate([jnp.sin(a), jnp.cos(a)], axis=-1)

    return jnp.concatenate([enc(r), enc(cl)], axis=-1)


def _router_weights(w_group, b_group, w_router, b_router):
    pad = LG_ROWS - N_EXPERTS - N_GROUPS
    w = jnp.concatenate([w_router.T, w_group.T, jnp.zeros((pad, D_MODEL), F32)], axis=0)
    b = jnp.concatenate([b_router, b_group, jnp.zeros((pad,), F32)])[:, None]
    return w, b


def kernel(x_prompt, x_sample, state_gla, state_mlstm_c, state_mlstm_n, state_mlstm_m, c, c_ctx, norm1_w, norm2_w, norm_f_w, ada_w, ada_b, rec_w_in, gla_gk_w, gla_gk_b, mlstm_gate_b, gla_norm_w, mlstm_norm_w, rec_w_out, hy_w_in, hy_conv_w, hy_f_w1, hy_f_b1, hy_f_freq1, hy_f_w2, hy_f_b2, hy_f_freq2, hy_f_w3, hy_f_bias, hy_w_out, moe_w_group, moe_b_group, moe_w_router, moe_b_router, moe_w_gate, moe_w_up, moe_w_down):
    bp, lp, _ = x_prompt.shape
    bs, ls, _ = x_sample.shape
    st = _Streams(bp, lp, bs, ls)
    xp = x_prompt.reshape(st.tp, D_MODEL)
    xs = x_sample.reshape(st.ts, D_MODEL)

    nrow = -(-(1 + bs) // 8) * 8
    cv = jnp.concatenate([c_ctx[None, :], c, jnp.zeros((nrow - 1 - bs, D_MODEL), F32)], axis=0)
    mods = _ada(cv, ada_w, ada_b)
    mods0 = mods[0].reshape(nrow, 1, N_MOD * D_MODEL)
    mods1 = mods[1].reshape(nrow, 1, N_MOD * D_MODEL)

    pos = _grid_pos_table(ls)

    w_in = rec_w_in[0]
    w_main = _bf(w_in[:, :_REC_MAIN])
    w_gate = jnp.concatenate([w_in[:, _REC_MAIN:], jnp.zeros((D_MODEL, LANES - _REC_GATE), F32)], axis=1)
    w_gate_t = w_in[:, _REC_MAIN + _MG0:].T
    main, g, mt = _inproj(st, xp, xs, pos, mods0, norm1_w[0][None, :], w_main, w_gate, w_gate_t)

    gkw = jnp.zeros((2, LANES, H_A * DK_A), F32)
    gkw = gkw.at[0, :GK_RANK].set(gla_gk_w[0, 0]).at[1, GK_RANK:2 * GK_RANK].set(gla_gk_w[0, 1])
    gkb = gla_gk_b[0][:, None, :]
    gb = mlstm_gate_b[0].reshape(16)
    gbrow = jnp.zeros((1, LANES), F32).at[0, _MG0:_MG0 + 16].set(gb)
    gbcol = gb[:, None]
    wr0, br0 = _router_weights(moe_w_group[0], moe_b_group[0], moe_w_router[0], moe_b_router[0])
    consts = [gkw, gkb, gbrow, gbcol, gla_norm_w[0][None, :], mlstm_norm_w[0][None, :],
              _bf(rec_w_out[0]), norm2_w[0][None, :], wr0, br0]

    def scan_states(sg, sc, sn, sm):
        nb = sg.shape[0]
        return (jnp.swapaxes(sg, -1, -2), sc, sn.reshape(nb, 2 * H_B, DK_B), sm.reshape(nb, 1, 2 * H_B))

    zero = (jnp.zeros((bp, 2, H_A, DK_A, DV_A), F32), jnp.zeros((bp, 2, H_B, DK_B, DV_B), F32),
            jnp.zeros((bp, 2, H_B, DK_B), F32), jnp.zeros((bp, 2, H_B), F32))
    cached = (state_gla[:, 0], state_mlstm_c[:, 0], state_mlstm_n[:, 0], state_mlstm_m[:, 0])
    x1, h2, lg, fg, fc, fn, fm = _scan(st, True, main, g, mt, xp, None, mods0, scan_states(*zero), consts, None)
    x1, h2, lg = _scan(st, False, main, g, mt, xs, pos, mods0, scan_states(*cached), consts, (x1, h2, lg))[:3]

    new_gla = jnp.swapaxes(fg, -1, -2)[:, None]
    new_c = fc[:, None]
    new_n = fn.reshape(bp, 1, 2, H_B, DK_B)
    new_m = fm.reshape(bp, 1, 2, H_B)

    moe0 = _moe(st, h2, _route(st, lg), _bf(moe_w_gate[0]), _bf(moe_w_up[0]), _bf(moe_w_down[0]))

    x2, z = _hy_inproj(st, x1, moe0, mods0, mods1, norm1_w[1][None, :], _bf(hy_w_in[0]))
    y = None
    for ctx in (True, False):
        seq = lp if ctx else ls
        tables = _dft_tables(seq)
        filt = _hy_filter(seq, hy_f_w1[0], hy_f_b1[0], hy_f_freq1[0], hy_f_w2[0], hy_f_b2[0], hy_f_freq2[0],
                          hy_f_w3[0], tables[0], tables[1])
        y = _hy_conv(st, ctx, z, hy_conv_w[0], filt, hy_f_bias[0], tuple(_bf(a) for a in tables), y)
    wr1, br1 = _router_weights(moe_w_group[1], moe_b_group[1], moe_w_router[1], moe_b_router[1])
    x3, h4, lg1 = _hy_outproj(st, y, x2, mods1, _bf(hy_w_out[0]), norm2_w[1][None, :], wr1, br1)

    moe1 = _moe(st, h4, _route(st, lg1), _bf(moe_w_gate[1]), _bf(moe_w_up[1]), _bf(moe_w_down[1]))
    y_prompt, y_sample = _final(st, x3, moe1, mods1, norm_f_w[None, :])
    return (y_prompt.reshape(bp, lp, D_MODEL), y_sample.reshape(bs, ls, D_MODEL), new_gla, new_c, new_n, new_m)
```

```python
import functools
import math

import jax
import jax.numpy as jnp
from jax import lax
from jax.experimental import pallas as pl
from jax.experimental.pallas import tpu as pltpu

F32 = jnp.float32
BF16 = jnp.bfloat16

D_MODEL = 1024
GRID_W = 64
H_A = 4
DK_A = D_MODEL // 16
DV_A = D_MODEL // 8
GK_RANK = 16
GATE_TEMP = 16.0
H_B = 4
DK_B = D_MODEL // 16
DV_B = D_MODEL // 8
CHUNK = 64
HY_ORDER = 2
HY_EMB = 33
HY_BANDS = (HY_EMB - 1) // 2
HY_FFN = 64
HY_TARGET = 1e-2
HY_MAX_DECAY = abs(math.log(HY_TARGET)) / 0.3
HY_MIN_DECAY = abs(math.log(HY_TARGET)) / 1.5
N_GROUPS = 4
EXP_PER_GROUP = 4
N_EXPERTS = N_GROUPS * EXP_PER_GROUP
D_EXPERT = D_MODEL // 2
N_MOD = 6
POS_THETA = 10000.0
EPS = 1e-6

_QA, _KA, _VA, _RA = 0, 256, 512, 1024
_QB, _KB, _VB, _OB = 1536, 1792, 2048, 2560
_REC_MAIN = 3072
_REC_GATE = 48
_MG0 = 2 * GK_RANK

LANES = 128
SUBLANES = 8
_RPV = D_MODEL // LANES
assert _RPV == SUBLANES
TM = 256
SB_MOE = 2048
ROW_PAD = 16
FFN_TILE = 128
LG_ROWS = 32
VMEM_LIMIT = 56 * 1024 * 1024


def _cparams(*sem):
    return pltpu.CompilerParams(dimension_semantics=sem, vmem_limit_bytes=VMEM_LIMIT)


def _bf(x):
    return x.astype(BF16)


def _dot(a, b):
    return jnp.dot(a, b, preferred_element_type=F32)


def _dot_nt(a, b):
    return lax.dot_general(a, b, (((1,), (1,)), ((), ())), preferred_element_type=F32)


def _dot_tn(a, b):
    return lax.dot_general(a, b, (((0,), (0,)), ((), ())), preferred_element_type=F32)


def _split2(x):
    hi = _bf(x)
    return hi, _bf(x - hi.astype(F32))


def _split3(x):
    hi = _bf(x)
    r = x - hi.astype(F32)
    mid = _bf(r)
    return hi, mid, _bf(r - mid.astype(F32))


def _dot_hp(a, b, dot=_dot):
    ah, al = _split2(a)
    bh, bl = _split2(b)
    return dot(ah, bh) + (dot(ah, bl) + dot(al, bh))


def _dot_mask_l(m, x):
    x1, x2, x3 = _split3(x)
    return _dot(m, x1) + (_dot(m, x2) + _dot(m, x3))


def _dot_mask_r(x, m):
    x1, x2, x3 = _split3(x)
    return _dot(x1, m) + (_dot(x2, m) + _dot(x3, m))


def _rms(x, w):
    return x * lax.rsqrt(jnp.mean(x * x, axis=-1, keepdims=True) + EPS) * w


def _mod_slices(mod):
    return [mod[:, k * D_MODEL:(k + 1) * D_MODEL] for k in range(N_MOD)]


def _store_rows_per_vreg(dst, h):
    rows = h.shape[0]
    for j in range(_RPV):
        dst[pl.ds(j, rows, stride=_RPV), :] = h[:, j * LANES:(j + 1) * LANES]


def _load_rows_per_vreg(src, rows):
    return jnp.concatenate([src[pl.ds(j, rows, stride=_RPV), :] for j in range(_RPV)], axis=-1)


def _router_logits(h, wrt_ref, brt_ref):
    return _dot_hp(wrt_ref[...], h, dot=_dot_nt) + brt_ref[...]


def _ada_kernel(cv_ref, w_ref, b_ref, o_ref):
    a = cv_ref[...]
    a = a * jax.nn.sigmoid(a)
    o_ref[...] = _dot_hp(a, w_ref[...]) + b_ref[...]


def _ada(cv, ada_w, ada_b):
    depth, d, n = ada_w.shape
    rows = cv.shape[0]
    tn = 768
    return pl.pallas_call(
        _ada_kernel,
        out_shape=jax.ShapeDtypeStruct((depth, rows, n), F32),
        grid=(depth, n // tn),
        in_specs=[
            pl.BlockSpec((rows, d), lambda l, j: (0, 0)),
            pl.BlockSpec((None, d, tn), lambda l, j: (l, 0, j)),
            pl.BlockSpec((None, 1, tn), lambda l, j: (l, 0, j)),
        ],
        out_specs=pl.BlockSpec((None, rows, tn), lambda l, j: (l, 0, j)),
        compiler_params=_cparams("arbitrary", "arbitrary"),
        name="ada_mod",
    )(cv, ada_w, ada_b.reshape(depth, 1, n))


class _Streams:
    def __init__(self, bp, lp, bs, ls):
        self.bp, self.lp, self.bs, self.ls = bp, lp, bs, ls
        self.tp, self.ts = bp * lp, bs * ls
        self.t = self.tp + self.ts
        assert lp % TM == 0 and ls % TM == 0 and self.tp % ls == 0
        self.sb = min(SB_MOE, math.gcd(self.tp, self.ts))
        assert self.t % self.sb == 0 and self.sb % TM == 0

    def mod_row(self, i, tm):
        ncb = self.tp // tm
        return jnp.where(i < ncb, 0, 1 + (i - ncb) // (self.ls // tm))


def _inproj_kernel(xp_ref, xs_ref, pos_ref, mod_ref, nw_ref, w_ref, wg_ref, wgt_ref,
                   x0_ref, main_ref, g_ref, g2_ref, mt_ref, *, ncb):
    i = pl.program_id(0)
    x = jnp.where(i < ncb, xp_ref[...], xs_ref[...] + pos_ref[...])
    x0_ref[...] = x
    sh1, sc1 = _mod_slices(mod_ref[...])[:2]
    h = _rms(x, nw_ref[...]) * (1.0 + sc1) + sh1
    main_ref[...] = _bf(_dot(_bf(h), w_ref[...]))
    gates = _dot_hp(h, wg_ref[...])
    g_ref[...] = gates[:, :LANES]
    g2_ref[...] = gates[:, LANES:]
    mt = _dot_hp(wgt_ref[...], h, dot=_dot_nt)
    for c in range(TM // CHUNK):
        piece = mt[:, c * CHUNK:(c + 1) * CHUNK]
        mt_ref[c] = jnp.concatenate([piece, piece], axis=1)


def _inproj(st, xp, xs, pos, mods, nw, w_main, w_gate, w_gate_t):
    ncb = st.tp // TM
    bps = st.ls // TM
    t = st.t
    return pl.pallas_call(
        functools.partial(_inproj_kernel, ncb=ncb),
        out_shape=(jax.ShapeDtypeStruct((t, D_MODEL), F32),
                   jax.ShapeDtypeStruct((t, _REC_MAIN), BF16),
                   jax.ShapeDtypeStruct((t, LANES), F32),
                   jax.ShapeDtypeStruct((t, LANES), F32),
                   jax.ShapeDtypeStruct((t // CHUNK, 16, 2 * CHUNK), F32)),
        grid=(t // TM,),
        in_specs=[
            pl.BlockSpec((TM, D_MODEL), lambda i: (jnp.minimum(i, ncb - 1), 0)),
            pl.BlockSpec((TM, D_MODEL), lambda i: (jnp.maximum(i - ncb, 0), 0)),
            pl.BlockSpec((TM, D_MODEL), lambda i: (jnp.maximum(i - ncb, 0) % bps, 0)),
            pl.BlockSpec((None, 1, N_MOD * D_MODEL), lambda i: (st.mod_row(i, TM), 0, 0)),
            pl.BlockSpec((1, D_MODEL), lambda i: (0, 0)),
            pl.BlockSpec((D_MODEL, _REC_MAIN), lambda i: (0, 0)),
            pl.BlockSpec((D_MODEL, 2 * LANES), lambda i: (0, 0)),
            pl.BlockSpec((16, D_MODEL), lambda i: (0, 0)),
        ],
        out_specs=(pl.BlockSpec((TM, D_MODEL), lambda i: (i, 0)),
                   pl.BlockSpec((TM, _REC_MAIN), lambda i: (i, 0)),
                   pl.BlockSpec((TM, LANES), lambda i: (i, 0)),
                   pl.BlockSpec((TM, LANES), lambda i: (i, 0)),
                   pl.BlockSpec((TM // CHUNK, 16, 2 * CHUNK), lambda i: (i, 0, 0))),
        compiler_params=_cparams("arbitrary"),
        name="rec_inproj",
    )(xp, xs, pos, mods, nw, w_main, w_gate, w_gate_t)


def _block_diag(x):
    left = lax.broadcasted_iota(jnp.int32, (1, x.shape[1]), 1) < x.shape[1] // 2
    zero = jnp.zeros_like(x)
    return jnp.concatenate([jnp.where(left, x, zero), jnp.where(left, zero, x)], axis=0)


def _block_diag_mask(rows, width):
    r = lax.broadcasted_iota(jnp.int32, (rows, width), 0) < rows // 2
    l = lax.broadcasted_iota(jnp.int32, (rows, width), 1) < width // 2
    return r == l


def _running_max(x, reverse):
    n = x.shape[0]
    row = lax.broadcasted_iota(jnp.int32, x.shape, 0)
    sh = 1
    while sh < n:
        if reverse:
            y = jnp.where(row < n - sh, pltpu.roll(x, n - sh, 0), -jnp.inf)
        else:
            y = jnp.where(row >= sh, pltpu.roll(x, sh, 0), -jnp.inf)
        x = jnp.maximum(x, y)
        sh *= 2
    return x


def _scan_kernel(*refs, seq):
    (main_ref, g_ref, g2_ref, mt_ref, x_ref, mod_ref, sg_ref, sc_ref, sn_ref, sm_ref,
     gkw_ref, gkb_ref, gbi_ref, gbf_ref, gbcol_ref, gnw_ref, mnw_ref, wout_ref, n2w_ref, wrt_ref, brt_ref) = refs[:21]
    (x1_ref, h2_ref, lg_ref, og_ref, oc_ref, on_ref, om_ref,
     oa_scr, ob_scr, sbd_scr, cbd_scr, nbd_scr, mgl_scr) = refs[-13:]
    c = CHUNK
    nchunks = seq // c
    npair = H_A // 2
    assert H_A == H_B and DK_A == DK_B == c and DV_A == DV_B == LANES and 2 * DK_A == LANES

    lane = lax.broadcasted_iota(jnp.int32, (1, LANES), 1)
    gate_lane = lambda d, h: _MG0 + d * 8 + h
    used = tuple((lane >= gate_lane(d, 0)) & (lane < gate_lane(d, H_B)) for d in (0, 1))

    for d in (0, 1):
        for p in range(npair):
            k = d * npair + p
            z_s = jnp.zeros((DV_A, DK_A), F32)
            sbd_scr[k] = jnp.concatenate([jnp.concatenate([sg_ref[d, 2 * p], z_s], axis=1),
                                          jnp.concatenate([z_s, sg_ref[d, 2 * p + 1]], axis=1)], axis=0)
            z_c = jnp.zeros((DK_B, DV_B), F32)
            cbd_scr[k] = jnp.concatenate([jnp.concatenate([sc_ref[d, 2 * p], z_c], axis=1),
                                          jnp.concatenate([z_c, sc_ref[d, 2 * p + 1]], axis=1)], axis=0)
            n_rep = [jnp.broadcast_to(sn_ref[d * H_B + 2 * p + q:d * H_B + 2 * p + q + 1, :], (DV_B, DK_B)).T
                     for q in (0, 1)]
            nbd_scr[k] = jnp.concatenate([jnp.concatenate([n_rep[0], z_c], axis=1),
                                          jnp.concatenate([z_c, n_rep[1]], axis=1)], axis=0)
    m_gl = jnp.zeros((1, LANES), F32)
    for d in (0, 1):
        for h in range(H_B):
            r = d * H_B + h
            m_gl = jnp.where(lane == gate_lane(d, h), sm_ref[:, r:r + 1], m_gl)
    mgl_scr[...] = m_gl

    row_p = lax.broadcasted_iota(jnp.int32, (c, LANES), 0)
    s_p = lax.broadcasted_iota(jnp.int32, (c, LANES), 1) % c
    live_p = (s_p <= row_p, s_p >= row_p)
    row = lax.broadcasted_iota(jnp.int32, (c, c), 0)
    col = lax.broadcasted_iota(jnp.int32, (c, c), 1)
    tri = tuple(jnp.where(m, 1.0, 0.0).astype(BF16) for m in (col <= row, col >= row))
    tri_t2 = tuple(jnp.where(m, 1.0, 0.0).astype(BF16) for m in (row_p <= s_p, row_p >= s_p))
    grow = lax.broadcasted_iota(jnp.int32, (16, 1), 0)
    f_row = (grow % 8) >= 4
    scale_q = DK_A ** -0.5
    bd_val = _block_diag_mask(2 * DV_A, 2 * DK_A)
    bd_key = _block_diag_mask(2 * DK_B, 2 * DV_B)
    ones_bd = jnp.where(bd_key, 1.0, 0.0).astype(BF16)
    ones_cv = jnp.ones((c, 2 * DV_B), BF16)

    def replicate(d, width):
        r = lax.broadcasted_iota(jnp.int32, (LANES, H_B * width), 0)
        h = lax.broadcasted_iota(jnp.int32, (LANES, H_B * width), 1) // width
        return jnp.where(r == gate_lane(d, 0) + h, 1.0, 0.0).astype(BF16)

    rep_k = tuple(replicate(d, DK_B) for d in (0, 1))
    rep_v = tuple(replicate(d, DV_B) for d in (0, 1))

    def chunk_step(i, carry):
        rows, g_in, g2_in, mt_in, gla_in, mls_in = [], [], [], [], [], []
        for d in (0, 1):
            ci = i if d == 0 else nchunks - 1 - i
            rows.append(pl.ds(pl.multiple_of(ci * c, c), c))
            g_in.append(g_ref[rows[d], :])
            g2_in.append(g2_ref[rows[d], :])
            mt_in.append(mt_ref[ci])
            gla_in.append(main_ref[rows[d], _QA:_RA])
            mls_in.append(main_ref[rows[d], _QB:_OB])
        s_bd = [sbd_scr[k] for k in range(2 * npair)]
        c_bd = [cbd_scr[k] for k in range(2 * npair)]
        n_bd = [nbd_scr[k] for k in range(2 * npair)]
        m_gl = mgl_scr[...]
        dirs = (0, 1)
        ends = (c - 1, 0)
        pairs = [(d, p) for d in dirs for p in range(npair)]
        ks_of = lambda p: slice(p * 2 * DK_A, (p + 1) * 2 * DK_A)
        vs_of = lambda p: slice(p * 2 * DV_A, (p + 1) * 2 * DV_A)

        glin = [_dot_hp(g_in[d], gkw_ref[d]) + gkb_ref[d] for d in dirs]
        gi = [jnp.where(used[d], g_in[d] + gbi_ref[...], 0.0) for d in dirs]
        lf = [jnp.where(used[d], jax.nn.log_sigmoid(g2_in[d] + gbf_ref[...]), 0.0) for d in dirs]
        mt = [mt_in[d] + gbcol_ref[...] for d in dirs]
        mt = [jnp.where(f_row, jax.nn.log_sigmoid(mt[d]), mt[d]) for d in dirs]
        glog = [jax.nn.log_sigmoid(glin[d]) / GATE_TEMP for d in dirs]
        bc = [_dot_mask_l(tri[d], glog[d]) for d in dirs]
        cum = [_dot_mask_l(tri[d], lf[d]) for d in dirs]
        cum_t = [_dot_mask_r(mt[d][:, 0:c], tri_t2[d]) for d in dirs]
        b_end = [bc[d][ends[d]:ends[d] + 1, :] for d in dirs]
        qa = [gla_in[d][:, _QA:_QA + 256].astype(F32) * scale_q for d in dirs]
        ka = [gla_in[d][:, _KA:_KA + 256].astype(F32) for d in dirs]
        va = [gla_in[d][:, _VA:_VA + 512] for d in dirs]
        qe = [_bf(qa[d] * jnp.exp(bc[d])) for d in dirs]
        ke = [_bf(ka[d] * jnp.exp(-bc[d])) for d in dirs]
        kd = [_bf(ka[d] * jnp.exp(b_end[d] - bc[d])) for d in dirs]
        eb_end = [jnp.exp(b_end[d]) for d in dirs]
        qb = [mls_in[d][:, 0:256] * jnp.asarray(DK_B ** -0.5, BF16) for d in dirs]
        kb = [mls_in[d][:, _KB - _QB:_KB - _QB + 256] for d in dirs]
        vb = [mls_in[d][:, _VB - _QB:_VB - _QB + 512] for d in dirs]
        a_raw = {(d, p): _dot_nt(qe[d][:, ks_of(p)], _block_diag(ke[d][:, ks_of(p)])) for d, p in pairs}
        qk = {(d, p): _dot_nt(qb[d][:, ks_of(p)], _block_diag(kb[d][:, ks_of(p)])) for d, p in pairs}
        s_upd = {(d, p): _dot_tn(va[d][:, vs_of(p)], kd[d][:, ks_of(p)]) for d, p in pairs}
        o_car = {(d, p): _dot_nt(qe[d][:, ks_of(p)], _bf(s_bd[d * npair + p])) for d, p in pairs}
        m_loc = [cum[d] + _running_max(gi[d] - cum[d], reverse=(d == 1)) for d in dirs]
        inter = [cum[d] + m_gl for d in dirs]
        m_t = [jnp.maximum(inter[d], m_loc[d]) for d in dirs]
        b_last = [cum[d][ends[d]:ends[d] + 1, :] for d in dirs]
        dend = [b_last[d] - cum[d] + gi[d] for d in dirs]
        m_new = [jnp.maximum(b_last[d] + m_gl, jnp.max(dend[d], axis=0, keepdims=True)) for d in dirs]
        zero = jnp.zeros((c, LANES), F32)
        per_key = [jnp.concatenate([jnp.where(used[d], cum[d] - m_t[d], zero),
                                    jnp.where(used[d], jnp.exp(inter[d] - m_t[d]), zero),
                                    jnp.where(used[d], jnp.exp(dend[d] - m_new[d]), zero)], axis=0) for d in dirs]
        per_val = [jnp.concatenate([jnp.where(used[d], jnp.exp(-m_t[d]), zero),
                                    jnp.broadcast_to(jnp.where(used[d], jnp.exp(b_last[d] + m_gl - m_new[d]), 0.0),
                                                     (SUBLANES, LANES))], axis=0) for d in dirs]
        per_key = [_dot_mask_r(per_key[d], rep_k[d]) for d in dirs]
        per_val = [_dot_mask_r(per_val[d], rep_v[d]) for d in dirs]
        o_par = {(d, p): _dot(_bf(jnp.where(live_p[d], a_raw[d, p], 0.0)), _block_diag(va[d][:, vs_of(p)]))
                 for d, p in pairs}
        for d, p in pairs:
            k = d * npair + p
            s_bd[k] = s_bd[k] * eb_end[d][:, ks_of(p)] + jnp.where(bd_val, s_upd[d, p], 0.0)
        o_gla = [jnp.concatenate([o_par[d, p] + o_car[d, p] for p in range(npair)], axis=-1) for d in dirs]
        for d in dirs:
            m_gl = jnp.where(used[d], m_new[d], m_gl)

        w, qa2, kw = {}, {}, {}
        for d, p in pairs:
            ks = ks_of(p)
            r_i, r_f = d * 8 + 2 * p, d * 8 + 4 + 2 * p
            sub = [cum_t[d][r_f + q:r_f + q + 1, :] - mt[d][r_i + q:r_i + q + 1, :] for q in (0, 1)]
            sub = jnp.where(lane < DK_B, sub[0], sub[1])
            w[d, p] = jnp.exp(jnp.where(live_p[d], per_key[d][0:c, ks] - sub, -jnp.inf))
            qa2[d, p] = _bf(qb[d][:, ks].astype(F32) * per_key[d][c:2 * c, ks])
            kw[d, p] = _bf(kb[d][:, ks].astype(F32) * per_key[d][2 * c:3 * c, ks])
        carried = {(d, p): _dot(qa2[d, p], jnp.concatenate([_bf(c_bd[d * npair + p]), _bf(n_bd[d * npair + p])], axis=1))
                   for d, p in pairs}
        upd = {(d, p): _dot_tn(kw[d, p], jnp.concatenate([vb[d][:, vs_of(p)], ones_cv], axis=1)) for d, p in pairs}
        intra = {(d, p): _dot(_bf(qk[d, p] * w[d, p]),
                              jnp.concatenate([_block_diag(vb[d][:, vs_of(p)]), ones_bd], axis=1)) for d, p in pairs}
        outs = {}
        for d, p in pairs:
            k = d * npair + p
            vs = vs_of(p)
            num = intra[d, p][:, :2 * DV_B] + carried[d, p][:, :2 * DV_B]
            den = intra[d, p][:, 2 * DV_B:] + carried[d, p][:, 2 * DV_B:]
            outs[d, p] = num / jnp.maximum(jnp.abs(den), per_val[d][0:c, vs])
            decay = per_val[d][c:c + 1, vs]
            dec = jnp.concatenate([jnp.broadcast_to(jnp.concatenate([decay[:, q * DV_B:(q + 1) * DV_B]] * 2, axis=1),
                                                    (DK_B, 2 * DV_B)) for q in (0, 1)], axis=0)
            c_bd[k] = dec * c_bd[k] + jnp.where(bd_key, upd[d, p][:, :2 * DV_B], 0.0)
            n_bd[k] = dec * n_bd[k] + jnp.where(bd_key, upd[d, p][:, 2 * DV_B:], 0.0)
        o_mls = [jnp.concatenate([outs[d, p] for p in range(npair)], axis=-1) for d in dirs]

        for d in (0, 1):
            oa_scr[d, rows[d], :] = o_gla[d]
            ob_scr[d, rows[d], :] = o_mls[d]
        for k in range(2 * npair):
            sbd_scr[k] = s_bd[k]
            cbd_scr[k] = c_bd[k]
            nbd_scr[k] = n_bd[k]
        mgl_scr[...] = m_gl
        return carry

    lax.fori_loop(0, nchunks, chunk_step, 0)

    for d in (0, 1):
        for p in range(npair):
            k = d * npair + p
            s_t, c_f, n_f = sbd_scr[k], cbd_scr[k], nbd_scr[k]
            for q in (0, 1):
                h = 2 * p + q
                og_ref[d, h] = s_t[q * DV_A:(q + 1) * DV_A, q * DK_A:(q + 1) * DK_A]
                oc_ref[d, h] = c_f[q * DK_B:(q + 1) * DK_B, q * DV_B:(q + 1) * DV_B]
                n_t = n_f[q * DK_B:(q + 1) * DK_B, q * DV_B:(q + 1) * DV_B].T
                on_ref[d * H_B + h:d * H_B + h + 1, :] = n_t[0:1, :]
    m_gl = mgl_scr[...]
    om_ref[...] = jnp.concatenate([m_gl[:, gate_lane(d, h):gate_lane(d, h) + 1]
                                   for d in (0, 1) for h in range(H_B)], axis=1)

    mod = _mod_slices(mod_ref[...])
    g1, sh2, sc2 = mod[2], mod[3], mod[4]

    def out_step(j, carry):
        r0 = pl.multiple_of(j * TM, TM)
        rows = pl.ds(r0, TM)
        parts = []
        for h in range(H_A):
            vs = slice(h * DV_A, (h + 1) * DV_A)
            ra = main_ref[rows, _RA + h * DV_A:_RA + (h + 1) * DV_A].astype(F32)
            parts.append(_rms(oa_scr[0, rows, vs] + oa_scr[1, rows, vs], gnw_ref[...]) * (ra * jax.nn.sigmoid(ra)))
        for h in range(H_B):
            vs = slice(h * DV_B, (h + 1) * DV_B)
            ob = main_ref[rows, _OB + h * DV_B:_OB + (h + 1) * DV_B].astype(F32)
            parts.append(_rms(ob_scr[0, rows, vs] + ob_scr[1, rows, vs], mnw_ref[...]) * jax.nn.sigmoid(ob))
        mix = _dot(_bf(jnp.concatenate(parts, axis=-1)), wout_ref[...])
        x1 = x_ref[rows, :] + g1 * mix
        x1_ref[rows, :] = x1
        h2 = _rms(x1, n2w_ref[...]) * (1.0 + sc2) + sh2
        _store_rows_per_vreg(h2_ref.at[pl.ds(pl.multiple_of(r0 * _RPV, TM * _RPV), TM * _RPV)], h2)
        lg_ref[j] = _router_logits(h2, wrt_ref, brt_ref)
        return carry

    lax.fori_loop(0, seq // TM, out_step, 0)


def _scan(st, ctx, main, g, g2, mt, x, mods, states, consts, prev):
    seq = st.lp if ctx else st.ls
    nb = st.bp if ctx else st.bs
    off = 0 if ctx else st.tp // st.ls
    sg, sc, sn, sm = states
    full = lambda shape: pl.BlockSpec(shape, lambda b: (0,) * len(shape))
    in_specs = [
        pl.BlockSpec((seq, _REC_MAIN), lambda b: (off + b, 0)),
        pl.BlockSpec((seq, LANES), lambda b: (off + b, 0)),
        pl.BlockSpec((seq, LANES), lambda b: (off + b, 0)),
        pl.BlockSpec((seq // CHUNK, 16, 2 * CHUNK), lambda b: (off + b, 0, 0)),
        pl.BlockSpec((seq, D_MODEL), lambda b: (off + b, 0)),
    ]
    args = [main, g, g2, mt, x]
    mod_row = (lambda b: (0, 0, 0)) if ctx else (lambda b: (1 + b, 0, 0))
    in_specs += [
        pl.BlockSpec((None, 1, N_MOD * D_MODEL), mod_row),
        pl.BlockSpec((None, 2, H_A, DV_A, DK_A), lambda b: (b, 0, 0, 0, 0)),
        pl.BlockSpec((None, 2, H_B, DK_B, DV_B), lambda b: (b, 0, 0, 0, 0)),
        pl.BlockSpec((None, 2 * H_B, DK_B), lambda b: (b, 0, 0)),
        pl.BlockSpec((None, 1, 2 * H_B), lambda b: (b, 0, 0)),
    ]
    args += [mods, sg, sc, sn, sm]
    for a in consts:
        in_specs.append(full(a.shape))
        args.append(a)
    aliases = {}
    if prev is not None:
        aliases = {len(args) + k: k for k in range(3)}
        in_specs += [pl.BlockSpec(memory_space=pl.ANY)] * 3
        args += list(prev)
    out_shape = (jax.ShapeDtypeStruct((st.t, D_MODEL), F32),
                 jax.ShapeDtypeStruct((st.t * _RPV, LANES), F32),
                 jax.ShapeDtypeStruct((st.t // TM, LG_ROWS, TM), F32),
                 jax.ShapeDtypeStruct(sg.shape, F32), jax.ShapeDtypeStruct(sc.shape, F32),
                 jax.ShapeDtypeStruct(sn.shape, F32), jax.ShapeDtypeStruct(sm.shape, F32))
    out_specs = (pl.BlockSpec((seq, D_MODEL), lambda b: (off + b, 0)),
                 pl.BlockSpec((seq * _RPV, LANES), lambda b: (off + b, 0)),
                 pl.BlockSpec((seq // TM, LG_ROWS, TM), lambda b: (off + b, 0, 0)),
                 pl.BlockSpec((None, 2, H_A, DV_A, DK_A), lambda b: (b, 0, 0, 0, 0)),
                 pl.BlockSpec((None, 2, H_B, DK_B, DV_B), lambda b: (b, 0, 0, 0, 0)),
                 pl.BlockSpec((None, 2 * H_B, DK_B), lambda b: (b, 0, 0)),
                 pl.BlockSpec((None, 1, 2 * H_B), lambda b: (b, 0, 0)))
    return pl.pallas_call(
        functools.partial(_scan_kernel, seq=seq),
        out_shape=out_shape,
        grid=(nb,),
        in_specs=in_specs,
        out_specs=out_specs,
        scratch_shapes=[pltpu.VMEM((2, seq, H_A * DV_A), F32), pltpu.VMEM((2, seq, H_B * DV_B), F32),
                        pltpu.VMEM((H_A, 2 * DV_A, 2 * DK_A), F32), pltpu.VMEM((H_B, 2 * DK_B, 2 * DV_B), F32),
                        pltpu.VMEM((H_B, 2 * DK_B, 2 * DV_B), F32), pltpu.VMEM((1, LANES), F32)],
        input_output_aliases=aliases,
        compiler_params=_cparams("arbitrary"),
        name="rec_scan_ctx" if ctx else "rec_scan_lat",
    )(*args)


def _first_max(rows):
    m = rows[0]
    for r in rows[1:]:
        m = jnp.maximum(m, r)
    idx = jnp.full(m.shape, len(rows) - 1, jnp.int32)
    for k in range(len(rows) - 2, -1, -1):
        idx = jnp.where(rows[k] == m, k, idx)
    return m, idx


def _route_kernel(lg_ref, pos1_ref, pos2_ref, w1_ref, w2_ref, tab_ref, *, sb):
    lg = jnp.concatenate([lg_ref[b] for b in range(sb // TM)], axis=1)
    rows = [lg[k:k + 1, :] for k in range(N_EXPERTS + N_GROUPS)]
    grp = rows[N_EXPERTS:]
    gmax, gidx = _first_max(grp)
    p_group = 1.0 / sum(jnp.exp(r - gmax) for r in grp)
    e_in = []
    for k in range(EXP_PER_GROUP):
        v = rows[(N_GROUPS - 1) * EXP_PER_GROUP + k]
        for g in range(N_GROUPS - 2, -1, -1):
            v = jnp.where(gidx == g, rows[g * EXP_PER_GROUP + k], v)
        e_in.append(v)
    v1, i1 = _first_max(e_in)
    v2, i2 = _first_max([jnp.where(i1 == k, -jnp.inf, e_in[k]) for k in range(EXP_PER_GROUP)])
    ex = jnp.exp(v2 - v1)
    w1_ref[...] = p_group / (1.0 + ex)
    w2_ref[...] = p_group * ex / (1.0 + ex)
    x1 = gidx * EXP_PER_GROUP + i1
    x2 = gidx * EXP_PER_GROUP + i2

    eid = lax.broadcasted_iota(jnp.int32, (N_EXPERTS, sb), 0)
    sel = jnp.where((eid == x1) | (eid == x2), 1.0, 0.0)
    r_i = lax.broadcasted_iota(jnp.int32, (TM, TM), 0)
    c_i = lax.broadcasted_iota(jnp.int32, (TM, TM), 1)
    before = jnp.where(r_i < c_i, 1.0, 0.0).astype(BF16)
    carry = jnp.zeros((N_EXPERTS, 1), F32)
    ranks = []
    for b in range(sb // TM):
        s_b = sel[:, b * TM:(b + 1) * TM]
        ranks.append(_dot(_bf(s_b), before) + carry)
        carry = carry + jnp.sum(s_b, axis=1, keepdims=True)
    rank = jnp.concatenate(ranks, axis=1)
    shift = ROW_PAD.bit_length() - 1
    npad = jnp.left_shift(jnp.right_shift(carry.astype(jnp.int32) + (ROW_PAD - 1), shift), shift)

    lane = lax.broadcasted_iota(jnp.int32, (1, LANES), 1)
    tab = jnp.zeros((1, LANES), jnp.int32)
    pos1 = jnp.zeros((1, sb), F32)
    pos2 = jnp.zeros((1, sb), F32)
    off = jnp.zeros((1, 1), jnp.int32)
    for e in range(N_EXPERTS):
        n_e = npad[e:e + 1, :]
        tab = jnp.where(lane == e, off, tab)
        tab = jnp.where(lane == N_EXPERTS + e, n_e, tab)
        row = off.astype(F32) + rank[e:e + 1, :]
        pos1 = jnp.where(x1 == e, row, pos1)
        pos2 = jnp.where(x2 == e, row, pos2)
        off = off + n_e
    pos1_ref[...] = pos1.astype(jnp.int32)
    pos2_ref[...] = pos2.astype(jnp.int32)
    tab_ref[...] = tab


def _route(st, lg):
    sb = st.sb
    nsb = st.t // sb
    row_i = jax.ShapeDtypeStruct((nsb, 1, sb), jnp.int32)
    row_f = jax.ShapeDtypeStruct((nsb, 1, sb), F32)
    rspec = pl.BlockSpec((None, 1, sb), lambda s: (s, 0, 0))
    return pl.pallas_call(
        functools.partial(_route_kernel, sb=sb),
        out_shape=(row_i, row_i, row_f, row_f, jax.ShapeDtypeStruct((nsb, 1, LANES), jnp.int32)),
        grid=(nsb,),
        in_specs=[pl.BlockSpec((sb // TM, LG_ROWS, TM), lambda s: (s, 0, 0))],
        out_specs=(rspec, rspec, rspec, rspec, pl.BlockSpec((None, 1, LANES), lambda s: (s, 0, 0))),
        compiler_params=_cparams("arbitrary"),
        name="moe_route",
    )(lg)


def _moe_rows(sb):
    return 2 * sb + N_EXPERTS * ROW_PAD + FFN_TILE


def _moe_kernel(pos1_ref, pos2_ref, w1_ref, w2_ref, tab_ref, xr_ref, wg_ref, wu_ref, wd_ref, o_ref,
                rows_scr, stage_scr, *, sb):
    s = pl.program_id(0)
    e = pl.program_id(1)

    def tile_at(ref, r):
        return ref.at[pl.ds(pl.multiple_of(r * _RPV, _RPV), _RPV)]

    @pl.when((s == 0) & (e == 0))
    def _():
        rows_scr[...] = jnp.zeros_like(rows_scr)

    @pl.when(e == 0)
    def _():
        def dispatch(t, carry):
            v = tile_at(xr_ref, t)[...]
            tile_at(rows_scr, pos1_ref[0, t])[...] = v
            tile_at(rows_scr, pos2_ref[0, t])[...] = v
            return carry

        lax.fori_loop(0, sb, dispatch, 0, unroll=8)

    def ffn_tile(r0, valid=None):
        win = rows_scr.at[pl.ds(pl.multiple_of(r0 * _RPV, ROW_PAD * _RPV), FFN_TILE * _RPV)]
        x = _load_rows_per_vreg(win, FFN_TILE)
        xb = _bf(x)
        hg = _dot(xb, wg_ref[...])
        hu = _dot(xb, wu_ref[...])
        y = _dot(_bf(hg * jax.nn.sigmoid(hg) * hu), wd_ref[...])
        if valid is not None:
            y = jnp.where(lax.broadcasted_iota(jnp.int32, (FFN_TILE, 1), 0) < valid, y, x)
        _store_rows_per_vreg(win, y)

    off = tab_ref[0, e]
    npad = tab_ref[0, N_EXPERTS + e]
    nfull = npad // FFN_TILE

    def full_tile(i, carry):
        ffn_tile(off + i * FFN_TILE)
        return carry

    lax.fori_loop(0, nfull, full_tile, 0)
    rem = npad - nfull * FFN_TILE

    @pl.when(rem > 0)
    def _():
        ffn_tile(off + nfull * FFN_TILE, valid=rem)

    @pl.when(e == N_EXPERTS - 1)
    def _():
        for c in range(sb // TM):
            def combine(t, carry, c=c):
                tt = c * TM + t
                y = (w1_ref[0, tt] * tile_at(rows_scr, pos1_ref[0, tt])[...]
                     + w2_ref[0, tt] * tile_at(rows_scr, pos2_ref[0, tt])[...])
                tile_at(stage_scr, t)[...] = y
                return carry

            lax.fori_loop(0, TM, combine, 0, unroll=8)
            o_ref[c * TM:(c + 1) * TM, :] = _bf(_load_rows_per_vreg(stage_scr, TM))


def _moe(st, xr, route, wg, wu, wd):
    sb = st.sb
    smem = lambda n: pl.BlockSpec((None, 1, n), lambda s, e: (s, 0, 0), memory_space=pltpu.SMEM)
    return pl.pallas_call(
        functools.partial(_moe_kernel, sb=sb),
        out_shape=jax.ShapeDtypeStruct((st.t, D_MODEL), BF16),
        grid=(st.t // sb, N_EXPERTS),
        in_specs=[
            smem(sb), smem(sb), smem(sb), smem(sb), smem(LANES),
            pl.BlockSpec((sb * _RPV, LANES), lambda s, e: (s, 0)),
            pl.BlockSpec((None, D_MODEL, D_EXPERT), lambda s, e: (e, 0, 0)),
            pl.BlockSpec((None, D_MODEL, D_EXPERT), lambda s, e: (e, 0, 0)),
            pl.BlockSpec((None, D_EXPERT, D_MODEL), lambda s, e: (e, 0, 0)),
        ],
        out_specs=pl.BlockSpec((sb, D_MODEL), lambda s, e: (s, 0)),
        scratch_shapes=[pltpu.VMEM((_moe_rows(sb) * _RPV, LANES), F32), pltpu.VMEM((TM * _RPV, LANES), F32)],
        compiler_params=_cparams("arbitrary", "arbitrary"),
        name="moe_ffn",
    )(*route, xr, wg, wu, wd)


def _hy_inproj_kernel(x_ref, m_ref, mod0_ref, mod_ref, nw_ref, w_ref, x2_ref, z_ref):
    g2 = _mod_slices(mod0_ref[...])[5]
    x2 = x_ref[...] + g2 * m_ref[...].astype(F32)
    x2_ref[...] = x2
    sh1, sc1 = _mod_slices(mod_ref[...])[:2]
    h = _rms(x2, nw_ref[...]) * (1.0 + sc1) + sh1
    z_ref[...] = _bf(_dot(_bf(h), w_ref[...]))


def _hy_inproj(st, x, moe, mods_prev, mods, nw, w):
    n = w.shape[1]
    mspec = pl.BlockSpec((None, 1, N_MOD * D_MODEL), lambda i: (st.mod_row(i, TM), 0, 0))
    return pl.pallas_call(
        _hy_inproj_kernel,
        out_shape=(jax.ShapeDtypeStruct((st.t, D_MODEL), F32), jax.ShapeDtypeStruct((st.t, n), BF16)),
        grid=(st.t // TM,),
        in_specs=[
            pl.BlockSpec((TM, D_MODEL), lambda i: (i, 0)),
            pl.BlockSpec((TM, D_MODEL), lambda i: (i, 0)),
            mspec, mspec,
            pl.BlockSpec((1, D_MODEL), lambda i: (0, 0)),
            pl.BlockSpec((D_MODEL, n), lambda i: (0, 0)),
        ],
        out_specs=(pl.BlockSpec((TM, D_MODEL), lambda i: (i, 0)), pl.BlockSpec((TM, n), lambda i: (i, 0))),
        compiler_params=_cparams("arbitrary"),
        name="hy_inproj",
    )(x, moe, mods_prev, mods, nw, w)


def _dft_tables(seq):
    n2 = 2 * seq
    k = jnp.arange(seq, dtype=jnp.int32)
    kn = (k[:, None] * k[None, :]) % n2
    ang = kn.astype(F32) * (2.0 * math.pi / n2)
    cos, sin = jnp.cos(ang), jnp.sin(ang)
    alt = jnp.where(k % 2 == 0, 1.0, -1.0).astype(F32)
    f_re = cos
    f_im = jnp.where(k[:, None] == 0, alt[None, :], -sin)
    wk = jnp.where(k == 0, 1.0, 2.0).astype(F32) / n2
    b_re = cos * wk[None, :]
    b_im = jnp.where(k[None, :] == 0, alt[:, None] / n2, -sin * wk[None, :])
    return f_re, f_im, b_re, b_im


def _hy_filter_kernel(emb_ref, dec_ref, w1_ref, b1_ref, f1_ref, w2_ref, b2_ref, f2_ref, w3_ref,
                      fre_ref, fim_ref, kr_ref, ki_ref, kr2_ref, *, seq):
    h = jnp.sin(f1_ref[...] * (_dot_hp(emb_ref[...], w1_ref[...]) + b1_ref[...]))
    h = jnp.sin(f2_ref[...] * (_dot_hp(h, w2_ref[...]) + b2_ref[...]))
    dec = dec_ref[...]
    row0 = lax.broadcasted_iota(jnp.int32, (seq, 1), 0) == 0
    f_re, f_im = fre_ref[...], fim_ref[...]
    for o in range(HY_ORDER):
        h_f = _dot_hp(h, w3_ref[:, 2 * o, :]) * dec
        h_b = jnp.where(row0, 0.0, _dot_hp(h, w3_ref[:, 2 * o + 1, :]) * dec)
        k_re = _dot_hp(f_re, h_f + h_b)
        p = _dot_hp(f_im, h_f)
        q = _dot_hp(f_im, h_b)
        kr_ref[o] = k_re
        ki_ref[o] = jnp.where(row0, 0.0, p - q)
        kr2_ref[o] = jnp.where(row0, p + q, k_re)


def _hy_filter(seq, w1, b1, f1, w2, b2, f2, w3, f_re, f_im):
    t = jnp.linspace(0.0, 1.0, seq, dtype=F32)[:, None]
    w = 2.0 * math.pi * jnp.arange(seq, dtype=F32)[:, None] / seq
    f = jnp.linspace(1e-4, HY_BANDS - 1, HY_BANDS, dtype=F32)[None, :]
    emb = jnp.concatenate([t, jnp.cos(f * w), -jnp.sin(f * w), jnp.zeros((seq, LANES - HY_EMB), F32)], axis=-1)
    decay = jnp.exp(-t * jnp.linspace(HY_MIN_DECAY, HY_MAX_DECAY, D_MODEL, dtype=F32)[None, :])
    w1p = jnp.concatenate([w1, jnp.zeros((LANES - HY_EMB, HY_FFN), F32)], axis=0)
    dblk = 256
    out = jax.ShapeDtypeStruct((HY_ORDER, seq, D_MODEL), F32)
    full = lambda shape: pl.BlockSpec(shape, lambda j: (0,) * len(shape))
    ospec = pl.BlockSpec((HY_ORDER, seq, dblk), lambda j: (0, 0, j))
    return pl.pallas_call(
        functools.partial(_hy_filter_kernel, seq=seq),
        out_shape=(out, out, out),
        grid=(D_MODEL // dblk,),
        in_specs=[
            full((seq, LANES)),
            pl.BlockSpec((seq, dblk), lambda j: (0, j)),
            full((LANES, HY_FFN)), full((1, HY_FFN)), full((1, HY_FFN)),
            full((HY_FFN, HY_FFN)), full((1, HY_FFN)), full((1, HY_FFN)),
            pl.BlockSpec((HY_FFN, 2 * HY_ORDER, dblk), lambda j: (0, 0, j)),
            full((seq, seq)), full((seq, seq)),
        ],
        out_specs=(ospec, ospec, ospec),
        compiler_params=_cparams("arbitrary"),
        name=f"hy_filter_{seq}",
    )(emb, decay, w1p, b1[None, :], f1[None, :], w2, b2[None, :], f2[None, :],
      w3.reshape(HY_FFN, 2 * HY_ORDER, D_MODEL), f_re, f_im)


def _hy_conv_kernel(*refs, seq, has_prev):
    (zv_ref, z1_ref, z2_ref, cv_ref, c1_ref, c2_ref, kr_ref, ki_ref, kr2_ref, bias_ref,
     fre_ref, fim_ref, bre_ref, bim_ref) = refs[:14]
    y_ref = refs[-1]
    t = lax.broadcasted_iota(jnp.int32, (seq, 1), 0)
    first, last = t == 0, t == seq - 1

    def short_conv(z_ref, c_ref):
        z = z_ref[...].astype(F32)
        prev = jnp.where(first, 0.0, pltpu.roll(z, 1, 0))
        nxt = jnp.where(last, 0.0, pltpu.roll(z, seq - 1, 0))
        return c_ref[0:1, :] * prev + c_ref[1:2, :] * z + c_ref[2:3, :] * nxt

    def long_conv(s, o):
        sb = _bf(s)
        x_re = _dot(fre_ref[...], sb)
        x_im = _dot(fim_ref[...], sb)
        k_re, k_im, k_re2 = kr_ref[o], ki_ref[o], kr2_ref[o]
        y_re = x_re * k_re - x_im * k_im
        y_im = x_re * k_im + x_im * k_re2
        y = _dot(bre_ref[...], _bf(y_re)) + _dot(bim_ref[...], _bf(y_im))
        return y + bias_ref[o] * s

    v = short_conv(zv_ref, cv_ref)
    x1 = short_conv(z1_ref, c1_ref)
    x2 = short_conv(z2_ref, c2_ref)
    y_ref[...] = _bf(x2 * long_conv(x1 * long_conv(v, 0), 1))


def _hy_conv(st, ctx, z, conv_w, filt, bias, tables, prev):
    seq = st.lp if ctx else st.ls
    nb = st.bp if ctx else st.bs
    off = 0 if ctx else st.tp // st.ls
    dblk = 256
    nd = D_MODEL // dblk
    kr, ki, kr2 = filt
    f_re, f_im, b_re, b_im = tables
    full = lambda shape: pl.BlockSpec(shape, lambda j, b: (0,) * len(shape))
    kspec = pl.BlockSpec((HY_ORDER, seq, dblk), lambda j, b: (0, 0, j))
    in_specs = [
        pl.BlockSpec((seq, dblk), lambda j, b: (off + b, j)),
        pl.BlockSpec((seq, dblk), lambda j, b: (off + b, nd + j)),
        pl.BlockSpec((seq, dblk), lambda j, b: (off + b, 2 * nd + j)),
        pl.BlockSpec((3, dblk), lambda j, b: (0, j)),
        pl.BlockSpec((3, dblk), lambda j, b: (0, nd + j)),
        pl.BlockSpec((3, dblk), lambda j, b: (0, 2 * nd + j)),
        kspec, kspec, kspec,
        pl.BlockSpec((HY_ORDER, 1, dblk), lambda j, b: (0, 0, j)),
        full((seq, seq)), full((seq, seq)), full((seq, seq)), full((seq, seq)),
    ]
    args = [z, z, z, conv_w, conv_w, conv_w, kr, ki, kr2, bias.reshape(HY_ORDER, 1, D_MODEL),
            f_re, f_im, b_re, b_im]
    aliases = {}
    if prev is not None:
        aliases = {len(args): 0}
        in_specs.append(pl.BlockSpec(memory_space=pl.ANY))
        args.append(prev)
    return pl.pallas_call(
        functools.partial(_hy_conv_kernel, seq=seq, has_prev=prev is not None),
        out_shape=jax.ShapeDtypeStruct((st.t, D_MODEL), BF16),
        grid=(nd, nb),
        in_specs=in_specs,
        out_specs=pl.BlockSpec((seq, dblk), lambda j, b: (off + b, j)),
        input_output_aliases=aliases,
        compiler_params=_cparams("arbitrary", "arbitrary"),
        name="hy_conv_ctx" if ctx else "hy_conv_lat",
    )(*args)


def _hy_outproj_kernel(y_ref, x_ref, mod_ref, w_ref, n2w_ref, wrt_ref, brt_ref, x3_ref, h2_ref, lg_ref):
    mod = _mod_slices(mod_ref[...])
    g1, sh2, sc2 = mod[2], mod[3], mod[4]
    x3 = x_ref[...] + g1 * _dot(y_ref[...], w_ref[...])
    x3_ref[...] = x3
    h2 = _rms(x3, n2w_ref[...]) * (1.0 + sc2) + sh2
    _store_rows_per_vreg(h2_ref, h2)
    lg_ref[...] = _router_logits(h2, wrt_ref, brt_ref)


def _hy_outproj(st, y, x, mods, w, n2w, wrt, brt):
    return pl.pallas_call(
        _hy_outproj_kernel,
        out_shape=(jax.ShapeDtypeStruct((st.t, D_MODEL), F32),
                   jax.ShapeDtypeStruct((st.t * _RPV, LANES), F32),
                   jax.ShapeDtypeStruct((st.t // TM, LG_ROWS, TM), F32)),
        grid=(st.t // TM,),
        in_specs=[
            pl.BlockSpec((TM, D_MODEL), lambda i: (i, 0)),
            pl.BlockSpec((TM, D_MODEL), lambda i: (i, 0)),
            pl.BlockSpec((None, 1, N_MOD * D_MODEL), lambda i: (st.mod_row(i, TM), 0, 0)),
            pl.BlockSpec((D_MODEL, D_MODEL), lambda i: (0, 0)),
            pl.BlockSpec((1, D_MODEL), lambda i: (0, 0)),
            pl.BlockSpec((LG_ROWS, D_MODEL), lambda i: (0, 0)),
            pl.BlockSpec((LG_ROWS, 1), lambda i: (0, 0)),
        ],
        out_specs=(pl.BlockSpec((TM, D_MODEL), lambda i: (i, 0)),
                   pl.BlockSpec((TM * _RPV, LANES), lambda i: (i, 0)),
                   pl.BlockSpec((None, LG_ROWS, TM), lambda i: (i, 0, 0))),
        compiler_params=_cparams("arbitrary"),
        name="hy_outproj",
    )(y, x, mods, w, n2w, wrt, brt)


def _final_kernel(x_ref, m_ref, mod_ref, nf_ref, yp_ref, ys_ref, *, ncb):
    i = pl.program_id(0)
    g2 = _mod_slices(mod_ref[...])[5]
    y = _rms(x_ref[...] + g2 * m_ref[...].astype(F32), nf_ref[...])

    @pl.when(i < ncb)
    def _():
        yp_ref[...] = y

    @pl.when(i >= ncb)
    def _():
        ys_ref[...] = y


def _final(st, x, moe, mods, nf):
    ncb = st.tp // TM
    return pl.pallas_call(
        functools.partial(_final_kernel, ncb=ncb),
        out_shape=(jax.ShapeDtypeStruct((st.tp, D_MODEL), F32), jax.ShapeDtypeStruct((st.ts, D_MODEL), F32)),
        grid=(st.t // TM,),
        in_specs=[
            pl.BlockSpec((TM, D_MODEL), lambda i: (i, 0)),
            pl.BlockSpec((TM, D_MODEL), lambda i: (i, 0)),
            pl.BlockSpec((None, 1, N_MOD * D_MODEL), lambda i: (st.mod_row(i, TM), 0, 0)),
            pl.BlockSpec((1, D_MODEL), lambda i: (0, 0)),
        ],
        out_specs=(pl.BlockSpec((TM, D_MODEL), lambda i: (jnp.minimum(i, ncb - 1), 0)),
                   pl.BlockSpec((TM, D_MODEL), lambda i: (jnp.maximum(i - ncb, 0), 0))),
        compiler_params=_cparams("arbitrary"),
        name="final_norm",
    )(x, moe, mods, nf)


def _grid_pos_table(seq):
    rows = seq // GRID_W
    r, cl = jnp.meshgrid(jnp.arange(rows, dtype=F32), jnp.arange(GRID_W, dtype=F32), indexing='ij')
    quarter = D_MODEL // 4
    omega = POS_THETA ** (-jnp.arange(quarter, dtype=F32) / quarter)

    def enc(pos):
        a = pos.reshape(-1, 1) * omega[None, :]
        return jnp.concatenate([jnp.sin(a), jnp.cos(a)], axis=-1)

    return jnp.concatenate([enc(r), enc(cl)], axis=-1)


def _router_weights(w_group, b_group, w_router, b_router):
    pad = LG_ROWS - N_EXPERTS - N_GROUPS
    w = jnp.concatenate([w_router.T, w_group.T, jnp.zeros((pad, D_MODEL), F32)], axis=0)
    b = jnp.concatenate([b_router, b_group, jnp.zeros((pad,), F32)])[:, None]
    return w, b


def kernel(x_prompt, x_sample, state_gla, state_mlstm_c, state_mlstm_n, state_mlstm_m, c, c_ctx, norm1_w, norm2_w, norm_f_w, ada_w, ada_b, rec_w_in, gla_gk_w, gla_gk_b, mlstm_gate_b, gla_norm_w, mlstm_norm_w, rec_w_out, hy_w_in, hy_conv_w, hy_f_w1, hy_f_b1, hy_f_freq1, hy_f_w2, hy_f_b2, hy_f_freq2, hy_f_w3, hy_f_bias, hy_w_out, moe_w_group, moe_b_group, moe_w_router, moe_b_router, moe_w_gate, moe_w_up, moe_w_down):
    bp, lp, _ = x_prompt.shape
    bs, ls, _ = x_sample.shape
    st = _Streams(bp, lp, bs, ls)
    xp = x_prompt.reshape(st.tp, D_MODEL)
    xs = x_sample.reshape(st.ts, D_MODEL)

    nrow = -(-(1 + bs) // 8) * 8
    cv = jnp.concatenate([c_ctx[None, :], c, jnp.zeros((nrow - 1 - bs, D_MODEL), F32)], axis=0)
    mods = _ada(cv, ada_w, ada_b)
    mods0 = mods[0].reshape(nrow, 1, N_MOD * D_MODEL)
    mods1 = mods[1].reshape(nrow, 1, N_MOD * D_MODEL)

    pos = _grid_pos_table(ls)

    w_in = rec_w_in[0]
    w_main = _bf(w_in[:, :_REC_MAIN])
    w_mg = w_in[:, _REC_MAIN + _MG0:].reshape(D_MODEL, 2, 2, H_B)
    w_fg = jnp.pad(w_mg[:, :, 1, :], ((0, 0), (0, 0), (0, 8 - H_B))).reshape(D_MODEL, 16)
    w_gate = jnp.concatenate([w_in[:, _REC_MAIN:], jnp.zeros((D_MODEL, LANES - _REC_GATE), F32),
                              jnp.zeros((D_MODEL, _MG0), F32), w_fg,
                              jnp.zeros((D_MODEL, LANES - _REC_GATE), F32)], axis=1)
    w_gate_t = w_in[:, _REC_MAIN + _MG0:].T
    x0, main, g, g2, mt = _inproj(st, xp, xs, pos, mods0, norm1_w[0][None, :], w_main, w_gate, w_gate_t)

    gkw = jnp.zeros((2, LANES, H_A * DK_A), F32)
    gkw = gkw.at[0, :GK_RANK].set(gla_gk_w[0, 0]).at[1, GK_RANK:2 * GK_RANK].set(gla_gk_w[0, 1])
    gkb = gla_gk_b[0][:, None, :]
    gb = mlstm_gate_b[0]
    gate_row = lambda b: jnp.pad(jnp.pad(b, ((0, 0), (0, 8 - H_B))).reshape(1, 16), ((0, 0), (_MG0, LANES - _REC_GATE)))
    gbcol = gb.reshape(16, 1)
    wr0, br0 = _router_weights(moe_w_group[0], moe_b_group[0], moe_w_router[0], moe_b_router[0])
    consts = [gkw, gkb, gate_row(gb[:, 0]), gate_row(gb[:, 1]), gbcol, gla_norm_w[0][None, :],
              mlstm_norm_w[0][None, :], _bf(rec_w_out[0]), norm2_w[0][None, :], wr0, br0]

    def scan_states(sg, sc, sn, sm):
        nb = sg.shape[0]
        return (jnp.swapaxes(sg, -1, -2), sc, sn.reshape(nb, 2 * H_B, DK_B), sm.reshape(nb, 1, 2 * H_B))

    zero = (jnp.zeros((bp, 2, H_A, DK_A, DV_A), F32), jnp.zeros((bp, 2, H_B, DK_B, DV_B), F32),
            jnp.zeros((bp, 2, H_B, DK_B), F32), jnp.zeros((bp, 2, H_B), F32))
    cached = (state_gla[:, 0], state_mlstm_c[:, 0], state_mlstm_n[:, 0], state_mlstm_m[:, 0])
    x1, h2, lg, fg, fc, fn, fm = _scan(st, True, main, g, g2, mt, x0, mods0, scan_states(*zero), consts, None)
    x1, h2, lg = _scan(st, False, main, g, g2, mt, x0, mods0, scan_states(*cached), consts, (x1, h2, lg))[:3]

    new_gla = jnp.swapaxes(fg, -1, -2)[:, None]
    new_c = fc[:, None]
    new_n = fn.reshape(bp, 1, 2, H_B, DK_B)
    new_m = fm.reshape(bp, 1, 2, H_B)

    moe0 = _moe(st, h2, _route(st, lg), _bf(moe_w_gate[0]), _bf(moe_w_up[0]), _bf(moe_w_down[0]))

    x2, z = _hy_inproj(st, x1, moe0, mods0, mods1, norm1_w[1][None, :], _bf(hy_w_in[0]))
    y = None
    for ctx in (True, False):
        seq = lp if ctx else ls
        tables = _dft_tables(seq)
        filt = _hy_filter(seq, hy_f_w1[0], hy_f_b1[0], hy_f_freq1[0], hy_f_w2[0], hy_f_b2[0], hy_f_freq2[0],
                          hy_f_w3[0], tables[0], tables[1])
        y = _hy_conv(st, ctx, z, hy_conv_w[0], filt, hy_f_bias[0], tuple(_bf(a) for a in tables), y)
    wr1, br1 = _router_weights(moe_w_group[1], moe_b_group[1], moe_w_router[1], moe_b_router[1])
    x3, h4, lg1 = _hy_outproj(st, y, x2, mods1, _bf(hy_w_out[0]), norm2_w[1][None, :], wr1, br1)

    moe1 = _moe(st, h4, _route(st, lg1), _bf(moe_w_gate[1]), _bf(moe_w_up[1]), _bf(moe_w_down[1]))
    y_prompt, y_sample = _final(st, x3, moe1, mods1, norm_f_w[None, :])
    return (y_prompt.reshape(bp, lp, D_MODEL), y_sample.reshape(bs, ls, D_MODEL), new_gla, new_c, new_n, new_m)
```

```python
import functools
import math

import jax
import jax.numpy as jnp
from jax import lax
from jax.experimental import pallas as pl
from jax.experimental.pallas import tpu as pltpu

F32 = jnp.float32
BF16 = jnp.bfloat16

D_MODEL = 1024
GRID_W = 64
H_A = 4
DK_A = D_MODEL // 16
DV_A = D_MODEL // 8
GK_RANK = 16
GATE_TEMP = 16.0
H_B = 4
DK_B = D_MODEL // 16
DV_B = D_MODEL // 8
CHUNK = 64
HY_ORDER = 2
HY_EMB = 33
HY_BANDS = (HY_EMB - 1) // 2
HY_FFN = 64
HY_TARGET = 1e-2
HY_MAX_DECAY = abs(math.log(HY_TARGET)) / 0.3
HY_MIN_DECAY = abs(math.log(HY_TARGET)) / 1.5
N_GROUPS = 4
EXP_PER_GROUP = 4
N_EXPERTS = N_GROUPS * EXP_PER_GROUP
D_EXPERT = D_MODEL // 2
N_MOD = 6
POS_THETA = 10000.0
EPS = 1e-6

_QA, _KA, _VA, _RA = 0, 256, 512, 1024
_QB, _KB, _VB, _OB = 1536, 1792, 2048, 2560
_REC_MAIN = 3072
_REC_GATE = 48
_MG0 = 2 * GK_RANK

LANES = 128
SUBLANES = 8
_RPV = D_MODEL // LANES
assert _RPV == SUBLANES
TM = 256
TM_TOK = 512
SB_MOE = 2048
ROW_PAD = 16
FFN_TILE = 256
LG_ROWS = 32
VMEM_LIMIT = 56 * 1024 * 1024


def _cparams(*sem):
    return pltpu.CompilerParams(dimension_semantics=sem, vmem_limit_bytes=VMEM_LIMIT)


def _bf(x):
    return x.astype(BF16)


def _dot(a, b):
    return jnp.dot(a, b, preferred_element_type=F32)


def _dot_nt(a, b):
    return lax.dot_general(a, b, (((1,), (1,)), ((), ())), preferred_element_type=F32)


def _dot_tn(a, b):
    return lax.dot_general(a, b, (((0,), (0,)), ((), ())), preferred_element_type=F32)


def _split2(x):
    hi = _bf(x)
    return hi, _bf(x - hi.astype(F32))


def _split3(x):
    hi = _bf(x)
    r = x - hi.astype(F32)
    mid = _bf(r)
    return hi, mid, _bf(r - mid.astype(F32))


def _dot_hp(a, b, dot=_dot):
    ah, al = _split2(a)
    bh, bl = _split2(b)
    return dot(ah, bh) + (dot(ah, bl) + dot(al, bh))


def _dot_mask_l(m, x):
    x1, x2, x3 = _split3(x)
    return _dot(m, x1) + (_dot(m, x2) + _dot(m, x3))


def _dot_mask_r(x, m):
    x1, x2, x3 = _split3(x)
    return _dot(x1, m) + (_dot(x2, m) + _dot(x3, m))


def _rms(x, w):
    return x * lax.rsqrt(jnp.mean(x * x, axis=-1, keepdims=True) + EPS) * w


def _mod_slices(mod):
    return [mod[:, k * D_MODEL:(k + 1) * D_MODEL] for k in range(N_MOD)]


def _store_rows_per_vreg(dst, h):
    rows = h.shape[0]
    for j in range(_RPV):
        dst[pl.ds(j, rows, stride=_RPV), :] = h[:, j * LANES:(j + 1) * LANES]


def _load_rows_per_vreg(src, rows):
    return jnp.concatenate([src[pl.ds(j, rows, stride=_RPV), :] for j in range(_RPV)], axis=-1)


def _router_logits(h, wrt_ref, brt_ref):
    return _dot_hp(wrt_ref[...], h, dot=_dot_nt) + brt_ref[...]


def _ada_kernel(cv_ref, w_ref, b_ref, o_ref):
    a = cv_ref[...]
    a = a * jax.nn.sigmoid(a)
    o_ref[...] = _dot_hp(a, w_ref[...]) + b_ref[...]


def _ada(cv, ada_w, ada_b):
    depth, d, n = ada_w.shape
    rows = cv.shape[0]
    tn = 768
    return pl.pallas_call(
        _ada_kernel,
        out_shape=jax.ShapeDtypeStruct((depth, rows, n), F32),
        grid=(depth, n // tn),
        in_specs=[
            pl.BlockSpec((rows, d), lambda l, j: (0, 0)),
            pl.BlockSpec((None, d, tn), lambda l, j: (l, 0, j)),
            pl.BlockSpec((None, 1, tn), lambda l, j: (l, 0, j)),
        ],
        out_specs=pl.BlockSpec((None, rows, tn), lambda l, j: (l, 0, j)),
        compiler_params=_cparams("arbitrary", "arbitrary"),
        name="ada_mod",
    )(cv, ada_w, ada_b.reshape(depth, 1, n))


class _Streams:
    def __init__(self, bp, lp, bs, ls):
        self.bp, self.lp, self.bs, self.ls = bp, lp, bs, ls
        self.tp, self.ts = bp * lp, bs * ls
        self.t = self.tp + self.ts
        assert lp % TM == 0 and ls % TM == 0 and self.tp % ls == 0
        self.sb = min(SB_MOE, math.gcd(self.tp, self.ts))
        assert self.t % self.sb == 0 and self.sb % TM == 0

    def mod_row(self, i, tm):
        ncb = self.tp // tm
        return jnp.where(i < ncb, 0, 1 + (i - ncb) // (self.ls // tm))


def _inproj_kernel(xp_ref, xs_ref, pos_ref, mod_ref, nw_ref, w_ref, wg_ref, wgt_ref,
                   x0_ref, main_ref, g_ref, g2_ref, mt_ref, *, ncb):
    i = pl.program_id(0)
    x = jnp.where(i < ncb, xp_ref[...], xs_ref[...] + pos_ref[...])
    x0_ref[...] = x
    sh1, sc1 = _mod_slices(mod_ref[...])[:2]
    h = _rms(x, nw_ref[...]) * (1.0 + sc1) + sh1
    main_ref[...] = _bf(_dot(_bf(h), w_ref[...]))
    gates = _dot_hp(h, wg_ref[...])
    g_ref[...] = gates[:, :LANES]
    g2_ref[...] = gates[:, LANES:]
    mt = _dot_hp(wgt_ref[...], h, dot=_dot_nt)
    for c in range(mt.shape[1] // CHUNK):
        piece = mt[:, c * CHUNK:(c + 1) * CHUNK]
        mt_ref[c] = jnp.concatenate([piece, piece], axis=1)


def _inproj(st, xp, xs, pos, mods, nw, w_main, w_gate, w_gate_t):
    tm = TM_TOK
    assert st.tp % tm == 0 and st.ls % tm == 0
    ncb = st.tp // tm
    bps = st.ls // tm
    t = st.t
    return pl.pallas_call(
        functools.partial(_inproj_kernel, ncb=ncb),
        out_shape=(jax.ShapeDtypeStruct((t, D_MODEL), F32),
                   jax.ShapeDtypeStruct((t, _REC_MAIN), BF16),
                   jax.ShapeDtypeStruct((t, LANES), F32),
                   jax.ShapeDtypeStruct((t, LANES), F32),
                   jax.ShapeDtypeStruct((t // CHUNK, 16, 2 * CHUNK), F32)),
        grid=(t // tm,),
        in_specs=[
            pl.BlockSpec((tm, D_MODEL), lambda i: (jnp.minimum(i, ncb - 1), 0)),
            pl.BlockSpec((tm, D_MODEL), lambda i: (jnp.maximum(i - ncb, 0), 0)),
            pl.BlockSpec((tm, D_MODEL), lambda i: (jnp.maximum(i - ncb, 0) % bps, 0)),
            pl.BlockSpec((None, 1, N_MOD * D_MODEL), lambda i: (st.mod_row(i, tm), 0, 0)),
            pl.BlockSpec((1, D_MODEL), lambda i: (0, 0)),
            pl.BlockSpec((D_MODEL, _REC_MAIN), lambda i: (0, 0)),
            pl.BlockSpec((D_MODEL, 2 * LANES), lambda i: (0, 0)),
            pl.BlockSpec((16, D_MODEL), lambda i: (0, 0)),
        ],
        out_specs=(pl.BlockSpec((tm, D_MODEL), lambda i: (i, 0)),
                   pl.BlockSpec((tm, _REC_MAIN), lambda i: (i, 0)),
                   pl.BlockSpec((tm, LANES), lambda i: (i, 0)),
                   pl.BlockSpec((tm, LANES), lambda i: (i, 0)),
                   pl.BlockSpec((tm // CHUNK, 16, 2 * CHUNK), lambda i: (i, 0, 0))),
        compiler_params=_cparams("arbitrary"),
        name="rec_inproj",
    )(xp, xs, pos, mods, nw, w_main, w_gate, w_gate_t)


def _block_diag(x):
    left = lax.broadcasted_iota(jnp.int32, (1, x.shape[1]), 1) < x.shape[1] // 2
    zero = jnp.zeros_like(x)
    return jnp.concatenate([jnp.where(left, x, zero), jnp.where(left, zero, x)], axis=0)


def _block_diag_mask(rows, width):
    r = lax.broadcasted_iota(jnp.int32, (rows, width), 0) < rows // 2
    l = lax.broadcasted_iota(jnp.int32, (rows, width), 1) < width // 2
    return r == l


def _running_max(x, reverse):
    n = x.shape[0]
    row = lax.broadcasted_iota(jnp.int32, x.shape, 0)
    sh = 1
    while sh < n:
        if reverse:
            y = jnp.where(row < n - sh, pltpu.roll(x, n - sh, 0), -jnp.inf)
        else:
            y = jnp.where(row >= sh, pltpu.roll(x, sh, 0), -jnp.inf)
        x = jnp.maximum(x, y)
        sh *= 2
    return x


def _scan_kernel(*refs, seq):
    (main_ref, g_ref, g2_ref, mt_ref, x_ref, mod_ref, sg_ref, sc_ref, sn_ref, sm_ref,
     gkw_ref, gkb_ref, gbi_ref, gbf_ref, gbcol_ref, gnw_ref, mnw_ref, wout_ref, n2w_ref, wrt_ref, brt_ref) = refs[:21]
    (x1_ref, h2_ref, lg_ref, og_ref, oc_ref, on_ref, om_ref,
     oa_scr, ob_scr, sbd_scr, cbd_scr, nbd_scr, mgl_scr) = refs[-13:]
    c = CHUNK
    nchunks = seq // c
    npair = H_A // 2
    assert H_A == H_B and DK_A == DK_B == c and DV_A == DV_B == LANES and 2 * DK_A == LANES

    lane = lax.broadcasted_iota(jnp.int32, (1, LANES), 1)
    gate_lane = lambda d, h: _MG0 + d * 8 + h
    used = tuple((lane >= gate_lane(d, 0)) & (lane < gate_lane(d, H_B)) for d in (0, 1))

    for d in (0, 1):
        for p in range(npair):
            k = d * npair + p
            z_s = jnp.zeros((DV_A, DK_A), F32)
            sbd_scr[k] = jnp.concatenate([jnp.concatenate([sg_ref[d, 2 * p], z_s], axis=1),
                                          jnp.concatenate([z_s, sg_ref[d, 2 * p + 1]], axis=1)], axis=0)
            z_c = jnp.zeros((DK_B, DV_B), F32)
            cbd_scr[k] = jnp.concatenate([jnp.concatenate([sc_ref[d, 2 * p], z_c], axis=1),
                                          jnp.concatenate([z_c, sc_ref[d, 2 * p + 1]], axis=1)], axis=0)
            n_rep = [jnp.broadcast_to(sn_ref[d * H_B + 2 * p + q:d * H_B + 2 * p + q + 1, :], (DV_B, DK_B)).T
                     for q in (0, 1)]
            nbd_scr[k] = jnp.concatenate([jnp.concatenate([n_rep[0], z_c], axis=1),
                                          jnp.concatenate([z_c, n_rep[1]], axis=1)], axis=0)
    m_gl = jnp.zeros((1, LANES), F32)
    for d in (0, 1):
        for h in range(H_B):
            r = d * H_B + h
            m_gl = jnp.where(lane == gate_lane(d, h), sm_ref[:, r:r + 1], m_gl)
    mgl_scr[...] = m_gl

    row_p = lax.broadcasted_iota(jnp.int32, (c, LANES), 0)
    s_p = lax.broadcasted_iota(jnp.int32, (c, LANES), 1) % c
    live_p = (s_p <= row_p, s_p >= row_p)
    row = lax.broadcasted_iota(jnp.int32, (c, c), 0)
    col = lax.broadcasted_iota(jnp.int32, (c, c), 1)
    tri = tuple(jnp.where(m, 1.0, 0.0).astype(BF16) for m in (col <= row, col >= row))
    tri_t2 = tuple(jnp.where(m, 1.0, 0.0).astype(BF16) for m in (row_p <= s_p, row_p >= s_p))
    grow = lax.broadcasted_iota(jnp.int32, (16, 1), 0)
    f_row = (grow % 8) >= 4
    scale_q = DK_A ** -0.5
    bd_val = _block_diag_mask(2 * DV_A, 2 * DK_A)
    bd_key = _block_diag_mask(2 * DK_B, 2 * DV_B)
    ones_bd = jnp.where(bd_key, 1.0, 0.0).astype(BF16)
    ones_cv = jnp.ones((c, 2 * DV_B), BF16)

    def replicate(d, width):
        r = lax.broadcasted_iota(jnp.int32, (LANES, H_B * width), 0)
        h = lax.broadcasted_iota(jnp.int32, (LANES, H_B * width), 1) // width
        return jnp.where(r == gate_lane(d, 0) + h, 1.0, 0.0).astype(BF16)

    rep_k = tuple(replicate(d, DK_B) for d in (0, 1))
    rep_v = tuple(replicate(d, DV_B) for d in (0, 1))

    def chunk_step(i, carry):
        rows, g_in, g2_in, mt_in, gla_in, mls_in = [], [], [], [], [], []
        for d in (0, 1):
            ci = i if d == 0 else nchunks - 1 - i
            rows.append(pl.ds(pl.multiple_of(ci * c, c), c))
            g_in.append(g_ref[rows[d], :])
            g2_in.append(g2_ref[rows[d], :])
            mt_in.append(mt_ref[ci])
            gla_in.append(main_ref[rows[d], _QA:_RA])
            mls_in.append(main_ref[rows[d], _QB:_OB])
        s_bd = [sbd_scr[k] for k in range(2 * npair)]
        c_bd = [cbd_scr[k] for k in range(2 * npair)]
        n_bd = [nbd_scr[k] for k in range(2 * npair)]
        m_gl = mgl_scr[...]
        dirs = (0, 1)
        ends = (c - 1, 0)
        pairs = [(d, p) for d in dirs for p in range(npair)]
        ks_of = lambda p: slice(p * 2 * DK_A, (p + 1) * 2 * DK_A)
        vs_of = lambda p: slice(p * 2 * DV_A, (p + 1) * 2 * DV_A)

        glin = [_dot_hp(g_in[d], gkw_ref[d]) + gkb_ref[d] for d in dirs]
        gi = [jnp.where(used[d], g_in[d] + gbi_ref[...], 0.0) for d in dirs]
        lf = [jnp.where(used[d], jax.nn.log_sigmoid(g2_in[d] + gbf_ref[...]), 0.0) for d in dirs]
        mt = [mt_in[d] + gbcol_ref[...] for d in dirs]
        mt = [jnp.where(f_row, jax.nn.log_sigmoid(mt[d]), mt[d]) for d in dirs]
        glog = [jax.nn.log_sigmoid(glin[d]) / GATE_TEMP for d in dirs]
        bc = [_dot_mask_l(tri[d], glog[d]) for d in dirs]
        cum = [_dot_mask_l(tri[d], lf[d]) for d in dirs]
        cum_t = [_dot_mask_r(mt[d][:, 0:c], tri_t2[d]) for d in dirs]
        b_end = [bc[d][ends[d]:ends[d] + 1, :] for d in dirs]
        qa = [gla_in[d][:, _QA:_QA + 256].astype(F32) * scale_q for d in dirs]
        ka = [gla_in[d][:, _KA:_KA + 256].astype(F32) for d in dirs]
        va = [gla_in[d][:, _VA:_VA + 512] for d in dirs]
        qe = [_bf(qa[d] * jnp.exp(bc[d])) for d in dirs]
        ke = [_bf(ka[d] * jnp.exp(-bc[d])) for d in dirs]
        kd = [_bf(ka[d] * jnp.exp(b_end[d] - bc[d])) for d in dirs]
        eb_end = [jnp.exp(b_end[d]) for d in dirs]
        qb = [mls_in[d][:, 0:256] * jnp.asarray(DK_B ** -0.5, BF16) for d in dirs]
        kb = [mls_in[d][:, _KB - _QB:_KB - _QB + 256] for d in dirs]
        vb = [mls_in[d][:, _VB - _QB:_VB - _QB + 512] for d in dirs]
        a_raw = {(d, p): _dot_nt(qe[d][:, ks_of(p)], _block_diag(ke[d][:, ks_of(p)])) for d, p in pairs}
        qk = {(d, p): _dot_nt(qb[d][:, ks_of(p)], _block_diag(kb[d][:, ks_of(p)])) for d, p in pairs}
        s_upd = {(d, p): _dot_tn(va[d][:, vs_of(p)], kd[d][:, ks_of(p)]) for d, p in pairs}
        o_car = {(d, p): _dot_nt(qe[d][:, ks_of(p)], _bf(s_bd[d * npair + p])) for d, p in pairs}
        m_loc = [cum[d] + _running_max(gi[d] - cum[d], reverse=(d == 1)) for d in dirs]
        inter = [cum[d] + m_gl for d in dirs]
        m_t = [jnp.maximum(inter[d], m_loc[d]) for d in dirs]
        b_last = [cum[d][ends[d]:ends[d] + 1, :] for d in dirs]
        dend = [b_last[d] - cum[d] + gi[d] for d in dirs]
        m_new = [jnp.maximum(b_last[d] + m_gl, jnp.max(dend[d], axis=0, keepdims=True)) for d in dirs]
        zero = jnp.zeros((c, LANES), F32)
        per_key = [jnp.concatenate([jnp.where(used[d], cum[d] - m_t[d], zero),
                                    jnp.where(used[d], jnp.exp(inter[d] - m_t[d]), zero),
                                    jnp.where(used[d], jnp.exp(dend[d] - m_new[d]), zero)], axis=0) for d in dirs]
        per_val = [jnp.concatenate([jnp.where(used[d], jnp.exp(-m_t[d]), zero),
                                    jnp.broadcast_to(jnp.where(used[d], jnp.exp(b_last[d] + m_gl - m_new[d]), 0.0),
                                                     (SUBLANES, LANES))], axis=0) for d in dirs]
        per_key = [_dot_mask_r(per_key[d], rep_k[d]) for d in dirs]
        per_val = [_dot_mask_r(per_val[d], rep_v[d]) for d in dirs]
        o_par = {(d, p): _dot(_bf(jnp.where(live_p[d], a_raw[d, p], 0.0)), _block_diag(va[d][:, vs_of(p)]))
                 for d, p in pairs}
        for d, p in pairs:
            k = d * npair + p
            s_bd[k] = s_bd[k] * eb_end[d][:, ks_of(p)] + jnp.where(bd_val, s_upd[d, p], 0.0)
        o_gla = [jnp.concatenate([o_par[d, p] + o_car[d, p] for p in range(npair)], axis=-1) for d in dirs]
        for d in dirs:
            m_gl = jnp.where(used[d], m_new[d], m_gl)

        w, qa2, kw = {}, {}, {}
        for d, p in pairs:
            ks = ks_of(p)
            r_i, r_f = d * 8 + 2 * p, d * 8 + 4 + 2 * p
            sub = [cum_t[d][r_f + q:r_f + q + 1, :] - mt[d][r_i + q:r_i + q + 1, :] for q in (0, 1)]
            sub = jnp.where(lane < DK_B, sub[0], sub[1])
            w[d, p] = jnp.exp(jnp.where(live_p[d], per_key[d][0:c, ks] - sub, -jnp.inf))
            qa2[d, p] = _bf(qb[d][:, ks].astype(F32) * per_key[d][c:2 * c, ks])
            kw[d, p] = _bf(kb[d][:, ks].astype(F32) * per_key[d][2 * c:3 * c, ks])
        carried = {(d, p): _dot(qa2[d, p], jnp.concatenate([_bf(c_bd[d * npair + p]), _bf(n_bd[d * npair + p])], axis=1))
                   for d, p in pairs}
        upd = {(d, p): _dot_tn(kw[d, p], jnp.concatenate([vb[d][:, vs_of(p)], ones_cv], axis=1)) for d, p in pairs}
        intra = {(d, p): _dot(_bf(qk[d, p] * w[d, p]),
                              jnp.concatenate([_block_diag(vb[d][:, vs_of(p)]), ones_bd], axis=1)) for d, p in pairs}
        outs = {}
        for d, p in pairs:
            k = d * npair + p
            vs = vs_of(p)
            num = intra[d, p][:, :2 * DV_B] + carried[d, p][:, :2 * DV_B]
            den = intra[d, p][:, 2 * DV_B:] + carried[d, p][:, 2 * DV_B:]
            outs[d, p] = num / jnp.maximum(jnp.abs(den), per_val[d][0:c, vs])
            decay = per_val[d][c:c + 1, vs]
            dec = jnp.concatenate([jnp.broadcast_to(jnp.concatenate([decay[:, q * DV_B:(q + 1) * DV_B]] * 2, axis=1),
                                                    (DK_B, 2 * DV_B)) for q in (0, 1)], axis=0)
            c_bd[k] = dec * c_bd[k] + jnp.where(bd_key, upd[d, p][:, :2 * DV_B], 0.0)
            n_bd[k] = dec * n_bd[k] + jnp.where(bd_key, upd[d, p][:, 2 * DV_B:], 0.0)
        o_mls = [jnp.concatenate([outs[d, p] for p in range(npair)], axis=-1) for d in dirs]

        for d in (0, 1):
            oa_scr[d, rows[d], :] = o_gla[d]
            ob_scr[d, rows[d], :] = o_mls[d]
        for k in range(2 * npair):
            sbd_scr[k] = s_bd[k]
            cbd_scr[k] = c_bd[k]
            nbd_scr[k] = n_bd[k]
        mgl_scr[...] = m_gl
        return carry

    lax.fori_loop(0, nchunks, chunk_step, 0)

    for d in (0, 1):
        for p in range(npair):
            k = d * npair + p
            s_t, c_f, n_f = sbd_scr[k], cbd_scr[k], nbd_scr[k]
            for q in (0, 1):
                h = 2 * p + q
                og_ref[d, h] = s_t[q * DV_A:(q + 1) * DV_A, q * DK_A:(q + 1) * DK_A]
                oc_ref[d, h] = c_f[q * DK_B:(q + 1) * DK_B, q * DV_B:(q + 1) * DV_B]
                n_t = n_f[q * DK_B:(q + 1) * DK_B, q * DV_B:(q + 1) * DV_B].T
                on_ref[d * H_B + h:d * H_B + h + 1, :] = n_t[0:1, :]
    m_gl = mgl_scr[...]
    om_ref[...] = jnp.concatenate([m_gl[:, gate_lane(d, h):gate_lane(d, h) + 1]
                                   for d in (0, 1) for h in range(H_B)], axis=1)

    mod = _mod_slices(mod_ref[...])
    g1, sh2, sc2 = mod[2], mod[3], mod[4]

    def out_step(j, carry):
        r0 = pl.multiple_of(j * TM, TM)
        rows = pl.ds(r0, TM)
        parts = []
        for h in range(H_A):
            vs = slice(h * DV_A, (h + 1) * DV_A)
            ra = main_ref[rows, _RA + h * DV_A:_RA + (h + 1) * DV_A].astype(F32)
            parts.append(_rms(oa_scr[0, rows, vs] + oa_scr[1, rows, vs], gnw_ref[...]) * (ra * jax.nn.sigmoid(ra)))
        for h in range(H_B):
            vs = slice(h * DV_B, (h + 1) * DV_B)
            ob = main_ref[rows, _OB + h * DV_B:_OB + (h + 1) * DV_B].astype(F32)
            parts.append(_rms(ob_scr[0, rows, vs] + ob_scr[1, rows, vs], mnw_ref[...]) * jax.nn.sigmoid(ob))
        mix = _dot(_bf(jnp.concatenate(parts, axis=-1)), wout_ref[...])
        x1 = x_ref[rows, :] + g1 * mix
        x1_ref[rows, :] = x1
        h2 = _rms(x1, n2w_ref[...]) * (1.0 + sc2) + sh2
        _store_rows_per_vreg(h2_ref.at[pl.ds(pl.multiple_of(r0 * _RPV, TM * _RPV), TM * _RPV)], h2)
        lg_ref[j] = _router_logits(h2, wrt_ref, brt_ref)
        return carry

    lax.fori_loop(0, seq // TM, out_step, 0)


def _scan(st, ctx, main, g, g2, mt, x, mods, states, consts, prev):
    seq = st.lp if ctx else st.ls
    nb = st.bp if ctx else st.bs
    off = 0 if ctx else st.tp // st.ls
    sg, sc, sn, sm = states
    full = lambda shape: pl.BlockSpec(shape, lambda b: (0,) * len(shape))
    in_specs = [
        pl.BlockSpec((seq, _REC_MAIN), lambda b: (off + b, 0)),
        pl.BlockSpec((seq, LANES), lambda b: (off + b, 0)),
        pl.BlockSpec((seq, LANES), lambda b: (off + b, 0)),
        pl.BlockSpec((seq // CHUNK, 16, 2 * CHUNK), lambda b: (off + b, 0, 0)),
        pl.BlockSpec((seq, D_MODEL), lambda b: (off + b, 0)),
    ]
    args = [main, g, g2, mt, x]
    mod_row = (lambda b: (0, 0, 0)) if ctx else (lambda b: (1 + b, 0, 0))
    in_specs += [
        pl.BlockSpec((None, 1, N_MOD * D_MODEL), mod_row),
        pl.BlockSpec((None, 2, H_A, DV_A, DK_A), lambda b: (b, 0, 0, 0, 0)),
        pl.BlockSpec((None, 2, H_B, DK_B, DV_B), lambda b: (b, 0, 0, 0, 0)),
        pl.BlockSpec((None, 2 * H_B, DK_B), lambda b: (b, 0, 0)),
        pl.BlockSpec((None, 1, 2 * H_B), lambda b: (b, 0, 0)),
    ]
    args += [mods, sg, sc, sn, sm]
    for a in consts:
        in_specs.append(full(a.shape))
        args.append(a)
    aliases = {}
    if prev is not None:
        aliases = {len(args) + k: k for k in range(3)}
        in_specs += [pl.BlockSpec(memory_space=pl.ANY)] * 3
        args += list(prev)
    out_shape = (jax.ShapeDtypeStruct((st.t, D_MODEL), F32),
                 jax.ShapeDtypeStruct((st.t * _RPV, LANES), F32),
                 jax.ShapeDtypeStruct((st.t // TM, LG_ROWS, TM), F32),
                 jax.ShapeDtypeStruct(sg.shape, F32), jax.ShapeDtypeStruct(sc.shape, F32),
                 jax.ShapeDtypeStruct(sn.shape, F32), jax.ShapeDtypeStruct(sm.shape, F32))
    out_specs = (pl.BlockSpec((seq, D_MODEL), lambda b: (off + b, 0)),
                 pl.BlockSpec((seq * _RPV, LANES), lambda b: (off + b, 0)),
                 pl.BlockSpec((seq // TM, LG_ROWS, TM), lambda b: (off + b, 0, 0)),
                 pl.BlockSpec((None, 2, H_A, DV_A, DK_A), lambda b: (b, 0, 0, 0, 0)),
                 pl.BlockSpec((None, 2, H_B, DK_B, DV_B), lambda b: (b, 0, 0, 0, 0)),
                 pl.BlockSpec((None, 2 * H_B, DK_B), lambda b: (b, 0, 0)),
                 pl.BlockSpec((None, 1, 2 * H_B), lambda b: (b, 0, 0)))
    return pl.pallas_call(
        functools.partial(_scan_kernel, seq=seq),
        out_shape=out_shape,
        grid=(nb,),
        in_specs=in_specs,
        out_specs=out_specs,
        scratch_shapes=[pltpu.VMEM((2, seq, H_A * DV_A), F32), pltpu.VMEM((2, seq, H_B * DV_B), F32),
                        pltpu.VMEM((H_A, 2 * DV_A, 2 * DK_A), F32), pltpu.VMEM((H_B, 2 * DK_B, 2 * DV_B), F32),
                        pltpu.VMEM((H_B, 2 * DK_B, 2 * DV_B), F32), pltpu.VMEM((1, LANES), F32)],
        input_output_aliases=aliases,
        compiler_params=_cparams("arbitrary"),
        name="rec_scan_ctx" if ctx else "rec_scan_lat",
    )(*args)


def _first_max(rows):
    m = rows[0]
    for r in rows[1:]:
        m = jnp.maximum(m, r)
    idx = jnp.full(m.shape, len(rows) - 1, jnp.int32)
    for k in range(len(rows) - 2, -1, -1):
        idx = jnp.where(rows[k] == m, k, idx)
    return m, idx


def _route_kernel(lg_ref, pos1_ref, pos2_ref, w1_ref, w2_ref, tab_ref, *, sb):
    lg = jnp.concatenate([lg_ref[b] for b in range(sb // TM)], axis=1)
    rows = [lg[k:k + 1, :] for k in range(N_EXPERTS + N_GROUPS)]
    grp = rows[N_EXPERTS:]
    gmax, gidx = _first_max(grp)
    p_group = 1.0 / sum(jnp.exp(r - gmax) for r in grp)
    e_in = []
    for k in range(EXP_PER_GROUP):
        v = rows[(N_GROUPS - 1) * EXP_PER_GROUP + k]
        for g in range(N_GROUPS - 2, -1, -1):
            v = jnp.where(gidx == g, rows[g * EXP_PER_GROUP + k], v)
        e_in.append(v)
    v1, i1 = _first_max(e_in)
    v2, i2 = _first_max([jnp.where(i1 == k, -jnp.inf, e_in[k]) for k in range(EXP_PER_GROUP)])
    ex = jnp.exp(v2 - v1)
    w1_ref[...] = p_group / (1.0 + ex)
    w2_ref[...] = p_group * ex / (1.0 + ex)
    x1 = gidx * EXP_PER_GROUP + i1
    x2 = gidx * EXP_PER_GROUP + i2

    eid = lax.broadcasted_iota(jnp.int32, (N_EXPERTS, sb), 0)
    sel = jnp.where((eid == x1) | (eid == x2), 1.0, 0.0)
    r_i = lax.broadcasted_iota(jnp.int32, (TM, TM), 0)
    c_i = lax.broadcasted_iota(jnp.int32, (TM, TM), 1)
    before = jnp.where(r_i < c_i, 1.0, 0.0).astype(BF16)
    carry = jnp.zeros((N_EXPERTS, 1), F32)
    ranks = []
    for b in range(sb // TM):
        s_b = sel[:, b * TM:(b + 1) * TM]
        ranks.append(_dot(_bf(s_b), before) + carry)
        carry = carry + jnp.sum(s_b, axis=1, keepdims=True)
    rank = jnp.concatenate(ranks, axis=1)
    shift = ROW_PAD.bit_length() - 1
    npad = jnp.left_shift(jnp.right_shift(carry.astype(jnp.int32) + (ROW_PAD - 1), shift), shift)

    lane = lax.broadcasted_iota(jnp.int32, (1, LANES), 1)
    tab = jnp.zeros((1, LANES), jnp.int32)
    pos1 = jnp.zeros((1, sb), F32)
    pos2 = jnp.zeros((1, sb), F32)
    off = jnp.zeros((1, 1), jnp.int32)
    for e in range(N_EXPERTS):
        n_e = npad[e:e + 1, :]
        tab = jnp.where(lane == e, off, tab)
        tab = jnp.where(lane == N_EXPERTS + e, n_e, tab)
        row = off.astype(F32) + rank[e:e + 1, :]
        pos1 = jnp.where(x1 == e, row, pos1)
        pos2 = jnp.where(x2 == e, row, pos2)
        off = off + n_e
    pos1_ref[...] = pos1.astype(jnp.int32)
    pos2_ref[...] = pos2.astype(jnp.int32)
    tab_ref[...] = tab


def _route(st, lg):
    sb = st.sb
    nsb = st.t // sb
    row_i = jax.ShapeDtypeStruct((nsb, 1, sb), jnp.int32)
    row_f = jax.ShapeDtypeStruct((nsb, 1, sb), F32)
    rspec = pl.BlockSpec((None, 1, sb), lambda s: (s, 0, 0))
    return pl.pallas_call(
        functools.partial(_route_kernel, sb=sb),
        out_shape=(row_i, row_i, row_f, row_f, jax.ShapeDtypeStruct((nsb, 1, LANES), jnp.int32)),
        grid=(nsb,),
        in_specs=[pl.BlockSpec((sb // TM, LG_ROWS, TM), lambda s: (s, 0, 0))],
        out_specs=(rspec, rspec, rspec, rspec, pl.BlockSpec((None, 1, LANES), lambda s: (s, 0, 0))),
        compiler_params=_cparams("arbitrary"),
        name="moe_route",
    )(lg)


def _moe_rows(sb):
    return 2 * sb + N_EXPERTS * ROW_PAD + FFN_TILE


def _moe_kernel(pos1_ref, pos2_ref, w1_ref, w2_ref, tab_ref, xr_ref, wg_ref, wu_ref, wd_ref, o_ref,
                rows_scr, stage_scr, *, sb):
    s = pl.program_id(0)
    e = pl.program_id(1)

    def tile_at(ref, r):
        return ref.at[pl.ds(pl.multiple_of(r * _RPV, _RPV), _RPV)]

    @pl.when((s == 0) & (e == 0))
    def _():
        rows_scr[...] = jnp.zeros_like(rows_scr)

    @pl.when(e == 0)
    def _():
        def dispatch(t, carry):
            v = tile_at(xr_ref, t)[...]
            tile_at(rows_scr, pos1_ref[0, t])[...] = v
            tile_at(rows_scr, pos2_ref[0, t])[...] = v
            return carry

        lax.fori_loop(0, sb, dispatch, 0, unroll=8)

    def ffn_tile(r0, m, valid=None):
        win = rows_scr.at[pl.ds(pl.multiple_of(r0 * _RPV, ROW_PAD * _RPV), m * _RPV)]
        x = _load_rows_per_vreg(win, m)
        xb = _bf(x)
        hg = _dot(xb, wg_ref[...])
        hu = _dot(xb, wu_ref[...])
        y = _dot(_bf(hg * jax.nn.sigmoid(hg) * hu), wd_ref[...])
        if valid is not None:
            y = jnp.where(lax.broadcasted_iota(jnp.int32, (m, 1), 0) < valid, y, x)
        _store_rows_per_vreg(win, y)

    off = tab_ref[0, e]
    npad = tab_ref[0, N_EXPERTS + e]
    nfull = npad // FFN_TILE

    def full_tile(i, carry):
        ffn_tile(off + i * FFN_TILE, FFN_TILE)
        return carry

    lax.fori_loop(0, nfull, full_tile, 0)
    rem = npad - nfull * FFN_TILE
    last = off + nfull * FFN_TILE

    @pl.when((rem > 0) & (rem <= FFN_TILE // 2))
    def _():
        ffn_tile(last, FFN_TILE // 2, valid=rem)

    @pl.when(rem > FFN_TILE // 2)
    def _():
        ffn_tile(last, FFN_TILE, valid=rem)

    @pl.when(e == N_EXPERTS - 1)
    def _():
        for c in range(sb // TM):
            def combine(t, carry, c=c):
                tt = c * TM + t
                y = (w1_ref[0, tt] * tile_at(rows_scr, pos1_ref[0, tt])[...]
                     + w2_ref[0, tt] * tile_at(rows_scr, pos2_ref[0, tt])[...])
                tile_at(stage_scr, t)[...] = y
                return carry

            lax.fori_loop(0, TM, combine, 0, unroll=8)
            o_ref[c * TM:(c + 1) * TM, :] = _bf(_load_rows_per_vreg(stage_scr, TM))


def _cast_kernel(x_ref, o_ref):
    o_ref[...] = _bf(x_ref[...])


def _expert_weights_bf16(w):
    depth, ne, a, b = w.shape
    spec = pl.BlockSpec((None, a, b), lambda i: (i, 0, 0))
    return pl.pallas_call(
        _cast_kernel,
        out_shape=jax.ShapeDtypeStruct((depth * ne, a, b), BF16),
        grid=(depth * ne,),
        in_specs=[spec],
        out_specs=spec,
        compiler_params=_cparams("arbitrary"),
        name="expert_weight_cast",
    )(w.reshape(depth * ne, a, b))


def _moe(st, layer, xr, route, wg, wu, wd):
    sb = st.sb
    e0 = layer * N_EXPERTS
    smem = lambda n: pl.BlockSpec((None, 1, n), lambda s, e: (s, 0, 0), memory_space=pltpu.SMEM)
    return pl.pallas_call(
        functools.partial(_moe_kernel, sb=sb),
        out_shape=jax.ShapeDtypeStruct((st.t, D_MODEL), BF16),
        grid=(st.t // sb, N_EXPERTS),
        in_specs=[
            smem(sb), smem(sb), smem(sb), smem(sb), smem(LANES),
            pl.BlockSpec((sb * _RPV, LANES), lambda s, e: (s, 0)),
            pl.BlockSpec((None, D_MODEL, D_EXPERT), lambda s, e: (e0 + e, 0, 0)),
            pl.BlockSpec((None, D_MODEL, D_EXPERT), lambda s, e: (e0 + e, 0, 0)),
            pl.BlockSpec((None, D_EXPERT, D_MODEL), lambda s, e: (e0 + e, 0, 0)),
        ],
        out_specs=pl.BlockSpec((sb, D_MODEL), lambda s, e: (s, 0)),
        scratch_shapes=[pltpu.VMEM((_moe_rows(sb) * _RPV, LANES), F32), pltpu.VMEM((TM * _RPV, LANES), F32)],
        compiler_params=_cparams("arbitrary", "arbitrary"),
        name="moe_ffn",
    )(*route, xr, wg, wu, wd)


def _hy_inproj_kernel(x_ref, m_ref, mod0_ref, mod_ref, nw_ref, w_ref, x2_ref, z_ref):
    g2 = _mod_slices(mod0_ref[...])[5]
    x2 = x_ref[...] + g2 * m_ref[...].astype(F32)
    x2_ref[...] = x2
    sh1, sc1 = _mod_slices(mod_ref[...])[:2]
    h = _rms(x2, nw_ref[...]) * (1.0 + sc1) + sh1
    z_ref[...] = _bf(_dot(_bf(h), w_ref[...]))


def _hy_inproj(st, x, moe, mods_prev, mods, nw, w):
    n = w.shape[1]
    tm = TM_TOK
    mspec = pl.BlockSpec((None, 1, N_MOD * D_MODEL), lambda i: (st.mod_row(i, tm), 0, 0))
    return pl.pallas_call(
        _hy_inproj_kernel,
        out_shape=(jax.ShapeDtypeStruct((st.t, D_MODEL), F32), jax.ShapeDtypeStruct((st.t, n), BF16)),
        grid=(st.t // tm,),
        in_specs=[
            pl.BlockSpec((tm, D_MODEL), lambda i: (i, 0)),
            pl.BlockSpec((tm, D_MODEL), lambda i: (i, 0)),
            mspec, mspec,
            pl.BlockSpec((1, D_MODEL), lambda i: (0, 0)),
            pl.BlockSpec((D_MODEL, n), lambda i: (0, 0)),
        ],
        out_specs=(pl.BlockSpec((tm, D_MODEL), lambda i: (i, 0)), pl.BlockSpec((tm, n), lambda i: (i, 0))),
        compiler_params=_cparams("arbitrary"),
        name="hy_inproj",
    )(x, moe, mods_prev, mods, nw, w)


def _dft_tables(seq):
    n2 = 2 * seq
    k = jnp.arange(seq, dtype=jnp.int32)
    ang = ((k[:, None] * k[None, :]) % n2).astype(F32) * (2.0 * math.pi / n2)
    return jnp.cos(ang), -jnp.sin(ang)


def _alternating(shape):
    return jnp.where(lax.broadcasted_iota(jnp.int32, shape, 0) % 2 == 0, 1.0, -1.0)


def _hy_filter_kernel(emb_ref, dec_ref, w1_ref, b1_ref, f1_ref, w2_ref, b2_ref, f2_ref, w3_ref,
                      cos_ref, msin_ref, kr_ref, ki_ref, *, seq):
    h = jnp.sin(f1_ref[...] * (_dot_hp(emb_ref[...], w1_ref[...]) + b1_ref[...]))
    h = jnp.sin(f2_ref[...] * (_dot_hp(h, w2_ref[...]) + b2_ref[...]))
    dec = dec_ref[...]
    row0 = lax.broadcasted_iota(jnp.int32, dec.shape, 0) == 0
    alt = _alternating(dec.shape)
    cos, msin = cos_ref[...], msin_ref[...]
    scale = jnp.where(row0, 1.0, 2.0) / (2 * seq)
    for o in range(HY_ORDER):
        h_f = _dot_hp(h, w3_ref[:, 2 * o, :]) * dec
        h_b = jnp.where(row0, 0.0, _dot_hp(h, w3_ref[:, 2 * o + 1, :]) * dec)
        k_nyq = jnp.sum(alt * (h_f + h_b), axis=0, keepdims=True)
        kr_ref[o] = _dot_hp(cos, h_f + h_b) * scale
        ki_ref[o] = jnp.where(row0, k_nyq, _dot_hp(msin, h_f) - _dot_hp(msin, h_b)) * scale


def _hy_filter(seq, w1, b1, f1, w2, b2, f2, w3, cos, msin):
    t = jnp.linspace(0.0, 1.0, seq, dtype=F32)[:, None]
    w = 2.0 * math.pi * jnp.arange(seq, dtype=F32)[:, None] / seq
    f = jnp.linspace(1e-4, HY_BANDS - 1, HY_BANDS, dtype=F32)[None, :]
    emb = jnp.concatenate([t, jnp.cos(f * w), -jnp.sin(f * w), jnp.zeros((seq, LANES - HY_EMB), F32)], axis=-1)
    decay = jnp.exp(-t * jnp.linspace(HY_MIN_DECAY, HY_MAX_DECAY, D_MODEL, dtype=F32)[None, :])
    w1p = jnp.concatenate([w1, jnp.zeros((LANES - HY_EMB, HY_FFN), F32)], axis=0)
    dblk = 256
    out = jax.ShapeDtypeStruct((HY_ORDER, seq, D_MODEL), F32)
    full = lambda shape: pl.BlockSpec(shape, lambda j: (0,) * len(shape))
    ospec = pl.BlockSpec((HY_ORDER, seq, dblk), lambda j: (0, 0, j))
    return pl.pallas_call(
        functools.partial(_hy_filter_kernel, seq=seq),
        out_shape=(out, out),
        grid=(D_MODEL // dblk,),
        in_specs=[
            full((seq, LANES)),
            pl.BlockSpec((seq, dblk), lambda j: (0, j)),
            full((LANES, HY_FFN)), full((1, HY_FFN)), full((1, HY_FFN)),
            full((HY_FFN, HY_FFN)), full((1, HY_FFN)), full((1, HY_FFN)),
            pl.BlockSpec((HY_FFN, 2 * HY_ORDER, dblk), lambda j: (0, 0, j)),
            full((seq, seq)), full((seq, seq)),
        ],
        out_specs=(ospec, ospec),
        compiler_params=_cparams("arbitrary"),
        name=f"hy_filter_{seq}",
    )(emb, decay, w1p, b1[None, :], f1[None, :], w2, b2[None, :], f2[None, :],
      w3.reshape(HY_FFN, 2 * HY_ORDER, D_MODEL), cos, msin)


def _hy_conv_kernel(*refs, seq, nseq):
    (zv_ref, z1_ref, z2_ref, cv_ref, c1_ref, c2_ref, kr_ref, ki_ref, bias_ref, cos_ref, msin_ref) = refs[:11]
    y_ref = refs[-1]
    dblk = y_ref.shape[1]
    t = lax.broadcasted_iota(jnp.int32, (seq, dblk), 0)
    first, last, row0 = t == 0, t == seq - 1, t == 0
    alt = _alternating((seq, dblk))
    seqs = range(nseq)
    rows = [slice(q * seq, (q + 1) * seq) for q in seqs]

    def short_conv(z_ref, c_ref):
        out = []
        for q in seqs:
            z = z_ref[rows[q], :].astype(F32)
            prev = jnp.where(first, 0.0, pltpu.roll(z, 1, 0))
            nxt = jnp.where(last, 0.0, pltpu.roll(z, seq - 1, 0))
            out.append(c_ref[0:1, :] * prev + c_ref[1:2, :] * z + c_ref[2:3, :] * nxt)
        return out

    def long_conv(s, o):
        sb = [_bf(s[q]) for q in seqs]
        x_re = [_dot(cos_ref[...], sb[q]) for q in seqs]
        x_im = [_dot(msin_ref[...], sb[q]) for q in seqs]
        x_im = [jnp.where(row0, jnp.sum(alt * s[q], axis=0, keepdims=True), x_im[q]) for q in seqs]
        k_re, k_im = kr_ref[o], ki_ref[o]
        y_re = [jnp.where(row0, x_re[q] * k_re, x_re[q] * k_re - x_im[q] * k_im) for q in seqs]
        y_im = [jnp.where(row0, x_im[q] * k_im, x_re[q] * k_im + x_im[q] * k_re) for q in seqs]
        y = [_dot(cos_ref[...], _bf(y_re[q])) + _dot(msin_ref[...], _bf(y_im[q])) for q in seqs]
        return [y[q] + alt * y_im[q][0:1, :] + bias_ref[o] * s[q] for q in seqs]

    v = short_conv(zv_ref, cv_ref)
    x1 = short_conv(z1_ref, c1_ref)
    x2 = short_conv(z2_ref, c2_ref)
    c1 = long_conv(v, 0)
    c2 = long_conv([x1[q] * c1[q] for q in seqs], 1)
    for q in seqs:
        y_ref[rows[q], :] = _bf(x2[q] * c2[q])


def _hy_conv(st, ctx, z, conv_w, filt, bias, tables, prev):
    seq = st.lp if ctx else st.ls
    nb = st.bp if ctx else st.bs
    nseq = max(1, min(nb, 2048 // seq))
    assert nb % nseq == 0
    off = 0 if ctx else st.tp // (nseq * seq)
    dblk = 256
    nd = D_MODEL // dblk
    kr, ki = filt
    cos, msin = tables
    full = lambda shape: pl.BlockSpec(shape, lambda j, b: (0,) * len(shape))
    kspec = pl.BlockSpec((HY_ORDER, seq, dblk), lambda j, b: (0, 0, j))
    in_specs = [
        pl.BlockSpec((nseq * seq, dblk), lambda j, b: (off + b, j)),
        pl.BlockSpec((nseq * seq, dblk), lambda j, b: (off + b, nd + j)),
        pl.BlockSpec((nseq * seq, dblk), lambda j, b: (off + b, 2 * nd + j)),
        pl.BlockSpec((3, dblk), lambda j, b: (0, j)),
        pl.BlockSpec((3, dblk), lambda j, b: (0, nd + j)),
        pl.BlockSpec((3, dblk), lambda j, b: (0, 2 * nd + j)),
        kspec, kspec,
        pl.BlockSpec((HY_ORDER, 1, dblk), lambda j, b: (0, 0, j)),
        full((seq, seq)), full((seq, seq)),
    ]
    args = [z, z, z, conv_w, conv_w, conv_w, kr, ki, bias.reshape(HY_ORDER, 1, D_MODEL), cos, msin]
    aliases = {}
    if prev is not None:
        aliases = {len(args): 0}
        in_specs.append(pl.BlockSpec(memory_space=pl.ANY))
        args.append(prev)
    return pl.pallas_call(
        functools.partial(_hy_conv_kernel, seq=seq, nseq=nseq),
        out_shape=jax.ShapeDtypeStruct((st.t, D_MODEL), BF16),
        grid=(nd, nb // nseq),
        in_specs=in_specs,
        out_specs=pl.BlockSpec((nseq * seq, dblk), lambda j, b: (off + b, j)),
        input_output_aliases=aliases,
        compiler_params=_cparams("arbitrary", "arbitrary"),
        name="hy_conv_ctx" if ctx else "hy_conv_lat",
    )(*args)


def _hy_outproj_kernel(y_ref, x_ref, mod_ref, w_ref, n2w_ref, wrt_ref, brt_ref, x3_ref, h2_ref, lg_ref):
    mod = _mod_slices(mod_ref[...])
    g1, sh2, sc2 = mod[2], mod[3], mod[4]
    x3 = x_ref[...] + g1 * _dot(y_ref[...], w_ref[...])
    x3_ref[...] = x3
    h2 = _rms(x3, n2w_ref[...]) * (1.0 + sc2) + sh2
    _store_rows_per_vreg(h2_ref, h2)
    lg = _router_logits(h2, wrt_ref, brt_ref)
    for b in range(lg.shape[1] // TM):
        lg_ref[b] = lg[:, b * TM:(b + 1) * TM]


def _hy_outproj(st, y, x, mods, w, n2w, wrt, brt):
    tm = TM_TOK
    return pl.pallas_call(
        _hy_outproj_kernel,
        out_shape=(jax.ShapeDtypeStruct((st.t, D_MODEL), F32),
                   jax.ShapeDtypeStruct((st.t * _RPV, LANES), F32),
                   jax.ShapeDtypeStruct((st.t // TM, LG_ROWS, TM), F32)),
        grid=(st.t // tm,),
        in_specs=[
            pl.BlockSpec((tm, D_MODEL), lambda i: (i, 0)),
            pl.BlockSpec((tm, D_MODEL), lambda i: (i, 0)),
            pl.BlockSpec((None, 1, N_MOD * D_MODEL), lambda i: (st.mod_row(i, tm), 0, 0)),
            pl.BlockSpec((D_MODEL, D_MODEL), lambda i: (0, 0)),
            pl.BlockSpec((1, D_MODEL), lambda i: (0, 0)),
            pl.BlockSpec((LG_ROWS, D_MODEL), lambda i: (0, 0)),
            pl.BlockSpec((LG_ROWS, 1), lambda i: (0, 0)),
        ],
        out_specs=(pl.BlockSpec((tm, D_MODEL), lambda i: (i, 0)),
                   pl.BlockSpec((tm * _RPV, LANES), lambda i: (i, 0)),
                   pl.BlockSpec((tm // TM, LG_ROWS, TM), lambda i: (i, 0, 0))),
        compiler_params=_cparams("arbitrary"),
        name="hy_outproj",
    )(y, x, mods, w, n2w, wrt, brt)


def _final_kernel(x_ref, m_ref, mod_ref, nf_ref, yp_ref, ys_ref, *, ncb):
    i = pl.program_id(0)
    g2 = _mod_slices(mod_ref[...])[5]
    y = _rms(x_ref[...] + g2 * m_ref[...].astype(F32), nf_ref[...])

    @pl.when(i < ncb)
    def _():
        yp_ref[...] = y

    @pl.when(i >= ncb)
    def _():
        ys_ref[...] = y


def _final(st, x, moe, mods, nf):
    tm = TM_TOK
    ncb = st.tp // tm
    return pl.pallas_call(
        functools.partial(_final_kernel, ncb=ncb),
        out_shape=(jax.ShapeDtypeStruct((st.tp, D_MODEL), F32), jax.ShapeDtypeStruct((st.ts, D_MODEL), F32)),
        grid=(st.t // tm,),
        in_specs=[
            pl.BlockSpec((tm, D_MODEL), lambda i: (i, 0)),
            pl.BlockSpec((tm, D_MODEL), lambda i: (i, 0)),
            pl.BlockSpec((None, 1, N_MOD * D_MODEL), lambda i: (st.mod_row(i, tm), 0, 0)),
            pl.BlockSpec((1, D_MODEL), lambda i: (0, 0)),
        ],
        out_specs=(pl.BlockSpec((tm, D_MODEL), lambda i: (jnp.minimum(i, ncb - 1), 0)),
                   pl.BlockSpec((tm, D_MODEL), lambda i: (jnp.maximum(i - ncb, 0), 0))),
        compiler_params=_cparams("arbitrary"),
        name="final_norm",
    )(x, moe, mods, nf)


def _grid_pos_table(seq):
    rows = seq // GRID_W
    r, cl = jnp.meshgrid(jnp.arange(rows, dtype=F32), jnp.arange(GRID_W, dtype=F32), indexing='ij')
    quarter = D_MODEL // 4
    omega = POS_THETA ** (-jnp.arange(quarter, dtype=F32) / quarter)

    def enc(pos):
        a = pos.reshape(-1, 1) * omega[None, :]
        return jnp.concatenate([jnp.sin(a), jnp.cos(a)], axis=-1)

    return jnp.concatenate([enc(r), enc(cl)], axis=-1)


def _router_weights(w_group, b_group, w_router, b_router):
    pad = LG_ROWS - N_EXPERTS - N_GROUPS
    w = jnp.concatenate([w_router.T, w_group.T, jnp.zeros((pad, D_MODEL), F32)], axis=0)
    b = jnp.concatenate([b_router, b_group, jnp.zeros((pad,), F32)])[:, None]
    return w, b


def kernel(x_prompt, x_sample, state_gla, state_mlstm_c, state_mlstm_n, state_mlstm_m, c, c_ctx, norm1_w, norm2_w, norm_f_w, ada_w, ada_b, rec_w_in, gla_gk_w, gla_gk_b, mlstm_gate_b, gla_norm_w, mlstm_norm_w, rec_w_out, hy_w_in, hy_conv_w, hy_f_w1, hy_f_b1, hy_f_freq1, hy_f_w2, hy_f_b2, hy_f_freq2, hy_f_w3, hy_f_bias, hy_w_out, moe_w_group, moe_b_group, moe_w_router, moe_b_router, moe_w_gate, moe_w_up, moe_w_down):
    bp, lp, _ = x_prompt.shape
    bs, ls, _ = x_sample.shape
    st = _Streams(bp, lp, bs, ls)
    xp = x_prompt.reshape(st.tp, D_MODEL)
    xs = x_sample.reshape(st.ts, D_MODEL)

    nrow = -(-(1 + bs) // 8) * 8
    cv = jnp.concatenate([c_ctx[None, :], c, jnp.zeros((nrow - 1 - bs, D_MODEL), F32)], axis=0)
    mods = _ada(cv, ada_w, ada_b)
    mods0 = mods[0].reshape(nrow, 1, N_MOD * D_MODEL)
    mods1 = mods[1].reshape(nrow, 1, N_MOD * D_MODEL)

    pos = _grid_pos_table(ls)

    w_in = rec_w_in[0]
    w_main = _bf(w_in[:, :_REC_MAIN])
    w_mg = w_in[:, _REC_MAIN + _MG0:].reshape(D_MODEL, 2, 2, H_B)
    w_fg = jnp.pad(w_mg[:, :, 1, :], ((0, 0), (0, 0), (0, 8 - H_B))).reshape(D_MODEL, 16)
    w_gate = jnp.concatenate([w_in[:, _REC_MAIN:], jnp.zeros((D_MODEL, LANES - _REC_GATE), F32),
                              jnp.zeros((D_MODEL, _MG0), F32), w_fg,
                              jnp.zeros((D_MODEL, LANES - _REC_GATE), F32)], axis=1)
    w_gate_t = w_in[:, _REC_MAIN + _MG0:].T
    x0, main, g, g2, mt = _inproj(st, xp, xs, pos, mods0, norm1_w[0][None, :], w_main, w_gate, w_gate_t)

    gkw = jnp.zeros((2, LANES, H_A * DK_A), F32)
    gkw = gkw.at[0, :GK_RANK].set(gla_gk_w[0, 0]).at[1, GK_RANK:2 * GK_RANK].set(gla_gk_w[0, 1])
    gkb = gla_gk_b[0][:, None, :]
    gb = mlstm_gate_b[0]
    gate_row = lambda b: jnp.pad(jnp.pad(b, ((0, 0), (0, 8 - H_B))).reshape(1, 16), ((0, 0), (_MG0, LANES - _REC_GATE)))
    gbcol = gb.reshape(16, 1)
    wr0, br0 = _router_weights(moe_w_group[0], moe_b_group[0], moe_w_router[0], moe_b_router[0])
    consts = [gkw, gkb, gate_row(gb[:, 0]), gate_row(gb[:, 1]), gbcol, gla_norm_w[0][None, :],
              mlstm_norm_w[0][None, :], _bf(rec_w_out[0]), norm2_w[0][None, :], wr0, br0]

    def scan_states(sg, sc, sn, sm):
        nb = sg.shape[0]
        return (jnp.swapaxes(sg, -1, -2), sc, sn.reshape(nb, 2 * H_B, DK_B), sm.reshape(nb, 1, 2 * H_B))

    zero = (jnp.zeros((bp, 2, H_A, DK_A, DV_A), F32), jnp.zeros((bp, 2, H_B, DK_B, DV_B), F32),
            jnp.zeros((bp, 2, H_B, DK_B), F32), jnp.zeros((bp, 2, H_B), F32))
    cached = (state_gla[:, 0], state_mlstm_c[:, 0], state_mlstm_n[:, 0], state_mlstm_m[:, 0])
    x1, h2, lg, fg, fc, fn, fm = _scan(st, True, main, g, g2, mt, x0, mods0, scan_states(*zero), consts, None)
    x1, h2, lg = _scan(st, False, main, g, g2, mt, x0, mods0, scan_states(*cached), consts, (x1, h2, lg))[:3]

    new_gla = jnp.swapaxes(fg, -1, -2)[:, None]
    new_c = fc[:, None]
    new_n = fn.reshape(bp, 1, 2, H_B, DK_B)
    new_m = fm.reshape(bp, 1, 2, H_B)

    wg, wu, wd = (_expert_weights_bf16(w) for w in (moe_w_gate, moe_w_up, moe_w_down))
    moe0 = _moe(st, 0, h2, _route(st, lg), wg, wu, wd)

    x2, z = _hy_inproj(st, x1, moe0, mods0, mods1, norm1_w[1][None, :], _bf(hy_w_in[0]))
    y = None
    for ctx in (True, False):
        seq = lp if ctx else ls
        tables = _dft_tables(seq)
        filt = _hy_filter(seq, hy_f_w1[0], hy_f_b1[0], hy_f_freq1[0], hy_f_w2[0], hy_f_b2[0], hy_f_freq2[0],
                          hy_f_w3[0], *tables)
        y = _hy_conv(st, ctx, z, hy_conv_w[0], filt, hy_f_bias[0], tuple(_bf(a) for a in tables), y)
    wr1, br1 = _router_weights(moe_w_group[1], moe_b_group[1], moe_w_router[1], moe_b_router[1])
    x3, h4, lg1 = _hy_outproj(st, y, x2, mods1, _bf(hy_w_out[0]), norm2_w[1][None, :], wr1, br1)

    moe1 = _moe(st, 1, h4, _route(st, lg1), wg, wu, wd)
    y_prompt, y_sample = _final(st, x3, moe1, mods1, norm_f_w[None, :])
    return (y_prompt.reshape(bp, lp, D_MODEL), y_sample.reshape(bs, ls, D_MODEL), new_gla, new_c, new_n, new_m)
```

```python
import functools
import math

import jax
import jax.numpy as jnp
from jax import lax
from jax.experimental import pallas as pl
from jax.experimental.pallas import tpu as pltpu

F32 = jnp.float32
BF16 = jnp.bfloat16

D_MODEL = 1024
GRID_W = 64
H_A = 4
DK_A = D_MODEL // 16
DV_A = D_MODEL // 8
GK_RANK = 16
GATE_TEMP = 16.0
H_B = 4
DK_B = D_MODEL // 16
DV_B = D_MODEL // 8
CHUNK = 64
HY_ORDER = 2
HY_EMB = 33
HY_BANDS = (HY_EMB - 1) // 2
HY_FFN = 64
HY_TARGET = 1e-2
HY_MAX_DECAY = abs(math.log(HY_TARGET)) / 0.3
HY_MIN_DECAY = abs(math.log(HY_TARGET)) / 1.5
N_GROUPS = 4
EXP_PER_GROUP = 4
N_EXPERTS = N_GROUPS * EXP_PER_GROUP
D_EXPERT = D_MODEL // 2
N_MOD = 6
POS_THETA = 10000.0
EPS = 1e-6

_QA, _KA, _VA, _RA = 0, 256, 512, 1024
_QB, _KB, _VB, _OB = 1536, 1792, 2048, 2560
_REC_MAIN = 3072
_REC_GATE = 48
_MG0 = 2 * GK_RANK

LANES = 128
SUBLANES = 8
_RPV = D_MODEL // LANES
assert _RPV == SUBLANES
TM = 256
TM_TOK = 512
SB_MOE = 2048
ROW_PAD = 16
FFN_TILE = 256
LG_ROWS = 32
VMEM_LIMIT = 56 * 1024 * 1024


def _cparams(*sem):
    return pltpu.CompilerParams(dimension_semantics=sem, vmem_limit_bytes=VMEM_LIMIT)


def _bf(x):
    return x.astype(BF16)


def _dot(a, b):
    return jnp.dot(a, b, preferred_element_type=F32)


def _dot_nt(a, b):
    return lax.dot_general(a, b, (((1,), (1,)), ((), ())), preferred_element_type=F32)


def _dot_tn(a, b):
    return lax.dot_general(a, b, (((0,), (0,)), ((), ())), preferred_element_type=F32)


def _split2(x):
    hi = _bf(x)
    return hi, _bf(x - hi.astype(F32))


def _split3(x):
    hi = _bf(x)
    r = x - hi.astype(F32)
    mid = _bf(r)
    return hi, mid, _bf(r - mid.astype(F32))


def _dot_hp(a, b, dot=_dot):
    ah, al = _split2(a)
    bh, bl = _split2(b)
    return dot(ah, bh) + (dot(ah, bl) + dot(al, bh))


def _dot_mask_l(m, x, terms=3):
    if terms == 2:
        x1, x2 = _split2(x)
        return _dot(m, x1) + _dot(m, x2)
    x1, x2, x3 = _split3(x)
    return _dot(m, x1) + (_dot(m, x2) + _dot(m, x3))


def _dot_mask_r(x, m, terms=3):
    if terms == 2:
        x1, x2 = _split2(x)
        return _dot(x1, m) + _dot(x2, m)
    x1, x2, x3 = _split3(x)
    return _dot(x1, m) + (_dot(x2, m) + _dot(x3, m))


def _rms(x, w):
    return x * lax.rsqrt(jnp.mean(x * x, axis=-1, keepdims=True) + EPS) * w


def _mod_slices(mod):
    return [mod[:, k * D_MODEL:(k + 1) * D_MODEL] for k in range(N_MOD)]


def _store_rows_per_vreg(dst, h):
    rows = h.shape[0]
    for j in range(_RPV):
        dst[pl.ds(j, rows, stride=_RPV), :] = h[:, j * LANES:(j + 1) * LANES]


def _load_rows_per_vreg(src, rows):
    return jnp.concatenate([src[pl.ds(j, rows, stride=_RPV), :] for j in range(_RPV)], axis=-1)


def _router_logits(h, wrt_ref, brt_ref):
    return _dot_hp(wrt_ref[...], h, dot=_dot_nt) + brt_ref[...]


def _ada_kernel(cv_ref, w_ref, b_ref, o_ref):
    a = cv_ref[...]
    a = a * jax.nn.sigmoid(a)
    o_ref[...] = _dot_hp(a, w_ref[...]) + b_ref[...]


def _ada(cv, ada_w, ada_b):
    depth, d, n = ada_w.shape
    rows = cv.shape[0]
    tn = 768
    return pl.pallas_call(
        _ada_kernel,
        out_shape=jax.ShapeDtypeStruct((depth, rows, n), F32),
        grid=(depth, n // tn),
        in_specs=[
            pl.BlockSpec((rows, d), lambda l, j: (0, 0)),
            pl.BlockSpec((None, d, tn), lambda l, j: (l, 0, j)),
            pl.BlockSpec((None, 1, tn), lambda l, j: (l, 0, j)),
        ],
        out_specs=pl.BlockSpec((None, rows, tn), lambda l, j: (l, 0, j)),
        compiler_params=_cparams("arbitrary", "arbitrary"),
        name="ada_mod",
    )(cv, ada_w, ada_b.reshape(depth, 1, n))


class _Streams:
    def __init__(self, bp, lp, bs, ls):
        self.bp, self.lp, self.bs, self.ls = bp, lp, bs, ls
        self.tp, self.ts = bp * lp, bs * ls
        self.t = self.tp + self.ts
        assert lp % TM == 0 and ls % TM == 0 and self.tp % ls == 0
        self.sb = min(SB_MOE, math.gcd(self.tp, self.ts))
        assert self.t % self.sb == 0 and self.sb % TM == 0

    def mod_row(self, i, tm):
        ncb = self.tp // tm
        return jnp.where(i < ncb, 0, 1 + (i - ncb) // (self.ls // tm))


def _inproj_kernel(xp_ref, xs_ref, pos_ref, mod_ref, nw_ref, w_ref, wg_ref, wgt_ref,
                   x0_ref, main_ref, g_ref, g2_ref, mt_ref, *, ncb):
    i = pl.program_id(0)
    x = jnp.where(i < ncb, xp_ref[...], xs_ref[...] + pos_ref[...])
    x0_ref[...] = x
    sh1, sc1 = _mod_slices(mod_ref[...])[:2]
    h = _rms(x, nw_ref[...]) * (1.0 + sc1) + sh1
    main_ref[...] = _bf(_dot(_bf(h), w_ref[...]))
    gates = _dot_hp(h, wg_ref[...])
    g_ref[...] = gates[:, :LANES]
    g2_ref[...] = gates[:, LANES:]
    mt = _dot_hp(wgt_ref[...], h, dot=_dot_nt)
    for c in range(mt.shape[1] // CHUNK):
        piece = mt[:, c * CHUNK:(c + 1) * CHUNK]
        mt_ref[c] = jnp.concatenate([piece, piece], axis=1)


def _inproj(st, xp, xs, pos, mods, nw, w_main, w_gate, w_gate_t):
    tm = TM_TOK
    assert st.tp % tm == 0 and st.ls % tm == 0
    ncb = st.tp // tm
    bps = st.ls // tm
    t = st.t
    return pl.pallas_call(
        functools.partial(_inproj_kernel, ncb=ncb),
        out_shape=(jax.ShapeDtypeStruct((t, D_MODEL), F32),
                   jax.ShapeDtypeStruct((t, _REC_MAIN), BF16),
                   jax.ShapeDtypeStruct((t, LANES), F32),
                   jax.ShapeDtypeStruct((t, LANES), F32),
                   jax.ShapeDtypeStruct((t // CHUNK, 16, 2 * CHUNK), F32)),
        grid=(t // tm,),
        in_specs=[
            pl.BlockSpec((tm, D_MODEL), lambda i: (jnp.minimum(i, ncb - 1), 0)),
            pl.BlockSpec((tm, D_MODEL), lambda i: (jnp.maximum(i - ncb, 0), 0)),
            pl.BlockSpec((tm, D_MODEL), lambda i: (jnp.maximum(i - ncb, 0) % bps, 0)),
            pl.BlockSpec((None, 1, N_MOD * D_MODEL), lambda i: (st.mod_row(i, tm), 0, 0)),
            pl.BlockSpec((1, D_MODEL), lambda i: (0, 0)),
            pl.BlockSpec((D_MODEL, _REC_MAIN), lambda i: (0, 0)),
            pl.BlockSpec((D_MODEL, 2 * LANES), lambda i: (0, 0)),
            pl.BlockSpec((16, D_MODEL), lambda i: (0, 0)),
        ],
        out_specs=(pl.BlockSpec((tm, D_MODEL), lambda i: (i, 0)),
                   pl.BlockSpec((tm, _REC_MAIN), lambda i: (i, 0)),
                   pl.BlockSpec((tm, LANES), lambda i: (i, 0)),
                   pl.BlockSpec((tm, LANES), lambda i: (i, 0)),
                   pl.BlockSpec((tm // CHUNK, 16, 2 * CHUNK), lambda i: (i, 0, 0))),
        compiler_params=_cparams("arbitrary"),
        name="rec_inproj",
    )(xp, xs, pos, mods, nw, w_main, w_gate, w_gate_t)


def _block_diag(x):
    left = lax.broadcasted_iota(jnp.int32, (1, x.shape[1]), 1) < x.shape[1] // 2
    zero = jnp.zeros_like(x)
    return jnp.concatenate([jnp.where(left, x, zero), jnp.where(left, zero, x)], axis=0)


def _block_diag_mask(rows, width):
    r = lax.broadcasted_iota(jnp.int32, (rows, width), 0) < rows // 2
    l = lax.broadcasted_iota(jnp.int32, (rows, width), 1) < width // 2
    return r == l


def _running_max(x, reverse):
    n = x.shape[0]
    row = lax.broadcasted_iota(jnp.int32, x.shape, 0)
    sh = 1
    while sh < n:
        if reverse:
            y = jnp.where(row < n - sh, pltpu.roll(x, n - sh, 0), -jnp.inf)
        else:
            y = jnp.where(row >= sh, pltpu.roll(x, sh, 0), -jnp.inf)
        x = jnp.maximum(x, y)
        sh *= 2
    return x


def _scan_kernel(*refs, seq):
    (main_ref, g_ref, g2_ref, mt_ref, x_ref, mod_ref, sg_ref, sc_ref, sn_ref, sm_ref,
     gkw_ref, gkb_ref, gbi_ref, gbf_ref, gbcol_ref, gnw_ref, mnw_ref, wout_ref, n2w_ref, wrt_ref, brt_ref) = refs[:21]
    (x1_ref, h2_ref, lg_ref, og_ref, oc_ref, on_ref, om_ref,
     oa_scr, ob_scr, sbd_scr, cbd_scr, nbd_scr, mgl_scr) = refs[-13:]
    c = CHUNK
    nchunks = seq // c
    npair = H_A // 2
    assert H_A == H_B and DK_A == DK_B == c and DV_A == DV_B == LANES and 2 * DK_A == LANES

    lane = lax.broadcasted_iota(jnp.int32, (1, LANES), 1)
    gate_lane = lambda d, h: _MG0 + d * 8 + h
    used = tuple((lane >= gate_lane(d, 0)) & (lane < gate_lane(d, H_B)) for d in (0, 1))

    for d in (0, 1):
        for p in range(npair):
            k = d * npair + p
            z_s = jnp.zeros((DV_A, DK_A), F32)
            sbd_scr[k] = jnp.concatenate([jnp.concatenate([sg_ref[d, 2 * p], z_s], axis=1),
                                          jnp.concatenate([z_s, sg_ref[d, 2 * p + 1]], axis=1)], axis=0)
            z_c = jnp.zeros((DK_B, DV_B), F32)
            cbd_scr[k] = jnp.concatenate([jnp.concatenate([sc_ref[d, 2 * p], z_c], axis=1),
                                          jnp.concatenate([z_c, sc_ref[d, 2 * p + 1]], axis=1)], axis=0)
            n_rep = [jnp.broadcast_to(sn_ref[d * H_B + 2 * p + q:d * H_B + 2 * p + q + 1, :], (DV_B, DK_B)).T
                     for q in (0, 1)]
            nbd_scr[k] = jnp.concatenate([jnp.concatenate([n_rep[0], z_c], axis=1),
                                          jnp.concatenate([z_c, n_rep[1]], axis=1)], axis=0)
    m_gl = jnp.zeros((1, LANES), F32)
    for d in (0, 1):
        for h in range(H_B):
            r = d * H_B + h
            m_gl = jnp.where(lane == gate_lane(d, h), sm_ref[:, r:r + 1], m_gl)
    mgl_scr[...] = m_gl

    row_p = lax.broadcasted_iota(jnp.int32, (c, LANES), 0)
    s_p = lax.broadcasted_iota(jnp.int32, (c, LANES), 1) % c
    live_p = (s_p <= row_p, s_p >= row_p)
    row = lax.broadcasted_iota(jnp.int32, (c, c), 0)
    col = lax.broadcasted_iota(jnp.int32, (c, c), 1)
    tri = tuple(jnp.where(m, 1.0, 0.0).astype(BF16) for m in (col <= row, col >= row))
    tri_t2 = tuple(jnp.where(m, 1.0, 0.0).astype(BF16) for m in (row_p <= s_p, row_p >= s_p))
    grow = lax.broadcasted_iota(jnp.int32, (16, 1), 0)
    f_row = (grow % 8) >= 4
    scale_q = DK_A ** -0.5
    bd_val = _block_diag_mask(2 * DV_A, 2 * DK_A)
    bd_key = _block_diag_mask(2 * DK_B, 2 * DV_B)
    ones_bd = jnp.where(bd_key, 1.0, 0.0).astype(BF16)
    ones_cv = jnp.ones((c, 2 * DV_B), BF16)

    def replicate(d, width):
        r = lax.broadcasted_iota(jnp.int32, (LANES, H_B * width), 0)
        h = lax.broadcasted_iota(jnp.int32, (LANES, H_B * width), 1) // width
        return jnp.where(r == gate_lane(d, 0) + h, 1.0, 0.0).astype(BF16)

    rep_k = tuple(replicate(d, DK_B) for d in (0, 1))
    rep_v = tuple(replicate(d, DV_B) for d in (0, 1))

    def chunk_step(i, carry):
        rows, g_in, g2_in, mt_in, gla_in, mls_in = [], [], [], [], [], []
        for d in (0, 1):
            ci = i if d == 0 else nchunks - 1 - i
            rows.append(pl.ds(pl.multiple_of(ci * c, c), c))
            g_in.append(g_ref[rows[d], :])
            g2_in.append(g2_ref[rows[d], :])
            mt_in.append(mt_ref[ci])
            gla_in.append(main_ref[rows[d], _QA:_RA])
            mls_in.append(main_ref[rows[d], _QB:_OB])
        s_bd = [sbd_scr[k] for k in range(2 * npair)]
        c_bd = [cbd_scr[k] for k in range(2 * npair)]
        n_bd = [nbd_scr[k] for k in range(2 * npair)]
        m_gl = mgl_scr[...]
        dirs = (0, 1)
        ends = (c - 1, 0)
        pairs = [(d, p) for d in dirs for p in range(npair)]
        ks_of = lambda p: slice(p * 2 * DK_A, (p + 1) * 2 * DK_A)
        vs_of = lambda p: slice(p * 2 * DV_A, (p + 1) * 2 * DV_A)

        glin = [_dot_hp(g_in[d], gkw_ref[d]) + gkb_ref[d] for d in dirs]
        gi = [jnp.where(used[d], g_in[d] + gbi_ref[...], 0.0) for d in dirs]
        lf = [jnp.where(used[d], jax.nn.log_sigmoid(g2_in[d] + gbf_ref[...]), 0.0) for d in dirs]
        mt = [mt_in[d] + gbcol_ref[...] for d in dirs]
        mt = [jnp.where(f_row, jax.nn.log_sigmoid(mt[d]), mt[d]) for d in dirs]
        glog = [jax.nn.log_sigmoid(glin[d]) / GATE_TEMP for d in dirs]
        bc = [_dot_mask_l(tri[d], glog[d], 2) for d in dirs]
        cum = [_dot_mask_l(tri[d], lf[d], 2) for d in dirs]
        cum_t = [_dot_mask_r(mt[d][:, 0:c], tri_t2[d], 2) for d in dirs]
        b_end = [bc[d][ends[d]:ends[d] + 1, :] for d in dirs]
        qa = [gla_in[d][:, _QA:_QA + 256].astype(F32) * scale_q for d in dirs]
        ka = [gla_in[d][:, _KA:_KA + 256].astype(F32) for d in dirs]
        va = [gla_in[d][:, _VA:_VA + 512] for d in dirs]
        qe = [_bf(qa[d] * jnp.exp(bc[d])) for d in dirs]
        ke = [_bf(ka[d] * jnp.exp(-bc[d])) for d in dirs]
        kd = [_bf(ka[d] * jnp.exp(b_end[d] - bc[d])) for d in dirs]
        eb_end = [jnp.exp(b_end[d]) for d in dirs]
        qb = [mls_in[d][:, 0:256] * jnp.asarray(DK_B ** -0.5, BF16) for d in dirs]
        kb = [mls_in[d][:, _KB - _QB:_KB - _QB + 256] for d in dirs]
        vb = [mls_in[d][:, _VB - _QB:_VB - _QB + 512] for d in dirs]
        a_raw = {(d, p): _dot_nt(qe[d][:, ks_of(p)], _block_diag(ke[d][:, ks_of(p)])) for d, p in pairs}
        qk = {(d, p): _dot_nt(qb[d][:, ks_of(p)], _block_diag(kb[d][:, ks_of(p)])) for d, p in pairs}
        s_upd = {(d, p): _dot_tn(va[d][:, vs_of(p)], kd[d][:, ks_of(p)]) for d, p in pairs}
        o_car = {(d, p): _dot_nt(qe[d][:, ks_of(p)], _bf(s_bd[d * npair + p])) for d, p in pairs}
        m_loc = [cum[d] + _running_max(gi[d] - cum[d], reverse=(d == 1)) for d in dirs]
        inter = [cum[d] + m_gl for d in dirs]
        m_t = [jnp.maximum(inter[d], m_loc[d]) for d in dirs]
        b_last = [cum[d][ends[d]:ends[d] + 1, :] for d in dirs]
        dend = [b_last[d] - cum[d] + gi[d] for d in dirs]
        m_new = [jnp.maximum(b_last[d] + m_gl, jnp.max(dend[d], axis=0, keepdims=True)) for d in dirs]
        zero = jnp.zeros((c, LANES), F32)
        per_key = [jnp.concatenate([jnp.where(used[d], cum[d] - m_t[d], zero),
                                    jnp.where(used[d], jnp.exp(inter[d] - m_t[d]), zero),
                                    jnp.where(used[d], jnp.exp(dend[d] - m_new[d]), zero)], axis=0) for d in dirs]
        per_val = [jnp.concatenate([jnp.where(used[d], jnp.exp(-m_t[d]), zero),
                                    jnp.broadcast_to(jnp.where(used[d], jnp.exp(b_last[d] + m_gl - m_new[d]), 0.0),
                                                     (SUBLANES, LANES))], axis=0) for d in dirs]
        per_key = [_dot_mask_r(per_key[d], rep_k[d], 2) for d in dirs]
        per_val = [_dot_mask_r(per_val[d], rep_v[d], 2) for d in dirs]
        o_par = {(d, p): _dot(_bf(jnp.where(live_p[d], a_raw[d, p], 0.0)), _block_diag(va[d][:, vs_of(p)]))
                 for d, p in pairs}
        for d, p in pairs:
            k = d * npair + p
            s_bd[k] = s_bd[k] * eb_end[d][:, ks_of(p)] + jnp.where(bd_val, s_upd[d, p], 0.0)
        o_gla = [jnp.concatenate([o_par[d, p] + o_car[d, p] for p in range(npair)], axis=-1) for d in dirs]
        for d in dirs:
            m_gl = jnp.where(used[d], m_new[d], m_gl)

        w, qa2, kw = {}, {}, {}
        for d, p in pairs:
            ks = ks_of(p)
            r_i, r_f = d * 8 + 2 * p, d * 8 + 4 + 2 * p
            sub = [cum_t[d][r_f + q:r_f + q + 1, :] - mt[d][r_i + q:r_i + q + 1, :] for q in (0, 1)]
            sub = jnp.where(lane < DK_B, sub[0], sub[1])
            w[d, p] = jnp.exp(jnp.where(live_p[d], per_key[d][0:c, ks] - sub, -jnp.inf))
            qa2[d, p] = _bf(qb[d][:, ks].astype(F32) * per_key[d][c:2 * c, ks])
            kw[d, p] = _bf(kb[d][:, ks].astype(F32) * per_key[d][2 * c:3 * c, ks])
        carried = {(d, p): _dot(qa2[d, p], jnp.concatenate([_bf(c_bd[d * npair + p]), _bf(n_bd[d * npair + p])], axis=1))
                   for d, p in pairs}
        upd = {(d, p): _dot_tn(kw[d, p], jnp.concatenate([vb[d][:, vs_of(p)], ones_cv], axis=1)) for d, p in pairs}
        intra = {(d, p): _dot(_bf(qk[d, p] * w[d, p]),
                              jnp.concatenate([_block_diag(vb[d][:, vs_of(p)]), ones_bd], axis=1)) for d, p in pairs}
        outs = {}
        for d, p in pairs:
            k = d * npair + p
            vs = vs_of(p)
            num = intra[d, p][:, :2 * DV_B] + carried[d, p][:, :2 * DV_B]
            den = intra[d, p][:, 2 * DV_B:] + carried[d, p][:, 2 * DV_B:]
            outs[d, p] = num / jnp.maximum(jnp.abs(den), per_val[d][0:c, vs])
            decay = per_val[d][c:c + 1, vs]
            dec = jnp.concatenate([jnp.broadcast_to(jnp.concatenate([decay[:, q * DV_B:(q + 1) * DV_B]] * 2, axis=1),
                                                    (DK_B, 2 * DV_B)) for q in (0, 1)], axis=0)
            c_bd[k] = dec * c_bd[k] + jnp.where(bd_key, upd[d, p][:, :2 * DV_B], 0.0)
            n_bd[k] = dec * n_bd[k] + jnp.where(bd_key, upd[d, p][:, 2 * DV_B:], 0.0)
        o_mls = [jnp.concatenate([outs[d, p] for p in range(npair)], axis=-1) for d in dirs]

        for d in (0, 1):
            oa_scr[d, rows[d], :] = o_gla[d]
            ob_scr[d, rows[d], :] = o_mls[d]
        for k in range(2 * npair):
            sbd_scr[k] = s_bd[k]
            cbd_scr[k] = c_bd[k]
            nbd_scr[k] = n_bd[k]
        mgl_scr[...] = m_gl
        return carry

    lax.fori_loop(0, nchunks, chunk_step, 0)

    for d in (0, 1):
        for p in range(npair):
            k = d * npair + p
            s_t, c_f, n_f = sbd_scr[k], cbd_scr[k], nbd_scr[k]
            for q in (0, 1):
                h = 2 * p + q
                og_ref[d, h] = s_t[q * DV_A:(q + 1) * DV_A, q * DK_A:(q + 1) * DK_A]
                oc_ref[d, h] = c_f[q * DK_B:(q + 1) * DK_B, q * DV_B:(q + 1) * DV_B]
                n_t = n_f[q * DK_B:(q + 1) * DK_B, q * DV_B:(q + 1) * DV_B].T
                on_ref[d * H_B + h:d * H_B + h + 1, :] = n_t[0:1, :]
    m_gl = mgl_scr[...]
    om_ref[...] = jnp.concatenate([m_gl[:, gate_lane(d, h):gate_lane(d, h) + 1]
                                   for d in (0, 1) for h in range(H_B)], axis=1)

    mod = _mod_slices(mod_ref[...])
    g1, sh2, sc2 = mod[2], mod[3], mod[4]

    def out_step(j, carry):
        r0 = pl.multiple_of(j * TM, TM)
        rows = pl.ds(r0, TM)
        parts = []
        for h in range(H_A):
            vs = slice(h * DV_A, (h + 1) * DV_A)
            ra = main_ref[rows, _RA + h * DV_A:_RA + (h + 1) * DV_A].astype(F32)
            parts.append(_rms(oa_scr[0, rows, vs] + oa_scr[1, rows, vs], gnw_ref[...]) * (ra * jax.nn.sigmoid(ra)))
        for h in range(H_B):
            vs = slice(h * DV_B, (h + 1) * DV_B)
            ob = main_ref[rows, _OB + h * DV_B:_OB + (h + 1) * DV_B].astype(F32)
            parts.append(_rms(ob_scr[0, rows, vs] + ob_scr[1, rows, vs], mnw_ref[...]) * jax.nn.sigmoid(ob))
        mix = _dot(_bf(jnp.concatenate(parts, axis=-1)), wout_ref[...])
        x1 = x_ref[rows, :] + g1 * mix
        x1_ref[rows, :] = x1
        h2 = _rms(x1, n2w_ref[...]) * (1.0 + sc2) + sh2
        _store_rows_per_vreg(h2_ref.at[pl.ds(pl.multiple_of(r0 * _RPV, TM * _RPV), TM * _RPV)], h2)
        lg_ref[j] = _router_logits(h2, wrt_ref, brt_ref)
        return carry

    lax.fori_loop(0, seq // TM, out_step, 0)


def _scan(st, ctx, main, g, g2, mt, x, mods, states, consts, prev):
    seq = st.lp if ctx else st.ls
    nb = st.bp if ctx else st.bs
    off = 0 if ctx else st.tp // st.ls
    sg, sc, sn, sm = states
    full = lambda shape: pl.BlockSpec(shape, lambda b: (0,) * len(shape))
    in_specs = [
        pl.BlockSpec((seq, _REC_MAIN), lambda b: (off + b, 0)),
        pl.BlockSpec((seq, LANES), lambda b: (off + b, 0)),
        pl.BlockSpec((seq, LANES), lambda b: (off + b, 0)),
        pl.BlockSpec((seq // CHUNK, 16, 2 * CHUNK), lambda b: (off + b, 0, 0)),
        pl.BlockSpec((seq, D_MODEL), lambda b: (off + b, 0)),
    ]
    args = [main, g, g2, mt, x]
    mod_row = (lambda b: (0, 0, 0)) if ctx else (lambda b: (1 + b, 0, 0))
    in_specs += [
        pl.BlockSpec((None, 1, N_MOD * D_MODEL), mod_row),
        pl.BlockSpec((None, 2, H_A, DV_A, DK_A), lambda b: (b, 0, 0, 0, 0)),
        pl.BlockSpec((None, 2, H_B, DK_B, DV_B), lambda b: (b, 0, 0, 0, 0)),
        pl.BlockSpec((None, 2 * H_B, DK_B), lambda b: (b, 0, 0)),
        pl.BlockSpec((None, 1, 2 * H_B), lambda b: (b, 0, 0)),
    ]
    args += [mods, sg, sc, sn, sm]
    for a in consts:
        in_specs.append(full(a.shape))
        args.append(a)
    aliases = {}
    if prev is not None:
        aliases = {len(args) + k: k for k in range(3)}
        in_specs += [pl.BlockSpec(memory_space=pl.ANY)] * 3
        args += list(prev)
    out_shape = (jax.ShapeDtypeStruct((st.t, D_MODEL), F32),
                 jax.ShapeDtypeStruct((st.t * _RPV, LANES), F32),
                 jax.ShapeDtypeStruct((st.t // TM, LG_ROWS, TM), F32),
                 jax.ShapeDtypeStruct(sg.shape, F32), jax.ShapeDtypeStruct(sc.shape, F32),
                 jax.ShapeDtypeStruct(sn.shape, F32), jax.ShapeDtypeStruct(sm.shape, F32))
    out_specs = (pl.BlockSpec((seq, D_MODEL), lambda b: (off + b, 0)),
                 pl.BlockSpec((seq * _RPV, LANES), lambda b: (off + b, 0)),
                 pl.BlockSpec((seq // TM, LG_ROWS, TM), lambda b: (off + b, 0, 0)),
                 pl.BlockSpec((None, 2, H_A, DV_A, DK_A), lambda b: (b, 0, 0, 0, 0)),
                 pl.BlockSpec((None, 2, H_B, DK_B, DV_B), lambda b: (b, 0, 0, 0, 0)),
                 pl.BlockSpec((None, 2 * H_B, DK_B), lambda b: (b, 0, 0)),
                 pl.BlockSpec((None, 1, 2 * H_B), lambda b: (b, 0, 0)))
    return pl.pallas_call(
        functools.partial(_scan_kernel, seq=seq),
        out_shape=out_shape,
        grid=(nb,),
        in_specs=in_specs,
        out_specs=out_specs,
        scratch_shapes=[pltpu.VMEM((2, seq, H_A * DV_A), F32), pltpu.VMEM((2, seq, H_B * DV_B), F32),
                        pltpu.VMEM((H_A, 2 * DV_A, 2 * DK_A), F32), pltpu.VMEM((H_B, 2 * DK_B, 2 * DV_B), F32),
                        pltpu.VMEM((H_B, 2 * DK_B, 2 * DV_B), F32), pltpu.VMEM((1, LANES), F32)],
        input_output_aliases=aliases,
        compiler_params=_cparams("arbitrary"),
        name="rec_scan_ctx" if ctx else "rec_scan_lat",
    )(*args)


def _first_max(rows):
    m = rows[0]
    for r in rows[1:]:
        m = jnp.maximum(m, r)
    idx = jnp.full(m.shape, len(rows) - 1, jnp.int32)
    for k in range(len(rows) - 2, -1, -1):
        idx = jnp.where(rows[k] == m, k, idx)
    return m, idx


def _route_kernel(lg_ref, pos1_ref, pos2_ref, w1_ref, w2_ref, tab_ref, *, sb):
    lg = jnp.concatenate([lg_ref[b] for b in range(sb // TM)], axis=1)
    rows = [lg[k:k + 1, :] for k in range(N_EXPERTS + N_GROUPS)]
    grp = rows[N_EXPERTS:]
    gmax, gidx = _first_max(grp)
    p_group = 1.0 / sum(jnp.exp(r - gmax) for r in grp)
    e_in = []
    for k in range(EXP_PER_GROUP):
        v = rows[(N_GROUPS - 1) * EXP_PER_GROUP + k]
        for g in range(N_GROUPS - 2, -1, -1):
            v = jnp.where(gidx == g, rows[g * EXP_PER_GROUP + k], v)
        e_in.append(v)
    v1, i1 = _first_max(e_in)
    v2, i2 = _first_max([jnp.where(i1 == k, -jnp.inf, e_in[k]) for k in range(EXP_PER_GROUP)])
    ex = jnp.exp(v2 - v1)
    w1_ref[...] = p_group / (1.0 + ex)
    w2_ref[...] = p_group * ex / (1.0 + ex)
    x1 = gidx * EXP_PER_GROUP + i1
    x2 = gidx * EXP_PER_GROUP + i2

    eid = lax.broadcasted_iota(jnp.int32, (N_EXPERTS, sb), 0)
    sel = jnp.where((eid == x1) | (eid == x2), 1.0, 0.0)
    r_i = lax.broadcasted_iota(jnp.int32, (TM, TM), 0)
    c_i = lax.broadcasted_iota(jnp.int32, (TM, TM), 1)
    before = jnp.where(r_i < c_i, 1.0, 0.0).astype(BF16)
    carry = jnp.zeros((N_EXPERTS, 1), F32)
    ranks = []
    for b in range(sb // TM):
        s_b = sel[:, b * TM:(b + 1) * TM]
        ranks.append(_dot(_bf(s_b), before) + carry)
        carry = carry + jnp.sum(s_b, axis=1, keepdims=True)
    rank = jnp.concatenate(ranks, axis=1)
    shift = ROW_PAD.bit_length() - 1
    npad = jnp.left_shift(jnp.right_shift(carry.astype(jnp.int32) + (ROW_PAD - 1), shift), shift)

    lane = lax.broadcasted_iota(jnp.int32, (1, LANES), 1)
    tab = jnp.zeros((1, LANES), jnp.int32)
    pos1 = jnp.zeros((1, sb), F32)
    pos2 = jnp.zeros((1, sb), F32)
    off = jnp.zeros((1, 1), jnp.int32)
    for e in range(N_EXPERTS):
        n_e = npad[e:e + 1, :]
        tab = jnp.where(lane == e, off, tab)
        tab = jnp.where(lane == N_EXPERTS + e, n_e, tab)
        row = off.astype(F32) + rank[e:e + 1, :]
        pos1 = jnp.where(x1 == e, row, pos1)
        pos2 = jnp.where(x2 == e, row, pos2)
        off = off + n_e
    pos1_ref[...] = pos1.astype(jnp.int32) * _RPV
    pos2_ref[...] = pos2.astype(jnp.int32) * _RPV
    tab_ref[...] = tab


def _route(st, lg):
    sb = st.sb
    nsb = st.t // sb
    row_i = jax.ShapeDtypeStruct((nsb, 1, sb), jnp.int32)
    row_f = jax.ShapeDtypeStruct((nsb, 1, sb), F32)
    rspec = pl.BlockSpec((None, 1, sb), lambda s: (s, 0, 0))
    return pl.pallas_call(
        functools.partial(_route_kernel, sb=sb),
        out_shape=(row_i, row_i, row_f, row_f, jax.ShapeDtypeStruct((nsb, 1, LANES), jnp.int32)),
        grid=(nsb,),
        in_specs=[pl.BlockSpec((sb // TM, LG_ROWS, TM), lambda s: (s, 0, 0))],
        out_specs=(rspec, rspec, rspec, rspec, pl.BlockSpec((None, 1, LANES), lambda s: (s, 0, 0))),
        compiler_params=_cparams("arbitrary"),
        name="moe_route",
    )(lg)


def _moe_rows(sb):
    return 2 * sb + N_EXPERTS * ROW_PAD + FFN_TILE


def _moe_kernel(pos1_ref, pos2_ref, w1_ref, w2_ref, tab_ref, xr_ref, wg_ref, wu_ref, wd_ref, o_ref,
                rows_scr, stage_scr, wd_scr, *, sb):
    s = pl.program_id(0)
    e = pl.program_id(1)

    def tile_at(ref, r8):
        return ref.at[pl.ds(pl.multiple_of(r8, _RPV), _RPV)]

    @pl.when((s == 0) & (e == 0))
    def _():
        rows_scr[...] = jnp.zeros_like(rows_scr)

    @pl.when(e == 0)
    def _():
        def dispatch(t, carry):
            v = tile_at(xr_ref, t * _RPV)[...]
            tile_at(rows_scr, pos1_ref[0, t])[...] = v
            tile_at(rows_scr, pos2_ref[0, t])[...] = v
            return carry

        lax.fori_loop(0, sb, dispatch, 0, unroll=8)

    wd_scr[...] = _bf(wd_ref[...])

    def ffn_tile(r0, m, valid=None):
        win = rows_scr.at[pl.ds(pl.multiple_of(r0 * _RPV, ROW_PAD * _RPV), m * _RPV)]
        x = _load_rows_per_vreg(win, m)
        xb = _bf(x)
        hg = _dot(xb, wg_ref[...])
        hu = _dot(xb, wu_ref[...])
        y = _dot(_bf(hg * jax.nn.sigmoid(hg) * hu), wd_scr[...])
        if valid is not None:
            y = jnp.where(lax.broadcasted_iota(jnp.int32, (m, 1), 0) < valid, y, x)
        _store_rows_per_vreg(win, y)

    off = tab_ref[0, e]
    npad = tab_ref[0, N_EXPERTS + e]
    nfull = npad // FFN_TILE

    def full_tile(i, carry):
        ffn_tile(off + i * FFN_TILE, FFN_TILE)
        return carry

    lax.fori_loop(0, nfull, full_tile, 0)
    rem = npad - nfull * FFN_TILE
    last = off + nfull * FFN_TILE

    @pl.when((rem > 0) & (rem <= FFN_TILE // 2))
    def _():
        ffn_tile(last, FFN_TILE // 2, valid=rem)

    @pl.when(rem > FFN_TILE // 2)
    def _():
        ffn_tile(last, FFN_TILE, valid=rem)

    @pl.when(e == N_EXPERTS - 1)
    def _():
        for c in range(sb // TM):
            def combine(t, carry, c=c):
                tt = c * TM + t
                y = (w1_ref[0, tt] * tile_at(rows_scr, pos1_ref[0, tt])[...]
                     + w2_ref[0, tt] * tile_at(rows_scr, pos2_ref[0, tt])[...])
                tile_at(stage_scr, t * _RPV)[...] = y
                return carry

            lax.fori_loop(0, TM, combine, 0, unroll=8)
            o_ref[c * TM:(c + 1) * TM, :] = _bf(_load_rows_per_vreg(stage_scr, TM))


def _cast_kernel(x_ref, o_ref):
    o_ref[...] = _bf(x_ref[...])


def _expert_weights_bf16(w):
    depth, ne, a, b = w.shape
    spec = pl.BlockSpec((None, a, b), lambda i: (i, 0, 0))
    return pl.pallas_call(
        _cast_kernel,
        out_shape=jax.ShapeDtypeStruct((depth * ne, a, b), BF16),
        grid=(depth * ne,),
        in_specs=[spec],
        out_specs=spec,
        compiler_params=_cparams("arbitrary"),
        name="expert_weight_cast",
    )(w.reshape(depth * ne, a, b))


def _moe(st, layer, xr, route, wg, wu, wd):
    sb = st.sb
    e0 = layer * N_EXPERTS
    smem = lambda n: pl.BlockSpec((None, 1, n), lambda s, e: (s, 0, 0), memory_space=pltpu.SMEM)
    return pl.pallas_call(
        functools.partial(_moe_kernel, sb=sb),
        out_shape=jax.ShapeDtypeStruct((st.t, D_MODEL), BF16),
        grid=(st.t // sb, N_EXPERTS),
        in_specs=[
            smem(sb), smem(sb), smem(sb), smem(sb), smem(LANES),
            pl.BlockSpec((sb * _RPV, LANES), lambda s, e: (s, 0)),
            pl.BlockSpec((None, D_MODEL, D_EXPERT), lambda s, e: (e0 + e, 0, 0)),
            pl.BlockSpec((None, D_MODEL, D_EXPERT), lambda s, e: (e0 + e, 0, 0)),
            pl.BlockSpec((None, D_EXPERT, D_MODEL), lambda s, e: (e0 + e, 0, 0)),
        ],
        out_specs=pl.BlockSpec((sb, D_MODEL), lambda s, e: (s, 0)),
        scratch_shapes=[pltpu.VMEM((_moe_rows(sb) * _RPV, LANES), F32), pltpu.VMEM((TM * _RPV, LANES), F32),
                        pltpu.VMEM((D_EXPERT, D_MODEL), BF16)],
        compiler_params=_cparams("arbitrary", "arbitrary"),
        name="moe_ffn",
    )(*route, xr, wg, wu, wd)


def _hy_inproj_kernel(x_ref, m_ref, mod0_ref, mod_ref, nw_ref, w_ref, x2_ref, z_ref):
    g2 = _mod_slices(mod0_ref[...])[5]
    x2 = x_ref[...] + g2 * m_ref[...].astype(F32)
    x2_ref[...] = x2
    sh1, sc1 = _mod_slices(mod_ref[...])[:2]
    h = _rms(x2, nw_ref[...]) * (1.0 + sc1) + sh1
    z_ref[...] = _bf(_dot(_bf(h), w_ref[...]))


def _hy_inproj(st, x, moe, mods_prev, mods, nw, w):
    n = w.shape[1]
    tm = TM_TOK
    mspec = pl.BlockSpec((None, 1, N_MOD * D_MODEL), lambda i: (st.mod_row(i, tm), 0, 0))
    return pl.pallas_call(
        _hy_inproj_kernel,
        out_shape=(jax.ShapeDtypeStruct((st.t, D_MODEL), F32), jax.ShapeDtypeStruct((st.t, n), BF16)),
        grid=(st.t // tm,),
        in_specs=[
            pl.BlockSpec((tm, D_MODEL), lambda i: (i, 0)),
            pl.BlockSpec((tm, D_MODEL), lambda i: (i, 0)),
            mspec, mspec,
            pl.BlockSpec((1, D_MODEL), lambda i: (0, 0)),
            pl.BlockSpec((D_MODEL, n), lambda i: (0, 0)),
        ],
        out_specs=(pl.BlockSpec((tm, D_MODEL), lambda i: (i, 0)), pl.BlockSpec((tm, n), lambda i: (i, 0))),
        compiler_params=_cparams("arbitrary"),
        name="hy_inproj",
    )(x, moe, mods_prev, mods, nw, w)


def _dft_tables(seq):
    n2 = 2 * seq
    k = jnp.arange(seq, dtype=jnp.int32)
    ang = ((k[:, None] * k[None, :]) % n2).astype(F32) * (2.0 * math.pi / n2)
    return jnp.cos(ang), -jnp.sin(ang)


def _alternating(shape):
    return jnp.where(lax.broadcasted_iota(jnp.int32, shape, 0) % 2 == 0, 1.0, -1.0)


def _hy_filter_kernel(emb_ref, dec_ref, w1_ref, b1_ref, f1_ref, w2_ref, b2_ref, f2_ref, w3_ref,
                      cos_ref, msin_ref, kr_ref, ki_ref, *, seq):
    h = jnp.sin(f1_ref[...] * (_dot_hp(emb_ref[...], w1_ref[...]) + b1_ref[...]))
    h = jnp.sin(f2_ref[...] * (_dot_hp(h, w2_ref[...]) + b2_ref[...]))
    dec = dec_ref[...]
    row0 = lax.broadcasted_iota(jnp.int32, dec.shape, 0) == 0
    alt = _alternating(dec.shape)
    cos, msin = cos_ref[...], msin_ref[...]
    scale = jnp.where(row0, 1.0, 2.0) / (2 * seq)
    for o in range(HY_ORDER):
        h_f = _dot_hp(h, w3_ref[:, 2 * o, :]) * dec
        h_b = jnp.where(row0, 0.0, _dot_hp(h, w3_ref[:, 2 * o + 1, :]) * dec)
        k_nyq = jnp.sum(alt * (h_f + h_b), axis=0, keepdims=True)
        kr_ref[o] = _dot_hp(cos, h_f + h_b) * scale
        ki_ref[o] = jnp.where(row0, k_nyq, _dot_hp(msin, h_f - h_b)) * scale


def _hy_filter(seq, w1, b1, f1, w2, b2, f2, w3, cos, msin):
    t = jnp.linspace(0.0, 1.0, seq, dtype=F32)[:, None]
    w = 2.0 * math.pi * jnp.arange(seq, dtype=F32)[:, None] / seq
    f = jnp.linspace(1e-4, HY_BANDS - 1, HY_BANDS, dtype=F32)[None, :]
    emb = jnp.concatenate([t, jnp.cos(f * w), -jnp.sin(f * w), jnp.zeros((seq, LANES - HY_EMB), F32)], axis=-1)
    decay = jnp.exp(-t * jnp.linspace(HY_MIN_DECAY, HY_MAX_DECAY, D_MODEL, dtype=F32)[None, :])
    w1p = jnp.concatenate([w1, jnp.zeros((LANES - HY_EMB, HY_FFN), F32)], axis=0)
    dblk = 256
    out = jax.ShapeDtypeStruct((HY_ORDER, seq, D_MODEL), F32)
    full = lambda shape: pl.BlockSpec(shape, lambda j: (0,) * len(shape))
    ospec = pl.BlockSpec((HY_ORDER, seq, dblk), lambda j: (0, 0, j))
    return pl.pallas_call(
        functools.partial(_hy_filter_kernel, seq=seq),
        out_shape=(out, out),
        grid=(D_MODEL // dblk,),
        in_specs=[
            full((seq, LANES)),
            pl.BlockSpec((seq, dblk), lambda j: (0, j)),
            full((LANES, HY_FFN)), full((1, HY_FFN)), full((1, HY_FFN)),
            full((HY_FFN, HY_FFN)), full((1, HY_FFN)), full((1, HY_FFN)),
            pl.BlockSpec((HY_FFN, 2 * HY_ORDER, dblk), lambda j: (0, 0, j)),
            full((seq, seq)), full((seq, seq)),
        ],
        out_specs=(ospec, ospec),
        compiler_params=_cparams("arbitrary"),
        name=f"hy_filter_{seq}",
    )(emb, decay, w1p, b1[None, :], f1[None, :], w2, b2[None, :], f2[None, :],
      w3.reshape(HY_FFN, 2 * HY_ORDER, D_MODEL), cos, msin)


def _hy_conv_kernel(*refs, seq, nseq):
    (zv_ref, z1_ref, z2_ref, cv_ref, c1_ref, c2_ref, kr_ref, ki_ref, bias_ref, cos_ref, msin_ref) = refs[:11]
    y_ref = refs[-1]
    dblk = y_ref.shape[1]
    t = lax.broadcasted_iota(jnp.int32, (seq, dblk), 0)
    first, last, row0 = t == 0, t == seq - 1, t == 0
    alt = _alternating((seq, dblk))
    seqs = range(nseq)
    rows = [slice(q * seq, (q + 1) * seq) for q in seqs]

    def short_conv(z_ref, c_ref):
        out = []
        for q in seqs:
            z = z_ref[rows[q], :].astype(F32)
            prev = jnp.where(first, 0.0, pltpu.roll(z, 1, 0))
            nxt = jnp.where(last, 0.0, pltpu.roll(z, seq - 1, 0))
            out.append(c_ref[0:1, :] * prev + c_ref[1:2, :] * z + c_ref[2:3, :] * nxt)
        return out

    def long_conv(s, o):
        sb = [_bf(s[q]) for q in seqs]
        x_re = [_dot(cos_ref[...], sb[q]) for q in seqs]
        x_im = [_dot(msin_ref[...], sb[q]) for q in seqs]
        x_im = [jnp.where(row0, jnp.sum(alt * s[q], axis=0, keepdims=True), x_im[q]) for q in seqs]
        k_re, k_im = kr_ref[o], ki_ref[o]
        y_re = [jnp.where(row0, x_re[q] * k_re, x_re[q] * k_re - x_im[q] * k_im) for q in seqs]
        y_im = [jnp.where(row0, x_im[q] * k_im, x_re[q] * k_im + x_im[q] * k_re) for q in seqs]
        y = [_dot(cos_ref[...], _bf(y_re[q])) + _dot(msin_ref[...], _bf(y_im[q])) for q in seqs]
        return [y[q] + alt * y_im[q][0:1, :] + bias_ref[o] * s[q] for q in seqs]

    v = short_conv(zv_ref, cv_ref)
    x1 = short_conv(z1_ref, c1_ref)
    x2 = short_conv(z2_ref, c2_ref)
    c1 = long_conv(v, 0)
    c2 = long_conv([x1[q] * c1[q] for q in seqs], 1)
    for q in seqs:
        y_ref[rows[q], :] = _bf(x2[q] * c2[q])


def _hy_conv(st, ctx, z, conv_w, filt, bias, tables, prev):
    seq = st.lp if ctx else st.ls
    nb = st.bp if ctx else st.bs
    nseq = max(1, min(nb, 2048 // seq))
    assert nb % nseq == 0
    off = 0 if ctx else st.tp // (nseq * seq)
    dblk = 256
    nd = D_MODEL // dblk
    kr, ki = filt
    cos, msin = tables
    full = lambda shape: pl.BlockSpec(shape, lambda j, b: (0,) * len(shape))
    kspec = pl.BlockSpec((HY_ORDER, seq, dblk), lambda j, b: (0, 0, j))
    in_specs = [
        pl.BlockSpec((nseq * seq, dblk), lambda j, b: (off + b, j)),
        pl.BlockSpec((nseq * seq, dblk), lambda j, b: (off + b, nd + j)),
        pl.BlockSpec((nseq * seq, dblk), lambda j, b: (off + b, 2 * nd + j)),
        pl.BlockSpec((3, dblk), lambda j, b: (0, j)),
        pl.BlockSpec((3, dblk), lambda j, b: (0, nd + j)),
        pl.BlockSpec((3, dblk), lambda j, b: (0, 2 * nd + j)),
        kspec, kspec,
        pl.BlockSpec((HY_ORDER, 1, dblk), lambda j, b: (0, 0, j)),
        full((seq, seq)), full((seq, seq)),
    ]
    args = [z, z, z, conv_w, conv_w, conv_w, kr, ki, bias.reshape(HY_ORDER, 1, D_MODEL), cos, msin]
    aliases = {}
    if prev is not None:
        aliases = {len(args): 0}
        in_specs.append(pl.BlockSpec(memory_space=pl.ANY))
        args.append(prev)
    return pl.pallas_call(
        functools.partial(_hy_conv_kernel, seq=seq, nseq=nseq),
        out_shape=jax.ShapeDtypeStruct((st.t, D_MODEL), BF16),
        grid=(nd, nb // nseq),
        in_specs=in_specs,
        out_specs=pl.BlockSpec((nseq * seq, dblk), lambda j, b: (off + b, j)),
        input_output_aliases=aliases,
        compiler_params=_cparams("arbitrary", "arbitrary"),
        name="hy_conv_ctx" if ctx else "hy_conv_lat",
    )(*args)


def _hy_outproj_kernel(y_ref, x_ref, mod_ref, w_ref, n2w_ref, wrt_ref, brt_ref, x3_ref, h2_ref, lg_ref):
    mod = _mod_slices(mod_ref[...])
    g1, sh2, sc2 = mod[2], mod[3], mod[4]
    x3 = x_ref[...] + g1 * _dot(y_ref[...], w_ref[...])
    x3_ref[...] = x3
    h2 = _rms(x3, n2w_ref[...]) * (1.0 + sc2) + sh2
    _store_rows_per_vreg(h2_ref, h2)
    lg = _router_logits(h2, wrt_ref, brt_ref)
    for b in range(lg.shape[1] // TM):
        lg_ref[b] = lg[:, b * TM:(b + 1) * TM]


def _hy_outproj(st, y, x, mods, w, n2w, wrt, brt):
    tm = TM_TOK
    return pl.pallas_call(
        _hy_outproj_kernel,
        out_shape=(jax.ShapeDtypeStruct((st.t, D_MODEL), F32),
                   jax.ShapeDtypeStruct((st.t * _RPV, LANES), F32),
                   jax.ShapeDtypeStruct((st.t // TM, LG_ROWS, TM), F32)),
        grid=(st.t // tm,),
        in_specs=[
            pl.BlockSpec((tm, D_MODEL), lambda i: (i, 0)),
            pl.BlockSpec((tm, D_MODEL), lambda i: (i, 0)),
            pl.BlockSpec((None, 1, N_MOD * D_MODEL), lambda i: (st.mod_row(i, tm), 0, 0)),
            pl.BlockSpec((D_MODEL, D_MODEL), lambda i: (0, 0)),
            pl.BlockSpec((1, D_MODEL), lambda i: (0, 0)),
            pl.BlockSpec((LG_ROWS, D_MODEL), lambda i: (0, 0)),
            pl.BlockSpec((LG_ROWS, 1), lambda i: (0, 0)),
        ],
        out_specs=(pl.BlockSpec((tm, D_MODEL), lambda i: (i, 0)),
                   pl.BlockSpec((tm * _RPV, LANES), lambda i: (i, 0)),
                   pl.BlockSpec((tm // TM, LG_ROWS, TM), lambda i: (i, 0, 0))),
        compiler_params=_cparams("arbitrary"),
        name="hy_outproj",
    )(y, x, mods, w, n2w, wrt, brt)


def _final_kernel(x_ref, m_ref, mod_ref, nf_ref, yp_ref, ys_ref, *, ncb):
    i = pl.program_id(0)
    g2 = _mod_slices(mod_ref[...])[5]
    y = _rms(x_ref[...] + g2 * m_ref[...].astype(F32), nf_ref[...])

    @pl.when(i < ncb)
    def _():
        yp_ref[...] = y

    @pl.when(i >= ncb)
    def _():
        ys_ref[...] = y


def _final(st, x, moe, mods, nf):
    tm = TM_TOK
    ncb = st.tp // tm
    return pl.pallas_call(
        functools.partial(_final_kernel, ncb=ncb),
        out_shape=(jax.ShapeDtypeStruct((st.tp, D_MODEL), F32), jax.ShapeDtypeStruct((st.ts, D_MODEL), F32)),
        grid=(st.t // tm,),
        in_specs=[
            pl.BlockSpec((tm, D_MODEL), lambda i: (i, 0)),
            pl.BlockSpec((tm, D_MODEL), lambda i: (i, 0)),
            pl.BlockSpec((None, 1, N_MOD * D_MODEL), lambda i: (st.mod_row(i, tm), 0, 0)),
            pl.BlockSpec((1, D_MODEL), lambda i: (0, 0)),
        ],
        out_specs=(pl.BlockSpec((tm, D_MODEL), lambda i: (jnp.minimum(i, ncb - 1), 0)),
                   pl.BlockSpec((tm, D_MODEL), lambda i: (jnp.maximum(i - ncb, 0), 0))),
        compiler_params=_cparams("arbitrary"),
        name="final_norm",
    )(x, moe, mods, nf)


def _grid_pos_table(seq):
    rows = seq // GRID_W
    r, cl = jnp.meshgrid(jnp.arange(rows, dtype=F32), jnp.arange(GRID_W, dtype=F32), indexing='ij')
    quarter = D_MODEL // 4
    omega = POS_THETA ** (-jnp.arange(quarter, dtype=F32) / quarter)

    def enc(pos):
        a = pos.reshape(-1, 1) * omega[None, :]
        return jnp.concatenate([jnp.sin(a), jnp.cos(a)], axis=-1)

    return jnp.concatenate([enc(r), enc(cl)], axis=-1)


def _router_weights(w_group, b_group, w_router, b_router):
    pad = LG_ROWS - N_EXPERTS - N_GROUPS
    w = jnp.concatenate([w_router.T, w_group.T, jnp.zeros((pad, D_MODEL), F32)], axis=0)
    b = jnp.concatenate([b_router, b_group, jnp.zeros((pad,), F32)])[:, None]
    return w, b


def kernel(x_prompt, x_sample, state_gla, state_mlstm_c, state_mlstm_n, state_mlstm_m, c, c_ctx, norm1_w, norm2_w, norm_f_w, ada_w, ada_b, rec_w_in, gla_gk_w, gla_gk_b, mlstm_gate_b, gla_norm_w, mlstm_norm_w, rec_w_out, hy_w_in, hy_conv_w, hy_f_w1, hy_f_b1, hy_f_freq1, hy_f_w2, hy_f_b2, hy_f_freq2, hy_f_w3, hy_f_bias, hy_w_out, moe_w_group, moe_b_group, moe_w_router, moe_b_router, moe_w_gate, moe_w_up, moe_w_down):
    bp, lp, _ = x_prompt.shape
    bs, ls, _ = x_sample.shape
    st = _Streams(bp, lp, bs, ls)
    xp = x_prompt.reshape(st.tp, D_MODEL)
    xs = x_sample.reshape(st.ts, D_MODEL)

    nrow = -(-(1 + bs) // 8) * 8
    cv = jnp.concatenate([c_ctx[None, :], c, jnp.zeros((nrow - 1 - bs, D_MODEL), F32)], axis=0)
    mods = _ada(cv, ada_w, ada_b)
    mods0 = mods[0].reshape(nrow, 1, N_MOD * D_MODEL)
    mods1 = mods[1].reshape(nrow, 1, N_MOD * D_MODEL)

    pos = _grid_pos_table(ls)

    w_in = rec_w_in[0]
    w_main = _bf(w_in[:, :_REC_MAIN])
    w_mg = w_in[:, _REC_MAIN + _MG0:].reshape(D_MODEL, 2, 2, H_B)
    w_fg = jnp.pad(w_mg[:, :, 1, :], ((0, 0), (0, 0), (0, 8 - H_B))).reshape(D_MODEL, 16)
    w_gate = jnp.concatenate([w_in[:, _REC_MAIN:], jnp.zeros((D_MODEL, LANES - _REC_GATE), F32),
                              jnp.zeros((D_MODEL, _MG0), F32), w_fg,
                              jnp.zeros((D_MODEL, LANES - _REC_GATE), F32)], axis=1)
    w_gate_t = w_in[:, _REC_MAIN + _MG0:].T
    x0, main, g, g2, mt = _inproj(st, xp, xs, pos, mods0, norm1_w[0][None, :], w_main, w_gate, w_gate_t)

    gkw = jnp.zeros((2, LANES, H_A * DK_A), F32)
    gkw = gkw.at[0, :GK_RANK].set(gla_gk_w[0, 0]).at[1, GK_RANK:2 * GK_RANK].set(gla_gk_w[0, 1])
    gkb = gla_gk_b[0][:, None, :]
    gb = mlstm_gate_b[0]
    gate_row = lambda b: jnp.pad(jnp.pad(b, ((0, 0), (0, 8 - H_B))).reshape(1, 16), ((0, 0), (_MG0, LANES - _REC_GATE)))
    gbcol = gb.reshape(16, 1)
    wr0, br0 = _router_weights(moe_w_group[0], moe_b_group[0], moe_w_router[0], moe_b_router[0])
    consts = [gkw, gkb, gate_row(gb[:, 0]), gate_row(gb[:, 1]), gbcol, gla_norm_w[0][None, :],
              mlstm_norm_w[0][None, :], _bf(rec_w_out[0]), norm2_w[0][None, :], wr0, br0]

    def scan_states(sg, sc, sn, sm):
        nb = sg.shape[0]
        return (jnp.swapaxes(sg, -1, -2), sc, sn.reshape(nb, 2 * H_B, DK_B), sm.reshape(nb, 1, 2 * H_B))

    zero = (jnp.zeros((bp, 2, H_A, DK_A, DV_A), F32), jnp.zeros((bp, 2, H_B, DK_B, DV_B), F32),
            jnp.zeros((bp, 2, H_B, DK_B), F32), jnp.zeros((bp, 2, H_B), F32))
    cached = (state_gla[:, 0], state_mlstm_c[:, 0], state_mlstm_n[:, 0], state_mlstm_m[:, 0])
    x1, h2, lg, fg, fc, fn, fm = _scan(st, True, main, g, g2, mt, x0, mods0, scan_states(*zero), consts, None)
    x1, h2, lg = _scan(st, False, main, g, g2, mt, x0, mods0, scan_states(*cached), consts, (x1, h2, lg))[:3]

    new_gla = jnp.swapaxes(fg, -1, -2)[:, None]
    new_c = fc[:, None]
    new_n = fn.reshape(bp, 1, 2, H_B, DK_B)
    new_m = fm.reshape(bp, 1, 2, H_B)

    wg, wu = (_expert_weights_bf16(w) for w in (moe_w_gate, moe_w_up))
    wd = moe_w_down.reshape(-1, D_EXPERT, D_MODEL)
    moe0 = _moe(st, 0, h2, _route(st, lg), wg, wu, wd)

    x2, z = _hy_inproj(st, x1, moe0, mods0, mods1, norm1_w[1][None, :], _bf(hy_w_in[0]))
    y = None
    for ctx in (True, False):
        seq = lp if ctx else ls
        tables = _dft_tables(seq)
        filt = _hy_filter(seq, hy_f_w1[0], hy_f_b1[0], hy_f_freq1[0], hy_f_w2[0], hy_f_b2[0], hy_f_freq2[0],
                          hy_f_w3[0], *tables)
        y = _hy_conv(st, ctx, z, hy_conv_w[0], filt, hy_f_bias[0], tuple(_bf(a) for a in tables), y)
    wr1, br1 = _router_weights(moe_w_group[1], moe_b_group[1], moe_w_router[1], moe_b_router[1])
    x3, h4, lg1 = _hy_outproj(st, y, x2, mods1, _bf(hy_w_out[0]), norm2_w[1][None, :], wr1, br1)

    moe1 = _moe(st, 1, h4, _route(st, lg1), wg, wu, wd)
    y_prompt, y_sample = _final(st, x3, moe1, mods1, norm_f_w[None, :])
    return (y_prompt.reshape(bp, lp, D_MODEL), y_sample.reshape(bs, ls, D_MODEL), new_gla, new_c, new_n, new_m)
```

```python
import functools
import math

import jax
import jax.numpy as jnp
from jax import lax
from jax.experimental import pallas as pl
from jax.experimental.pallas import tpu as pltpu

F32 = jnp.float32
BF16 = jnp.bfloat16

D_MODEL = 1024
GRID_W = 64
H_A = 4
DK_A = D_MODEL // 16
DV_A = D_MODEL // 8
GK_RANK = 16
GATE_TEMP = 16.0
H_B = 4
DK_B = D_MODEL // 16
DV_B = D_MODEL // 8
CHUNK = 64
HY_ORDER = 2
HY_EMB = 33
HY_BANDS = (HY_EMB - 1) // 2
HY_FFN = 64
HY_TARGET = 1e-2
HY_MAX_DECAY = abs(math.log(HY_TARGET)) / 0.3
HY_MIN_DECAY = abs(math.log(HY_TARGET)) / 1.5
N_GROUPS = 4
EXP_PER_GROUP = 4
N_EXPERTS = N_GROUPS * EXP_PER_GROUP
D_EXPERT = D_MODEL // 2
N_MOD = 6
POS_THETA = 10000.0
EPS = 1e-6

_QA, _KA, _VA, _RA = 0, 256, 512, 1024
_QB, _KB, _VB, _OB = 1536, 1792, 2048, 2560
_REC_MAIN = 3072
_REC_GATE = 48
_MG0 = 2 * GK_RANK

LANES = 128
SUBLANES = 8
_RPV = D_MODEL // LANES
assert _RPV == SUBLANES
TM = 256
TM_TOK = 512
SB_MOE = 2048
ROW_PAD = 16
FFN_TILE = 288
LG_ROWS = 32
VMEM_LIMIT = 56 * 1024 * 1024


def _cparams(*sem):
    return pltpu.CompilerParams(dimension_semantics=sem, vmem_limit_bytes=VMEM_LIMIT)


def _bf(x):
    return x.astype(BF16)


def _dot(a, b):
    return jnp.dot(a, b, preferred_element_type=F32)


def _dot_nt(a, b):
    return lax.dot_general(a, b, (((1,), (1,)), ((), ())), preferred_element_type=F32)


def _dot_tn(a, b):
    return lax.dot_general(a, b, (((0,), (0,)), ((), ())), preferred_element_type=F32)


def _split2(x):
    hi = _bf(x)
    return hi, _bf(x - hi.astype(F32))


def _split3(x):
    hi = _bf(x)
    r = x - hi.astype(F32)
    mid = _bf(r)
    return hi, mid, _bf(r - mid.astype(F32))


def _dot_hp(a, b, dot=_dot):
    ah, al = _split2(a)
    bh, bl = _split2(b)
    return dot(ah, bh) + (dot(ah, bl) + dot(al, bh))


def _dot_mask_l(m, x, terms=3):
    if terms == 2:
        x1, x2 = _split2(x)
        return _dot(m, x1) + _dot(m, x2)
    x1, x2, x3 = _split3(x)
    return _dot(m, x1) + (_dot(m, x2) + _dot(m, x3))


def _dot_mask_r(x, m, terms=3):
    if terms == 2:
        x1, x2 = _split2(x)
        return _dot(x1, m) + _dot(x2, m)
    x1, x2, x3 = _split3(x)
    return _dot(x1, m) + (_dot(x2, m) + _dot(x3, m))


def _rms(x, w):
    return x * lax.rsqrt(jnp.mean(x * x, axis=-1, keepdims=True) + EPS) * w


def _mod_slices(mod):
    return [mod[:, k * D_MODEL:(k + 1) * D_MODEL] for k in range(N_MOD)]


def _store_rows_per_vreg(dst, h):
    rows = h.shape[0]
    for j in range(_RPV):
        dst[pl.ds(j, rows, stride=_RPV), :] = h[:, j * LANES:(j + 1) * LANES]


def _load_rows_per_vreg(src, rows):
    return jnp.concatenate([src[pl.ds(j, rows, stride=_RPV), :] for j in range(_RPV)], axis=-1)


def _router_logits(h, wrt_ref, brt_ref):
    return _dot_hp(wrt_ref[...], h, dot=_dot_nt) + brt_ref[...]


def _ada_kernel(cv_ref, w_ref, b_ref, o_ref):
    a = cv_ref[...]
    a = a * jax.nn.sigmoid(a)
    o_ref[...] = _dot_hp(a, w_ref[...]) + b_ref[...]


def _ada(cv, ada_w, ada_b):
    depth, d, n = ada_w.shape
    rows = cv.shape[0]
    tn = 768
    return pl.pallas_call(
        _ada_kernel,
        out_shape=jax.ShapeDtypeStruct((depth, rows, n), F32),
        grid=(depth, n // tn),
        in_specs=[
            pl.BlockSpec((rows, d), lambda l, j: (0, 0)),
            pl.BlockSpec((None, d, tn), lambda l, j: (l, 0, j)),
            pl.BlockSpec((None, 1, tn), lambda l, j: (l, 0, j)),
        ],
        out_specs=pl.BlockSpec((None, rows, tn), lambda l, j: (l, 0, j)),
        compiler_params=_cparams("arbitrary", "arbitrary"),
        name="ada_mod",
    )(cv, ada_w, ada_b.reshape(depth, 1, n))


class _Streams:
    def __init__(self, bp, lp, bs, ls):
        self.bp, self.lp, self.bs, self.ls = bp, lp, bs, ls
        self.tp, self.ts = bp * lp, bs * ls
        self.t = self.tp + self.ts
        assert lp % TM == 0 and ls % TM == 0 and self.tp % ls == 0
        self.sb = min(SB_MOE, math.gcd(self.tp, self.ts))
        assert self.t % self.sb == 0 and self.sb % TM == 0

    def mod_row(self, i, tm):
        ncb = self.tp // tm
        return jnp.where(i < ncb, 0, 1 + (i - ncb) // (self.ls // tm))


def _inproj_kernel(xp_ref, xs_ref, pos_ref, mod_ref, nw_ref, w_ref, wg_ref, wgt_ref,
                   x0_ref, main_ref, g_ref, g2_ref, mt_ref, *, ncb):
    i = pl.program_id(0)
    x = jnp.where(i < ncb, xp_ref[...], xs_ref[...] + pos_ref[...])
    x0_ref[...] = x
    sh1, sc1 = _mod_slices(mod_ref[...])[:2]
    h = _rms(x, nw_ref[...]) * (1.0 + sc1) + sh1
    main_ref[...] = _bf(_dot(_bf(h), w_ref[...]))
    gates = _dot_hp(h, wg_ref[...])
    g_ref[...] = gates[:, :LANES]
    g2_ref[...] = gates[:, LANES:]
    mt = _dot_hp(wgt_ref[...], h, dot=_dot_nt)
    for c in range(mt.shape[1] // CHUNK):
        piece = mt[:, c * CHUNK:(c + 1) * CHUNK]
        mt_ref[c] = jnp.concatenate([piece, piece], axis=1)


def _inproj(st, xp, xs, pos, mods, nw, w_main, w_gate, w_gate_t):
    tm = TM_TOK
    assert st.tp % tm == 0 and st.ls % tm == 0
    ncb = st.tp // tm
    bps = st.ls // tm
    t = st.t
    return pl.pallas_call(
        functools.partial(_inproj_kernel, ncb=ncb),
        out_shape=(jax.ShapeDtypeStruct((t, D_MODEL), F32),
                   jax.ShapeDtypeStruct((t, _REC_MAIN), BF16),
                   jax.ShapeDtypeStruct((t, LANES), F32),
                   jax.ShapeDtypeStruct((t, LANES), F32),
                   jax.ShapeDtypeStruct((t // CHUNK, 16, 2 * CHUNK), F32)),
        grid=(t // tm,),
        in_specs=[
            pl.BlockSpec((tm, D_MODEL), lambda i: (jnp.minimum(i, ncb - 1), 0)),
            pl.BlockSpec((tm, D_MODEL), lambda i: (jnp.maximum(i - ncb, 0), 0)),
            pl.BlockSpec((tm, D_MODEL), lambda i: (jnp.maximum(i - ncb, 0) % bps, 0)),
            pl.BlockSpec((None, 1, N_MOD * D_MODEL), lambda i: (st.mod_row(i, tm), 0, 0)),
            pl.BlockSpec((1, D_MODEL), lambda i: (0, 0)),
            pl.BlockSpec((D_MODEL, _REC_MAIN), lambda i: (0, 0)),
            pl.BlockSpec((D_MODEL, 2 * LANES), lambda i: (0, 0)),
            pl.BlockSpec((16, D_MODEL), lambda i: (0, 0)),
        ],
        out_specs=(pl.BlockSpec((tm, D_MODEL), lambda i: (i, 0)),
                   pl.BlockSpec((tm, _REC_MAIN), lambda i: (i, 0)),
                   pl.BlockSpec((tm, LANES), lambda i: (i, 0)),
                   pl.BlockSpec((tm, LANES), lambda i: (i, 0)),
                   pl.BlockSpec((tm // CHUNK, 16, 2 * CHUNK), lambda i: (i, 0, 0))),
        compiler_params=_cparams("arbitrary"),
        name="rec_inproj",
    )(xp, xs, pos, mods, nw, w_main, w_gate, w_gate_t)


def _block_diag(x):
    left = lax.broadcasted_iota(jnp.int32, (1, x.shape[1]), 1) < x.shape[1] // 2
    zero = jnp.zeros_like(x)
    return jnp.concatenate([jnp.where(left, x, zero), jnp.where(left, zero, x)], axis=0)


def _block_diag_mask(rows, width):
    r = lax.broadcasted_iota(jnp.int32, (rows, width), 0) < rows // 2
    l = lax.broadcasted_iota(jnp.int32, (rows, width), 1) < width // 2
    return r == l


def _running_max(x, reverse):
    n = x.shape[0]
    row = lax.broadcasted_iota(jnp.int32, x.shape, 0)
    sh = 1
    while sh < n:
        if reverse:
            y = jnp.where(row < n - sh, pltpu.roll(x, n - sh, 0), -jnp.inf)
        else:
            y = jnp.where(row >= sh, pltpu.roll(x, sh, 0), -jnp.inf)
        x = jnp.maximum(x, y)
        sh *= 2
    return x


def _scan_kernel(*refs, seq):
    (main_ref, g_ref, g2_ref, mt_ref, x_ref, mod_ref, sg_ref, sc_ref, sn_ref, sm_ref,
     gkw_ref, gkb_ref, gbi_ref, gbf_ref, gbcol_ref, gnw_ref, mnw_ref, wout_ref, n2w_ref, wrt_ref, brt_ref) = refs[:21]
    (x1_ref, h2_ref, lg_ref, og_ref, oc_ref, on_ref, om_ref,
     oa_scr, ob_scr, sbd_scr, cbd_scr, nbd_scr, mgl_scr) = refs[-13:]
    c = CHUNK
    nchunks = seq // c
    npair = H_A // 2
    assert H_A == H_B and DK_A == DK_B == c and DV_A == DV_B == LANES and 2 * DK_A == LANES

    lane = lax.broadcasted_iota(jnp.int32, (1, LANES), 1)
    gate_lane = lambda d, h: _MG0 + d * 8 + h
    used = tuple((lane >= gate_lane(d, 0)) & (lane < gate_lane(d, H_B)) for d in (0, 1))

    for d in (0, 1):
        for p in range(npair):
            k = d * npair + p
            z_s = jnp.zeros((DV_A, DK_A), F32)
            sbd_scr[k] = jnp.concatenate([jnp.concatenate([sg_ref[d, 2 * p], z_s], axis=1),
                                          jnp.concatenate([z_s, sg_ref[d, 2 * p + 1]], axis=1)], axis=0)
            z_c = jnp.zeros((DK_B, DV_B), F32)
            cbd_scr[k] = jnp.concatenate([jnp.concatenate([sc_ref[d, 2 * p], z_c], axis=1),
                                          jnp.concatenate([z_c, sc_ref[d, 2 * p + 1]], axis=1)], axis=0)
            n_rep = [jnp.broadcast_to(sn_ref[d * H_B + 2 * p + q:d * H_B + 2 * p + q + 1, :], (DV_B, DK_B)).T
                     for q in (0, 1)]
            nbd_scr[k] = jnp.concatenate([jnp.concatenate([n_rep[0], z_c], axis=1),
                                          jnp.concatenate([z_c, n_rep[1]], axis=1)], axis=0)
    m_gl = jnp.zeros((1, LANES), F32)
    for d in (0, 1):
        for h in range(H_B):
            r = d * H_B + h
            m_gl = jnp.where(lane == gate_lane(d, h), sm_ref[:, r:r + 1], m_gl)
    mgl_scr[...] = m_gl

    row_p = lax.broadcasted_iota(jnp.int32, (c, LANES), 0)
    s_p = lax.broadcasted_iota(jnp.int32, (c, LANES), 1) % c
    live_p = (s_p <= row_p, s_p >= row_p)
    row = lax.broadcasted_iota(jnp.int32, (c, c), 0)
    col = lax.broadcasted_iota(jnp.int32, (c, c), 1)
    tri = tuple(jnp.where(m, 1.0, 0.0).astype(BF16) for m in (col <= row, col >= row))
    tri_t2 = tuple(jnp.where(m, 1.0, 0.0).astype(BF16) for m in (row_p <= s_p, row_p >= s_p))
    grow = lax.broadcasted_iota(jnp.int32, (16, 1), 0)
    f_row = (grow % 8) >= 4
    scale_q = DK_A ** -0.5
    bd_val = _block_diag_mask(2 * DV_A, 2 * DK_A)
    bd_key = _block_diag_mask(2 * DK_B, 2 * DV_B)
    ones_bd = jnp.where(bd_key, 1.0, 0.0).astype(BF16)
    ones_cv = jnp.ones((c, 2 * DV_B), BF16)

    def replicate(d, width):
        r = lax.broadcasted_iota(jnp.int32, (LANES, H_B * width), 0)
        h = lax.broadcasted_iota(jnp.int32, (LANES, H_B * width), 1) // width
        return jnp.where(r == gate_lane(d, 0) + h, 1.0, 0.0).astype(BF16)

    rep_k = tuple(replicate(d, DK_B) for d in (0, 1))
    rep_v = tuple(replicate(d, DV_B) for d in (0, 1))

    def chunk_step(i, carry):
        rows, g_in, g2_in, mt_in, gla_in, mls_in = [], [], [], [], [], []
        for d in (0, 1):
            ci = i if d == 0 else nchunks - 1 - i
            rows.append(pl.ds(pl.multiple_of(ci * c, c), c))
            g_in.append(g_ref[rows[d], :])
            g2_in.append(g2_ref[rows[d], :])
            mt_in.append(mt_ref[ci])
            gla_in.append(main_ref[rows[d], _QA:_RA])
            mls_in.append(main_ref[rows[d], _QB:_OB])
        s_bd = [sbd_scr[k] for k in range(2 * npair)]
        c_bd = [cbd_scr[k] for k in range(2 * npair)]
        n_bd = [nbd_scr[k] for k in range(2 * npair)]
        m_gl = mgl_scr[...]
        dirs = (0, 1)
        ends = (c - 1, 0)
        pairs = [(d, p) for d in dirs for p in range(npair)]
        ks_of = lambda p: slice(p * 2 * DK_A, (p + 1) * 2 * DK_A)
        vs_of = lambda p: slice(p * 2 * DV_A, (p + 1) * 2 * DV_A)

        glin = [_dot_hp(g_in[d], gkw_ref[d]) + gkb_ref[d] for d in dirs]
        gi = [jnp.where(used[d], g_in[d] + gbi_ref[...], 0.0) for d in dirs]
        lf = [jnp.where(used[d], jax.nn.log_sigmoid(g2_in[d] + gbf_ref[...]), 0.0) for d in dirs]
        mt = [mt_in[d] + gbcol_ref[...] for d in dirs]
        mt = [jnp.where(f_row, jax.nn.log_sigmoid(mt[d]), mt[d]) for d in dirs]
        glog = [jax.nn.log_sigmoid(glin[d]) / GATE_TEMP for d in dirs]
        bc = [_dot_mask_l(tri[d], glog[d], 2) for d in dirs]
        cum = [_dot_mask_l(tri[d], lf[d], 2) for d in dirs]
        cum_t = [_dot_mask_r(mt[d][:, 0:c], tri_t2[d], 2) for d in dirs]
        b_end = [bc[d][ends[d]:ends[d] + 1, :] for d in dirs]
        qa = [gla_in[d][:, _QA:_QA + 256].astype(F32) * scale_q for d in dirs]
        ka = [gla_in[d][:, _KA:_KA + 256].astype(F32) for d in dirs]
        va = [gla_in[d][:, _VA:_VA + 512] for d in dirs]
        qe = [_bf(qa[d] * jnp.exp(bc[d])) for d in dirs]
        ke = [_bf(ka[d] * jnp.exp(-bc[d])) for d in dirs]
        kd = [_bf(ka[d] * jnp.exp(b_end[d] - bc[d])) for d in dirs]
        eb_end = [jnp.exp(b_end[d]) for d in dirs]
        qb = [mls_in[d][:, 0:256] * jnp.asarray(DK_B ** -0.5, BF16) for d in dirs]
        kb = [mls_in[d][:, _KB - _QB:_KB - _QB + 256] for d in dirs]
        vb = [mls_in[d][:, _VB - _QB:_VB - _QB + 512] for d in dirs]
        a_raw = {(d, p): _dot_nt(qe[d][:, ks_of(p)], _block_diag(ke[d][:, ks_of(p)])) for d, p in pairs}
        qk = {(d, p): _dot_nt(qb[d][:, ks_of(p)], _block_diag(kb[d][:, ks_of(p)])) for d, p in pairs}
        s_upd = {(d, p): _dot_tn(va[d][:, vs_of(p)], kd[d][:, ks_of(p)]) for d, p in pairs}
        o_car = {(d, p): _dot_nt(qe[d][:, ks_of(p)], _bf(s_bd[d * npair + p])) for d, p in pairs}
        m_loc = [cum[d] + _running_max(gi[d] - cum[d], reverse=(d == 1)) for d in dirs]
        inter = [cum[d] + m_gl for d in dirs]
        m_t = [jnp.maximum(inter[d], m_loc[d]) for d in dirs]
        b_last = [cum[d][ends[d]:ends[d] + 1, :] for d in dirs]
        dend = [b_last[d] - cum[d] + gi[d] for d in dirs]
        m_new = [jnp.maximum(b_last[d] + m_gl, jnp.max(dend[d], axis=0, keepdims=True)) for d in dirs]
        zero = jnp.zeros((c, LANES), F32)
        per_key = [jnp.concatenate([jnp.where(used[d], cum[d] - m_t[d], zero),
                                    jnp.where(used[d], jnp.exp(inter[d] - m_t[d]), zero),
                                    jnp.where(used[d], jnp.exp(dend[d] - m_new[d]), zero)], axis=0) for d in dirs]
        per_val = [jnp.concatenate([jnp.where(used[d], jnp.exp(-m_t[d]), zero),
                                    jnp.broadcast_to(jnp.where(used[d], jnp.exp(b_last[d] + m_gl - m_new[d]), 0.0),
                                                     (SUBLANES, LANES))], axis=0) for d in dirs]
        per_key = [_dot_mask_r(per_key[d], rep_k[d], 2) for d in dirs]
        per_val = [_dot_mask_r(per_val[d], rep_v[d], 2) for d in dirs]
        o_par = {(d, p): _dot(_bf(jnp.where(live_p[d], a_raw[d, p], 0.0)), _block_diag(va[d][:, vs_of(p)]))
                 for d, p in pairs}
        for d, p in pairs:
            k = d * npair + p
            s_bd[k] = s_bd[k] * eb_end[d][:, ks_of(p)] + jnp.where(bd_val, s_upd[d, p], 0.0)
        o_gla = [jnp.concatenate([o_par[d, p] + o_car[d, p] for p in range(npair)], axis=-1) for d in dirs]
        for d in dirs:
            m_gl = jnp.where(used[d], m_new[d], m_gl)

        w, qa2, kw = {}, {}, {}
        for d, p in pairs:
            ks = ks_of(p)
            r_i, r_f = d * 8 + 2 * p, d * 8 + 4 + 2 * p
            sub = [cum_t[d][r_f + q:r_f + q + 1, :] - mt[d][r_i + q:r_i + q + 1, :] for q in (0, 1)]
            sub = jnp.where(lane < DK_B, sub[0], sub[1])
            w[d, p] = jnp.exp(jnp.where(live_p[d], per_key[d][0:c, ks] - sub, -jnp.inf))
            qa2[d, p] = _bf(qb[d][:, ks].astype(F32) * per_key[d][c:2 * c, ks])
            kw[d, p] = _bf(kb[d][:, ks].astype(F32) * per_key[d][2 * c:3 * c, ks])
        carried = {(d, p): _dot(qa2[d, p], jnp.concatenate([_bf(c_bd[d * npair + p]), _bf(n_bd[d * npair + p])], axis=1))
                   for d, p in pairs}
        upd = {(d, p): _dot_tn(kw[d, p], jnp.concatenate([vb[d][:, vs_of(p)], ones_cv], axis=1)) for d, p in pairs}
        intra = {(d, p): _dot(_bf(qk[d, p] * w[d, p]),
                              jnp.concatenate([_block_diag(vb[d][:, vs_of(p)]), ones_bd], axis=1)) for d, p in pairs}
        outs = {}
        for d, p in pairs:
            k = d * npair + p
            vs = vs_of(p)
            num = intra[d, p][:, :2 * DV_B] + carried[d, p][:, :2 * DV_B]
            den = intra[d, p][:, 2 * DV_B:] + carried[d, p][:, 2 * DV_B:]
            outs[d, p] = num / jnp.maximum(jnp.abs(den), per_val[d][0:c, vs])
            decay = per_val[d][c:c + 1, vs]
            dec = jnp.concatenate([jnp.broadcast_to(jnp.concatenate([decay[:, q * DV_B:(q + 1) * DV_B]] * 2, axis=1),
                                                    (DK_B, 2 * DV_B)) for q in (0, 1)], axis=0)
            c_bd[k] = dec * c_bd[k] + jnp.where(bd_key, upd[d, p][:, :2 * DV_B], 0.0)
            n_bd[k] = dec * n_bd[k] + jnp.where(bd_key, upd[d, p][:, 2 * DV_B:], 0.0)
        o_mls = [jnp.concatenate([outs[d, p] for p in range(npair)], axis=-1) for d in dirs]

        for d in (0, 1):
            oa_scr[d, rows[d], :] = o_gla[d]
            ob_scr[d, rows[d], :] = o_mls[d]
        for k in range(2 * npair):
            sbd_scr[k] = s_bd[k]
            cbd_scr[k] = c_bd[k]
            nbd_scr[k] = n_bd[k]
        mgl_scr[...] = m_gl
        return carry

    lax.fori_loop(0, nchunks, chunk_step, 0)

    for d in (0, 1):
        for p in range(npair):
            k = d * npair + p
            s_t, c_f, n_f = sbd_scr[k], cbd_scr[k], nbd_scr[k]
            for q in (0, 1):
                h = 2 * p + q
                og_ref[d, h] = s_t[q * DV_A:(q + 1) * DV_A, q * DK_A:(q + 1) * DK_A]
                oc_ref[d, h] = c_f[q * DK_B:(q + 1) * DK_B, q * DV_B:(q + 1) * DV_B]
                n_t = n_f[q * DK_B:(q + 1) * DK_B, q * DV_B:(q + 1) * DV_B].T
                on_ref[d * H_B + h:d * H_B + h + 1, :] = n_t[0:1, :]
    m_gl = mgl_scr[...]
    om_ref[...] = jnp.concatenate([m_gl[:, gate_lane(d, h):gate_lane(d, h) + 1]
                                   for d in (0, 1) for h in range(H_B)], axis=1)

    mod = _mod_slices(mod_ref[...])
    g1, sh2, sc2 = mod[2], mod[3], mod[4]

    def out_step(j, carry):
        r0 = pl.multiple_of(j * TM, TM)
        rows = pl.ds(r0, TM)
        parts = []
        for h in range(H_A):
            vs = slice(h * DV_A, (h + 1) * DV_A)
            ra = main_ref[rows, _RA + h * DV_A:_RA + (h + 1) * DV_A].astype(F32)
            parts.append(_rms(oa_scr[0, rows, vs] + oa_scr[1, rows, vs], gnw_ref[...]) * (ra * jax.nn.sigmoid(ra)))
        for h in range(H_B):
            vs = slice(h * DV_B, (h + 1) * DV_B)
            ob = main_ref[rows, _OB + h * DV_B:_OB + (h + 1) * DV_B].astype(F32)
            parts.append(_rms(ob_scr[0, rows, vs] + ob_scr[1, rows, vs], mnw_ref[...]) * jax.nn.sigmoid(ob))
        mix = _dot(_bf(jnp.concatenate(parts, axis=-1)), wout_ref[...])
        x1 = x_ref[rows, :] + g1 * mix
        x1_ref[rows, :] = x1
        h2 = _rms(x1, n2w_ref[...]) * (1.0 + sc2) + sh2
        _store_rows_per_vreg(h2_ref.at[pl.ds(pl.multiple_of(r0 * _RPV, TM * _RPV), TM * _RPV)], h2)
        lg_ref[j] = _router_logits(h2, wrt_ref, brt_ref)
        return carry

    lax.fori_loop(0, seq // TM, out_step, 0)


def _scan(st, ctx, main, g, g2, mt, x, mods, states, consts, prev):
    seq = st.lp if ctx else st.ls
    nb = st.bp if ctx else st.bs
    off = 0 if ctx else st.tp // st.ls
    sg, sc, sn, sm = states
    full = lambda shape: pl.BlockSpec(shape, lambda b: (0,) * len(shape))
    in_specs = [
        pl.BlockSpec((seq, _REC_MAIN), lambda b: (off + b, 0)),
        pl.BlockSpec((seq, LANES), lambda b: (off + b, 0)),
        pl.BlockSpec((seq, LANES), lambda b: (off + b, 0)),
        pl.BlockSpec((seq // CHUNK, 16, 2 * CHUNK), lambda b: (off + b, 0, 0)),
        pl.BlockSpec((seq, D_MODEL), lambda b: (off + b, 0)),
    ]
    args = [main, g, g2, mt, x]
    mod_row = (lambda b: (0, 0, 0)) if ctx else (lambda b: (1 + b, 0, 0))
    in_specs += [
        pl.BlockSpec((None, 1, N_MOD * D_MODEL), mod_row),
        pl.BlockSpec((None, 2, H_A, DV_A, DK_A), lambda b: (b, 0, 0, 0, 0)),
        pl.BlockSpec((None, 2, H_B, DK_B, DV_B), lambda b: (b, 0, 0, 0, 0)),
        pl.BlockSpec((None, 2 * H_B, DK_B), lambda b: (b, 0, 0)),
        pl.BlockSpec((None, 1, 2 * H_B), lambda b: (b, 0, 0)),
    ]
    args += [mods, sg, sc, sn, sm]
    for a in consts:
        in_specs.append(full(a.shape))
        args.append(a)
    aliases = {}
    if prev is not None:
        aliases = {len(args) + k: k for k in range(3)}
        in_specs += [pl.BlockSpec(memory_space=pl.ANY)] * 3
        args += list(prev)
    out_shape = (jax.ShapeDtypeStruct((st.t, D_MODEL), F32),
                 jax.ShapeDtypeStruct((st.t * _RPV, LANES), F32),
                 jax.ShapeDtypeStruct((st.t // TM, LG_ROWS, TM), F32),
                 jax.ShapeDtypeStruct(sg.shape, F32), jax.ShapeDtypeStruct(sc.shape, F32),
                 jax.ShapeDtypeStruct(sn.shape, F32), jax.ShapeDtypeStruct(sm.shape, F32))
    out_specs = (pl.BlockSpec((seq, D_MODEL), lambda b: (off + b, 0)),
                 pl.BlockSpec((seq * _RPV, LANES), lambda b: (off + b, 0)),
                 pl.BlockSpec((seq // TM, LG_ROWS, TM), lambda b: (off + b, 0, 0)),
                 pl.BlockSpec((None, 2, H_A, DV_A, DK_A), lambda b: (b, 0, 0, 0, 0)),
                 pl.BlockSpec((None, 2, H_B, DK_B, DV_B), lambda b: (b, 0, 0, 0, 0)),
                 pl.BlockSpec((None, 2 * H_B, DK_B), lambda b: (b, 0, 0)),
                 pl.BlockSpec((None, 1, 2 * H_B), lambda b: (b, 0, 0)))
    return pl.pallas_call(
        functools.partial(_scan_kernel, seq=seq),
        out_shape=out_shape,
        grid=(nb,),
        in_specs=in_specs,
        out_specs=out_specs,
        scratch_shapes=[pltpu.VMEM((2, seq, H_A * DV_A), F32), pltpu.VMEM((2, seq, H_B * DV_B), F32),
                        pltpu.VMEM((H_A, 2 * DV_A, 2 * DK_A), F32), pltpu.VMEM((H_B, 2 * DK_B, 2 * DV_B), F32),
                        pltpu.VMEM((H_B, 2 * DK_B, 2 * DV_B), F32), pltpu.VMEM((1, LANES), F32)],
        input_output_aliases=aliases,
        compiler_params=_cparams("arbitrary"),
        name="rec_scan_ctx" if ctx else "rec_scan_lat",
    )(*args)


def _first_max(rows):
    m = rows[0]
    for r in rows[1:]:
        m = jnp.maximum(m, r)
    idx = jnp.full(m.shape, len(rows) - 1, jnp.int32)
    for k in range(len(rows) - 2, -1, -1):
        idx = jnp.where(rows[k] == m, k, idx)
    return m, idx


def _route_kernel(lg_ref, pos1_ref, pos2_ref, w1_ref, w2_ref, tab_ref, *, sb):
    lg = jnp.concatenate([lg_ref[b] for b in range(sb // TM)], axis=1)
    rows = [lg[k:k + 1, :] for k in range(N_EXPERTS + N_GROUPS)]
    grp = rows[N_EXPERTS:]
    gmax, gidx = _first_max(grp)
    p_group = 1.0 / sum(jnp.exp(r - gmax) for r in grp)
    e_in = []
    for k in range(EXP_PER_GROUP):
        v = rows[(N_GROUPS - 1) * EXP_PER_GROUP + k]
        for g in range(N_GROUPS - 2, -1, -1):
            v = jnp.where(gidx == g, rows[g * EXP_PER_GROUP + k], v)
        e_in.append(v)
    v1, i1 = _first_max(e_in)
    v2, i2 = _first_max([jnp.where(i1 == k, -jnp.inf, e_in[k]) for k in range(EXP_PER_GROUP)])
    ex = jnp.exp(v2 - v1)
    w1_ref[...] = p_group / (1.0 + ex)
    w2_ref[...] = p_group * ex / (1.0 + ex)
    x1 = gidx * EXP_PER_GROUP + i1
    x2 = gidx * EXP_PER_GROUP + i2

    eid = lax.broadcasted_iota(jnp.int32, (N_EXPERTS, sb), 0)
    sel = jnp.where((eid == x1) | (eid == x2), 1.0, 0.0)
    r_i = lax.broadcasted_iota(jnp.int32, (TM, TM), 0)
    c_i = lax.broadcasted_iota(jnp.int32, (TM, TM), 1)
    before = jnp.where(r_i < c_i, 1.0, 0.0).astype(BF16)
    carry = jnp.zeros((N_EXPERTS, 1), F32)
    ranks = []
    for b in range(sb // TM):
        s_b = sel[:, b * TM:(b + 1) * TM]
        ranks.append(_dot(_bf(s_b), before) + carry)
        carry = carry + jnp.sum(s_b, axis=1, keepdims=True)
    rank = jnp.concatenate(ranks, axis=1)
    shift = ROW_PAD.bit_length() - 1
    npad = jnp.left_shift(jnp.right_shift(carry.astype(jnp.int32) + (ROW_PAD - 1), shift), shift)

    lane = lax.broadcasted_iota(jnp.int32, (1, LANES), 1)
    tab = jnp.zeros((1, LANES), jnp.int32)
    pos1 = jnp.zeros((1, sb), F32)
    pos2 = jnp.zeros((1, sb), F32)
    off = jnp.zeros((1, 1), jnp.int32)
    for e in range(N_EXPERTS):
        n_e = npad[e:e + 1, :]
        tab = jnp.where(lane == e, off, tab)
        tab = jnp.where(lane == N_EXPERTS + e, n_e, tab)
        row = off.astype(F32) + rank[e:e + 1, :]
        pos1 = jnp.where(x1 == e, row, pos1)
        pos2 = jnp.where(x2 == e, row, pos2)
        off = off + n_e
    pos1_ref[...] = pos1.astype(jnp.int32) * _RPV
    pos2_ref[...] = pos2.astype(jnp.int32) * _RPV
    tab_ref[...] = tab


def _route(st, lg):
    sb = st.sb
    nsb = st.t // sb
    row_i = jax.ShapeDtypeStruct((nsb, 1, sb), jnp.int32)
    row_f = jax.ShapeDtypeStruct((nsb, 1, sb), F32)
    rspec = pl.BlockSpec((None, 1, sb), lambda s: (s, 0, 0))
    return pl.pallas_call(
        functools.partial(_route_kernel, sb=sb),
        out_shape=(row_i, row_i, row_f, row_f, jax.ShapeDtypeStruct((nsb, 1, LANES), jnp.int32)),
        grid=(nsb,),
        in_specs=[pl.BlockSpec((sb // TM, LG_ROWS, TM), lambda s: (s, 0, 0))],
        out_specs=(rspec, rspec, rspec, rspec, pl.BlockSpec((None, 1, LANES), lambda s: (s, 0, 0))),
        compiler_params=_cparams("arbitrary"),
        name="moe_route",
    )(lg)


def _moe_rows(sb):
    return 2 * sb + N_EXPERTS * ROW_PAD + FFN_TILE


def _moe_kernel(pos1_ref, pos2_ref, w1_ref, w2_ref, tab_ref, xr_ref, wg_ref, wu_ref, wd_ref, o_ref,
                rows_scr, stage_scr, wd_scr, *, sb):
    s = pl.program_id(0)
    e = pl.program_id(1)

    def tile_at(ref, r8):
        return ref.at[pl.ds(pl.multiple_of(r8, _RPV), _RPV)]

    @pl.when((s == 0) & (e == 0))
    def _():
        rows_scr[...] = jnp.zeros_like(rows_scr)

    @pl.when(e == 0)
    def _():
        def dispatch(t, carry):
            v = tile_at(xr_ref, t * _RPV)[...]
            tile_at(rows_scr, pos1_ref[0, t])[...] = v
            tile_at(rows_scr, pos2_ref[0, t])[...] = v
            return carry

        lax.fori_loop(0, sb, dispatch, 0, unroll=8)

    wd_scr[...] = _bf(wd_ref[...])

    def ffn_tile(r0, m, valid=None):
        win = rows_scr.at[pl.ds(pl.multiple_of(r0 * _RPV, ROW_PAD * _RPV), m * _RPV)]
        x = _load_rows_per_vreg(win, m)
        xb = _bf(x)
        hg = _dot(xb, wg_ref[...])
        hu = _dot(xb, wu_ref[...])
        y = _dot(_bf(hg * jax.nn.sigmoid(hg) * hu), wd_scr[...])
        if valid is not None:
            y = jnp.where(lax.broadcasted_iota(jnp.int32, (m, 1), 0) < valid, y, x)
        _store_rows_per_vreg(win, y)

    off = tab_ref[0, e]
    npad = tab_ref[0, N_EXPERTS + e]
    nfull = npad // FFN_TILE

    def full_tile(i, carry):
        ffn_tile(off + i * FFN_TILE, FFN_TILE)
        return carry

    lax.fori_loop(0, nfull, full_tile, 0)
    rem = npad - nfull * FFN_TILE
    last = off + nfull * FFN_TILE

    @pl.when((rem > 0) & (rem <= FFN_TILE // 2))
    def _():
        ffn_tile(last, FFN_TILE // 2, valid=rem)

    @pl.when(rem > FFN_TILE // 2)
    def _():
        ffn_tile(last, FFN_TILE, valid=rem)

    @pl.when(e == N_EXPERTS - 1)
    def _():
        for c in range(sb // TM):
            def combine(t, carry, c=c):
                tt = c * TM + t
                y = (w1_ref[0, tt] * tile_at(rows_scr, pos1_ref[0, tt])[...]
                     + w2_ref[0, tt] * tile_at(rows_scr, pos2_ref[0, tt])[...])
                tile_at(stage_scr, t * _RPV)[...] = y
                return carry

            lax.fori_loop(0, TM, combine, 0, unroll=8)
            o_ref[c * TM:(c + 1) * TM, :] = _bf(_load_rows_per_vreg(stage_scr, TM))


def _cast_kernel(x_ref, o_ref):
    o_ref[...] = _bf(x_ref[...])


def _expert_weights_bf16(w):
    depth, ne, a, b = w.shape
    spec = pl.BlockSpec((None, a, b), lambda i: (i, 0, 0))
    return pl.pallas_call(
        _cast_kernel,
        out_shape=jax.ShapeDtypeStruct((depth * ne, a, b), BF16),
        grid=(depth * ne,),
        in_specs=[spec],
        out_specs=spec,
        compiler_params=_cparams("arbitrary"),
        name="expert_weight_cast",
    )(w.reshape(depth * ne, a, b))


def _moe(st, layer, xr, route, wg, wu, wd):
    sb = st.sb
    e0 = layer * N_EXPERTS
    smem = lambda n: pl.BlockSpec((None, 1, n), lambda s, e: (s, 0, 0), memory_space=pltpu.SMEM)
    return pl.pallas_call(
        functools.partial(_moe_kernel, sb=sb),
        out_shape=jax.ShapeDtypeStruct((st.t, D_MODEL), BF16),
        grid=(st.t // sb, N_EXPERTS),
        in_specs=[
            smem(sb), smem(sb), smem(sb), smem(sb), smem(LANES),
            pl.BlockSpec((sb * _RPV, LANES), lambda s, e: (s, 0)),
            pl.BlockSpec((None, D_MODEL, D_EXPERT), lambda s, e: (e0 + e, 0, 0)),
            pl.BlockSpec((None, D_MODEL, D_EXPERT), lambda s, e: (e0 + e, 0, 0)),
            pl.BlockSpec((None, D_EXPERT, D_MODEL), lambda s, e: (e0 + e, 0, 0)),
        ],
        out_specs=pl.BlockSpec((sb, D_MODEL), lambda s, e: (s, 0)),
        scratch_shapes=[pltpu.VMEM((_moe_rows(sb) * _RPV, LANES), F32), pltpu.VMEM((TM * _RPV, LANES), F32),
                        pltpu.VMEM((D_EXPERT, D_MODEL), BF16)],
        compiler_params=_cparams("arbitrary", "arbitrary"),
        name="moe_ffn",
    )(*route, xr, wg, wu, wd)


def _hy_inproj_kernel(x_ref, m_ref, mod0_ref, mod_ref, nw_ref, w_ref, x2_ref, z_ref):
    g2 = _mod_slices(mod0_ref[...])[5]
    x2 = x_ref[...] + g2 * m_ref[...].astype(F32)
    x2_ref[...] = x2
    sh1, sc1 = _mod_slices(mod_ref[...])[:2]
    h = _rms(x2, nw_ref[...]) * (1.0 + sc1) + sh1
    z_ref[...] = _bf(_dot(_bf(h), w_ref[...]))


def _hy_inproj(st, x, moe, mods_prev, mods, nw, w):
    n = w.shape[1]
    tm = TM_TOK
    mspec = pl.BlockSpec((None, 1, N_MOD * D_MODEL), lambda i: (st.mod_row(i, tm), 0, 0))
    return pl.pallas_call(
        _hy_inproj_kernel,
        out_shape=(jax.ShapeDtypeStruct((st.t, D_MODEL), F32), jax.ShapeDtypeStruct((st.t, n), BF16)),
        grid=(st.t // tm,),
        in_specs=[
            pl.BlockSpec((tm, D_MODEL), lambda i: (i, 0)),
            pl.BlockSpec((tm, D_MODEL), lambda i: (i, 0)),
            mspec, mspec,
            pl.BlockSpec((1, D_MODEL), lambda i: (0, 0)),
            pl.BlockSpec((D_MODEL, n), lambda i: (0, 0)),
        ],
        out_specs=(pl.BlockSpec((tm, D_MODEL), lambda i: (i, 0)), pl.BlockSpec((tm, n), lambda i: (i, 0))),
        compiler_params=_cparams("arbitrary"),
        name="hy_inproj",
    )(x, moe, mods_prev, mods, nw, w)


def _dft_tables(seq):
    n2 = 2 * seq
    assert n2 & (n2 - 1) == 0
    tr = min(seq, TM)

    def table_kernel(cos_ref, msin_ref, cosb_ref, msinb_ref):
        k = lax.broadcasted_iota(jnp.int32, (tr, seq), 0) + pl.program_id(0) * tr
        n = lax.broadcasted_iota(jnp.int32, (tr, seq), 1)
        ang = ((k * n) & (n2 - 1)).astype(F32) * (2.0 * math.pi / n2)
        c, s = jnp.cos(ang), -jnp.sin(ang)
        cos_ref[...] = c
        msin_ref[...] = s
        cosb_ref[...] = _bf(c)
        msinb_ref[...] = _bf(s)

    spec = pl.BlockSpec((tr, seq), lambda i: (i, 0))
    f32, b16 = jax.ShapeDtypeStruct((seq, seq), F32), jax.ShapeDtypeStruct((seq, seq), BF16)
    return pl.pallas_call(table_kernel, out_shape=(f32, f32, b16, b16), grid=(seq // tr,),
                          out_specs=(spec, spec, spec, spec), compiler_params=_cparams("arbitrary"),
                          name=f"dft_tables_{seq}")()


def _alternating(shape):
    return jnp.where(lax.broadcasted_iota(jnp.int32, shape, 0) % 2 == 0, 1.0, -1.0)


def _hy_filter_kernel(emb_ref, dec_ref, w1_ref, b1_ref, f1_ref, w2_ref, b2_ref, f2_ref, w3_ref,
                      cos_ref, msin_ref, kr_ref, ki_ref, *, seq):
    h = jnp.sin(f1_ref[...] * (_dot_hp(emb_ref[...], w1_ref[...]) + b1_ref[...]))
    h = jnp.sin(f2_ref[...] * (_dot_hp(h, w2_ref[...]) + b2_ref[...]))
    dec = dec_ref[...]
    row0 = lax.broadcasted_iota(jnp.int32, dec.shape, 0) == 0
    alt = _alternating(dec.shape)
    cos, msin = cos_ref[...], msin_ref[...]
    scale = jnp.where(row0, 1.0, 2.0) / (2 * seq)
    for o in range(HY_ORDER):
        h_f = _dot_hp(h, w3_ref[:, 2 * o, :]) * dec
        h_b = jnp.where(row0, 0.0, _dot_hp(h, w3_ref[:, 2 * o + 1, :]) * dec)
        k_nyq = jnp.sum(alt * (h_f + h_b), axis=0, keepdims=True)
        kr_ref[o] = _dot_hp(cos, h_f + h_b) * scale
        ki_ref[o] = jnp.where(row0, k_nyq, _dot_hp(msin, h_f - h_b)) * scale


def _hy_filter(seq, w1, b1, f1, w2, b2, f2, w3, cos, msin):
    t = jnp.linspace(0.0, 1.0, seq, dtype=F32)[:, None]
    w = 2.0 * math.pi * jnp.arange(seq, dtype=F32)[:, None] / seq
    f = jnp.linspace(1e-4, HY_BANDS - 1, HY_BANDS, dtype=F32)[None, :]
    emb = jnp.concatenate([t, jnp.cos(f * w), -jnp.sin(f * w), jnp.zeros((seq, LANES - HY_EMB), F32)], axis=-1)
    decay = jnp.exp(-t * jnp.linspace(HY_MIN_DECAY, HY_MAX_DECAY, D_MODEL, dtype=F32)[None, :])
    w1p = jnp.concatenate([w1, jnp.zeros((LANES - HY_EMB, HY_FFN), F32)], axis=0)
    dblk = 256
    out = jax.ShapeDtypeStruct((HY_ORDER, seq, D_MODEL), F32)
    full = lambda shape: pl.BlockSpec(shape, lambda j: (0,) * len(shape))
    ospec = pl.BlockSpec((HY_ORDER, seq, dblk), lambda j: (0, 0, j))
    return pl.pallas_call(
        functools.partial(_hy_filter_kernel, seq=seq),
        out_shape=(out, out),
        grid=(D_MODEL // dblk,),
        in_specs=[
            full((seq, LANES)),
            pl.BlockSpec((seq, dblk), lambda j: (0, j)),
            full((LANES, HY_FFN)), full((1, HY_FFN)), full((1, HY_FFN)),
            full((HY_FFN, HY_FFN)), full((1, HY_FFN)), full((1, HY_FFN)),
            pl.BlockSpec((HY_FFN, 2 * HY_ORDER, dblk), lambda j: (0, 0, j)),
            full((seq, seq)), full((seq, seq)),
        ],
        out_specs=(ospec, ospec),
        compiler_params=_cparams("arbitrary"),
        name=f"hy_filter_{seq}",
    )(emb, decay, w1p, b1[None, :], f1[None, :], w2, b2[None, :], f2[None, :],
      w3.reshape(HY_FFN, 2 * HY_ORDER, D_MODEL), cos, msin)


def _hy_conv_kernel(*refs, seq, nseq):
    (zv_ref, z1_ref, z2_ref, cv_ref, c1_ref, c2_ref, kr_ref, ki_ref, bias_ref, cos_ref, msin_ref) = refs[:11]
    y_ref = refs[-1]
    dblk = y_ref.shape[1]
    t = lax.broadcasted_iota(jnp.int32, (seq, dblk), 0)
    first, last, row0 = t == 0, t == seq - 1, t == 0
    alt = _alternating((seq, dblk))
    seqs = range(nseq)
    rows = [slice(q * seq, (q + 1) * seq) for q in seqs]

    def short_conv(z_ref, c_ref):
        out = []
        for q in seqs:
            z = z_ref[rows[q], :].astype(F32)
            prev = jnp.where(first, 0.0, pltpu.roll(z, 1, 0))
            nxt = jnp.where(last, 0.0, pltpu.roll(z, seq - 1, 0))
            out.append(c_ref[0:1, :] * prev + c_ref[1:2, :] * z + c_ref[2:3, :] * nxt)
        return out

    def long_conv(s, o):
        sb = [_bf(s[q]) for q in seqs]
        x_re = [_dot(cos_ref[...], sb[q]) for q in seqs]
        x_im = [_dot(msin_ref[...], sb[q]) for q in seqs]
        x_im = [jnp.where(row0, jnp.sum(alt * s[q], axis=0, keepdims=True), x_im[q]) for q in seqs]
        k_re, k_im = kr_ref[o], ki_ref[o]
        y_re = [jnp.where(row0, x_re[q] * k_re, x_re[q] * k_re - x_im[q] * k_im) for q in seqs]
        y_im = [jnp.where(row0, x_im[q] * k_im, x_re[q] * k_im + x_im[q] * k_re) for q in seqs]
        y = [_dot(cos_ref[...], _bf(y_re[q])) + _dot(msin_ref[...], _bf(y_im[q])) for q in seqs]
        return [y[q] + alt * y_im[q][0:1, :] + bias_ref[o] * s[q] for q in seqs]

    v = short_conv(zv_ref, cv_ref)
    x1 = short_conv(z1_ref, c1_ref)
    x2 = short_conv(z2_ref, c2_ref)
    c1 = long_conv(v, 0)
    c2 = long_conv([x1[q] * c1[q] for q in seqs], 1)
    for q in seqs:
        y_ref[rows[q], :] = _bf(x2[q] * c2[q])


def _hy_conv(st, ctx, z, conv_w, filt, bias, tables, prev):
    seq = st.lp if ctx else st.ls
    nb = st.bp if ctx else st.bs
    nseq = max(1, min(nb, 4, 2048 // seq))
    assert nb % nseq == 0
    off = 0 if ctx else st.tp // (nseq * seq)
    dblk = 256
    nd = D_MODEL // dblk
    kr, ki = filt
    cos, msin = tables
    full = lambda shape: pl.BlockSpec(shape, lambda j, b: (0,) * len(shape))
    kspec = pl.BlockSpec((HY_ORDER, seq, dblk), lambda j, b: (0, 0, j))
    in_specs = [
        pl.BlockSpec((nseq * seq, dblk), lambda j, b: (off + b, j)),
        pl.BlockSpec((nseq * seq, dblk), lambda j, b: (off + b, nd + j)),
        pl.BlockSpec((nseq * seq, dblk), lambda j, b: (off + b, 2 * nd + j)),
        pl.BlockSpec((3, dblk), lambda j, b: (0, j)),
        pl.BlockSpec((3, dblk), lambda j, b: (0, nd + j)),
        pl.BlockSpec((3, dblk), lambda j, b: (0, 2 * nd + j)),
        kspec, kspec,
        pl.BlockSpec((HY_ORDER, 1, dblk), lambda j, b: (0, 0, j)),
        full((seq, seq)), full((seq, seq)),
    ]
    args = [z, z, z, conv_w, conv_w, conv_w, kr, ki, bias.reshape(HY_ORDER, 1, D_MODEL), cos, msin]
    aliases = {}
    if prev is not None:
        aliases = {len(args): 0}
        in_specs.append(pl.BlockSpec(memory_space=pl.ANY))
        args.append(prev)
    return pl.pallas_call(
        functools.partial(_hy_conv_kernel, seq=seq, nseq=nseq),
        out_shape=jax.ShapeDtypeStruct((st.t, D_MODEL), BF16),
        grid=(nd, nb // nseq),
        in_specs=in_specs,
        out_specs=pl.BlockSpec((nseq * seq, dblk), lambda j, b: (off + b, j)),
        input_output_aliases=aliases,
        compiler_params=_cparams("arbitrary", "arbitrary"),
        name="hy_conv_ctx" if ctx else "hy_conv_lat",
    )(*args)


def _hy_outproj_kernel(y_ref, x_ref, mod_ref, w_ref, n2w_ref, wrt_ref, brt_ref, x3_ref, h2_ref, lg_ref):
    mod = _mod_slices(mod_ref[...])
    g1, sh2, sc2 = mod[2], mod[3], mod[4]
    x3 = x_ref[...] + g1 * _dot(y_ref[...], w_ref[...])
    x3_ref[...] = x3
    h2 = _rms(x3, n2w_ref[...]) * (1.0 + sc2) + sh2
    _store_rows_per_vreg(h2_ref, h2)
    lg = _router_logits(h2, wrt_ref, brt_ref)
    for b in range(lg.shape[1] // TM):
        lg_ref[b] = lg[:, b * TM:(b + 1) * TM]


def _hy_outproj(st, y, x, mods, w, n2w, wrt, brt):
    tm = TM_TOK
    return pl.pallas_call(
        _hy_outproj_kernel,
        out_shape=(jax.ShapeDtypeStruct((st.t, D_MODEL), F32),
                   jax.ShapeDtypeStruct((st.t * _RPV, LANES), F32),
                   jax.ShapeDtypeStruct((st.t // TM, LG_ROWS, TM), F32)),
        grid=(st.t // tm,),
        in_specs=[
            pl.BlockSpec((tm, D_MODEL), lambda i: (i, 0)),
            pl.BlockSpec((tm, D_MODEL), lambda i: (i, 0)),
            pl.BlockSpec((None, 1, N_MOD * D_MODEL), lambda i: (st.mod_row(i, tm), 0, 0)),
            pl.BlockSpec((D_MODEL, D_MODEL), lambda i: (0, 0)),
            pl.BlockSpec((1, D_MODEL), lambda i: (0, 0)),
            pl.BlockSpec((LG_ROWS, D_MODEL), lambda i: (0, 0)),
            pl.BlockSpec((LG_ROWS, 1), lambda i: (0, 0)),
        ],
        out_specs=(pl.BlockSpec((tm, D_MODEL), lambda i: (i, 0)),
                   pl.BlockSpec((tm * _RPV, LANES), lambda i: (i, 0)),
                   pl.BlockSpec((tm // TM, LG_ROWS, TM), lambda i: (i, 0, 0))),
        compiler_params=_cparams("arbitrary"),
        name="hy_outproj",
    )(y, x, mods, w, n2w, wrt, brt)


def _final_kernel(x_ref, m_ref, mod_ref, nf_ref, yp_ref, ys_ref, *, ncb):
    i = pl.program_id(0)
    g2 = _mod_slices(mod_ref[...])[5]
    y = _rms(x_ref[...] + g2 * m_ref[...].astype(F32), nf_ref[...])

    @pl.when(i < ncb)
    def _():
        yp_ref[...] = y

    @pl.when(i >= ncb)
    def _():
        ys_ref[...] = y


def _final(st, x, moe, mods, nf):
    tm = TM_TOK
    ncb = st.tp // tm
    return pl.pallas_call(
        functools.partial(_final_kernel, ncb=ncb),
        out_shape=(jax.ShapeDtypeStruct((st.tp, D_MODEL), F32), jax.ShapeDtypeStruct((st.ts, D_MODEL), F32)),
        grid=(st.t // tm,),
        in_specs=[
            pl.BlockSpec((tm, D_MODEL), lambda i: (i, 0)),
            pl.BlockSpec((tm, D_MODEL), lambda i: (i, 0)),
            pl.BlockSpec((None, 1, N_MOD * D_MODEL), lambda i: (st.mod_row(i, tm), 0, 0)),
            pl.BlockSpec((1, D_MODEL), lambda i: (0, 0)),
        ],
        out_specs=(pl.BlockSpec((tm, D_MODEL), lambda i: (jnp.minimum(i, ncb - 1), 0)),
                   pl.BlockSpec((tm, D_MODEL), lambda i: (jnp.maximum(i - ncb, 0), 0))),
        compiler_params=_cparams("arbitrary"),
        name="final_norm",
    )(x, moe, mods, nf)


def _grid_pos_table(seq):
    rows = seq // GRID_W
    r, cl = jnp.meshgrid(jnp.arange(rows, dtype=F32), jnp.arange(GRID_W, dtype=F32), indexing='ij')
    quarter = D_MODEL // 4
    omega = POS_THETA ** (-jnp.arange(quarter, dtype=F32) / quarter)

    def enc(pos):
        a = pos.reshape(-1, 1) * omega[None, :]
        return jnp.concatenate([jnp.sin(a), jnp.cos(a)], axis=-1)

    return jnp.concatenate([enc(r), enc(cl)], axis=-1)


def _router_weights(w_group, b_group, w_router, b_router):
    pad = LG_ROWS - N_EXPERTS - N_GROUPS
    w = jnp.concatenate([w_router.T, w_group.T, jnp.zeros((pad, D_MODEL), F32)], axis=0)
    b = jnp.concatenate([b_router, b_group, jnp.zeros((pad,), F32)])[:, None]
    return w, b


def kernel(x_prompt, x_sample, state_gla, state_mlstm_c, state_mlstm_n, state_mlstm_m, c, c_ctx, norm1_w, norm2_w, norm_f_w, ada_w, ada_b, rec_w_in, gla_gk_w, gla_gk_b, mlstm_gate_b, gla_norm_w, mlstm_norm_w, rec_w_out, hy_w_in, hy_conv_w, hy_f_w1, hy_f_b1, hy_f_freq1, hy_f_w2, hy_f_b2, hy_f_freq2, hy_f_w3, hy_f_bias, hy_w_out, moe_w_group, moe_b_group, moe_w_router, moe_b_router, moe_w_gate, moe_w_up, moe_w_down):
    bp, lp, _ = x_prompt.shape
    bs, ls, _ = x_sample.shape
    st = _Streams(bp, lp, bs, ls)
    xp = x_prompt.reshape(st.tp, D_MODEL)
    xs = x_sample.reshape(st.ts, D_MODEL)

    nrow = -(-(1 + bs) // 8) * 8
    cv = jnp.concatenate([c_ctx[None, :], c, jnp.zeros((nrow - 1 - bs, D_MODEL), F32)], axis=0)
    mods = _ada(cv, ada_w, ada_b)
    mods0 = mods[0].reshape(nrow, 1, N_MOD * D_MODEL)
    mods1 = mods[1].reshape(nrow, 1, N_MOD * D_MODEL)

    pos = _grid_pos_table(ls)

    w_in = rec_w_in[0]
    w_main = _bf(w_in[:, :_REC_MAIN])
    w_mg = w_in[:, _REC_MAIN + _MG0:].reshape(D_MODEL, 2, 2, H_B)
    w_fg = jnp.pad(w_mg[:, :, 1, :], ((0, 0), (0, 0), (0, 8 - H_B))).reshape(D_MODEL, 16)
    w_gate = jnp.concatenate([w_in[:, _REC_MAIN:], jnp.zeros((D_MODEL, LANES - _REC_GATE), F32),
                              jnp.zeros((D_MODEL, _MG0), F32), w_fg,
                              jnp.zeros((D_MODEL, LANES - _REC_GATE), F32)], axis=1)
    w_gate_t = w_in[:, _REC_MAIN + _MG0:].T
    x0, main, g, g2, mt = _inproj(st, xp, xs, pos, mods0, norm1_w[0][None, :], w_main, w_gate, w_gate_t)

    gkw = jnp.zeros((2, LANES, H_A * DK_A), F32)
    gkw = gkw.at[0, :GK_RANK].set(gla_gk_w[0, 0]).at[1, GK_RANK:2 * GK_RANK].set(gla_gk_w[0, 1])
    gkb = gla_gk_b[0][:, None, :]
    gb = mlstm_gate_b[0]
    gate_row = lambda b: jnp.pad(jnp.pad(b, ((0, 0), (0, 8 - H_B))).reshape(1, 16), ((0, 0), (_MG0, LANES - _REC_GATE)))
    gbcol = gb.reshape(16, 1)
    wr0, br0 = _router_weights(moe_w_group[0], moe_b_group[0], moe_w_router[0], moe_b_router[0])
    consts = [gkw, gkb, gate_row(gb[:, 0]), gate_row(gb[:, 1]), gbcol, gla_norm_w[0][None, :],
              mlstm_norm_w[0][None, :], _bf(rec_w_out[0]), norm2_w[0][None, :], wr0, br0]

    def scan_states(sg, sc, sn, sm):
        nb = sg.shape[0]
        return (jnp.swapaxes(sg, -1, -2), sc, sn.reshape(nb, 2 * H_B, DK_B), sm.reshape(nb, 1, 2 * H_B))

    zero = (jnp.zeros((bp, 2, H_A, DK_A, DV_A), F32), jnp.zeros((bp, 2, H_B, DK_B, DV_B), F32),
            jnp.zeros((bp, 2, H_B, DK_B), F32), jnp.zeros((bp, 2, H_B), F32))
    cached = (state_gla[:, 0], state_mlstm_c[:, 0], state_mlstm_n[:, 0], state_mlstm_m[:, 0])
    x1, h2, lg, fg, fc, fn, fm = _scan(st, True, main, g, g2, mt, x0, mods0, scan_states(*zero), consts, None)
    x1, h2, lg = _scan(st, False, main, g, g2, mt, x0, mods0, scan_states(*cached), consts, (x1, h2, lg))[:3]

    new_gla = jnp.swapaxes(fg, -1, -2)[:, None]
    new_c = fc[:, None]
    new_n = fn.reshape(bp, 1, 2, H_B, DK_B)
    new_m = fm.reshape(bp, 1, 2, H_B)

    wg, wu = (_expert_weights_bf16(w) for w in (moe_w_gate, moe_w_up))
    wd = moe_w_down.reshape(-1, D_EXPERT, D_MODEL)
    moe0 = _moe(st, 0, h2, _route(st, lg), wg, wu, wd)

    x2, z = _hy_inproj(st, x1, moe0, mods0, mods1, norm1_w[1][None, :], _bf(hy_w_in[0]))
    y = None
    for ctx in (True, False):
        seq = lp if ctx else ls
        cos, msin, cos_b, msin_b = _dft_tables(seq)
        filt = _hy_filter(seq, hy_f_w1[0], hy_f_b1[0], hy_f_freq1[0], hy_f_w2[0], hy_f_b2[0], hy_f_freq2[0],
                          hy_f_w3[0], cos, msin)
        y = _hy_conv(st, ctx, z, hy_conv_w[0], filt, hy_f_bias[0], (cos_b, msin_b), y)
    wr1, br1 = _router_weights(moe_w_group[1], moe_b_group[1], moe_w_router[1], moe_b_router[1])
    x3, h4, lg1 = _hy_outproj(st, y, x2, mods1, _bf(hy_w_out[0]), norm2_w[1][None, :], wr1, br1)

    moe1 = _moe(st, 1, h4, _route(st, lg1), wg, wu, wd)
    y_prompt, y_sample = _final(st, x3, moe1, mods1, norm_f_w[None, :])
    return (y_prompt.reshape(bp, lp, D_MODEL), y_sample.reshape(bs, ls, D_MODEL), new_gla, new_c, new_n, new_m)
```

```python
import functools
import math

import jax
import jax.numpy as jnp
from jax import lax
from jax.experimental import pallas as pl
from jax.experimental.pallas import tpu as pltpu

F32 = jnp.float32
BF16 = jnp.bfloat16

D_MODEL = 1024
GRID_W = 64
H_A = 4
DK_A = D_MODEL // 16
DV_A = D_MODEL // 8
GK_RANK = 16
GATE_TEMP = 16.0
H_B = 4
DK_B = D_MODEL // 16
DV_B = D_MODEL // 8
CHUNK = 64
HY_ORDER = 2
HY_EMB = 33
HY_BANDS = (HY_EMB - 1) // 2
HY_FFN = 64
HY_TARGET = 1e-2
HY_MAX_DECAY = abs(math.log(HY_TARGET)) / 0.3
HY_MIN_DECAY = abs(math.log(HY_TARGET)) / 1.5
N_GROUPS = 4
EXP_PER_GROUP = 4
N_EXPERTS = N_GROUPS * EXP_PER_GROUP
D_EXPERT = D_MODEL // 2
N_MOD = 6
POS_THETA = 10000.0
EPS = 1e-6

_QA, _KA, _VA, _RA = 0, 256, 512, 1024
_QB, _KB, _VB, _OB = 1536, 1792, 2048, 2560
_REC_MAIN = 3072
_REC_GATE = 48
_MG0 = 2 * GK_RANK

LANES = 128
SUBLANES = 8
_RPV = D_MODEL // LANES
assert _RPV == SUBLANES
TM = 256
TM_TOK = 1024
TM_IN = 512
SB_MOE = 2048
ROW_PAD = 16
FFN_TILE = 288
LG_ROWS = 32
VMEM_LIMIT = 56 * 1024 * 1024


def _cparams(*sem):
    return pltpu.CompilerParams(dimension_semantics=sem, vmem_limit_bytes=VMEM_LIMIT)


def _bf(x):
    return x.astype(BF16)


def _dot(a, b):
    return jnp.dot(a, b, preferred_element_type=F32)


def _dot_nt(a, b):
    return lax.dot_general(a, b, (((1,), (1,)), ((), ())), preferred_element_type=F32)


def _dot_tn(a, b):
    return lax.dot_general(a, b, (((0,), (0,)), ((), ())), preferred_element_type=F32)


def _split2(x):
    hi = _bf(x)
    return hi, _bf(x - hi.astype(F32))


def _split3(x):
    hi = _bf(x)
    r = x - hi.astype(F32)
    mid = _bf(r)
    return hi, mid, _bf(r - mid.astype(F32))


def _dot_hp(a, b, dot=_dot):
    ah, al = _split2(a)
    bh, bl = _split2(b)
    return dot(ah, bh) + (dot(ah, bl) + dot(al, bh))


def _dot_mask_l(m, x, terms=3):
    if terms == 2:
        x1, x2 = _split2(x)
        return _dot(m, x1) + _dot(m, x2)
    x1, x2, x3 = _split3(x)
    return _dot(m, x1) + (_dot(m, x2) + _dot(m, x3))


def _dot_mask_r(x, m, terms=3):
    if terms == 2:
        x1, x2 = _split2(x)
        return _dot(x1, m) + _dot(x2, m)
    x1, x2, x3 = _split3(x)
    return _dot(x1, m) + (_dot(x2, m) + _dot(x3, m))


def _rms(x, w):
    return x * lax.rsqrt(jnp.mean(x * x, axis=-1, keepdims=True) + EPS) * w


def _mod_slices(mod):
    return [mod[:, k * D_MODEL:(k + 1) * D_MODEL] for k in range(N_MOD)]


def _store_rows_per_vreg(dst, h):
    rows = h.shape[0]
    for j in range(_RPV):
        dst[pl.ds(j, rows, stride=_RPV), :] = h[:, j * LANES:(j + 1) * LANES]


def _load_rows_per_vreg(src, rows):
    return jnp.concatenate([src[pl.ds(j, rows, stride=_RPV), :] for j in range(_RPV)], axis=-1)


def _router_logits(h, wrt_ref, brt_ref):
    return _dot_hp(wrt_ref[...], h, dot=_dot_nt) + brt_ref[...]


def _ada_kernel(cv_ref, w_ref, b_ref, o_ref):
    a = cv_ref[...]
    a = a * jax.nn.sigmoid(a)
    o_ref[...] = _dot_hp(a, w_ref[...]) + b_ref[...]


def _ada(cv, ada_w, ada_b):
    depth, d, n = ada_w.shape
    rows = cv.shape[0]
    tn = 768
    return pl.pallas_call(
        _ada_kernel,
        out_shape=jax.ShapeDtypeStruct((depth, rows, n), F32),
        grid=(depth, n // tn),
        in_specs=[
            pl.BlockSpec((rows, d), lambda l, j: (0, 0)),
            pl.BlockSpec((None, d, tn), lambda l, j: (l, 0, j)),
            pl.BlockSpec((None, 1, tn), lambda l, j: (l, 0, j)),
        ],
        out_specs=pl.BlockSpec((None, rows, tn), lambda l, j: (l, 0, j)),
        compiler_params=_cparams("arbitrary", "arbitrary"),
        name="ada_mod",
    )(cv, ada_w, ada_b.reshape(depth, 1, n))


class _Streams:
    def __init__(self, bp, lp, bs, ls):
        self.bp, self.lp, self.bs, self.ls = bp, lp, bs, ls
        self.tp, self.ts = bp * lp, bs * ls
        self.t = self.tp + self.ts
        assert lp % TM == 0 and ls % TM == 0 and self.tp % ls == 0
        assert self.tp % TM_TOK == 0 and ls % TM_TOK == 0
        self.sb = min(SB_MOE, math.gcd(self.tp, self.ts))
        assert self.t % self.sb == 0 and self.sb % TM == 0

    def mod_row(self, i, tm):
        ncb = self.tp // tm
        return jnp.where(i < ncb, 0, 1 + (i - ncb) // (self.ls // tm))


def _inproj_kernel(xp_ref, xs_ref, pos_ref, mod_ref, nw_ref, w_ref, wg_ref, wgt_ref,
                   x0_ref, main_ref, g_ref, g2_ref, mt_ref, *, ncb):
    i = pl.program_id(0)
    x = jnp.where(i < ncb, xp_ref[...], xs_ref[...] + pos_ref[...])
    x0_ref[...] = x
    sh1, sc1 = _mod_slices(mod_ref[...])[:2]
    h = _rms(x, nw_ref[...]) * (1.0 + sc1) + sh1
    main_ref[...] = _bf(_dot(_bf(h), w_ref[...]))
    gates = _dot_hp(h, wg_ref[...])
    g_ref[...] = gates[:, :LANES]
    g2_ref[...] = gates[:, LANES:]
    mt = _dot_hp(wgt_ref[...], h, dot=_dot_nt)
    for c in range(mt.shape[1] // CHUNK):
        piece = mt[:, c * CHUNK:(c + 1) * CHUNK]
        mt_ref[c] = jnp.concatenate([piece, piece], axis=1)


def _inproj(st, xp, xs, pos, mods, nw, w_main, w_gate, w_gate_t):
    tm = TM_IN
    assert st.tp % tm == 0 and st.ls % tm == 0
    ncb = st.tp // tm
    bps = st.ls // tm
    t = st.t
    return pl.pallas_call(
        functools.partial(_inproj_kernel, ncb=ncb),
        out_shape=(jax.ShapeDtypeStruct((t, D_MODEL), F32),
                   jax.ShapeDtypeStruct((t, _REC_MAIN), BF16),
                   jax.ShapeDtypeStruct((t, LANES), F32),
                   jax.ShapeDtypeStruct((t, LANES), F32),
                   jax.ShapeDtypeStruct((t // CHUNK, 16, 2 * CHUNK), F32)),
        grid=(t // tm,),
        in_specs=[
            pl.BlockSpec((tm, D_MODEL), lambda i: (jnp.minimum(i, ncb - 1), 0)),
            pl.BlockSpec((tm, D_MODEL), lambda i: (jnp.maximum(i - ncb, 0), 0)),
            pl.BlockSpec((tm, D_MODEL), lambda i: (jnp.maximum(i - ncb, 0) % bps, 0)),
            pl.BlockSpec((None, 1, N_MOD * D_MODEL), lambda i: (st.mod_row(i, tm), 0, 0)),
            pl.BlockSpec((1, D_MODEL), lambda i: (0, 0)),
            pl.BlockSpec((D_MODEL, _REC_MAIN), lambda i: (0, 0)),
            pl.BlockSpec((D_MODEL, 2 * LANES), lambda i: (0, 0)),
            pl.BlockSpec((16, D_MODEL), lambda i: (0, 0)),
        ],
        out_specs=(pl.BlockSpec((tm, D_MODEL), lambda i: (i, 0)),
                   pl.BlockSpec((tm, _REC_MAIN), lambda i: (i, 0)),
                   pl.BlockSpec((tm, LANES), lambda i: (i, 0)),
                   pl.BlockSpec((tm, LANES), lambda i: (i, 0)),
                   pl.BlockSpec((tm // CHUNK, 16, 2 * CHUNK), lambda i: (i, 0, 0))),
        compiler_params=_cparams("arbitrary"),
        name="rec_inproj",
    )(xp, xs, pos, mods, nw, w_main, w_gate, w_gate_t)


def _block_diag(x):
    left = lax.broadcasted_iota(jnp.int32, (1, x.shape[1]), 1) < x.shape[1] // 2
    zero = jnp.zeros_like(x)
    return jnp.concatenate([jnp.where(left, x, zero), jnp.where(left, zero, x)], axis=0)


def _block_diag_mask(rows, width):
    r = lax.broadcasted_iota(jnp.int32, (rows, width), 0) < rows // 2
    l = lax.broadcasted_iota(jnp.int32, (rows, width), 1) < width // 2
    return r == l


def _running_max(x, reverse):
    n = x.shape[0]
    row = lax.broadcasted_iota(jnp.int32, x.shape, 0)
    sh = 1
    while sh < n:
        if reverse:
            y = jnp.where(row < n - sh, pltpu.roll(x, n - sh, 0), -jnp.inf)
        else:
            y = jnp.where(row >= sh, pltpu.roll(x, sh, 0), -jnp.inf)
        x = jnp.maximum(x, y)
        sh *= 2
    return x


def _scan_kernel(*refs, seq):
    (main_ref, g_ref, g2_ref, mt_ref, x_ref, mod_ref, sg_ref, sc_ref, sn_ref, sm_ref,
     gkw_ref, gkb_ref, gbi_ref, gbf_ref, gbcol_ref, gnw_ref, mnw_ref, wout_ref, n2w_ref, wrt_ref, brt_ref) = refs[:21]
    (x1_ref, h2_ref, lg_ref, og_ref, oc_ref, on_ref, om_ref,
     oa_scr, ob_scr, sbd_scr, cbd_scr, nbd_scr, mgl_scr) = refs[-13:]
    c = CHUNK
    nchunks = seq // c
    npair = H_A // 2
    assert H_A == H_B and DK_A == DK_B == c and DV_A == DV_B == LANES and 2 * DK_A == LANES

    lane = lax.broadcasted_iota(jnp.int32, (1, LANES), 1)
    gate_lane = lambda d, h: _MG0 + d * 8 + h
    used = tuple((lane >= gate_lane(d, 0)) & (lane < gate_lane(d, H_B)) for d in (0, 1))

    for d in (0, 1):
        for p in range(npair):
            k = d * npair + p
            z_s = jnp.zeros((DV_A, DK_A), F32)
            sbd_scr[k] = jnp.concatenate([jnp.concatenate([sg_ref[d, 2 * p], z_s], axis=1),
                                          jnp.concatenate([z_s, sg_ref[d, 2 * p + 1]], axis=1)], axis=0)
            z_c = jnp.zeros((DK_B, DV_B), F32)
            cbd_scr[k] = jnp.concatenate([jnp.concatenate([sc_ref[d, 2 * p], z_c], axis=1),
                                          jnp.concatenate([z_c, sc_ref[d, 2 * p + 1]], axis=1)], axis=0)
            n_rep = [jnp.broadcast_to(sn_ref[d * H_B + 2 * p + q:d * H_B + 2 * p + q + 1, :], (DV_B, DK_B)).T
                     for q in (0, 1)]
            nbd_scr[k] = jnp.concatenate([jnp.concatenate([n_rep[0], z_c], axis=1),
                                          jnp.concatenate([z_c, n_rep[1]], axis=1)], axis=0)
    m_gl = jnp.zeros((1, LANES), F32)
    for d in (0, 1):
        for h in range(H_B):
            r = d * H_B + h
            m_gl = jnp.where(lane == gate_lane(d, h), sm_ref[:, r:r + 1], m_gl)
    mgl_scr[...] = m_gl

    row_p = lax.broadcasted_iota(jnp.int32, (c, LANES), 0)
    s_p = lax.broadcasted_iota(jnp.int32, (c, LANES), 1) % c
    live_p = (s_p <= row_p, s_p >= row_p)
    row = lax.broadcasted_iota(jnp.int32, (c, c), 0)
    col = lax.broadcasted_iota(jnp.int32, (c, c), 1)
    tri = tuple(jnp.where(m, 1.0, 0.0).astype(BF16) for m in (col <= row, col >= row))
    tri_t2 = tuple(jnp.where(m, 1.0, 0.0).astype(BF16) for m in (row_p <= s_p, row_p >= s_p))
    grow = lax.broadcasted_iota(jnp.int32, (16, 1), 0)
    f_row = (grow % 8) >= 4
    scale_q = DK_A ** -0.5
    bd_val = _block_diag_mask(2 * DV_A, 2 * DK_A)
    bd_key = _block_diag_mask(2 * DK_B, 2 * DV_B)
    ones_bd = jnp.where(bd_key, 1.0, 0.0).astype(BF16)
    ones_cv = jnp.ones((c, 2 * DV_B), BF16)

    def replicate(d, width):
        r = lax.broadcasted_iota(jnp.int32, (LANES, H_B * width), 0)
        h = lax.broadcasted_iota(jnp.int32, (LANES, H_B * width), 1) // width
        return jnp.where(r == gate_lane(d, 0) + h, 1.0, 0.0).astype(BF16)

    rep_k = tuple(replicate(d, DK_B) for d in (0, 1))
    rep_v = tuple(replicate(d, DV_B) for d in (0, 1))

    def chunk_step(i, carry):
        rows, g_in, g2_in, mt_in, gla_in, mls_in = [], [], [], [], [], []
        for d in (0, 1):
            ci = i if d == 0 else nchunks - 1 - i
            rows.append(pl.ds(pl.multiple_of(ci * c, c), c))
            g_in.append(g_ref[rows[d], :])
            g2_in.append(g2_ref[rows[d], :])
            mt_in.append(mt_ref[ci])
            gla_in.append(main_ref[rows[d], _QA:_RA])
            mls_in.append(main_ref[rows[d], _QB:_OB])
        s_bd = [sbd_scr[k] for k in range(2 * npair)]
        c_bd = [cbd_scr[k] for k in range(2 * npair)]
        n_bd = [nbd_scr[k] for k in range(2 * npair)]
        m_gl = mgl_scr[...]
        dirs = (0, 1)
        ends = (c - 1, 0)
        pairs = [(d, p) for d in dirs for p in range(npair)]
        ks_of = lambda p: slice(p * 2 * DK_A, (p + 1) * 2 * DK_A)
        vs_of = lambda p: slice(p * 2 * DV_A, (p + 1) * 2 * DV_A)

        glin = [_dot_hp(g_in[d], gkw_ref[d]) + gkb_ref[d] for d in dirs]
        gi = [jnp.where(used[d], g_in[d] + gbi_ref[...], 0.0) for d in dirs]
        lf = [jnp.where(used[d], jax.nn.log_sigmoid(g2_in[d] + gbf_ref[...]), 0.0) for d in dirs]
        mt = [mt_in[d] + gbcol_ref[...] for d in dirs]
        mt = [jnp.where(f_row, jax.nn.log_sigmoid(mt[d]), mt[d]) for d in dirs]
        glog = [jax.nn.log_sigmoid(glin[d]) / GATE_TEMP for d in dirs]
        bc = [_dot_mask_l(tri[d], glog[d], 2) for d in dirs]
        cum = [_dot_mask_l(tri[d], lf[d], 2) for d in dirs]
        cum_t = [_dot_mask_r(mt[d][:, 0:c], tri_t2[d], 2) for d in dirs]
        b_end = [bc[d][ends[d]:ends[d] + 1, :] for d in dirs]
        qa = [gla_in[d][:, _QA:_QA + 256].astype(F32) * scale_q for d in dirs]
        ka = [gla_in[d][:, _KA:_KA + 256].astype(F32) for d in dirs]
        va = [gla_in[d][:, _VA:_VA + 512] for d in dirs]
        qe = [_bf(qa[d] * jnp.exp(bc[d])) for d in dirs]
        ke = [_bf(ka[d] * jnp.exp(-bc[d])) for d in dirs]
        kd = [_bf(ka[d] * jnp.exp(b_end[d] - bc[d])) for d in dirs]
        eb_end = [jnp.exp(b_end[d]) for d in dirs]
        qb = [mls_in[d][:, 0:256] * jnp.asarray(DK_B ** -0.5, BF16) for d in dirs]
        kb = [mls_in[d][:, _KB - _QB:_KB - _QB + 256] for d in dirs]
        vb = [mls_in[d][:, _VB - _QB:_VB - _QB + 512] for d in dirs]
        a_raw = {(d, p): _dot_nt(qe[d][:, ks_of(p)], _block_diag(ke[d][:, ks_of(p)])) for d, p in pairs}
        qk = {(d, p): _dot_nt(qb[d][:, ks_of(p)], _block_diag(kb[d][:, ks_of(p)])) for d, p in pairs}
        s_upd = {(d, p): _dot_tn(va[d][:, vs_of(p)], kd[d][:, ks_of(p)]) for d, p in pairs}
        o_car = {(d, p): _dot_nt(qe[d][:, ks_of(p)], _bf(s_bd[d * npair + p])) for d, p in pairs}
        m_loc = [cum[d] + _running_max(gi[d] - cum[d], reverse=(d == 1)) for d in dirs]
        inter = [cum[d] + m_gl for d in dirs]
        m_t = [jnp.maximum(inter[d], m_loc[d]) for d in dirs]
        b_last = [cum[d][ends[d]:ends[d] + 1, :] for d in dirs]
        dend = [b_last[d] - cum[d] + gi[d] for d in dirs]
        m_new = [jnp.maximum(b_last[d] + m_gl, jnp.max(dend[d], axis=0, keepdims=True)) for d in dirs]
        zero = jnp.zeros((c, LANES), F32)
        per_key = [jnp.concatenate([jnp.where(used[d], cum[d] - m_t[d], zero),
                                    jnp.where(used[d], jnp.exp(inter[d] - m_t[d]), zero),
                                    jnp.where(used[d], jnp.exp(dend[d] - m_new[d]), zero)], axis=0) for d in dirs]
        per_val = [jnp.concatenate([jnp.where(used[d], jnp.exp(-m_t[d]), zero),
                                    jnp.broadcast_to(jnp.where(used[d], jnp.exp(b_last[d] + m_gl - m_new[d]), 0.0),
                                                     (SUBLANES, LANES))], axis=0) for d in dirs]
        per_key = [_dot_mask_r(per_key[d], rep_k[d], 2) for d in dirs]
        per_val = [_dot_mask_r(per_val[d], rep_v[d], 2) for d in dirs]
        o_par = {(d, p): _dot(_bf(jnp.where(live_p[d], a_raw[d, p], 0.0)), _block_diag(va[d][:, vs_of(p)]))
                 for d, p in pairs}
        for d, p in pairs:
            k = d * npair + p
            s_bd[k] = s_bd[k] * eb_end[d][:, ks_of(p)] + jnp.where(bd_val, s_upd[d, p], 0.0)
        o_gla = [jnp.concatenate([o_par[d, p] + o_car[d, p] for p in range(npair)], axis=-1) for d in dirs]
        for d in dirs:
            m_gl = jnp.where(used[d], m_new[d], m_gl)

        w, qa2, kw = {}, {}, {}
        for d, p in pairs:
            ks = ks_of(p)
            r_i, r_f = d * 8 + 2 * p, d * 8 + 4 + 2 * p
            sub = [cum_t[d][r_f + q:r_f + q + 1, :] - mt[d][r_i + q:r_i + q + 1, :] for q in (0, 1)]
            sub = jnp.where(lane < DK_B, sub[0], sub[1])
            w[d, p] = jnp.exp(jnp.where(live_p[d], per_key[d][0:c, ks] - sub, -jnp.inf))
            qa2[d, p] = _bf(qb[d][:, ks].astype(F32) * per_key[d][c:2 * c, ks])
            kw[d, p] = _bf(kb[d][:, ks].astype(F32) * per_key[d][2 * c:3 * c, ks])
        carried = {(d, p): _dot(qa2[d, p], jnp.concatenate([_bf(c_bd[d * npair + p]), _bf(n_bd[d * npair + p])], axis=1))
                   for d, p in pairs}
        upd = {(d, p): _dot_tn(kw[d, p], jnp.concatenate([vb[d][:, vs_of(p)], ones_cv], axis=1)) for d, p in pairs}
        intra = {(d, p): _dot(_bf(qk[d, p] * w[d, p]),
                              jnp.concatenate([_block_diag(vb[d][:, vs_of(p)]), ones_bd], axis=1)) for d, p in pairs}
        outs = {}
        for d, p in pairs:
            k = d * npair + p
            vs = vs_of(p)
            num = intra[d, p][:, :2 * DV_B] + carried[d, p][:, :2 * DV_B]
            den = intra[d, p][:, 2 * DV_B:] + carried[d, p][:, 2 * DV_B:]
            outs[d, p] = num / jnp.maximum(jnp.abs(den), per_val[d][0:c, vs])
            decay = per_val[d][c:c + 1, vs]
            dec = jnp.concatenate([jnp.broadcast_to(jnp.concatenate([decay[:, q * DV_B:(q + 1) * DV_B]] * 2, axis=1),
                                                    (DK_B, 2 * DV_B)) for q in (0, 1)], axis=0)
            c_bd[k] = dec * c_bd[k] + jnp.where(bd_key, upd[d, p][:, :2 * DV_B], 0.0)
            n_bd[k] = dec * n_bd[k] + jnp.where(bd_key, upd[d, p][:, 2 * DV_B:], 0.0)
        o_mls = [jnp.concatenate([outs[d, p] for p in range(npair)], axis=-1) for d in dirs]

        for d in (0, 1):
            oa_scr[d, rows[d], :] = o_gla[d]
            ob_scr[d, rows[d], :] = o_mls[d]
        for k in range(2 * npair):
            sbd_scr[k] = s_bd[k]
            cbd_scr[k] = c_bd[k]
            nbd_scr[k] = n_bd[k]
        mgl_scr[...] = m_gl
        return carry

    lax.fori_loop(0, nchunks, chunk_step, 0)

    for d in (0, 1):
        for p in range(npair):
            k = d * npair + p
            s_t, c_f, n_f = sbd_scr[k], cbd_scr[k], nbd_scr[k]
            for q in (0, 1):
                h = 2 * p + q
                og_ref[d, h] = s_t[q * DV_A:(q + 1) * DV_A, q * DK_A:(q + 1) * DK_A]
                oc_ref[d, h] = c_f[q * DK_B:(q + 1) * DK_B, q * DV_B:(q + 1) * DV_B]
                n_t = n_f[q * DK_B:(q + 1) * DK_B, q * DV_B:(q + 1) * DV_B].T
                on_ref[d * H_B + h:d * H_B + h + 1, :] = n_t[0:1, :]
    m_gl = mgl_scr[...]
    om_ref[...] = jnp.concatenate([m_gl[:, gate_lane(d, h):gate_lane(d, h) + 1]
                                   for d in (0, 1) for h in range(H_B)], axis=1)

    mod = _mod_slices(mod_ref[...])
    g1, sh2, sc2 = mod[2], mod[3], mod[4]

    def out_step(j, carry):
        r0 = pl.multiple_of(j * TM, TM)
        rows = pl.ds(r0, TM)
        parts = []
        for h in range(H_A):
            vs = slice(h * DV_A, (h + 1) * DV_A)
            ra = main_ref[rows, _RA + h * DV_A:_RA + (h + 1) * DV_A].astype(F32)
            parts.append(_rms(oa_scr[0, rows, vs] + oa_scr[1, rows, vs], gnw_ref[...]) * (ra * jax.nn.sigmoid(ra)))
        for h in range(H_B):
            vs = slice(h * DV_B, (h + 1) * DV_B)
            ob = main_ref[rows, _OB + h * DV_B:_OB + (h + 1) * DV_B].astype(F32)
            parts.append(_rms(ob_scr[0, rows, vs] + ob_scr[1, rows, vs], mnw_ref[...]) * jax.nn.sigmoid(ob))
        mix = _dot(_bf(jnp.concatenate(parts, axis=-1)), wout_ref[...])
        x1 = x_ref[rows, :] + g1 * mix
        x1_ref[rows, :] = x1
        h2 = _rms(x1, n2w_ref[...]) * (1.0 + sc2) + sh2
        _store_rows_per_vreg(h2_ref.at[pl.ds(pl.multiple_of(r0 * _RPV, TM * _RPV), TM * _RPV)], h2)
        lg_ref[j] = _router_logits(h2, wrt_ref, brt_ref)
        return carry

    lax.fori_loop(0, seq // TM, out_step, 0)


def _scan(st, ctx, main, g, g2, mt, x, mods, states, consts, prev):
    seq = st.lp if ctx else st.ls
    nb = st.bp if ctx else st.bs
    off = 0 if ctx else st.tp // st.ls
    sg, sc, sn, sm = states
    full = lambda shape: pl.BlockSpec(shape, lambda b: (0,) * len(shape))
    in_specs = [
        pl.BlockSpec((seq, _REC_MAIN), lambda b: (off + b, 0)),
        pl.BlockSpec((seq, LANES), lambda b: (off + b, 0)),
        pl.BlockSpec((seq, LANES), lambda b: (off + b, 0)),
        pl.BlockSpec((seq // CHUNK, 16, 2 * CHUNK), lambda b: (off + b, 0, 0)),
        pl.BlockSpec((seq, D_MODEL), lambda b: (off + b, 0)),
    ]
    args = [main, g, g2, mt, x]
    mod_row = (lambda b: (0, 0, 0)) if ctx else (lambda b: (1 + b, 0, 0))
    in_specs += [
        pl.BlockSpec((None, 1, N_MOD * D_MODEL), mod_row),
        pl.BlockSpec((None, 2, H_A, DV_A, DK_A), lambda b: (b, 0, 0, 0, 0)),
        pl.BlockSpec((None, 2, H_B, DK_B, DV_B), lambda b: (b, 0, 0, 0, 0)),
        pl.BlockSpec((None, 2 * H_B, DK_B), lambda b: (b, 0, 0)),
        pl.BlockSpec((None, 1, 2 * H_B), lambda b: (b, 0, 0)),
    ]
    args += [mods, sg, sc, sn, sm]
    for a in consts:
        in_specs.append(full(a.shape))
        args.append(a)
    aliases = {}
    if prev is not None:
        aliases = {len(args) + k: k for k in range(3)}
        in_specs += [pl.BlockSpec(memory_space=pl.ANY)] * 3
        args += list(prev)
    out_shape = (jax.ShapeDtypeStruct((st.t, D_MODEL), F32),
                 jax.ShapeDtypeStruct((st.t * _RPV, LANES), F32),
                 jax.ShapeDtypeStruct((st.t // TM, LG_ROWS, TM), F32),
                 jax.ShapeDtypeStruct(sg.shape, F32), jax.ShapeDtypeStruct(sc.shape, F32),
                 jax.ShapeDtypeStruct(sn.shape, F32), jax.ShapeDtypeStruct(sm.shape, F32))
    out_specs = (pl.BlockSpec((seq, D_MODEL), lambda b: (off + b, 0)),
                 pl.BlockSpec((seq * _RPV, LANES), lambda b: (off + b, 0)),
                 pl.BlockSpec((seq // TM, LG_ROWS, TM), lambda b: (off + b, 0, 0)),
                 pl.BlockSpec((None, 2, H_A, DV_A, DK_A), lambda b: (b, 0, 0, 0, 0)),
                 pl.BlockSpec((None, 2, H_B, DK_B, DV_B), lambda b: (b, 0, 0, 0, 0)),
                 pl.BlockSpec((None, 2 * H_B, DK_B), lambda b: (b, 0, 0)),
                 pl.BlockSpec((None, 1, 2 * H_B), lambda b: (b, 0, 0)))
    return pl.pallas_call(
        functools.partial(_scan_kernel, seq=seq),
        out_shape=out_shape,
        grid=(nb,),
        in_specs=in_specs,
        out_specs=out_specs,
        scratch_shapes=[pltpu.VMEM((2, seq, H_A * DV_A), F32), pltpu.VMEM((2, seq, H_B * DV_B), F32),
                        pltpu.VMEM((H_A, 2 * DV_A, 2 * DK_A), F32), pltpu.VMEM((H_B, 2 * DK_B, 2 * DV_B), F32),
                        pltpu.VMEM((H_B, 2 * DK_B, 2 * DV_B), F32), pltpu.VMEM((1, LANES), F32)],
        input_output_aliases=aliases,
        compiler_params=_cparams("arbitrary"),
        name="rec_scan_ctx" if ctx else "rec_scan_lat",
    )(*args)


def _first_max(rows):
    m = rows[0]
    for r in rows[1:]:
        m = jnp.maximum(m, r)
    idx = jnp.full(m.shape, len(rows) - 1, jnp.int32)
    for k in range(len(rows) - 2, -1, -1):
        idx = jnp.where(rows[k] == m, k, idx)
    return m, idx


def _route_kernel(lg_ref, pos1_ref, pos2_ref, w1_ref, w2_ref, tab_ref, *, sb):
    lg = jnp.concatenate([lg_ref[b] for b in range(sb // TM)], axis=1)
    rows = [lg[k:k + 1, :] for k in range(N_EXPERTS + N_GROUPS)]
    grp = rows[N_EXPERTS:]
    gmax, gidx = _first_max(grp)
    p_group = 1.0 / sum(jnp.exp(r - gmax) for r in grp)
    e_in = []
    for k in range(EXP_PER_GROUP):
        v = rows[(N_GROUPS - 1) * EXP_PER_GROUP + k]
        for g in range(N_GROUPS - 2, -1, -1):
            v = jnp.where(gidx == g, rows[g * EXP_PER_GROUP + k], v)
        e_in.append(v)
    v1, i1 = _first_max(e_in)
    v2, i2 = _first_max([jnp.where(i1 == k, -jnp.inf, e_in[k]) for k in range(EXP_PER_GROUP)])
    ex = jnp.exp(v2 - v1)
    w1_ref[...] = p_group / (1.0 + ex)
    w2_ref[...] = p_group * ex / (1.0 + ex)
    x1 = gidx * EXP_PER_GROUP + i1
    x2 = gidx * EXP_PER_GROUP + i2

    eid = lax.broadcasted_iota(jnp.int32, (N_EXPERTS, sb), 0)
    sel = jnp.where((eid == x1) | (eid == x2), 1.0, 0.0)
    r_i = lax.broadcasted_iota(jnp.int32, (TM, TM), 0)
    c_i = lax.broadcasted_iota(jnp.int32, (TM, TM), 1)
    before = jnp.where(r_i < c_i, 1.0, 0.0).astype(BF16)
    carry = jnp.zeros((N_EXPERTS, 1), F32)
    ranks = []
    for b in range(sb // TM):
        s_b = sel[:, b * TM:(b + 1) * TM]
        ranks.append(_dot(_bf(s_b), before) + carry)
        carry = carry + jnp.sum(s_b, axis=1, keepdims=True)
    rank = jnp.concatenate(ranks, axis=1)
    shift = ROW_PAD.bit_length() - 1
    npad = jnp.left_shift(jnp.right_shift(carry.astype(jnp.int32) + (ROW_PAD - 1), shift), shift)

    lane = lax.broadcasted_iota(jnp.int32, (1, LANES), 1)
    tab = jnp.zeros((1, LANES), jnp.int32)
    pos1 = jnp.zeros((1, sb), F32)
    pos2 = jnp.zeros((1, sb), F32)
    off = jnp.zeros((1, 1), jnp.int32)
    for e in range(N_EXPERTS):
        n_e = npad[e:e + 1, :]
        tab = jnp.where(lane == e, off, tab)
        tab = jnp.where(lane == N_EXPERTS + e, n_e, tab)
        row = off.astype(F32) + rank[e:e + 1, :]
        pos1 = jnp.where(x1 == e, row, pos1)
        pos2 = jnp.where(x2 == e, row, pos2)
        off = off + n_e
    pos1_ref[...] = pos1.astype(jnp.int32) * _RPV
    pos2_ref[...] = pos2.astype(jnp.int32) * _RPV
    tab_ref[...] = tab


def _route(st, lg):
    sb = st.sb
    nsb = st.t // sb
    row_i = jax.ShapeDtypeStruct((nsb, 1, sb), jnp.int32)
    row_f = jax.ShapeDtypeStruct((nsb, 1, sb), F32)
    rspec = pl.BlockSpec((None, 1, sb), lambda s: (s, 0, 0))
    return pl.pallas_call(
        functools.partial(_route_kernel, sb=sb),
        out_shape=(row_i, row_i, row_f, row_f, jax.ShapeDtypeStruct((nsb, 1, LANES), jnp.int32)),
        grid=(nsb,),
        in_specs=[pl.BlockSpec((sb // TM, LG_ROWS, TM), lambda s: (s, 0, 0))],
        out_specs=(rspec, rspec, rspec, rspec, pl.BlockSpec((None, 1, LANES), lambda s: (s, 0, 0))),
        compiler_params=_cparams("arbitrary"),
        name="moe_route",
    )(lg)


def _moe_rows(sb):
    return 2 * sb + N_EXPERTS * ROW_PAD + FFN_TILE


def _moe_kernel(pos1_ref, pos2_ref, w1_ref, w2_ref, tab_ref, xr_ref, wg_hbm, wu_hbm, wd_hbm, o_ref,
                rows_scr, stage_scr, wg_buf, wu_buf, wd_buf, wd_scr, sems, *, sb, e0):
    s = pl.program_id(0)

    def weight_copies(e, slot):
        return [pltpu.make_async_copy(hbm.at[e0 + e], buf.at[slot], sems.at[k, slot])
                for k, (hbm, buf) in enumerate(((wg_hbm, wg_buf), (wu_hbm, wu_buf), (wd_hbm, wd_buf)))]

    for cp in weight_copies(0, 0):
        cp.start()

    def tile_at(ref, r8):
        return ref.at[pl.ds(pl.multiple_of(r8, _RPV), _RPV)]

    @pl.when(s == 0)
    def _():
        rows_scr[...] = jnp.zeros_like(rows_scr)

    def dispatch(t, carry):
        v = tile_at(xr_ref, t * _RPV)[...]
        tile_at(rows_scr, pos1_ref[0, t])[...] = v
        tile_at(rows_scr, pos2_ref[0, t])[...] = v
        return carry

    lax.fori_loop(0, sb, dispatch, 0, unroll=8)

    def expert(e, slot):
        for cp in weight_copies(e, slot):
            cp.wait()

        @pl.when(e + 1 < N_EXPERTS)
        def _():
            for cp in weight_copies(e + 1, 1 - slot):
                cp.start()

        wd_scr[...] = _bf(wd_buf[slot])

        def ffn_tile(r0, m, valid=None):
            win = rows_scr.at[pl.ds(pl.multiple_of(r0 * _RPV, ROW_PAD * _RPV), m * _RPV)]
            x = _load_rows_per_vreg(win, m)
            xb = _bf(x)
            hg = _dot(xb, wg_buf[slot])
            hu = _dot(xb, wu_buf[slot])
            y = _dot(_bf(hg * jax.nn.sigmoid(hg) * hu), wd_scr[...])
            if valid is not None:
                y = jnp.where(lax.broadcasted_iota(jnp.int32, (m, 1), 0) < valid, y, x)
            _store_rows_per_vreg(win, y)

        off = tab_ref[0, e]
        npad = tab_ref[0, N_EXPERTS + e]
        nfull = npad // FFN_TILE

        def full_tile(i, carry):
            ffn_tile(off + i * FFN_TILE, FFN_TILE)
            return carry

        lax.fori_loop(0, nfull, full_tile, 0)
        rem = npad - nfull * FFN_TILE
        last = off + nfull * FFN_TILE

        @pl.when((rem > 0) & (rem <= FFN_TILE // 2))
        def _():
            ffn_tile(last, FFN_TILE // 2, valid=rem)

        @pl.when(rem > FFN_TILE // 2)
        def _():
            ffn_tile(last, FFN_TILE, valid=rem)

    def expert_pair(i, carry):
        expert(2 * i, 0)
        expert(2 * i + 1, 1)
        return carry

    lax.fori_loop(0, N_EXPERTS // 2, expert_pair, 0)

    for c in range(sb // TM):
        def combine(t, carry, c=c):
            tt = c * TM + t
            y = (w1_ref[0, tt] * tile_at(rows_scr, pos1_ref[0, tt])[...]
                 + w2_ref[0, tt] * tile_at(rows_scr, pos2_ref[0, tt])[...])
            tile_at(stage_scr, t * _RPV)[...] = y
            return carry

        lax.fori_loop(0, TM, combine, 0, unroll=8)
        o_ref[c * TM:(c + 1) * TM, :] = _bf(_load_rows_per_vreg(stage_scr, TM))


def _cast_kernel(x_ref, o_ref):
    o_ref[...] = _bf(x_ref[...])


def _expert_weights_bf16(w):
    depth, ne, a, b = w.shape
    per_step = 4
    assert (depth * ne) % per_step == 0
    spec = pl.BlockSpec((per_step, a, b), lambda i: (i, 0, 0))
    return pl.pallas_call(
        _cast_kernel,
        out_shape=jax.ShapeDtypeStruct((depth * ne, a, b), BF16),
        grid=(depth * ne // per_step,),
        in_specs=[spec],
        out_specs=spec,
        compiler_params=_cparams("arbitrary"),
        name="expert_weight_cast",
    )(w.reshape(depth * ne, a, b))


def _moe(st, layer, xr, route, wg, wu, wd):
    sb = st.sb
    smem = lambda n: pl.BlockSpec((None, 1, n), lambda s: (s, 0, 0), memory_space=pltpu.SMEM)
    hbm = pl.BlockSpec(memory_space=pl.ANY)
    return pl.pallas_call(
        functools.partial(_moe_kernel, sb=sb, e0=layer * N_EXPERTS),
        out_shape=jax.ShapeDtypeStruct((st.t, D_MODEL), BF16),
        grid=(st.t // sb,),
        in_specs=[
            smem(sb), smem(sb), smem(sb), smem(sb), smem(LANES),
            pl.BlockSpec((sb * _RPV, LANES), lambda s: (s, 0)),
            hbm, hbm, hbm,
        ],
        out_specs=pl.BlockSpec((sb, D_MODEL), lambda s: (s, 0)),
        scratch_shapes=[pltpu.VMEM((_moe_rows(sb) * _RPV, LANES), F32), pltpu.VMEM((TM * _RPV, LANES), F32),
                        pltpu.VMEM((2, D_MODEL, D_EXPERT), BF16), pltpu.VMEM((2, D_MODEL, D_EXPERT), BF16),
                        pltpu.VMEM((2, D_EXPERT, D_MODEL), F32), pltpu.VMEM((D_EXPERT, D_MODEL), BF16),
                        pltpu.SemaphoreType.DMA((3, 2))],
        compiler_params=_cparams("arbitrary"),
        name="moe_ffn",
    )(*route, xr, wg, wu, wd)


def _hy_inproj_kernel(x_ref, m_ref, mod0_ref, mod_ref, nw_ref, w_ref, x2_ref, z_ref):
    g2 = _mod_slices(mod0_ref[...])[5]
    x2 = x_ref[...] + g2 * m_ref[...].astype(F32)
    x2_ref[...] = x2
    sh1, sc1 = _mod_slices(mod_ref[...])[:2]
    h = _rms(x2, nw_ref[...]) * (1.0 + sc1) + sh1
    z_ref[...] = _bf(_dot(_bf(h), w_ref[...]))


def _hy_inproj(st, x, moe, mods_prev, mods, nw, w):
    n = w.shape[1]
    tm = TM_TOK
    mspec = pl.BlockSpec((None, 1, N_MOD * D_MODEL), lambda i: (st.mod_row(i, tm), 0, 0))
    return pl.pallas_call(
        _hy_inproj_kernel,
        out_shape=(jax.ShapeDtypeStruct((st.t, D_MODEL), F32), jax.ShapeDtypeStruct((st.t, n), BF16)),
        grid=(st.t // tm,),
        in_specs=[
            pl.BlockSpec((tm, D_MODEL), lambda i: (i, 0)),
            pl.BlockSpec((tm, D_MODEL), lambda i: (i, 0)),
            mspec, mspec,
            pl.BlockSpec((1, D_MODEL), lambda i: (0, 0)),
            pl.BlockSpec((D_MODEL, n), lambda i: (0, 0)),
        ],
        out_specs=(pl.BlockSpec((tm, D_MODEL), lambda i: (i, 0)), pl.BlockSpec((tm, n), lambda i: (i, 0))),
        compiler_params=_cparams("arbitrary"),
        name="hy_inproj",
    )(x, moe, mods_prev, mods, nw, w)


def _dft_tables(seq):
    n2 = 2 * seq
    assert n2 & (n2 - 1) == 0
    tr = min(seq, TM)

    def table_kernel(cos_ref, msin_ref, cosb_ref, msinb_ref):
        k = lax.broadcasted_iota(jnp.int32, (tr, seq), 0) + pl.program_id(0) * tr
        n = lax.broadcasted_iota(jnp.int32, (tr, seq), 1)
        ang = ((k * n) & (n2 - 1)).astype(F32) * (2.0 * math.pi / n2)
        c, s = jnp.cos(ang), -jnp.sin(ang)
        cos_ref[...] = c
        msin_ref[...] = s
        cosb_ref[...] = _bf(c)
        msinb_ref[...] = _bf(s)

    spec = pl.BlockSpec((tr, seq), lambda i: (i, 0))
    f32, b16 = jax.ShapeDtypeStruct((seq, seq), F32), jax.ShapeDtypeStruct((seq, seq), BF16)
    return pl.pallas_call(table_kernel, out_shape=(f32, f32, b16, b16), grid=(seq // tr,),
                          out_specs=(spec, spec, spec, spec), compiler_params=_cparams("arbitrary"),
                          name=f"dft_tables_{seq}")()


def _alternating(shape):
    return jnp.where(lax.broadcasted_iota(jnp.int32, shape, 0) % 2 == 0, 1.0, -1.0)


def _hy_filter_kernel(emb_ref, dec_ref, w1_ref, b1_ref, f1_ref, w2_ref, b2_ref, f2_ref, w3_ref,
                      cos_ref, msin_ref, kr_ref, ki_ref, *, seq):
    h = jnp.sin(f1_ref[...] * (_dot_hp(emb_ref[...], w1_ref[...]) + b1_ref[...]))
    h = jnp.sin(f2_ref[...] * (_dot_hp(h, w2_ref[...]) + b2_ref[...]))
    dec = dec_ref[...]
    row0 = lax.broadcasted_iota(jnp.int32, dec.shape, 0) == 0
    alt = _alternating(dec.shape)
    cos, msin = cos_ref[...], msin_ref[...]
    scale = jnp.where(row0, 1.0, 2.0) / (2 * seq)
    for o in range(HY_ORDER):
        h_f = _dot_hp(h, w3_ref[:, 2 * o, :]) * dec
        h_b = jnp.where(row0, 0.0, _dot_hp(h, w3_ref[:, 2 * o + 1, :]) * dec)
        k_nyq = jnp.sum(alt * (h_f + h_b), axis=0, keepdims=True)
        kr_ref[o] = _dot_hp(cos, h_f + h_b) * scale
        ki_ref[o] = jnp.where(row0, k_nyq, _dot_hp(msin, h_f - h_b)) * scale


def _hy_filter(seq, w1, b1, f1, w2, b2, f2, w3, cos, msin):
    t = jnp.linspace(0.0, 1.0, seq, dtype=F32)[:, None]
    w = 2.0 * math.pi * jnp.arange(seq, dtype=F32)[:, None] / seq
    f = jnp.linspace(1e-4, HY_BANDS - 1, HY_BANDS, dtype=F32)[None, :]
    emb = jnp.concatenate([t, jnp.cos(f * w), -jnp.sin(f * w), jnp.zeros((seq, LANES - HY_EMB), F32)], axis=-1)
    decay = jnp.exp(-t * jnp.linspace(HY_MIN_DECAY, HY_MAX_DECAY, D_MODEL, dtype=F32)[None, :])
    w1p = jnp.concatenate([w1, jnp.zeros((LANES - HY_EMB, HY_FFN), F32)], axis=0)
    dblk = 256
    out = jax.ShapeDtypeStruct((HY_ORDER, seq, D_MODEL), F32)
    full = lambda shape: pl.BlockSpec(shape, lambda j: (0,) * len(shape))
    ospec = pl.BlockSpec((HY_ORDER, seq, dblk), lambda j: (0, 0, j))
    return pl.pallas_call(
        functools.partial(_hy_filter_kernel, seq=seq),
        out_shape=(out, out),
        grid=(D_MODEL // dblk,),
        in_specs=[
            full((seq, LANES)),
            pl.BlockSpec((seq, dblk), lambda j: (0, j)),
            full((LANES, HY_FFN)), full((1, HY_FFN)), full((1, HY_FFN)),
            full((HY_FFN, HY_FFN)), full((1, HY_FFN)), full((1, HY_FFN)),
            pl.BlockSpec((HY_FFN, 2 * HY_ORDER, dblk), lambda j: (0, 0, j)),
            full((seq, seq)), full((seq, seq)),
        ],
        out_specs=(ospec, ospec),
        compiler_params=_cparams("arbitrary"),
        name=f"hy_filter_{seq}",
    )(emb, decay, w1p, b1[None, :], f1[None, :], w2, b2[None, :], f2[None, :],
      w3.reshape(HY_FFN, 2 * HY_ORDER, D_MODEL), cos, msin)


def _hy_conv_kernel(*refs, seq, nseq):
    (zv_ref, z1_ref, z2_ref, cv_ref, c1_ref, c2_ref, kr_ref, ki_ref, bias_ref, cos_ref, msin_ref) = refs[:11]
    y_ref = refs[-1]
    dblk = y_ref.shape[1]
    t = lax.broadcasted_iota(jnp.int32, (seq, dblk), 0)
    first, last, row0 = t == 0, t == seq - 1, t == 0
    alt = _alternating((seq, dblk))
    seqs = range(nseq)
    rows = [slice(q * seq, (q + 1) * seq) for q in seqs]

    def short_conv(z_ref, c_ref):
        out = []
        for q in seqs:
            z = z_ref[rows[q], :].astype(F32)
            prev = jnp.where(first, 0.0, pltpu.roll(z, 1, 0))
            nxt = jnp.where(last, 0.0, pltpu.roll(z, seq - 1, 0))
            out.append(c_ref[0:1, :] * prev + c_ref[1:2, :] * z + c_ref[2:3, :] * nxt)
        return out

    def long_conv(s, o):
        sb = [_bf(s[q]) for q in seqs]
        x_re = [_dot(cos_ref[...], sb[q]) for q in seqs]
        x_im = [_dot(msin_ref[...], sb[q]) for q in seqs]
        x_im = [jnp.where(row0, jnp.sum(alt * s[q], axis=0, keepdims=True), x_im[q]) for q in seqs]
        k_re, k_im = kr_ref[o], ki_ref[o]
        y_re = [jnp.where(row0, x_re[q] * k_re, x_re[q] * k_re - x_im[q] * k_im) for q in seqs]
        y_im = [jnp.where(row0, x_im[q] * k_im, x_re[q] * k_im + x_im[q] * k_re) for q in seqs]
        y = [_dot(cos_ref[...], _bf(y_re[q])) + _dot(msin_ref[...], _bf(y_im[q])) for q in seqs]
        return [y[q] + alt * y_im[q][0:1, :] + bias_ref[o] * s[q] for q in seqs]

    v = short_conv(zv_ref, cv_ref)
    x1 = short_conv(z1_ref, c1_ref)
    x2 = short_conv(z2_ref, c2_ref)
    c1 = long_conv(v, 0)
    c2 = long_conv([x1[q] * c1[q] for q in seqs], 1)
    for q in seqs:
        y_ref[rows[q], :] = _bf(x2[q] * c2[q])


def _hy_conv(st, ctx, z, conv_w, filt, bias, tables, prev):
    seq = st.lp if ctx else st.ls
    nb = st.bp if ctx else st.bs
    nseq = max(1, min(nb, 2048 // seq))
    assert nb % nseq == 0
    off = 0 if ctx else st.tp // (nseq * seq)
    dblk = 256
    nd = D_MODEL // dblk
    kr, ki = filt
    cos, msin = tables
    full = lambda shape: pl.BlockSpec(shape, lambda j, b: (0,) * len(shape))
    kspec = pl.BlockSpec((HY_ORDER, seq, dblk), lambda j, b: (0, 0, j))
    in_specs = [
        pl.BlockSpec((nseq * seq, dblk), lambda j, b: (off + b, j)),
        pl.BlockSpec((nseq * seq, dblk), lambda j, b: (off + b, nd + j)),
        pl.BlockSpec((nseq * seq, dblk), lambda j, b: (off + b, 2 * nd + j)),
        pl.BlockSpec((3, dblk), lambda j, b: (0, j)),
        pl.BlockSpec((3, dblk), lambda j, b: (0, nd + j)),
        pl.BlockSpec((3, dblk), lambda j, b: (0, 2 * nd + j)),
        kspec, kspec,
        pl.BlockSpec((HY_ORDER, 1, dblk), lambda j, b: (0, 0, j)),
        full((seq, seq)), full((seq, seq)),
    ]
    args = [z, z, z, conv_w, conv_w, conv_w, kr, ki, bias.reshape(HY_ORDER, 1, D_MODEL), cos, msin]
    aliases = {}
    if prev is not None:
        aliases = {len(args): 0}
        in_specs.append(pl.BlockSpec(memory_space=pl.ANY))
        args.append(prev)
    return pl.pallas_call(
        functools.partial(_hy_conv_kernel, seq=seq, nseq=nseq),
        out_shape=jax.ShapeDtypeStruct((st.t, D_MODEL), BF16),
        grid=(nd, nb // nseq),
        in_specs=in_specs,
        out_specs=pl.BlockSpec((nseq * seq, dblk), lambda j, b: (off + b, j)),
        input_output_aliases=aliases,
        compiler_params=_cparams("arbitrary", "arbitrary"),
        name="hy_conv_ctx" if ctx else "hy_conv_lat",
    )(*args)


def _hy_outproj_kernel(y_ref, x_ref, mod_ref, w_ref, n2w_ref, wrt_ref, brt_ref, x3_ref, h2_ref, lg_ref):
    mod = _mod_slices(mod_ref[...])
    g1, sh2, sc2 = mod[2], mod[3], mod[4]
    x3 = x_ref[...] + g1 * _dot(y_ref[...], w_ref[...])
    x3_ref[...] = x3
    h2 = _rms(x3, n2w_ref[...]) * (1.0 + sc2) + sh2
    _store_rows_per_vreg(h2_ref, h2)
    lg = _router_logits(h2, wrt_ref, brt_ref)
    for b in range(lg.shape[1] // TM):
        lg_ref[b] = lg[:, b * TM:(b + 1) * TM]


def _hy_outproj(st, y, x, mods, w, n2w, wrt, brt):
    tm = TM_TOK
    return pl.pallas_call(
        _hy_outproj_kernel,
        out_shape=(jax.ShapeDtypeStruct((st.t, D_MODEL), F32),
                   jax.ShapeDtypeStruct((st.t * _RPV, LANES), F32),
                   jax.ShapeDtypeStruct((st.t // TM, LG_ROWS, TM), F32)),
        grid=(st.t // tm,),
        in_specs=[
            pl.BlockSpec((tm, D_MODEL), lambda i: (i, 0)),
            pl.BlockSpec((tm, D_MODEL), lambda i: (i, 0)),
            pl.BlockSpec((None, 1, N_MOD * D_MODEL), lambda i: (st.mod_row(i, tm), 0, 0)),
            pl.BlockSpec((D_MODEL, D_MODEL), lambda i: (0, 0)),
            pl.BlockSpec((1, D_MODEL), lambda i: (0, 0)),
            pl.BlockSpec((LG_ROWS, D_MODEL), lambda i: (0, 0)),
            pl.BlockSpec((LG_ROWS, 1), lambda i: (0, 0)),
        ],
        out_specs=(pl.BlockSpec((tm, D_MODEL), lambda i: (i, 0)),
                   pl.BlockSpec((tm * _RPV, LANES), lambda i: (i, 0)),
                   pl.BlockSpec((tm // TM, LG_ROWS, TM), lambda i: (i, 0, 0))),
        compiler_params=_cparams("arbitrary"),
        name="hy_outproj",
    )(y, x, mods, w, n2w, wrt, brt)


def _final_kernel(x_ref, m_ref, mod_ref, nf_ref, yp_ref, ys_ref, *, ncb):
    i = pl.program_id(0)
    g2 = _mod_slices(mod_ref[...])[5]
    y = _rms(x_ref[...] + g2 * m_ref[...].astype(F32), nf_ref[...])

    @pl.when(i < ncb)
    def _():
        yp_ref[...] = y

    @pl.when(i >= ncb)
    def _():
        ys_ref[...] = y


def _final(st, x, moe, mods, nf):
    tm = TM_TOK
    ncb = st.tp // tm
    return pl.pallas_call(
        functools.partial(_final_kernel, ncb=ncb),
        out_shape=(jax.ShapeDtypeStruct((st.tp, D_MODEL), F32), jax.ShapeDtypeStruct((st.ts, D_MODEL), F32)),
        grid=(st.t // tm,),
        in_specs=[
            pl.BlockSpec((tm, D_MODEL), lambda i: (i, 0)),
            pl.BlockSpec((tm, D_MODEL), lambda i: (i, 0)),
            pl.BlockSpec((None, 1, N_MOD * D_MODEL), lambda i: (st.mod_row(i, tm), 0, 0)),
            pl.BlockSpec((1, D_MODEL), lambda i: (0, 0)),
        ],
        out_specs=(pl.BlockSpec((tm, D_MODEL), lambda i: (jnp.minimum(i, ncb - 1), 0)),
                   pl.BlockSpec((tm, D_MODEL), lambda i: (jnp.maximum(i - ncb, 0), 0))),
        compiler_params=_cparams("arbitrary"),
        name="final_norm",
    )(x, moe, mods, nf)


def _grid_pos_table(seq):
    rows = seq // GRID_W
    r, cl = jnp.meshgrid(jnp.arange(rows, dtype=F32), jnp.arange(GRID_W, dtype=F32), indexing='ij')
    quarter = D_MODEL // 4
    omega = POS_THETA ** (-jnp.arange(quarter, dtype=F32) / quarter)

    def enc(pos):
        a = pos.reshape(-1, 1) * omega[None, :]
        return jnp.concatenate([jnp.sin(a), jnp.cos(a)], axis=-1)

    return jnp.concatenate([enc(r), enc(cl)], axis=-1)


def _router_weights(w_group, b_group, w_router, b_router):
    pad = LG_ROWS - N_EXPERTS - N_GROUPS
    w = jnp.concatenate([w_router.T, w_group.T, jnp.zeros((pad, D_MODEL), F32)], axis=0)
    b = jnp.concatenate([b_router, b_group, jnp.zeros((pad,), F32)])[:, None]
    return w, b


def kernel(x_prompt, x_sample, state_gla, state_mlstm_c, state_mlstm_n, state_mlstm_m, c, c_ctx, norm1_w, norm2_w, norm_f_w, ada_w, ada_b, rec_w_in, gla_gk_w, gla_gk_b, mlstm_gate_b, gla_norm_w, mlstm_norm_w, rec_w_out, hy_w_in, hy_conv_w, hy_f_w1, hy_f_b1, hy_f_freq1, hy_f_w2, hy_f_b2, hy_f_freq2, hy_f_w3, hy_f_bias, hy_w_out, moe_w_group, moe_b_group, moe_w_router, moe_b_router, moe_w_gate, moe_w_up, moe_w_down):
    bp, lp, _ = x_prompt.shape
    bs, ls, _ = x_sample.shape
    st = _Streams(bp, lp, bs, ls)
    xp = x_prompt.reshape(st.tp, D_MODEL)
    xs = x_sample.reshape(st.ts, D_MODEL)

    nrow = -(-(1 + bs) // 8) * 8
    cv = jnp.concatenate([c_ctx[None, :], c, jnp.zeros((nrow - 1 - bs, D_MODEL), F32)], axis=0)
    mods = _ada(cv, ada_w, ada_b)
    mods0 = mods[0].reshape(nrow, 1, N_MOD * D_MODEL)
    mods1 = mods[1].reshape(nrow, 1, N_MOD * D_MODEL)

    pos = _grid_pos_table(ls)

    w_in = rec_w_in[0]
    w_main = _bf(w_in[:, :_REC_MAIN])
    w_mg = w_in[:, _REC_MAIN + _MG0:].reshape(D_MODEL, 2, 2, H_B)
    w_fg = jnp.pad(w_mg[:, :, 1, :], ((0, 0), (0, 0), (0, 8 - H_B))).reshape(D_MODEL, 16)
    w_gate = jnp.concatenate([w_in[:, _REC_MAIN:], jnp.zeros((D_MODEL, LANES - _REC_GATE), F32),
                              jnp.zeros((D_MODEL, _MG0), F32), w_fg,
                              jnp.zeros((D_MODEL, LANES - _REC_GATE), F32)], axis=1)
    w_gate_t = w_in[:, _REC_MAIN + _MG0:].T
    x0, main, g, g2, mt = _inproj(st, xp, xs, pos, mods0, norm1_w[0][None, :], w_main, w_gate, w_gate_t)

    gkw = jnp.zeros((2, LANES, H_A * DK_A), F32)
    gkw = gkw.at[0, :GK_RANK].set(gla_gk_w[0, 0]).at[1, GK_RANK:2 * GK_RANK].set(gla_gk_w[0, 1])
    gkb = gla_gk_b[0][:, None, :]
    gb = mlstm_gate_b[0]
    gate_row = lambda b: jnp.pad(jnp.pad(b, ((0, 0), (0, 8 - H_B))).reshape(1, 16), ((0, 0), (_MG0, LANES - _REC_GATE)))
    gbcol = gb.reshape(16, 1)
    wr0, br0 = _router_weights(moe_w_group[0], moe_b_group[0], moe_w_router[0], moe_b_router[0])
    consts = [gkw, gkb, gate_row(gb[:, 0]), gate_row(gb[:, 1]), gbcol, gla_norm_w[0][None, :],
              mlstm_norm_w[0][None, :], _bf(rec_w_out[0]), norm2_w[0][None, :], wr0, br0]

    def scan_states(sg, sc, sn, sm):
        nb = sg.shape[0]
        return (jnp.swapaxes(sg, -1, -2), sc, sn.reshape(nb, 2 * H_B, DK_B), sm.reshape(nb, 1, 2 * H_B))

    zero = (jnp.zeros((bp, 2, H_A, DK_A, DV_A), F32), jnp.zeros((bp, 2, H_B, DK_B, DV_B), F32),
            jnp.zeros((bp, 2, H_B, DK_B), F32), jnp.zeros((bp, 2, H_B), F32))
    cached = (state_gla[:, 0], state_mlstm_c[:, 0], state_mlstm_n[:, 0], state_mlstm_m[:, 0])
    x1, h2, lg, fg, fc, fn, fm = _scan(st, True, main, g, g2, mt, x0, mods0, scan_states(*zero), consts, None)
    x1, h2, lg = _scan(st, False, main, g, g2, mt, x0, mods0, scan_states(*cached), consts, (x1, h2, lg))[:3]

    new_gla = jnp.swapaxes(fg, -1, -2)[:, None]
    new_c = fc[:, None]
    new_n = fn.reshape(bp, 1, 2, H_B, DK_B)
    new_m = fm.reshape(bp, 1, 2, H_B)

    wg, wu = (_expert_weights_bf16(w) for w in (moe_w_gate, moe_w_up))
    wd = moe_w_down.reshape(-1, D_EXPERT, D_MODEL)
    moe0 = _moe(st, 0, h2, _route(st, lg), wg, wu, wd)

    x2, z = _hy_inproj(st, x1, moe0, mods0, mods1, norm1_w[1][None, :], _bf(hy_w_in[0]))
    y = None
    for ctx in (True, False):
        seq = lp if ctx else ls
        cos, msin, cos_b, msin_b = _dft_tables(seq)
        filt = _hy_filter(seq, hy_f_w1[0], hy_f_b1[0], hy_f_freq1[0], hy_f_w2[0], hy_f_b2[0], hy_f_freq2[0],
                          hy_f_w3[0], cos, msin)
        y = _hy_conv(st, ctx, z, hy_conv_w[0], filt, hy_f_bias[0], (cos_b, msin_b), y)
    wr1, br1 = _router_weights(moe_w_group[1], moe_b_group[1], moe_w_router[1], moe_b_router[1])
    x3, h4, lg1 = _hy_outproj(st, y, x2, mods1, _bf(hy_w_out[0]), norm2_w[1][None, :], wr1, br1)

    moe1 = _moe(st, 1, h4, _route(st, lg1), wg, wu, wd)
    y_prompt, y_sample = _final(st, x3, moe1, mods1, norm_f_w[None, :])
    return (y_prompt.reshape(bp, lp, D_MODEL), y_sample.reshape(bs, ls, D_MODEL), new_gla, new_c, new_n, new_m)
```

```python
import functools
import math

import jax
import jax.numpy as jnp
from jax import lax
from jax.experimental import pallas as pl
from jax.experimental.pallas import tpu as pltpu

F32 = jnp.float32
BF16 = jnp.bfloat16

D_MODEL = 1024
GRID_W = 64
H_A = 4
DK_A = D_MODEL // 16
DV_A = D_MODEL // 8
GK_RANK = 16
GATE_TEMP = 16.0
H_B = 4
DK_B = D_MODEL // 16
DV_B = D_MODEL // 8
CHUNK = 64
HY_ORDER = 2
HY_EMB = 33
HY_BANDS = (HY_EMB - 1) // 2
HY_FFN = 64
HY_TARGET = 1e-2
HY_MAX_DECAY = abs(math.log(HY_TARGET)) / 0.3
HY_MIN_DECAY = abs(math.log(HY_TARGET)) / 1.5
N_GROUPS = 4
EXP_PER_GROUP = 4
N_EXPERTS = N_GROUPS * EXP_PER_GROUP
D_EXPERT = D_MODEL // 2
N_MOD = 6
POS_THETA = 10000.0
EPS = 1e-6

_QA, _KA, _VA, _RA = 0, 256, 512, 1024
_QB, _KB, _VB, _OB = 1536, 1792, 2048, 2560
_REC_MAIN = 3072
_REC_GATE = 48
_MG0 = 2 * GK_RANK

LANES = 128
SUBLANES = 8
_RPV = D_MODEL // LANES
assert _RPV == SUBLANES
TM = 256
TM_TOK = 1024
TM_IN = 512
SB_MOE = 2048
ROW_PAD = 16
FFN_TILE = 288
LG_ROWS = 32
VMEM_LIMIT = 58 * 1024 * 1024


def _cparams(*sem):
    return pltpu.CompilerParams(dimension_semantics=sem, vmem_limit_bytes=VMEM_LIMIT)


def _bf(x):
    return x.astype(BF16)


def _dot(a, b):
    return jnp.dot(a, b, preferred_element_type=F32)


def _dot_nt(a, b):
    return lax.dot_general(a, b, (((1,), (1,)), ((), ())), preferred_element_type=F32)


def _dot_tn(a, b):
    return lax.dot_general(a, b, (((0,), (0,)), ((), ())), preferred_element_type=F32)


def _split2(x):
    hi = _bf(x)
    return hi, _bf(x - hi.astype(F32))


def _split3(x):
    hi = _bf(x)
    r = x - hi.astype(F32)
    mid = _bf(r)
    return hi, mid, _bf(r - mid.astype(F32))


def _dot_hp(a, b, dot=_dot):
    ah, al = _split2(a)
    bh, bl = _split2(b)
    return dot(ah, bh) + (dot(ah, bl) + dot(al, bh))


def _dot_mask_l(m, x, terms=3):
    if terms == 2:
        x1, x2 = _split2(x)
        return _dot(m, x1) + _dot(m, x2)
    x1, x2, x3 = _split3(x)
    return _dot(m, x1) + (_dot(m, x2) + _dot(m, x3))


def _dot_mask_r(x, m, terms=3):
    if terms == 2:
        x1, x2 = _split2(x)
        return _dot(x1, m) + _dot(x2, m)
    x1, x2, x3 = _split3(x)
    return _dot(x1, m) + (_dot(x2, m) + _dot(x3, m))


def _rms(x, w):
    return x * lax.rsqrt(jnp.mean(x * x, axis=-1, keepdims=True) + EPS) * w


def _mod_slices(mod):
    return [mod[:, k * D_MODEL:(k + 1) * D_MODEL] for k in range(N_MOD)]


def _store_rows_per_vreg(dst, h):
    rows = h.shape[0]
    for j in range(_RPV):
        dst[pl.ds(j, rows, stride=_RPV), :] = h[:, j * LANES:(j + 1) * LANES]


def _load_rows_per_vreg(src, rows):
    return jnp.concatenate([src[pl.ds(j, rows, stride=_RPV), :] for j in range(_RPV)], axis=-1)


def _router_logits(h, wrt_ref, brt_ref):
    return _dot_hp(wrt_ref[...], h, dot=_dot_nt) + brt_ref[...]


def _ada_kernel(cv_ref, w_ref, b_ref, o_ref):
    a = cv_ref[...]
    a = a * jax.nn.sigmoid(a)
    o_ref[...] = _dot_hp(a, w_ref[...]) + b_ref[...]


def _ada(cv, ada_w, ada_b):
    depth, d, n = ada_w.shape
    rows = cv.shape[0]
    tn = 768
    return pl.pallas_call(
        _ada_kernel,
        out_shape=jax.ShapeDtypeStruct((depth, rows, n), F32),
        grid=(depth, n // tn),
        in_specs=[
            pl.BlockSpec((rows, d), lambda l, j: (0, 0)),
            pl.BlockSpec((None, d, tn), lambda l, j: (l, 0, j)),
            pl.BlockSpec((None, 1, tn), lambda l, j: (l, 0, j)),
        ],
        out_specs=pl.BlockSpec((None, rows, tn), lambda l, j: (l, 0, j)),
        compiler_params=_cparams("arbitrary", "arbitrary"),
        name="ada_mod",
    )(cv, ada_w, ada_b.reshape(depth, 1, n))


class _Streams:
    def __init__(self, bp, lp, bs, ls):
        self.bp, self.lp, self.bs, self.ls = bp, lp, bs, ls
        self.tp, self.ts = bp * lp, bs * ls
        self.t = self.tp + self.ts
        assert lp % TM == 0 and ls % TM == 0 and self.tp % ls == 0
        assert self.tp % TM_TOK == 0 and ls % TM_TOK == 0
        self.sb = min(SB_MOE, math.gcd(self.tp, self.ts))
        assert self.t % self.sb == 0 and self.sb % TM == 0

    def mod_row(self, i, tm):
        ncb = self.tp // tm
        return jnp.where(i < ncb, 0, 1 + (i - ncb) // (self.ls // tm))


def _inproj_kernel(xp_ref, xs_ref, pos_ref, mod_ref, nw_ref, w_ref, wg_ref, wgt_ref,
                   x0_ref, main_ref, g_ref, g2_ref, mt_ref, *, ncb):
    i = pl.program_id(0)
    x = jnp.where(i < ncb, xp_ref[...], xs_ref[...] + pos_ref[...])
    x0_ref[...] = x
    sh1, sc1 = _mod_slices(mod_ref[...])[:2]
    h = _rms(x, nw_ref[...]) * (1.0 + sc1) + sh1
    main_ref[...] = _bf(_dot(_bf(h), w_ref[...]))
    gates = _dot_hp(h, wg_ref[...])
    g_ref[...] = gates[:, :LANES]
    g2_ref[...] = gates[:, LANES:]
    mt = _dot_hp(wgt_ref[...], h, dot=_dot_nt)
    for c in range(mt.shape[1] // CHUNK):
        piece = mt[:, c * CHUNK:(c + 1) * CHUNK]
        mt_ref[c] = jnp.concatenate([piece, piece], axis=1)


def _inproj(st, xp, xs, pos, mods, nw, w_main, w_gate, w_gate_t):
    tm = TM_IN
    assert st.tp % tm == 0 and st.ls % tm == 0
    ncb = st.tp // tm
    bps = st.ls // tm
    t = st.t
    return pl.pallas_call(
        functools.partial(_inproj_kernel, ncb=ncb),
        out_shape=(jax.ShapeDtypeStruct((t, D_MODEL), F32),
                   jax.ShapeDtypeStruct((t, _REC_MAIN), BF16),
                   jax.ShapeDtypeStruct((t, LANES), F32),
                   jax.ShapeDtypeStruct((t, LANES), F32),
                   jax.ShapeDtypeStruct((t // CHUNK, 16, 2 * CHUNK), F32)),
        grid=(t // tm,),
        in_specs=[
            pl.BlockSpec((tm, D_MODEL), lambda i: (jnp.minimum(i, ncb - 1), 0)),
            pl.BlockSpec((tm, D_MODEL), lambda i: (jnp.maximum(i - ncb, 0), 0)),
            pl.BlockSpec((tm, D_MODEL), lambda i: (jnp.maximum(i - ncb, 0) % bps, 0)),
            pl.BlockSpec((None, 1, N_MOD * D_MODEL), lambda i: (st.mod_row(i, tm), 0, 0)),
            pl.BlockSpec((1, D_MODEL), lambda i: (0, 0)),
            pl.BlockSpec((D_MODEL, _REC_MAIN), lambda i: (0, 0)),
            pl.BlockSpec((D_MODEL, 2 * LANES), lambda i: (0, 0)),
            pl.BlockSpec((16, D_MODEL), lambda i: (0, 0)),
        ],
        out_specs=(pl.BlockSpec((tm, D_MODEL), lambda i: (i, 0)),
                   pl.BlockSpec((tm, _REC_MAIN), lambda i: (i, 0)),
                   pl.BlockSpec((tm, LANES), lambda i: (i, 0)),
                   pl.BlockSpec((tm, LANES), lambda i: (i, 0)),
                   pl.BlockSpec((tm // CHUNK, 16, 2 * CHUNK), lambda i: (i, 0, 0))),
        compiler_params=_cparams("arbitrary"),
        name="rec_inproj",
    )(xp, xs, pos, mods, nw, w_main, w_gate, w_gate_t)


def _block_diag(x):
    left = lax.broadcasted_iota(jnp.int32, (1, x.shape[1]), 1) < x.shape[1] // 2
    zero = jnp.zeros_like(x)
    return jnp.concatenate([jnp.where(left, x, zero), jnp.where(left, zero, x)], axis=0)


def _block_diag_mask(rows, width):
    r = lax.broadcasted_iota(jnp.int32, (rows, width), 0) < rows // 2
    l = lax.broadcasted_iota(jnp.int32, (rows, width), 1) < width // 2
    return r == l


def _running_max(x, reverse):
    n = x.shape[0]
    row = lax.broadcasted_iota(jnp.int32, x.shape, 0)
    sh = 1
    while sh < n:
        if reverse:
            y = jnp.where(row < n - sh, pltpu.roll(x, n - sh, 0), -jnp.inf)
        else:
            y = jnp.where(row >= sh, pltpu.roll(x, sh, 0), -jnp.inf)
        x = jnp.maximum(x, y)
        sh *= 2
    return x


def _scan_kernel(*refs, seq):
    (main_ref, g_ref, g2_ref, mt_ref, x_ref, mod_ref, sg_ref, sc_ref, sn_ref, sm_ref,
     gkw_ref, gkb_ref, gbi_ref, gbf_ref, gbcol_ref, gnw_ref, mnw_ref, wout_ref, n2w_ref, wrt_ref, brt_ref) = refs[:21]
    (x1_ref, h2_ref, lg_ref, og_ref, oc_ref, on_ref, om_ref,
     oa_scr, ob_scr, sbd_scr, cbd_scr, nbd_scr, mgl_scr) = refs[-13:]
    c = CHUNK
    nchunks = seq // c
    npair = H_A // 2
    assert H_A == H_B and DK_A == DK_B == c and DV_A == DV_B == LANES and 2 * DK_A == LANES

    lane = lax.broadcasted_iota(jnp.int32, (1, LANES), 1)
    gate_lane = lambda d, h: _MG0 + d * 8 + h
    used = tuple((lane >= gate_lane(d, 0)) & (lane < gate_lane(d, H_B)) for d in (0, 1))

    for d in (0, 1):
        for p in range(npair):
            k = d * npair + p
            z_s = jnp.zeros((DV_A, DK_A), F32)
            sbd_scr[k] = jnp.concatenate([jnp.concatenate([sg_ref[d, 2 * p], z_s], axis=1),
                                          jnp.concatenate([z_s, sg_ref[d, 2 * p + 1]], axis=1)], axis=0)
            z_c = jnp.zeros((DK_B, DV_B), F32)
            cbd_scr[k] = jnp.concatenate([jnp.concatenate([sc_ref[d, 2 * p], z_c], axis=1),
                                          jnp.concatenate([z_c, sc_ref[d, 2 * p + 1]], axis=1)], axis=0)
            n_rep = [jnp.broadcast_to(sn_ref[d * H_B + 2 * p + q:d * H_B + 2 * p + q + 1, :], (DV_B, DK_B)).T
                     for q in (0, 1)]
            nbd_scr[k] = jnp.concatenate([jnp.concatenate([n_rep[0], z_c], axis=1),
                                          jnp.concatenate([z_c, n_rep[1]], axis=1)], axis=0)
    m_gl = jnp.zeros((1, LANES), F32)
    for d in (0, 1):
        for h in range(H_B):
            r = d * H_B + h
            m_gl = jnp.where(lane == gate_lane(d, h), sm_ref[:, r:r + 1], m_gl)
    mgl_scr[...] = m_gl

    row_p = lax.broadcasted_iota(jnp.int32, (c, LANES), 0)
    s_p = lax.broadcasted_iota(jnp.int32, (c, LANES), 1) % c
    live_p = (s_p <= row_p, s_p >= row_p)
    row = lax.broadcasted_iota(jnp.int32, (c, c), 0)
    col = lax.broadcasted_iota(jnp.int32, (c, c), 1)
    tri = tuple(jnp.where(m, 1.0, 0.0).astype(BF16) for m in (col <= row, col >= row))
    tri_t2 = tuple(jnp.where(m, 1.0, 0.0).astype(BF16) for m in (row_p <= s_p, row_p >= s_p))
    grow = lax.broadcasted_iota(jnp.int32, (16, 1), 0)
    f_row = (grow % 8) >= 4
    scale_q = DK_A ** -0.5
    bd_val = _block_diag_mask(2 * DV_A, 2 * DK_A)
    bd_key = _block_diag_mask(2 * DK_B, 2 * DV_B)
    ones_bd = jnp.where(bd_key, 1.0, 0.0).astype(BF16)
    ones_cv = jnp.ones((c, 2 * DV_B), BF16)

    def replicate(d, width):
        r = lax.broadcasted_iota(jnp.int32, (LANES, H_B * width), 0)
        h = lax.broadcasted_iota(jnp.int32, (LANES, H_B * width), 1) // width
        return jnp.where(r == gate_lane(d, 0) + h, 1.0, 0.0).astype(BF16)

    rep_k = tuple(replicate(d, DK_B) for d in (0, 1))
    rep_v = tuple(replicate(d, DV_B) for d in (0, 1))

    def chunk_step(i, carry):
        rows, g_in, g2_in, mt_in, gla_in, mls_in = [], [], [], [], [], []
        for d in (0, 1):
            ci = i if d == 0 else nchunks - 1 - i
            rows.append(pl.ds(pl.multiple_of(ci * c, c), c))
            g_in.append(g_ref[rows[d], :])
            g2_in.append(g2_ref[rows[d], :])
            mt_in.append(mt_ref[ci])
            gla_in.append(main_ref[rows[d], _QA:_RA])
            mls_in.append(main_ref[rows[d], _QB:_OB])
        s_bd = [sbd_scr[k] for k in range(2 * npair)]
        c_bd = [cbd_scr[k] for k in range(2 * npair)]
        n_bd = [nbd_scr[k] for k in range(2 * npair)]
        m_gl = mgl_scr[...]
        dirs = (0, 1)
        ends = (c - 1, 0)
        pairs = [(d, p) for d in dirs for p in range(npair)]
        ks_of = lambda p: slice(p * 2 * DK_A, (p + 1) * 2 * DK_A)
        vs_of = lambda p: slice(p * 2 * DV_A, (p + 1) * 2 * DV_A)

        glin = [_dot_hp(g_in[d], gkw_ref[d]) + gkb_ref[d] for d in dirs]
        gi = [jnp.where(used[d], g_in[d] + gbi_ref[...], 0.0) for d in dirs]
        lf = [jnp.where(used[d], jax.nn.log_sigmoid(g2_in[d] + gbf_ref[...]), 0.0) for d in dirs]
        mt = [mt_in[d] + gbcol_ref[...] for d in dirs]
        mt = [jnp.where(f_row, jax.nn.log_sigmoid(mt[d]), mt[d]) for d in dirs]
        glog = [jax.nn.log_sigmoid(glin[d]) / GATE_TEMP for d in dirs]
        bc = [_dot_mask_l(tri[d], glog[d], 2) for d in dirs]
        cum = [_dot_mask_l(tri[d], lf[d], 2) for d in dirs]
        cum_t = [_dot_mask_r(mt[d][:, 0:c], tri_t2[d], 2) for d in dirs]
        b_end = [bc[d][ends[d]:ends[d] + 1, :] for d in dirs]
        qa = [gla_in[d][:, _QA:_QA + 256].astype(F32) * scale_q for d in dirs]
        ka = [gla_in[d][:, _KA:_KA + 256].astype(F32) for d in dirs]
        va = [gla_in[d][:, _VA:_VA + 512] for d in dirs]
        qe = [_bf(qa[d] * jnp.exp(bc[d])) for d in dirs]
        ke = [_bf(ka[d] * jnp.exp(-bc[d])) for d in dirs]
        kd = [_bf(ka[d] * jnp.exp(b_end[d] - bc[d])) for d in dirs]
        eb_end = [jnp.exp(b_end[d]) for d in dirs]
        qb = [mls_in[d][:, 0:256] * jnp.asarray(DK_B ** -0.5, BF16) for d in dirs]
        kb = [mls_in[d][:, _KB - _QB:_KB - _QB + 256] for d in dirs]
        vb = [mls_in[d][:, _VB - _QB:_VB - _QB + 512] for d in dirs]
        a_raw = {(d, p): _dot_nt(qe[d][:, ks_of(p)], _block_diag(ke[d][:, ks_of(p)])) for d, p in pairs}
        qk = {(d, p): _dot_nt(qb[d][:, ks_of(p)], _block_diag(kb[d][:, ks_of(p)])) for d, p in pairs}
        s_upd = {(d, p): _dot_tn(va[d][:, vs_of(p)], kd[d][:, ks_of(p)]) for d, p in pairs}
        o_car = {(d, p): _dot_nt(qe[d][:, ks_of(p)], _bf(s_bd[d * npair + p])) for d, p in pairs}
        m_loc = [cum[d] + _running_max(gi[d] - cum[d], reverse=(d == 1)) for d in dirs]
        inter = [cum[d] + m_gl for d in dirs]
        m_t = [jnp.maximum(inter[d], m_loc[d]) for d in dirs]
        b_last = [cum[d][ends[d]:ends[d] + 1, :] for d in dirs]
        dend = [b_last[d] - cum[d] + gi[d] for d in dirs]
        m_new = [jnp.maximum(b_last[d] + m_gl, jnp.max(dend[d], axis=0, keepdims=True)) for d in dirs]
        zero = jnp.zeros((c, LANES), F32)
        per_key = [jnp.concatenate([jnp.where(used[d], cum[d] - m_t[d], zero),
                                    jnp.where(used[d], jnp.exp(inter[d] - m_t[d]), zero),
                                    jnp.where(used[d], jnp.exp(dend[d] - m_new[d]), zero)], axis=0) for d in dirs]
        per_val = [jnp.concatenate([jnp.where(used[d], jnp.exp(-m_t[d]), zero),
                                    jnp.broadcast_to(jnp.where(used[d], jnp.exp(b_last[d] + m_gl - m_new[d]), 0.0),
                                                     (SUBLANES, LANES))], axis=0) for d in dirs]
        per_key = [_dot_mask_r(per_key[d], rep_k[d], 2) for d in dirs]
        per_val = [_dot_mask_r(per_val[d], rep_v[d], 2) for d in dirs]
        o_par = {(d, p): _dot(_bf(jnp.where(live_p[d], a_raw[d, p], 0.0)), _block_diag(va[d][:, vs_of(p)]))
                 for d, p in pairs}
        for d, p in pairs:
            k = d * npair + p
            s_bd[k] = s_bd[k] * eb_end[d][:, ks_of(p)] + jnp.where(bd_val, s_upd[d, p], 0.0)
        o_gla = [jnp.concatenate([o_par[d, p] + o_car[d, p] for p in range(npair)], axis=-1) for d in dirs]
        for d in dirs:
            m_gl = jnp.where(used[d], m_new[d], m_gl)

        w, qa2, kw = {}, {}, {}
        for d, p in pairs:
            ks = ks_of(p)
            r_i, r_f = d * 8 + 2 * p, d * 8 + 4 + 2 * p
            sub = [cum_t[d][r_f + q:r_f + q + 1, :] - mt[d][r_i + q:r_i + q + 1, :] for q in (0, 1)]
            sub = jnp.where(lane < DK_B, sub[0], sub[1])
            w[d, p] = jnp.exp(jnp.where(live_p[d], per_key[d][0:c, ks] - sub, -jnp.inf))
            qa2[d, p] = _bf(qb[d][:, ks].astype(F32) * per_key[d][c:2 * c, ks])
            kw[d, p] = _bf(kb[d][:, ks].astype(F32) * per_key[d][2 * c:3 * c, ks])
        carried = {(d, p): _dot(qa2[d, p], jnp.concatenate([_bf(c_bd[d * npair + p]), _bf(n_bd[d * npair + p])], axis=1))
                   for d, p in pairs}
        upd = {(d, p): _dot_tn(kw[d, p], jnp.concatenate([vb[d][:, vs_of(p)], ones_cv], axis=1)) for d, p in pairs}
        intra = {(d, p): _dot(_bf(qk[d, p] * w[d, p]),
                              jnp.concatenate([_block_diag(vb[d][:, vs_of(p)]), ones_bd], axis=1)) for d, p in pairs}
        outs = {}
        for d, p in pairs:
            k = d * npair + p
            vs = vs_of(p)
            num = intra[d, p][:, :2 * DV_B] + carried[d, p][:, :2 * DV_B]
            den = intra[d, p][:, 2 * DV_B:] + carried[d, p][:, 2 * DV_B:]
            outs[d, p] = num / jnp.maximum(jnp.abs(den), per_val[d][0:c, vs])
            decay = per_val[d][c:c + 1, vs]
            dec = jnp.concatenate([jnp.broadcast_to(jnp.concatenate([decay[:, q * DV_B:(q + 1) * DV_B]] * 2, axis=1),
                                                    (DK_B, 2 * DV_B)) for q in (0, 1)], axis=0)
            c_bd[k] = dec * c_bd[k] + jnp.where(bd_key, upd[d, p][:, :2 * DV_B], 0.0)
            n_bd[k] = dec * n_bd[k] + jnp.where(bd_key, upd[d, p][:, 2 * DV_B:], 0.0)
        o_mls = [jnp.concatenate([outs[d, p] for p in range(npair)], axis=-1) for d in dirs]

        for d in (0, 1):
            oa_scr[d, rows[d], :] = o_gla[d]
            ob_scr[d, rows[d], :] = o_mls[d]
        for k in range(2 * npair):
            sbd_scr[k] = s_bd[k]
            cbd_scr[k] = c_bd[k]
            nbd_scr[k] = n_bd[k]
        mgl_scr[...] = m_gl
        return carry

    lax.fori_loop(0, nchunks, chunk_step, 0)

    for d in (0, 1):
        for p in range(npair):
            k = d * npair + p
            s_t, c_f, n_f = sbd_scr[k], cbd_scr[k], nbd_scr[k]
            for q in (0, 1):
                h = 2 * p + q
                og_ref[d, h] = s_t[q * DV_A:(q + 1) * DV_A, q * DK_A:(q + 1) * DK_A]
                oc_ref[d, h] = c_f[q * DK_B:(q + 1) * DK_B, q * DV_B:(q + 1) * DV_B]
                n_t = n_f[q * DK_B:(q + 1) * DK_B, q * DV_B:(q + 1) * DV_B].T
                on_ref[d * H_B + h:d * H_B + h + 1, :] = n_t[0:1, :]
    m_gl = mgl_scr[...]
    om_ref[...] = jnp.concatenate([m_gl[:, gate_lane(d, h):gate_lane(d, h) + 1]
                                   for d in (0, 1) for h in range(H_B)], axis=1)

    mod = _mod_slices(mod_ref[...])
    g1, sh2, sc2 = mod[2], mod[3], mod[4]

    def out_step(j, carry):
        hm = TM // 2
        halves = (0, 1)
        r0 = [pl.multiple_of(j * TM + q * hm, hm) for q in halves]
        rows = [pl.ds(r0[q], hm) for q in halves]
        parts = [[], []]
        for h in range(H_A):
            vs = slice(h * DV_A, (h + 1) * DV_A)
            for q in halves:
                ra = main_ref[rows[q], _RA + h * DV_A:_RA + (h + 1) * DV_A].astype(F32)
                o = oa_scr[0, rows[q], vs] + oa_scr[1, rows[q], vs]
                parts[q].append(_rms(o, gnw_ref[...]) * (ra * jax.nn.sigmoid(ra)))
        for h in range(H_B):
            vs = slice(h * DV_B, (h + 1) * DV_B)
            for q in halves:
                ob = main_ref[rows[q], _OB + h * DV_B:_OB + (h + 1) * DV_B].astype(F32)
                o = ob_scr[0, rows[q], vs] + ob_scr[1, rows[q], vs]
                parts[q].append(_rms(o, mnw_ref[...]) * jax.nn.sigmoid(ob))
        mix = [_dot(_bf(jnp.concatenate(parts[q], axis=-1)), wout_ref[...]) for q in halves]
        x1 = [x_ref[rows[q], :] + g1 * mix[q] for q in halves]
        h2 = [_rms(x1[q], n2w_ref[...]) * (1.0 + sc2) + sh2 for q in halves]
        lg = [_router_logits(h2[q], wrt_ref, brt_ref) for q in halves]
        for q in halves:
            x1_ref[rows[q], :] = x1[q]
            _store_rows_per_vreg(h2_ref.at[pl.ds(pl.multiple_of(r0[q] * _RPV, hm * _RPV), hm * _RPV)], h2[q])
        lg_ref[j] = jnp.concatenate(lg, axis=1)
        return carry

    lax.fori_loop(0, seq // TM, out_step, 0)


def _scan(st, ctx, main, g, g2, mt, x, mods, states, consts, prev):
    seq = st.lp if ctx else st.ls
    nb = st.bp if ctx else st.bs
    off = 0 if ctx else st.tp // st.ls
    sg, sc, sn, sm = states
    full = lambda shape: pl.BlockSpec(shape, lambda b: (0,) * len(shape))
    in_specs = [
        pl.BlockSpec((seq, _REC_MAIN), lambda b: (off + b, 0)),
        pl.BlockSpec((seq, LANES), lambda b: (off + b, 0)),
        pl.BlockSpec((seq, LANES), lambda b: (off + b, 0)),
        pl.BlockSpec((seq // CHUNK, 16, 2 * CHUNK), lambda b: (off + b, 0, 0)),
        pl.BlockSpec((seq, D_MODEL), lambda b: (off + b, 0)),
    ]
    args = [main, g, g2, mt, x]
    mod_row = (lambda b: (0, 0, 0)) if ctx else (lambda b: (1 + b, 0, 0))
    in_specs += [
        pl.BlockSpec((None, 1, N_MOD * D_MODEL), mod_row),
        pl.BlockSpec((None, 2, H_A, DV_A, DK_A), lambda b: (b, 0, 0, 0, 0)),
        pl.BlockSpec((None, 2, H_B, DK_B, DV_B), lambda b: (b, 0, 0, 0, 0)),
        pl.BlockSpec((None, 2 * H_B, DK_B), lambda b: (b, 0, 0)),
        pl.BlockSpec((None, 1, 2 * H_B), lambda b: (b, 0, 0)),
    ]
    args += [mods, sg, sc, sn, sm]
    for a in consts:
        in_specs.append(full(a.shape))
        args.append(a)
    aliases = {}
    if prev is not None:
        aliases = {len(args) + k: k for k in range(3)}
        in_specs += [pl.BlockSpec(memory_space=pl.ANY)] * 3
        args += list(prev)
    out_shape = (jax.ShapeDtypeStruct((st.t, D_MODEL), F32),
                 jax.ShapeDtypeStruct((st.t * _RPV, LANES), F32),
                 jax.ShapeDtypeStruct((st.t // TM, LG_ROWS, TM), F32),
                 jax.ShapeDtypeStruct(sg.shape, F32), jax.ShapeDtypeStruct(sc.shape, F32),
                 jax.ShapeDtypeStruct(sn.shape, F32), jax.ShapeDtypeStruct(sm.shape, F32))
    out_specs = (pl.BlockSpec((seq, D_MODEL), lambda b: (off + b, 0)),
                 pl.BlockSpec((seq * _RPV, LANES), lambda b: (off + b, 0)),
                 pl.BlockSpec((seq // TM, LG_ROWS, TM), lambda b: (off + b, 0, 0)),
                 pl.BlockSpec((None, 2, H_A, DV_A, DK_A), lambda b: (b, 0, 0, 0, 0)),
                 pl.BlockSpec((None, 2, H_B, DK_B, DV_B), lambda b: (b, 0, 0, 0, 0)),
                 pl.BlockSpec((None, 2 * H_B, DK_B), lambda b: (b, 0, 0)),
                 pl.BlockSpec((None, 1, 2 * H_B), lambda b: (b, 0, 0)))
    return pl.pallas_call(
        functools.partial(_scan_kernel, seq=seq),
        out_shape=out_shape,
        grid=(nb,),
        in_specs=in_specs,
        out_specs=out_specs,
        scratch_shapes=[pltpu.VMEM((2, seq, H_A * DV_A), F32), pltpu.VMEM((2, seq, H_B * DV_B), F32),
                        pltpu.VMEM((H_A, 2 * DV_A, 2 * DK_A), F32), pltpu.VMEM((H_B, 2 * DK_B, 2 * DV_B), F32),
                        pltpu.VMEM((H_B, 2 * DK_B, 2 * DV_B), F32), pltpu.VMEM((1, LANES), F32)],
        input_output_aliases=aliases,
        compiler_params=_cparams("arbitrary"),
        name="rec_scan_ctx" if ctx else "rec_scan_lat",
    )(*args)


def _first_max(rows):
    m = rows[0]
    for r in rows[1:]:
        m = jnp.maximum(m, r)
    idx = jnp.full(m.shape, len(rows) - 1, jnp.int32)
    for k in range(len(rows) - 2, -1, -1):
        idx = jnp.where(rows[k] == m, k, idx)
    return m, idx


def _route_kernel(lg_ref, pos1_ref, pos2_ref, w1_ref, w2_ref, tab_ref, *, sb):
    lg = jnp.concatenate([lg_ref[b] for b in range(sb // TM)], axis=1)
    rows = [lg[k:k + 1, :] for k in range(N_EXPERTS + N_GROUPS)]
    grp = rows[N_EXPERTS:]
    gmax, gidx = _first_max(grp)
    p_group = 1.0 / sum(jnp.exp(r - gmax) for r in grp)
    e_in = []
    for k in range(EXP_PER_GROUP):
        v = rows[(N_GROUPS - 1) * EXP_PER_GROUP + k]
        for g in range(N_GROUPS - 2, -1, -1):
            v = jnp.where(gidx == g, rows[g * EXP_PER_GROUP + k], v)
        e_in.append(v)
    v1, i1 = _first_max(e_in)
    v2, i2 = _first_max([jnp.where(i1 == k, -jnp.inf, e_in[k]) for k in range(EXP_PER_GROUP)])
    ex = jnp.exp(v2 - v1)
    w1_ref[...] = p_group / (1.0 + ex)
    w2_ref[...] = p_group * ex / (1.0 + ex)
    x1 = gidx * EXP_PER_GROUP + i1
    x2 = gidx * EXP_PER_GROUP + i2

    eid = lax.broadcasted_iota(jnp.int32, (N_EXPERTS, sb), 0)
    sel = jnp.where((eid == x1) | (eid == x2), 1.0, 0.0)
    r_i = lax.broadcasted_iota(jnp.int32, (TM, TM), 0)
    c_i = lax.broadcasted_iota(jnp.int32, (TM, TM), 1)
    before = jnp.where(r_i < c_i, 1.0, 0.0).astype(BF16)
    carry = jnp.zeros((N_EXPERTS, 1), F32)
    ranks = []
    for b in range(sb // TM):
        s_b = sel[:, b * TM:(b + 1) * TM]
        ranks.append(_dot(_bf(s_b), before) + carry)
        carry = carry + jnp.sum(s_b, axis=1, keepdims=True)
    rank = jnp.concatenate(ranks, axis=1)
    shift = ROW_PAD.bit_length() - 1
    npad = jnp.left_shift(jnp.right_shift(carry.astype(jnp.int32) + (ROW_PAD - 1), shift), shift)

    lane = lax.broadcasted_iota(jnp.int32, (1, LANES), 1)
    tab = jnp.zeros((1, LANES), jnp.int32)
    pos1 = jnp.zeros((1, sb), F32)
    pos2 = jnp.zeros((1, sb), F32)
    off = jnp.zeros((1, 1), jnp.int32)
    for e in range(N_EXPERTS):
        n_e = npad[e:e + 1, :]
        tab = jnp.where(lane == e, off, tab)
        tab = jnp.where(lane == N_EXPERTS + e, n_e, tab)
        row = off.astype(F32) + rank[e:e + 1, :]
        pos1 = jnp.where(x1 == e, row, pos1)
        pos2 = jnp.where(x2 == e, row, pos2)
        off = off + n_e
    pos1_ref[...] = pos1.astype(jnp.int32) * _RPV
    pos2_ref[...] = pos2.astype(jnp.int32) * _RPV
    tab_ref[...] = tab


def _route(st, lg):
    sb = st.sb
    nsb = st.t // sb
    row_i = jax.ShapeDtypeStruct((nsb, 1, sb), jnp.int32)
    row_f = jax.ShapeDtypeStruct((nsb, 1, sb), F32)
    rspec = pl.BlockSpec((None, 1, sb), lambda s: (s, 0, 0))
    return pl.pallas_call(
        functools.partial(_route_kernel, sb=sb),
        out_shape=(row_i, row_i, row_f, row_f, jax.ShapeDtypeStruct((nsb, 1, LANES), jnp.int32)),
        grid=(nsb,),
        in_specs=[pl.BlockSpec((sb // TM, LG_ROWS, TM), lambda s: (s, 0, 0))],
        out_specs=(rspec, rspec, rspec, rspec, pl.BlockSpec((None, 1, LANES), lambda s: (s, 0, 0))),
        compiler_params=_cparams("arbitrary"),
        name="moe_route",
    )(lg)


def _moe_rows(sb):
    return 2 * sb + N_EXPERTS * ROW_PAD + FFN_TILE


def _moe_kernel(pos1_ref, pos2_ref, w1_ref, w2_ref, tab_ref, xr_ref, wg_ref, wu_ref, wd_ref, o_ref,
                rows_scr, stage_scr, wd_scr, *, sb):
    s = pl.program_id(0)
    e = pl.program_id(1)

    def tile_at(ref, r8):
        return ref.at[pl.ds(pl.multiple_of(r8, _RPV), _RPV)]

    @pl.when((s == 0) & (e == 0))
    def _():
        rows_scr[...] = jnp.zeros_like(rows_scr)

    @pl.when(e == 0)
    def _():
        def dispatch(t, carry):
            v = tile_at(xr_ref, t * _RPV)[...]
            tile_at(rows_scr, pos1_ref[0, t])[...] = v
            tile_at(rows_scr, pos2_ref[0, t])[...] = v
            return carry

        lax.fori_loop(0, sb, dispatch, 0, unroll=8)

    wd_scr[...] = _bf(wd_ref[...])

    def ffn_tile(r0, m, valid=None):
        win = rows_scr.at[pl.ds(pl.multiple_of(r0 * _RPV, ROW_PAD * _RPV), m * _RPV)]
        x = _load_rows_per_vreg(win, m)
        xb = _bf(x)
        hg = _dot(xb, wg_ref[...])
        hu = _dot(xb, wu_ref[...])
        y = _dot(_bf(hg * jax.nn.sigmoid(hg) * hu), wd_scr[...])
        if valid is not None:
            y = jnp.where(lax.broadcasted_iota(jnp.int32, (m, 1), 0) < valid, y, x)
        _store_rows_per_vreg(win, y)

    off = tab_ref[0, e]
    npad = tab_ref[0, N_EXPERTS + e]
    nfull = npad // FFN_TILE

    def full_tile(i, carry):
        ffn_tile(off + i * FFN_TILE, FFN_TILE)
        return carry

    lax.fori_loop(0, nfull, full_tile, 0)
    rem = npad - nfull * FFN_TILE
    last = off + nfull * FFN_TILE

    @pl.when((rem > 0) & (rem <= FFN_TILE // 2))
    def _():
        ffn_tile(last, FFN_TILE // 2, valid=rem)

    @pl.when(rem > FFN_TILE // 2)
    def _():
        ffn_tile(last, FFN_TILE, valid=rem)

    @pl.when(e == N_EXPERTS - 1)
    def _():
        for c in range(sb // TM):
            def combine(t, carry, c=c):
                tt = c * TM + t
                y = (w1_ref[0, tt] * tile_at(rows_scr, pos1_ref[0, tt])[...]
                     + w2_ref[0, tt] * tile_at(rows_scr, pos2_ref[0, tt])[...])
                tile_at(stage_scr, t * _RPV)[...] = y
                return carry

            lax.fori_loop(0, TM, combine, 0, unroll=8)
            o_ref[c * TM:(c + 1) * TM, :] = _bf(_load_rows_per_vreg(stage_scr, TM))


def _cast_kernel(x_ref, o_ref):
    o_ref[...] = _bf(x_ref[...])


def _expert_weights_bf16(w):
    depth, ne, a, b = w.shape
    per_step = 4
    assert (depth * ne) % per_step == 0
    spec = pl.BlockSpec((per_step, a, b), lambda i: (i, 0, 0))
    return pl.pallas_call(
        _cast_kernel,
        out_shape=jax.ShapeDtypeStruct((depth * ne, a, b), BF16),
        grid=(depth * ne // per_step,),
        in_specs=[spec],
        out_specs=spec,
        compiler_params=_cparams("arbitrary"),
        name="expert_weight_cast",
    )(w.reshape(depth * ne, a, b))


def _moe(st, layer, xr, route, wg, wu, wd):
    sb = st.sb
    e0 = layer * N_EXPERTS
    smem = lambda n: pl.BlockSpec((None, 1, n), lambda s, e: (s, 0, 0), memory_space=pltpu.SMEM)
    return pl.pallas_call(
        functools.partial(_moe_kernel, sb=sb),
        out_shape=jax.ShapeDtypeStruct((st.t, D_MODEL), BF16),
        grid=(st.t // sb, N_EXPERTS),
        in_specs=[
            smem(sb), smem(sb), smem(sb), smem(sb), smem(LANES),
            pl.BlockSpec((sb * _RPV, LANES), lambda s, e: (s, 0)),
            pl.BlockSpec((None, D_MODEL, D_EXPERT), lambda s, e: (e0 + e, 0, 0)),
            pl.BlockSpec((None, D_MODEL, D_EXPERT), lambda s, e: (e0 + e, 0, 0)),
            pl.BlockSpec((None, D_EXPERT, D_MODEL), lambda s, e: (e0 + e, 0, 0)),
        ],
        out_specs=pl.BlockSpec((sb, D_MODEL), lambda s, e: (s, 0)),
        scratch_shapes=[pltpu.VMEM((_moe_rows(sb) * _RPV, LANES), F32), pltpu.VMEM((TM * _RPV, LANES), F32),
                        pltpu.VMEM((D_EXPERT, D_MODEL), BF16)],
        compiler_params=_cparams("arbitrary", "arbitrary"),
        name="moe_ffn",
    )(*route, xr, wg, wu, wd)


def _hy_inproj_kernel(x_ref, m_ref, mod0_ref, mod_ref, nw_ref, w_ref, x2_ref, z_ref):
    g2 = _mod_slices(mod0_ref[...])[5]
    x2 = x_ref[...] + g2 * m_ref[...].astype(F32)
    x2_ref[...] = x2
    sh1, sc1 = _mod_slices(mod_ref[...])[:2]
    h = _rms(x2, nw_ref[...]) * (1.0 + sc1) + sh1
    z_ref[...] = _bf(_dot(_bf(h), w_ref[...]))


def _hy_inproj(st, x, moe, mods_prev, mods, nw, w):
    n = w.shape[1]
    tm = TM_TOK
    mspec = pl.BlockSpec((None, 1, N_MOD * D_MODEL), lambda i: (st.mod_row(i, tm), 0, 0))
    return pl.pallas_call(
        _hy_inproj_kernel,
        out_shape=(jax.ShapeDtypeStruct((st.t, D_MODEL), F32), jax.ShapeDtypeStruct((st.t, n), BF16)),
        grid=(st.t // tm,),
        in_specs=[
            pl.BlockSpec((tm, D_MODEL), lambda i: (i, 0)),
            pl.BlockSpec((tm, D_MODEL), lambda i: (i, 0)),
            mspec, mspec,
            pl.BlockSpec((1, D_MODEL), lambda i: (0, 0)),
            pl.BlockSpec((D_MODEL, n), lambda i: (0, 0)),
        ],
        out_specs=(pl.BlockSpec((tm, D_MODEL), lambda i: (i, 0)), pl.BlockSpec((tm, n), lambda i: (i, 0))),
        compiler_params=_cparams("arbitrary"),
        name="hy_inproj",
    )(x, moe, mods_prev, mods, nw, w)


def _dft_tables(seq):
    n2 = 2 * seq
    assert n2 & (n2 - 1) == 0
    tr = min(seq, TM)

    def table_kernel(cos_ref, msin_ref, cosb_ref, msinb_ref):
        k = lax.broadcasted_iota(jnp.int32, (tr, seq), 0) + pl.program_id(0) * tr
        n = lax.broadcasted_iota(jnp.int32, (tr, seq), 1)
        ang = ((k * n) & (n2 - 1)).astype(F32) * (2.0 * math.pi / n2)
        c, s = jnp.cos(ang), -jnp.sin(ang)
        cos_ref[...] = c
        msin_ref[...] = s
        cosb_ref[...] = _bf(c)
        msinb_ref[...] = _bf(s)

    spec = pl.BlockSpec((tr, seq), lambda i: (i, 0))
    f32, b16 = jax.ShapeDtypeStruct((seq, seq), F32), jax.ShapeDtypeStruct((seq, seq), BF16)
    return pl.pallas_call(table_kernel, out_shape=(f32, f32, b16, b16), grid=(seq // tr,),
                          out_specs=(spec, spec, spec, spec), compiler_params=_cparams("arbitrary"),
                          name=f"dft_tables_{seq}")()


def _alternating(shape):
    return jnp.where(lax.broadcasted_iota(jnp.int32, shape, 0) % 2 == 0, 1.0, -1.0)


def _hy_filter_kernel(emb_ref, dec_ref, w1_ref, b1_ref, f1_ref, w2_ref, b2_ref, f2_ref, w3_ref,
                      cos_ref, msin_ref, kr_ref, ki_ref, *, seq):
    h = jnp.sin(f1_ref[...] * (_dot_hp(emb_ref[...], w1_ref[...]) + b1_ref[...]))
    h = jnp.sin(f2_ref[...] * (_dot_hp(h, w2_ref[...]) + b2_ref[...]))
    dec = dec_ref[...]
    row0 = lax.broadcasted_iota(jnp.int32, dec.shape, 0) == 0
    alt = _alternating(dec.shape)
    cos, msin = cos_ref[...], msin_ref[...]
    scale = jnp.where(row0, 1.0, 2.0) / (2 * seq)
    for o in range(HY_ORDER):
        h_f = _dot_hp(h, w3_ref[:, 2 * o, :]) * dec
        h_b = jnp.where(row0, 0.0, _dot_hp(h, w3_ref[:, 2 * o + 1, :]) * dec)
        k_nyq = jnp.sum(alt * (h_f + h_b), axis=0, keepdims=True)
        kr_ref[o] = _dot_hp(cos, h_f + h_b) * scale
        ki_ref[o] = jnp.where(row0, k_nyq, _dot_hp(msin, h_f - h_b)) * scale


def _hy_filter(seq, w1, b1, f1, w2, b2, f2, w3, cos, msin):
    t = jnp.linspace(0.0, 1.0, seq, dtype=F32)[:, None]
    w = 2.0 * math.pi * jnp.arange(seq, dtype=F32)[:, None] / seq
    f = jnp.linspace(1e-4, HY_BANDS - 1, HY_BANDS, dtype=F32)[None, :]
    emb = jnp.concatenate([t, jnp.cos(f * w), -jnp.sin(f * w), jnp.zeros((seq, LANES - HY_EMB), F32)], axis=-1)
    decay = jnp.exp(-t * jnp.linspace(HY_MIN_DECAY, HY_MAX_DECAY, D_MODEL, dtype=F32)[None, :])
    w1p = jnp.concatenate([w1, jnp.zeros((LANES - HY_EMB, HY_FFN), F32)], axis=0)
    dblk = 256
    out = jax.ShapeDtypeStruct((HY_ORDER, seq, D_MODEL), F32)
    full = lambda shape: pl.BlockSpec(shape, lambda j: (0,) * len(shape))
    ospec = pl.BlockSpec((HY_ORDER, seq, dblk), lambda j: (0, 0, j))
    return pl.pallas_call(
        functools.partial(_hy_filter_kernel, seq=seq),
        out_shape=(out, out),
        grid=(D_MODEL // dblk,),
        in_specs=[
            full((seq, LANES)),
            pl.BlockSpec((seq, dblk), lambda j: (0, j)),
            full((LANES, HY_FFN)), full((1, HY_FFN)), full((1, HY_FFN)),
            full((HY_FFN, HY_FFN)), full((1, HY_FFN)), full((1, HY_FFN)),
            pl.BlockSpec((HY_FFN, 2 * HY_ORDER, dblk), lambda j: (0, 0, j)),
            full((seq, seq)), full((seq, seq)),
        ],
        out_specs=(ospec, ospec),
        compiler_params=_cparams("arbitrary"),
        name=f"hy_filter_{seq}",
    )(emb, decay, w1p, b1[None, :], f1[None, :], w2, b2[None, :], f2[None, :],
      w3.reshape(HY_FFN, 2 * HY_ORDER, D_MODEL), cos, msin)


def _hy_conv_kernel(*refs, seq, nseq):
    (zv_ref, z1_ref, z2_ref, cv_ref, c1_ref, c2_ref, kr_ref, ki_ref, bias_ref, cos_ref, msin_ref) = refs[:11]
    y_ref = refs[-1]
    dblk = y_ref.shape[1]
    t = lax.broadcasted_iota(jnp.int32, (seq, dblk), 0)
    first, last, row0 = t == 0, t == seq - 1, t == 0
    alt = _alternating((seq, dblk))
    seqs = range(nseq)
    rows = [slice(q * seq, (q + 1) * seq) for q in seqs]

    def short_conv(z_ref, c_ref):
        out = []
        for q in seqs:
            z = z_ref[rows[q], :].astype(F32)
            prev = jnp.where(first, 0.0, pltpu.roll(z, 1, 0))
            nxt = jnp.where(last, 0.0, pltpu.roll(z, seq - 1, 0))
            out.append(c_ref[0:1, :] * prev + c_ref[1:2, :] * z + c_ref[2:3, :] * nxt)
        return out

    def long_conv(s, o):
        sb = [_bf(s[q]) for q in seqs]
        x_re = [_dot(cos_ref[...], sb[q]) for q in seqs]
        x_im = [_dot(msin_ref[...], sb[q]) for q in seqs]
        x_im = [jnp.where(row0, jnp.sum(alt * s[q], axis=0, keepdims=True), x_im[q]) for q in seqs]
        k_re, k_im = kr_ref[o], ki_ref[o]
        y_re = [jnp.where(row0, x_re[q] * k_re, x_re[q] * k_re - x_im[q] * k_im) for q in seqs]
        y_im = [jnp.where(row0, x_im[q] * k_im, x_re[q] * k_im + x_im[q] * k_re) for q in seqs]
        y = [_dot(cos_ref[...], _bf(y_re[q])) + _dot(msin_ref[...], _bf(y_im[q])) for q in seqs]
        return [y[q] + alt * y_im[q][0:1, :] + bias_ref[o] * s[q] for q in seqs]

    v = short_conv(zv_ref, cv_ref)
    x1 = short_conv(z1_ref, c1_ref)
    x2 = short_conv(z2_ref, c2_ref)
    c1 = long_conv(v, 0)
    c2 = long_conv([x1[q] * c1[q] for q in seqs], 1)
    for q in seqs:
        y_ref[rows[q], :] = _bf(x2[q] * c2[q])


def _hy_conv(st, ctx, z, conv_w, filt, bias, tables, prev):
    seq = st.lp if ctx else st.ls
    nb = st.bp if ctx else st.bs
    nseq = max(1, min(nb, 4, 2048 // seq))
    assert nb % nseq == 0
    off = 0 if ctx else st.tp // (nseq * seq)
    dblk = 256
    nd = D_MODEL // dblk
    kr, ki = filt
    cos, msin = tables
    full = lambda shape: pl.BlockSpec(shape, lambda j, b: (0,) * len(shape))
    kspec = pl.BlockSpec((HY_ORDER, seq, dblk), lambda j, b: (0, 0, j))
    in_specs = [
        pl.BlockSpec((nseq * seq, dblk), lambda j, b: (off + b, j)),
        pl.BlockSpec((nseq * seq, dblk), lambda j, b: (off + b, nd + j)),
        pl.BlockSpec((nseq * seq, dblk), lambda j, b: (off + b, 2 * nd + j)),
        pl.BlockSpec((3, dblk), lambda j, b: (0, j)),
        pl.BlockSpec((3, dblk), lambda j, b: (0, nd + j)),
        pl.BlockSpec((3, dblk), lambda j, b: (0, 2 * nd + j)),
        kspec, kspec,
        pl.BlockSpec((HY_ORDER, 1, dblk), lambda j, b: (0, 0, j)),
        full((seq, seq)), full((seq, seq)),
    ]
    args = [z, z, z, conv_w, conv_w, conv_w, kr, ki, bias.reshape(HY_ORDER, 1, D_MODEL), cos, msin]
    aliases = {}
    if prev is not None:
        aliases = {len(args): 0}
        in_specs.append(pl.BlockSpec(memory_space=pl.ANY))
        args.append(prev)
    return pl.pallas_call(
        functools.partial(_hy_conv_kernel, seq=seq, nseq=nseq),
        out_shape=jax.ShapeDtypeStruct((st.t, D_MODEL), BF16),
        grid=(nd, nb // nseq),
        in_specs=in_specs,
        out_specs=pl.BlockSpec((nseq * seq, dblk), lambda j, b: (off + b, j)),
        input_output_aliases=aliases,
        compiler_params=_cparams("arbitrary", "arbitrary"),
        name="hy_conv_ctx" if ctx else "hy_conv_lat",
    )(*args)


def _hy_outproj_kernel(y_ref, x_ref, mod_ref, w_ref, n2w_ref, wrt_ref, brt_ref, x3_ref, h2_ref, lg_ref):
    mod = _mod_slices(mod_ref[...])
    g1, sh2, sc2 = mod[2], mod[3], mod[4]
    x3 = x_ref[...] + g1 * _dot(y_ref[...], w_ref[...])
    x3_ref[...] = x3
    h2 = _rms(x3, n2w_ref[...]) * (1.0 + sc2) + sh2
    _store_rows_per_vreg(h2_ref, h2)
    lg = _router_logits(h2, wrt_ref, brt_ref)
    for b in range(lg.shape[1] // TM):
        lg_ref[b] = lg[:, b * TM:(b + 1) * TM]


def _hy_outproj(st, y, x, mods, w, n2w, wrt, brt):
    tm = TM_TOK
    return pl.pallas_call(
        _hy_outproj_kernel,
        out_shape=(jax.ShapeDtypeStruct((st.t, D_MODEL), F32),
                   jax.ShapeDtypeStruct((st.t * _RPV, LANES), F32),
                   jax.ShapeDtypeStruct((st.t // TM, LG_ROWS, TM), F32)),
        grid=(st.t // tm,),
        in_specs=[
            pl.BlockSpec((tm, D_MODEL), lambda i: (i, 0)),
            pl.BlockSpec((tm, D_MODEL), lambda i: (i, 0)),
            pl.BlockSpec((None, 1, N_MOD * D_MODEL), lambda i: (st.mod_row(i, tm), 0, 0)),
            pl.BlockSpec((D_MODEL, D_MODEL), lambda i: (0, 0)),
            pl.BlockSpec((1, D_MODEL), lambda i: (0, 0)),
            pl.BlockSpec((LG_ROWS, D_MODEL), lambda i: (0, 0)),
            pl.BlockSpec((LG_ROWS, 1), lambda i: (0, 0)),
        ],
        out_specs=(pl.BlockSpec((tm, D_MODEL), lambda i: (i, 0)),
                   pl.BlockSpec((tm * _RPV, LANES), lambda i: (i, 0)),
                   pl.BlockSpec((tm // TM, LG_ROWS, TM), lambda i: (i, 0, 0))),
        compiler_params=_cparams("arbitrary"),
        name="hy_outproj",
    )(y, x, mods, w, n2w, wrt, brt)


def _final_kernel(x_ref, m_ref, mod_ref, nf_ref, yp_ref, ys_ref, *, ncb):
    i = pl.program_id(0)
    g2 = _mod_slices(mod_ref[...])[5]
    y = _rms(x_ref[...] + g2 * m_ref[...].astype(F32), nf_ref[...])

    @pl.when(i < ncb)
    def _():
        yp_ref[...] = y

    @pl.when(i >= ncb)
    def _():
        ys_ref[...] = y


def _final(st, x, moe, mods, nf):
    tm = TM_TOK
    ncb = st.tp // tm
    return pl.pallas_call(
        functools.partial(_final_kernel, ncb=ncb),
        out_shape=(jax.ShapeDtypeStruct((st.tp, D_MODEL), F32), jax.ShapeDtypeStruct((st.ts, D_MODEL), F32)),
        grid=(st.t // tm,),
        in_specs=[
            pl.BlockSpec((tm, D_MODEL), lambda i: (i, 0)),
            pl.BlockSpec((tm, D_MODEL), lambda i: (i, 0)),
            pl.BlockSpec((None, 1, N_MOD * D_MODEL), lambda i: (st.mod_row(i, tm), 0, 0)),
            pl.BlockSpec((1, D_MODEL), lambda i: (0, 0)),
        ],
        out_specs=(pl.BlockSpec((tm, D_MODEL), lambda i: (jnp.minimum(i, ncb - 1), 0)),
                   pl.BlockSpec((tm, D_MODEL), lambda i: (jnp.maximum(i - ncb, 0), 0))),
        compiler_params=_cparams("arbitrary"),
        name="final_norm",
    )(x, moe, mods, nf)


def _grid_pos_table(seq):
    rows = seq // GRID_W
    r, cl = jnp.meshgrid(jnp.arange(rows, dtype=F32), jnp.arange(GRID_W, dtype=F32), indexing='ij')
    quarter = D_MODEL // 4
    omega = POS_THETA ** (-jnp.arange(quarter, dtype=F32) / quarter)

    def enc(pos):
        a = pos.reshape(-1, 1) * omega[None, :]
        return jnp.concatenate([jnp.sin(a), jnp.cos(a)], axis=-1)

    return jnp.concatenate([enc(r), enc(cl)], axis=-1)


def _router_weights(w_group, b_group, w_router, b_router):
    pad = LG_ROWS - N_EXPERTS - N_GROUPS
    w = jnp.concatenate([w_router.T, w_group.T, jnp.zeros((pad, D_MODEL), F32)], axis=0)
    b = jnp.concatenate([b_router, b_group, jnp.zeros((pad,), F32)])[:, None]
    return w, b


def kernel(x_prompt, x_sample, state_gla, state_mlstm_c, state_mlstm_n, state_mlstm_m, c, c_ctx, norm1_w, norm2_w, norm_f_w, ada_w, ada_b, rec_w_in, gla_gk_w, gla_gk_b, mlstm_gate_b, gla_norm_w, mlstm_norm_w, rec_w_out, hy_w_in, hy_conv_w, hy_f_w1, hy_f_b1, hy_f_freq1, hy_f_w2, hy_f_b2, hy_f_freq2, hy_f_w3, hy_f_bias, hy_w_out, moe_w_group, moe_b_group, moe_w_router, moe_b_router, moe_w_gate, moe_w_up, moe_w_down):
    bp, lp, _ = x_prompt.shape
    bs, ls, _ = x_sample.shape
    st = _Streams(bp, lp, bs, ls)
    xp = x_prompt.reshape(st.tp, D_MODEL)
    xs = x_sample.reshape(st.ts, D_MODEL)

    nrow = -(-(1 + bs) // 8) * 8
    cv = jnp.concatenate([c_ctx[None, :], c, jnp.zeros((nrow - 1 - bs, D_MODEL), F32)], axis=0)
    mods = _ada(cv, ada_w, ada_b)
    mods0 = mods[0].reshape(nrow, 1, N_MOD * D_MODEL)
    mods1 = mods[1].reshape(nrow, 1, N_MOD * D_MODEL)

    pos = _grid_pos_table(ls)

    w_in = rec_w_in[0]
    w_main = _bf(w_in[:, :_REC_MAIN])
    w_mg = w_in[:, _REC_MAIN + _MG0:].reshape(D_MODEL, 2, 2, H_B)
    w_fg = jnp.pad(w_mg[:, :, 1, :], ((0, 0), (0, 0), (0, 8 - H_B))).reshape(D_MODEL, 16)
    w_gate = jnp.concatenate([w_in[:, _REC_MAIN:], jnp.zeros((D_MODEL, LANES - _REC_GATE), F32),
                              jnp.zeros((D_MODEL, _MG0), F32), w_fg,
                              jnp.zeros((D_MODEL, LANES - _REC_GATE), F32)], axis=1)
    w_gate_t = w_in[:, _REC_MAIN + _MG0:].T
    x0, main, g, g2, mt = _inproj(st, xp, xs, pos, mods0, norm1_w[0][None, :], w_main, w_gate, w_gate_t)

    gkw = jnp.zeros((2, LANES, H_A * DK_A), F32)
    gkw = gkw.at[0, :GK_RANK].set(gla_gk_w[0, 0]).at[1, GK_RANK:2 * GK_RANK].set(gla_gk_w[0, 1])
    gkb = gla_gk_b[0][:, None, :]
    gb = mlstm_gate_b[0]
    gate_row = lambda b: jnp.pad(jnp.pad(b, ((0, 0), (0, 8 - H_B))).reshape(1, 16), ((0, 0), (_MG0, LANES - _REC_GATE)))
    gbcol = gb.reshape(16, 1)
    wr0, br0 = _router_weights(moe_w_group[0], moe_b_group[0], moe_w_router[0], moe_b_router[0])
    consts = [gkw, gkb, gate_row(gb[:, 0]), gate_row(gb[:, 1]), gbcol, gla_norm_w[0][None, :],
              mlstm_norm_w[0][None, :], _bf(rec_w_out[0]), norm2_w[0][None, :], wr0, br0]

    def scan_states(sg, sc, sn, sm):
        nb = sg.shape[0]
        return (jnp.swapaxes(sg, -1, -2), sc, sn.reshape(nb, 2 * H_B, DK_B), sm.reshape(nb, 1, 2 * H_B))

    zero = (jnp.zeros((bp, 2, H_A, DK_A, DV_A), F32), jnp.zeros((bp, 2, H_B, DK_B, DV_B), F32),
            jnp.zeros((bp, 2, H_B, DK_B), F32), jnp.zeros((bp, 2, H_B), F32))
    cached = (state_gla[:, 0], state_mlstm_c[:, 0], state_mlstm_n[:, 0], state_mlstm_m[:, 0])
    x1, h2, lg, fg, fc, fn, fm = _scan(st, True, main, g, g2, mt, x0, mods0, scan_states(*zero), consts, None)
    x1, h2, lg = _scan(st, False, main, g, g2, mt, x0, mods0, scan_states(*cached), consts, (x1, h2, lg))[:3]

    new_gla = jnp.swapaxes(fg, -1, -2)[:, None]
    new_c = fc[:, None]
    new_n = fn.reshape(bp, 1, 2, H_B, DK_B)
    new_m = fm.reshape(bp, 1, 2, H_B)

    wg, wu = (_expert_weights_bf16(w) for w in (moe_w_gate, moe_w_up))
    wd = moe_w_down.reshape(-1, D_EXPERT, D_MODEL)
    moe0 = _moe(st, 0, h2, _route(st, lg), wg, wu, wd)

    x2, z = _hy_inproj(st, x1, moe0, mods0, mods1, norm1_w[1][None, :], _bf(hy_w_in[0]))
    y = None
    for ctx in (True, False):
        seq = lp if ctx else ls
        cos, msin, cos_b, msin_b = _dft_tables(seq)
        filt = _hy_filter(seq, hy_f_w1[0], hy_f_b1[0], hy_f_freq1[0], hy_f_w2[0], hy_f_b2[0], hy_f_freq2[0],
                          hy_f_w3[0], cos, msin)
        y = _hy_conv(st, ctx, z, hy_conv_w[0], filt, hy_f_bias[0], (cos_b, msin_b), y)
    wr1, br1 = _router_weights(moe_w_group[1], moe_b_group[1], moe_w_router[1], moe_b_router[1])
    x3, h4, lg1 = _hy_outproj(st, y, x2, mods1, _bf(hy_w_out[0]), norm2_w[1][None, :], wr1, br1)

    moe1 = _moe(st, 1, h4, _route(st, lg1), wg, wu, wd)
    y_prompt, y_sample = _final(st, x3, moe1, mods1, norm_f_w[None, :])
    return (y_prompt.reshape(bp, lp, D_MODEL), y_sample.reshape(bs, ls, D_MODEL), new_gla, new_c, new_n, new_m)
```

```python
import functools
import math

import jax
import jax.numpy as jnp
from jax import lax
from jax.experimental import pallas as pl
from jax.experimental.pallas import tpu as pltpu

F32 = jnp.float32
BF16 = jnp.bfloat16

D_MODEL = 1024
GRID_W = 64
H_A = 4
DK_A = D_MODEL // 16
DV_A = D_MODEL // 8
GK_RANK = 16
GATE_TEMP = 16.0
H_B = 4
DK_B = D_MODEL // 16
DV_B = D_MODEL // 8
CHUNK = 64
HY_ORDER = 2
HY_EMB = 33
HY_BANDS = (HY_EMB - 1) // 2
HY_FFN = 64
HY_TARGET = 1e-2
HY_MAX_DECAY = abs(math.log(HY_TARGET)) / 0.3
HY_MIN_DECAY = abs(math.log(HY_TARGET)) / 1.5
N_GROUPS = 4
EXP_PER_GROUP = 4
N_EXPERTS = N_GROUPS * EXP_PER_GROUP
D_EXPERT = D_MODEL // 2
N_MOD = 6
POS_THETA = 10000.0
EPS = 1e-6

_QA, _KA, _VA, _RA = 0, 256, 512, 1024
_QB, _KB, _VB, _OB = 1536, 1792, 2048, 2560
_REC_MAIN = 3072
_REC_GATE = 48
_MG0 = 2 * GK_RANK

LANES = 128
SUBLANES = 8
_RPV = D_MODEL // LANES
assert _RPV == SUBLANES
TM = 256
TM_TOK = 1024
TM_IN = 512
SB_MOE = 2048
ROW_PAD = 16
EXPERTS_PER_STEP = 2
FFN_TILE = 288
LG_ROWS = 32
VMEM_LIMIT = 58 * 1024 * 1024


def _cparams(*sem):
    return pltpu.CompilerParams(dimension_semantics=sem, vmem_limit_bytes=VMEM_LIMIT)


def _bf(x):
    return x.astype(BF16)


def _dot(a, b):
    return jnp.dot(a, b, preferred_element_type=F32)


def _dot_nt(a, b):
    return lax.dot_general(a, b, (((1,), (1,)), ((), ())), preferred_element_type=F32)


def _dot_tn(a, b):
    return lax.dot_general(a, b, (((0,), (0,)), ((), ())), preferred_element_type=F32)


def _split2(x):
    hi = _bf(x)
    return hi, _bf(x - hi.astype(F32))


def _split3(x):
    hi = _bf(x)
    r = x - hi.astype(F32)
    mid = _bf(r)
    return hi, mid, _bf(r - mid.astype(F32))


def _dot_hp(a, b, dot=_dot):
    ah, al = _split2(a)
    bh, bl = _split2(b)
    return dot(ah, bh) + (dot(ah, bl) + dot(al, bh))


def _dot_mask_l(m, x, terms=3):
    if terms == 2:
        x1, x2 = _split2(x)
        return _dot(m, x1) + _dot(m, x2)
    x1, x2, x3 = _split3(x)
    return _dot(m, x1) + (_dot(m, x2) + _dot(m, x3))


def _dot_mask_r(x, m, terms=3):
    if terms == 2:
        x1, x2 = _split2(x)
        return _dot(x1, m) + _dot(x2, m)
    x1, x2, x3 = _split3(x)
    return _dot(x1, m) + (_dot(x2, m) + _dot(x3, m))


def _rms(x, w):
    return x * lax.rsqrt(jnp.mean(x * x, axis=-1, keepdims=True) + EPS) * w


def _mod_slices(mod):
    return [mod[:, k * D_MODEL:(k + 1) * D_MODEL] for k in range(N_MOD)]


def _store_rows_per_vreg(dst, h):
    rows = h.shape[0]
    for j in range(_RPV):
        dst[pl.ds(j, rows, stride=_RPV), :] = h[:, j * LANES:(j + 1) * LANES]


def _load_rows_per_vreg(src, rows):
    return jnp.concatenate([src[pl.ds(j, rows, stride=_RPV), :] for j in range(_RPV)], axis=-1)


def _router_logits(h, wrt_ref, brt_ref):
    return _dot_hp(wrt_ref[...], h, dot=_dot_nt) + brt_ref[...]


def _ada_kernel(cv_ref, w_ref, b_ref, o_ref):
    a = cv_ref[...]
    a = a * jax.nn.sigmoid(a)
    o_ref[...] = _dot_hp(a, w_ref[...]) + b_ref[...]


def _ada(cv, ada_w, ada_b):
    depth, d, n = ada_w.shape
    rows = cv.shape[0]
    tn = 768
    return pl.pallas_call(
        _ada_kernel,
        out_shape=jax.ShapeDtypeStruct((depth, rows, n), F32),
        grid=(depth, n // tn),
        in_specs=[
            pl.BlockSpec((rows, d), lambda l, j: (0, 0)),
            pl.BlockSpec((None, d, tn), lambda l, j: (l, 0, j)),
            pl.BlockSpec((None, 1, tn), lambda l, j: (l, 0, j)),
        ],
        out_specs=pl.BlockSpec((None, rows, tn), lambda l, j: (l, 0, j)),
        compiler_params=_cparams("arbitrary", "arbitrary"),
        name="ada_mod",
    )(cv, ada_w, ada_b.reshape(depth, 1, n))


class _Streams:
    def __init__(self, bp, lp, bs, ls):
        self.bp, self.lp, self.bs, self.ls = bp, lp, bs, ls
        self.tp, self.ts = bp * lp, bs * ls
        self.t = self.tp + self.ts
        assert lp % TM == 0 and ls % TM == 0 and self.tp % ls == 0
        assert self.tp % TM_TOK == 0 and ls % TM_TOK == 0
        self.sb = min(SB_MOE, math.gcd(self.tp, self.ts))
        assert self.t % self.sb == 0 and self.sb % TM == 0

    def mod_row(self, i, tm):
        ncb = self.tp // tm
        return jnp.where(i < ncb, 0, 1 + (i - ncb) // (self.ls // tm))


def _inproj_kernel(xp_ref, xs_ref, pos_ref, mod_ref, nw_ref, w_ref, wg_ref, wgt_ref,
                   x0_ref, main_ref, g_ref, g2_ref, mt_ref, *, ncb):
    i = pl.program_id(0)
    x = jnp.where(i < ncb, xp_ref[...], xs_ref[...] + pos_ref[...])
    x0_ref[...] = x
    sh1, sc1 = _mod_slices(mod_ref[...])[:2]
    h = _rms(x, nw_ref[...]) * (1.0 + sc1) + sh1
    main_ref[...] = _bf(_dot(_bf(h), w_ref[...]))
    gates = _dot_hp(h, wg_ref[...])
    g_ref[...] = gates[:, :LANES]
    g2_ref[...] = gates[:, LANES:]
    mt = _dot_hp(wgt_ref[...], h, dot=_dot_nt)
    for c in range(mt.shape[1] // CHUNK):
        piece = mt[:, c * CHUNK:(c + 1) * CHUNK]
        mt_ref[c] = jnp.concatenate([piece, piece], axis=1)


def _inproj(st, xp, xs, pos, mods, nw, w_main, w_gate, w_gate_t):
    tm = TM_IN
    assert st.tp % tm == 0 and st.ls % tm == 0
    ncb = st.tp // tm
    bps = st.ls // tm
    t = st.t
    return pl.pallas_call(
        functools.partial(_inproj_kernel, ncb=ncb),
        out_shape=(jax.ShapeDtypeStruct((t, D_MODEL), F32),
                   jax.ShapeDtypeStruct((t, _REC_MAIN), BF16),
                   jax.ShapeDtypeStruct((t, LANES), F32),
                   jax.ShapeDtypeStruct((t, LANES), F32),
                   jax.ShapeDtypeStruct((t // CHUNK, 16, 2 * CHUNK), F32)),
        grid=(t // tm,),
        in_specs=[
            pl.BlockSpec((tm, D_MODEL), lambda i: (jnp.minimum(i, ncb - 1), 0)),
            pl.BlockSpec((tm, D_MODEL), lambda i: (jnp.maximum(i - ncb, 0), 0)),
            pl.BlockSpec((tm, D_MODEL), lambda i: (jnp.maximum(i - ncb, 0) % bps, 0)),
            pl.BlockSpec((None, 1, N_MOD * D_MODEL), lambda i: (st.mod_row(i, tm), 0, 0)),
            pl.BlockSpec((1, D_MODEL), lambda i: (0, 0)),
            pl.BlockSpec((D_MODEL, _REC_MAIN), lambda i: (0, 0)),
            pl.BlockSpec((D_MODEL, 2 * LANES), lambda i: (0, 0)),
            pl.BlockSpec((16, D_MODEL), lambda i: (0, 0)),
        ],
        out_specs=(pl.BlockSpec((tm, D_MODEL), lambda i: (i, 0)),
                   pl.BlockSpec((tm, _REC_MAIN), lambda i: (i, 0)),
                   pl.BlockSpec((tm, LANES), lambda i: (i, 0)),
                   pl.BlockSpec((tm, LANES), lambda i: (i, 0)),
                   pl.BlockSpec((tm // CHUNK, 16, 2 * CHUNK), lambda i: (i, 0, 0))),
        compiler_params=_cparams("arbitrary"),
        name="rec_inproj",
    )(xp, xs, pos, mods, nw, w_main, w_gate, w_gate_t)


def _block_diag(x):
    left = lax.broadcasted_iota(jnp.int32, (1, x.shape[1]), 1) < x.shape[1] // 2
    zero = jnp.zeros_like(x)
    return jnp.concatenate([jnp.where(left, x, zero), jnp.where(left, zero, x)], axis=0)


def _block_diag_mask(rows, width):
    r = lax.broadcasted_iota(jnp.int32, (rows, width), 0) < rows // 2
    l = lax.broadcasted_iota(jnp.int32, (rows, width), 1) < width // 2
    return r == l


def _running_max(x, reverse):
    n = x.shape[0]
    row = lax.broadcasted_iota(jnp.int32, x.shape, 0)
    sh = 1
    while sh < n:
        if reverse:
            y = jnp.where(row < n - sh, pltpu.roll(x, n - sh, 0), -jnp.inf)
        else:
            y = jnp.where(row >= sh, pltpu.roll(x, sh, 0), -jnp.inf)
        x = jnp.maximum(x, y)
        sh *= 2
    return x


def _scan_kernel(*refs, seq):
    (main_ref, g_ref, g2_ref, mt_ref, x_ref, mod_ref, sg_ref, sc_ref, sn_ref, sm_ref,
     gkw_ref, gkb_ref, gbi_ref, gbf_ref, gbcol_ref, gnw_ref, mnw_ref, wout_ref, n2w_ref, wrt_ref, brt_ref) = refs[:21]
    (x1_ref, h2_ref, lg_ref, og_ref, oc_ref, on_ref, om_ref,
     oa_scr, ob_scr, sbd_scr, cbd_scr, nbd_scr, mgl_scr) = refs[-13:]
    c = CHUNK
    nchunks = seq // c
    npair = H_A // 2
    assert H_A == H_B and DK_A == DK_B == c and DV_A == DV_B == LANES and 2 * DK_A == LANES

    lane = lax.broadcasted_iota(jnp.int32, (1, LANES), 1)
    gate_lane = lambda d, h: _MG0 + d * 8 + h
    used = tuple((lane >= gate_lane(d, 0)) & (lane < gate_lane(d, H_B)) for d in (0, 1))

    for d in (0, 1):
        for p in range(npair):
            k = d * npair + p
            z_s = jnp.zeros((DV_A, DK_A), F32)
            sbd_scr[k] = jnp.concatenate([jnp.concatenate([sg_ref[d, 2 * p], z_s], axis=1),
                                          jnp.concatenate([z_s, sg_ref[d, 2 * p + 1]], axis=1)], axis=0)
            z_c = jnp.zeros((DK_B, DV_B), F32)
            cbd_scr[k] = jnp.concatenate([jnp.concatenate([sc_ref[d, 2 * p], z_c], axis=1),
                                          jnp.concatenate([z_c, sc_ref[d, 2 * p + 1]], axis=1)], axis=0)
            n_rep = [jnp.broadcast_to(sn_ref[d * H_B + 2 * p + q:d * H_B + 2 * p + q + 1, :], (DV_B, DK_B)).T
                     for q in (0, 1)]
            nbd_scr[k] = jnp.concatenate([jnp.concatenate([n_rep[0], z_c], axis=1),
                                          jnp.concatenate([z_c, n_rep[1]], axis=1)], axis=0)
    m_gl = jnp.zeros((1, LANES), F32)
    for d in (0, 1):
        for h in range(H_B):
            r = d * H_B + h
            m_gl = jnp.where(lane == gate_lane(d, h), sm_ref[:, r:r + 1], m_gl)
    mgl_scr[...] = m_gl

    row_p = lax.broadcasted_iota(jnp.int32, (c, LANES), 0)
    s_p = lax.broadcasted_iota(jnp.int32, (c, LANES), 1) % c
    live_p = (s_p <= row_p, s_p >= row_p)
    row = lax.broadcasted_iota(jnp.int32, (c, c), 0)
    col = lax.broadcasted_iota(jnp.int32, (c, c), 1)
    tri = tuple(jnp.where(m, 1.0, 0.0).astype(BF16) for m in (col <= row, col >= row))
    tri_t2 = tuple(jnp.where(m, 1.0, 0.0).astype(BF16) for m in (row_p <= s_p, row_p >= s_p))
    grow = lax.broadcasted_iota(jnp.int32, (16, 1), 0)
    f_row = (grow % 8) >= 4
    scale_q = DK_A ** -0.5
    bd_val = _block_diag_mask(2 * DV_A, 2 * DK_A)
    bd_key = _block_diag_mask(2 * DK_B, 2 * DV_B)
    ones_bd = jnp.where(bd_key, 1.0, 0.0).astype(BF16)
    ones_cv = jnp.ones((c, 2 * DV_B), BF16)

    def replicate(d, width):
        r = lax.broadcasted_iota(jnp.int32, (LANES, H_B * width), 0)
        h = lax.broadcasted_iota(jnp.int32, (LANES, H_B * width), 1) // width
        return jnp.where(r == gate_lane(d, 0) + h, 1.0, 0.0).astype(BF16)

    rep_k = tuple(replicate(d, DK_B) for d in (0, 1))
    rep_v = tuple(replicate(d, DV_B) for d in (0, 1))

    def chunk_step(i, carry):
        rows, g_in, g2_in, mt_in, gla_in, mls_in = [], [], [], [], [], []
        for d in (0, 1):
            ci = i if d == 0 else nchunks - 1 - i
            rows.append(pl.ds(pl.multiple_of(ci * c, c), c))
            g_in.append(g_ref[rows[d], :])
            g2_in.append(g2_ref[rows[d], :])
            mt_in.append(mt_ref[ci])
            gla_in.append(main_ref[rows[d], _QA:_RA])
            mls_in.append(main_ref[rows[d], _QB:_OB])
        s_bd = [sbd_scr[k] for k in range(2 * npair)]
        c_bd = [cbd_scr[k] for k in range(2 * npair)]
        n_bd = [nbd_scr[k] for k in range(2 * npair)]
        m_gl = mgl_scr[...]
        dirs = (0, 1)
        ends = (c - 1, 0)
        pairs = [(d, p) for d in dirs for p in range(npair)]
        ks_of = lambda p: slice(p * 2 * DK_A, (p + 1) * 2 * DK_A)
        vs_of = lambda p: slice(p * 2 * DV_A, (p + 1) * 2 * DV_A)

        glin = [_dot_hp(g_in[d], gkw_ref[d]) + gkb_ref[d] for d in dirs]
        gi = [jnp.where(used[d], g_in[d] + gbi_ref[...], 0.0) for d in dirs]
        lf = [jnp.where(used[d], jax.nn.log_sigmoid(g2_in[d] + gbf_ref[...]), 0.0) for d in dirs]
        mt = [mt_in[d] + gbcol_ref[...] for d in dirs]
        mt = [jnp.where(f_row, jax.nn.log_sigmoid(mt[d]), mt[d]) for d in dirs]
        glog = [jax.nn.log_sigmoid(glin[d]) / GATE_TEMP for d in dirs]
        bc = [_dot_mask_l(tri[d], glog[d], 2) for d in dirs]
        cum = [_dot_mask_l(tri[d], lf[d], 2) for d in dirs]
        cum_t = [_dot_mask_r(mt[d][:, 0:c], tri_t2[d], 2) for d in dirs]
        b_end = [bc[d][ends[d]:ends[d] + 1, :] for d in dirs]
        qa = [gla_in[d][:, _QA:_QA + 256].astype(F32) * scale_q for d in dirs]
        ka = [gla_in[d][:, _KA:_KA + 256].astype(F32) for d in dirs]
        va = [gla_in[d][:, _VA:_VA + 512] for d in dirs]
        qe = [_bf(qa[d] * jnp.exp(bc[d])) for d in dirs]
        ke = [_bf(ka[d] * jnp.exp(-bc[d])) for d in dirs]
        kd = [_bf(ka[d] * jnp.exp(b_end[d] - bc[d])) for d in dirs]
        eb_end = [jnp.exp(b_end[d]) for d in dirs]
        qb = [mls_in[d][:, 0:256] * jnp.asarray(DK_B ** -0.5, BF16) for d in dirs]
        kb = [mls_in[d][:, _KB - _QB:_KB - _QB + 256] for d in dirs]
        vb = [mls_in[d][:, _VB - _QB:_VB - _QB + 512] for d in dirs]
        a_raw = {(d, p): _dot_nt(qe[d][:, ks_of(p)], _block_diag(ke[d][:, ks_of(p)])) for d, p in pairs}
        qk = {(d, p): _dot_nt(qb[d][:, ks_of(p)], _block_diag(kb[d][:, ks_of(p)])) for d, p in pairs}
        s_upd = {(d, p): _dot_tn(va[d][:, vs_of(p)], kd[d][:, ks_of(p)]) for d, p in pairs}
        o_car = {(d, p): _dot_nt(qe[d][:, ks_of(p)], _bf(s_bd[d * npair + p])) for d, p in pairs}
        m_loc = [cum[d] + _running_max(gi[d] - cum[d], reverse=(d == 1)) for d in dirs]
        inter = [cum[d] + m_gl for d in dirs]
        m_t = [jnp.maximum(inter[d], m_loc[d]) for d in dirs]
        b_last = [cum[d][ends[d]:ends[d] + 1, :] for d in dirs]
        dend = [b_last[d] - cum[d] + gi[d] for d in dirs]
        m_new = [jnp.maximum(b_last[d] + m_gl, jnp.max(dend[d], axis=0, keepdims=True)) for d in dirs]
        zero = jnp.zeros((c, LANES), F32)
        per_key = [jnp.concatenate([jnp.where(used[d], cum[d] - m_t[d], zero),
                                    jnp.where(used[d], jnp.exp(inter[d] - m_t[d]), zero),
                                    jnp.where(used[d], jnp.exp(dend[d] - m_new[d]), zero)], axis=0) for d in dirs]
        per_val = [jnp.concatenate([jnp.where(used[d], jnp.exp(-m_t[d]), zero),
                                    jnp.broadcast_to(jnp.where(used[d], jnp.exp(b_last[d] + m_gl - m_new[d]), 0.0),
                                                     (SUBLANES, LANES))], axis=0) for d in dirs]
        per_key = [_dot_mask_r(per_key[d], rep_k[d], 2) for d in dirs]
        per_val = [_dot_mask_r(per_val[d], rep_v[d], 2) for d in dirs]
        o_par = {(d, p): _dot(_bf(jnp.where(live_p[d], a_raw[d, p], 0.0)), _block_diag(va[d][:, vs_of(p)]))
                 for d, p in pairs}
        for d, p in pairs:
            k = d * npair + p
            s_bd[k] = s_bd[k] * eb_end[d][:, ks_of(p)] + jnp.where(bd_val, s_upd[d, p], 0.0)
        o_gla = [jnp.concatenate([o_par[d, p] + o_car[d, p] for p in range(npair)], axis=-1) for d in dirs]
        for d in dirs:
            m_gl = jnp.where(used[d], m_new[d], m_gl)

        w, qa2, kw = {}, {}, {}
        for d, p in pairs:
            ks = ks_of(p)
            r_i, r_f = d * 8 + 2 * p, d * 8 + 4 + 2 * p
            sub = [cum_t[d][r_f + q:r_f + q + 1, :] - mt[d][r_i + q:r_i + q + 1, :] for q in (0, 1)]
            sub = jnp.where(lane < DK_B, sub[0], sub[1])
            w[d, p] = jnp.exp(jnp.where(live_p[d], per_key[d][0:c, ks] - sub, -jnp.inf))
            qa2[d, p] = _bf(qb[d][:, ks].astype(F32) * per_key[d][c:2 * c, ks])
            kw[d, p] = _bf(kb[d][:, ks].astype(F32) * per_key[d][2 * c:3 * c, ks])
        carried = {(d, p): _dot(qa2[d, p], jnp.concatenate([_bf(c_bd[d * npair + p]), _bf(n_bd[d * npair + p])], axis=1))
                   for d, p in pairs}
        upd = {(d, p): _dot_tn(kw[d, p], jnp.concatenate([vb[d][:, vs_of(p)], ones_cv], axis=1)) for d, p in pairs}
        intra = {(d, p): _dot(_bf(qk[d, p] * w[d, p]),
                              jnp.concatenate([_block_diag(vb[d][:, vs_of(p)]), ones_bd], axis=1)) for d, p in pairs}
        outs = {}
        for d, p in pairs:
            k = d * npair + p
            vs = vs_of(p)
            num = intra[d, p][:, :2 * DV_B] + carried[d, p][:, :2 * DV_B]
            den = intra[d, p][:, 2 * DV_B:] + carried[d, p][:, 2 * DV_B:]
            outs[d, p] = num / jnp.maximum(jnp.abs(den), per_val[d][0:c, vs])
            decay = per_val[d][c:c + 1, vs]
            dec = jnp.concatenate([jnp.broadcast_to(jnp.concatenate([decay[:, q * DV_B:(q + 1) * DV_B]] * 2, axis=1),
                                                    (DK_B, 2 * DV_B)) for q in (0, 1)], axis=0)
            c_bd[k] = dec * c_bd[k] + jnp.where(bd_key, upd[d, p][:, :2 * DV_B], 0.0)
            n_bd[k] = dec * n_bd[k] + jnp.where(bd_key, upd[d, p][:, 2 * DV_B:], 0.0)
        o_mls = [jnp.concatenate([outs[d, p] for p in range(npair)], axis=-1) for d in dirs]

        for d in (0, 1):
            oa_scr[d, rows[d], :] = o_gla[d]
            ob_scr[d, rows[d], :] = o_mls[d]
        for k in range(2 * npair):
            sbd_scr[k] = s_bd[k]
            cbd_scr[k] = c_bd[k]
            nbd_scr[k] = n_bd[k]
        mgl_scr[...] = m_gl
        return carry

    lax.fori_loop(0, nchunks, chunk_step, 0)

    for d in (0, 1):
        for p in range(npair):
            k = d * npair + p
            s_t, c_f, n_f = sbd_scr[k], cbd_scr[k], nbd_scr[k]
            for q in (0, 1):
                h = 2 * p + q
                og_ref[d, h] = s_t[q * DV_A:(q + 1) * DV_A, q * DK_A:(q + 1) * DK_A]
                oc_ref[d, h] = c_f[q * DK_B:(q + 1) * DK_B, q * DV_B:(q + 1) * DV_B]
                n_t = n_f[q * DK_B:(q + 1) * DK_B, q * DV_B:(q + 1) * DV_B].T
                on_ref[d * H_B + h:d * H_B + h + 1, :] = n_t[0:1, :]
    m_gl = mgl_scr[...]
    om_ref[...] = jnp.concatenate([m_gl[:, gate_lane(d, h):gate_lane(d, h) + 1]
                                   for d in (0, 1) for h in range(H_B)], axis=1)

    mod = _mod_slices(mod_ref[...])
    g1, sh2, sc2 = mod[2], mod[3], mod[4]

    def out_step(j, carry):
        hm = TM // 2
        halves = (0, 1)
        r0 = [pl.multiple_of(j * TM + q * hm, hm) for q in halves]
        rows = [pl.ds(r0[q], hm) for q in halves]
        parts = [[], []]
        for h in range(H_A):
            vs = slice(h * DV_A, (h + 1) * DV_A)
            for q in halves:
                ra = main_ref[rows[q], _RA + h * DV_A:_RA + (h + 1) * DV_A].astype(F32)
                o = oa_scr[0, rows[q], vs] + oa_scr[1, rows[q], vs]
                parts[q].append(_rms(o, gnw_ref[...]) * (ra * jax.nn.sigmoid(ra)))
        for h in range(H_B):
            vs = slice(h * DV_B, (h + 1) * DV_B)
            for q in halves:
                ob = main_ref[rows[q], _OB + h * DV_B:_OB + (h + 1) * DV_B].astype(F32)
                o = ob_scr[0, rows[q], vs] + ob_scr[1, rows[q], vs]
                parts[q].append(_rms(o, mnw_ref[...]) * jax.nn.sigmoid(ob))
        mix = [_dot(_bf(jnp.concatenate(parts[q], axis=-1)), wout_ref[...]) for q in halves]
        x1 = [x_ref[rows[q], :] + g1 * mix[q] for q in halves]
        h2 = [_rms(x1[q], n2w_ref[...]) * (1.0 + sc2) + sh2 for q in halves]
        lg = [_router_logits(h2[q], wrt_ref, brt_ref) for q in halves]
        for q in halves:
            x1_ref[rows[q], :] = x1[q]
            _store_rows_per_vreg(h2_ref.at[pl.ds(pl.multiple_of(r0[q] * _RPV, hm * _RPV), hm * _RPV)], h2[q])
        lg_ref[j] = jnp.concatenate(lg, axis=1)
        return carry

    lax.fori_loop(0, seq // TM, out_step, 0)


def _scan(st, ctx, main, g, g2, mt, x, mods, states, consts, prev):
    seq = st.lp if ctx else st.ls
    nb = st.bp if ctx else st.bs
    off = 0 if ctx else st.tp // st.ls
    sg, sc, sn, sm = states
    full = lambda shape: pl.BlockSpec(shape, lambda b: (0,) * len(shape))
    in_specs = [
        pl.BlockSpec((seq, _REC_MAIN), lambda b: (off + b, 0)),
        pl.BlockSpec((seq, LANES), lambda b: (off + b, 0)),
        pl.BlockSpec((seq, LANES), lambda b: (off + b, 0)),
        pl.BlockSpec((seq // CHUNK, 16, 2 * CHUNK), lambda b: (off + b, 0, 0)),
        pl.BlockSpec((seq, D_MODEL), lambda b: (off + b, 0)),
    ]
    args = [main, g, g2, mt, x]
    mod_row = (lambda b: (0, 0, 0)) if ctx else (lambda b: (1 + b, 0, 0))
    in_specs += [
        pl.BlockSpec((None, 1, N_MOD * D_MODEL), mod_row),
        pl.BlockSpec((None, 2, H_A, DV_A, DK_A), lambda b: (b, 0, 0, 0, 0)),
        pl.BlockSpec((None, 2, H_B, DK_B, DV_B), lambda b: (b, 0, 0, 0, 0)),
        pl.BlockSpec((None, 2 * H_B, DK_B), lambda b: (b, 0, 0)),
        pl.BlockSpec((None, 1, 2 * H_B), lambda b: (b, 0, 0)),
    ]
    args += [mods, sg, sc, sn, sm]
    for a in consts:
        in_specs.append(full(a.shape))
        args.append(a)
    aliases = {}
    if prev is not None:
        aliases = {len(args) + k: k for k in range(3)}
        in_specs += [pl.BlockSpec(memory_space=pl.ANY)] * 3
        args += list(prev)
    out_shape = (jax.ShapeDtypeStruct((st.t, D_MODEL), F32),
                 jax.ShapeDtypeStruct((st.t * _RPV, LANES), F32),
                 jax.ShapeDtypeStruct((st.t // TM, LG_ROWS, TM), F32),
                 jax.ShapeDtypeStruct(sg.shape, F32), jax.ShapeDtypeStruct(sc.shape, F32),
                 jax.ShapeDtypeStruct(sn.shape, F32), jax.ShapeDtypeStruct(sm.shape, F32))
    out_specs = (pl.BlockSpec((seq, D_MODEL), lambda b: (off + b, 0)),
                 pl.BlockSpec((seq * _RPV, LANES), lambda b: (off + b, 0)),
                 pl.BlockSpec((seq // TM, LG_ROWS, TM), lambda b: (off + b, 0, 0)),
                 pl.BlockSpec((None, 2, H_A, DV_A, DK_A), lambda b: (b, 0, 0, 0, 0)),
                 pl.BlockSpec((None, 2, H_B, DK_B, DV_B), lambda b: (b, 0, 0, 0, 0)),
                 pl.BlockSpec((None, 2 * H_B, DK_B), lambda b: (b, 0, 0)),
                 pl.BlockSpec((None, 1, 2 * H_B), lambda b: (b, 0, 0)))
    return pl.pallas_call(
        functools.partial(_scan_kernel, seq=seq),
        out_shape=out_shape,
        grid=(nb,),
        in_specs=in_specs,
        out_specs=out_specs,
        scratch_shapes=[pltpu.VMEM((2, seq, H_A * DV_A), F32), pltpu.VMEM((2, seq, H_B * DV_B), F32),
                        pltpu.VMEM((H_A, 2 * DV_A, 2 * DK_A), F32), pltpu.VMEM((H_B, 2 * DK_B, 2 * DV_B), F32),
                        pltpu.VMEM((H_B, 2 * DK_B, 2 * DV_B), F32), pltpu.VMEM((1, LANES), F32)],
        input_output_aliases=aliases,
        compiler_params=_cparams("arbitrary"),
        name="rec_scan_ctx" if ctx else "rec_scan_lat",
    )(*args)


def _first_max(rows):
    m = rows[0]
    for r in rows[1:]:
        m = jnp.maximum(m, r)
    idx = jnp.full(m.shape, len(rows) - 1, jnp.int32)
    for k in range(len(rows) - 2, -1, -1):
        idx = jnp.where(rows[k] == m, k, idx)
    return m, idx


def _route_kernel(lg_ref, pos1_ref, pos2_ref, w1_ref, w2_ref, tab_ref, *, sb):
    lg = jnp.concatenate([lg_ref[b] for b in range(sb // TM)], axis=1)
    rows = [lg[k:k + 1, :] for k in range(N_EXPERTS + N_GROUPS)]
    grp = rows[N_EXPERTS:]
    gmax, gidx = _first_max(grp)
    p_group = 1.0 / sum(jnp.exp(r - gmax) for r in grp)
    e_in = []
    for k in range(EXP_PER_GROUP):
        v = rows[(N_GROUPS - 1) * EXP_PER_GROUP + k]
        for g in range(N_GROUPS - 2, -1, -1):
            v = jnp.where(gidx == g, rows[g * EXP_PER_GROUP + k], v)
        e_in.append(v)
    v1, i1 = _first_max(e_in)
    v2, i2 = _first_max([jnp.where(i1 == k, -jnp.inf, e_in[k]) for k in range(EXP_PER_GROUP)])
    ex = jnp.exp(v2 - v1)
    w1_ref[...] = p_group / (1.0 + ex)
    w2_ref[...] = p_group * ex / (1.0 + ex)
    x1 = gidx * EXP_PER_GROUP + i1
    x2 = gidx * EXP_PER_GROUP + i2

    eid = lax.broadcasted_iota(jnp.int32, (N_EXPERTS, sb), 0)
    sel = jnp.where((eid == x1) | (eid == x2), 1.0, 0.0)
    r_i = lax.broadcasted_iota(jnp.int32, (TM, TM), 0)
    c_i = lax.broadcasted_iota(jnp.int32, (TM, TM), 1)
    before = jnp.where(r_i < c_i, 1.0, 0.0).astype(BF16)
    carry = jnp.zeros((N_EXPERTS, 1), F32)
    ranks = []
    for b in range(sb // TM):
        s_b = sel[:, b * TM:(b + 1) * TM]
        ranks.append(_dot(_bf(s_b), before) + carry)
        carry = carry + jnp.sum(s_b, axis=1, keepdims=True)
    rank = jnp.concatenate(ranks, axis=1)
    shift = ROW_PAD.bit_length() - 1
    npad = jnp.left_shift(jnp.right_shift(carry.astype(jnp.int32) + (ROW_PAD - 1), shift), shift)

    lane = lax.broadcasted_iota(jnp.int32, (1, LANES), 1)
    tab = jnp.zeros((1, LANES), jnp.int32)
    pos1 = jnp.zeros((1, sb), F32)
    pos2 = jnp.zeros((1, sb), F32)
    off = jnp.zeros((1, 1), jnp.int32)
    for e in range(N_EXPERTS):
        n_e = npad[e:e + 1, :]
        tab = jnp.where(lane == e, off, tab)
        tab = jnp.where(lane == N_EXPERTS + e, n_e, tab)
        row = off.astype(F32) + rank[e:e + 1, :]
        pos1 = jnp.where(x1 == e, row, pos1)
        pos2 = jnp.where(x2 == e, row, pos2)
        off = off + n_e
    pos1_ref[...] = pos1.astype(jnp.int32) * _RPV
    pos2_ref[...] = pos2.astype(jnp.int32) * _RPV
    tab_ref[...] = tab


def _route(st, lg):
    sb = st.sb
    nsb = st.t // sb
    row_i = jax.ShapeDtypeStruct((nsb, 1, sb), jnp.int32)
    row_f = jax.ShapeDtypeStruct((nsb, 1, sb), F32)
    rspec = pl.BlockSpec((None, 1, sb), lambda s: (s, 0, 0))
    return pl.pallas_call(
        functools.partial(_route_kernel, sb=sb),
        out_shape=(row_i, row_i, row_f, row_f, jax.ShapeDtypeStruct((nsb, 1, LANES), jnp.int32)),
        grid=(nsb,),
        in_specs=[pl.BlockSpec((sb // TM, LG_ROWS, TM), lambda s: (s, 0, 0))],
        out_specs=(rspec, rspec, rspec, rspec, pl.BlockSpec((None, 1, LANES), lambda s: (s, 0, 0))),
        compiler_params=_cparams("arbitrary"),
        name="moe_route",
    )(lg)


def _moe_rows(sb):
    return 2 * sb + N_EXPERTS * ROW_PAD + FFN_TILE


def _moe_kernel(pos1_ref, pos2_ref, w1_ref, w2_ref, tab_ref, xr_ref, wg_ref, wu_ref, wd_ref, o_ref,
                rows_scr, stage_scr, *, sb):
    s = pl.program_id(0)
    e = pl.program_id(1)

    def tile_at(ref, r8):
        return ref.at[pl.ds(pl.multiple_of(r8, _RPV), _RPV)]

    @pl.when((s == 0) & (e == 0))
    def _():
        rows_scr[...] = jnp.zeros_like(rows_scr)

    @pl.when(e == 0)
    def _():
        def dispatch(t, carry):
            v = tile_at(xr_ref, t * _RPV)[...]
            tile_at(rows_scr, pos1_ref[0, t])[...] = v
            tile_at(rows_scr, pos2_ref[0, t])[...] = v
            return carry

        lax.fori_loop(0, sb, dispatch, 0, unroll=8)

    def expert(k):
        ex = e * EXPERTS_PER_STEP + k

        def ffn_tile(r0, m, valid=None):
            win = rows_scr.at[pl.ds(pl.multiple_of(r0 * _RPV, ROW_PAD * _RPV), m * _RPV)]
            x = _load_rows_per_vreg(win, m)
            xb = _bf(x)
            hg = _dot(xb, wg_ref[k])
            hu = _dot(xb, wu_ref[k])
            y = _dot(_bf(hg * jax.nn.sigmoid(hg) * hu), wd_ref[k])
            if valid is not None:
                y = jnp.where(lax.broadcasted_iota(jnp.int32, (m, 1), 0) < valid, y, x)
            _store_rows_per_vreg(win, y)

        off = tab_ref[0, ex]
        npad = tab_ref[0, N_EXPERTS + ex]
        nfull = npad // FFN_TILE

        def full_tile(i, carry):
            ffn_tile(off + i * FFN_TILE, FFN_TILE)
            return carry

        lax.fori_loop(0, nfull, full_tile, 0)
        rem = npad - nfull * FFN_TILE
        last = off + nfull * FFN_TILE

        @pl.when((rem > 0) & (rem <= FFN_TILE // 2))
        def _():
            ffn_tile(last, FFN_TILE // 2, valid=rem)

        @pl.when(rem > FFN_TILE // 2)
        def _():
            ffn_tile(last, FFN_TILE, valid=rem)

    for k in range(EXPERTS_PER_STEP):
        expert(k)

    @pl.when(e == N_EXPERTS // EXPERTS_PER_STEP - 1)
    def _():
        for c in range(sb // TM):
            def combine(t, carry, c=c):
                tt = c * TM + t
                y = (w1_ref[0, tt] * tile_at(rows_scr, pos1_ref[0, tt])[...]
                     + w2_ref[0, tt] * tile_at(rows_scr, pos2_ref[0, tt])[...])
                tile_at(stage_scr, t * _RPV)[...] = y
                return carry

            lax.fori_loop(0, TM, combine, 0, unroll=8)
            o_ref[c * TM:(c + 1) * TM, :] = _bf(_load_rows_per_vreg(stage_scr, TM))


def _cast_kernel(x_ref, o_ref):
    o_ref[...] = _bf(x_ref[...])


def _expert_weights_bf16(w):
    depth, ne, a, b = w.shape
    per_step = 4
    assert (depth * ne) % per_step == 0
    spec = pl.BlockSpec((per_step, a, b), lambda i: (i, 0, 0))
    return pl.pallas_call(
        _cast_kernel,
        out_shape=jax.ShapeDtypeStruct((depth * ne, a, b), BF16),
        grid=(depth * ne // per_step,),
        in_specs=[spec],
        out_specs=spec,
        compiler_params=_cparams("arbitrary"),
        name="expert_weight_cast",
    )(w.reshape(depth * ne, a, b))


def _moe(st, layer, xr, route, wg, wu, wd):
    sb = st.sb
    eps = EXPERTS_PER_STEP
    b0 = layer * N_EXPERTS // eps
    smem = lambda n: pl.BlockSpec((None, 1, n), lambda s, e: (s, 0, 0), memory_space=pltpu.SMEM)
    return pl.pallas_call(
        functools.partial(_moe_kernel, sb=sb),
        out_shape=jax.ShapeDtypeStruct((st.t, D_MODEL), BF16),
        grid=(st.t // sb, N_EXPERTS // eps),
        in_specs=[
            smem(sb), smem(sb), smem(sb), smem(sb), smem(LANES),
            pl.BlockSpec((sb * _RPV, LANES), lambda s, e: (s, 0)),
            pl.BlockSpec((eps, D_MODEL, D_EXPERT), lambda s, e: (b0 + e, 0, 0)),
            pl.BlockSpec((eps, D_MODEL, D_EXPERT), lambda s, e: (b0 + e, 0, 0)),
            pl.BlockSpec((eps, D_EXPERT, D_MODEL), lambda s, e: (b0 + e, 0, 0)),
        ],
        out_specs=pl.BlockSpec((sb, D_MODEL), lambda s, e: (s, 0)),
        scratch_shapes=[pltpu.VMEM((_moe_rows(sb) * _RPV, LANES), F32), pltpu.VMEM((TM * _RPV, LANES), F32)],
        compiler_params=_cparams("arbitrary", "arbitrary"),
        name="moe_ffn",
    )(*route, xr, wg, wu, wd)


def _hy_inproj_kernel(x_ref, m_ref, mod0_ref, mod_ref, nw_ref, w_ref, x2_ref, z_ref):
    g2 = _mod_slices(mod0_ref[...])[5]
    x2 = x_ref[...] + g2 * m_ref[...].astype(F32)
    x2_ref[...] = x2
    sh1, sc1 = _mod_slices(mod_ref[...])[:2]
    h = _rms(x2, nw_ref[...]) * (1.0 + sc1) + sh1
    z_ref[...] = _bf(_dot(_bf(h), w_ref[...]))


def _hy_inproj(st, x, moe, mods_prev, mods, nw, w):
    n = w.shape[1]
    tm = TM_TOK
    mspec = pl.BlockSpec((None, 1, N_MOD * D_MODEL), lambda i: (st.mod_row(i, tm), 0, 0))
    return pl.pallas_call(
        _hy_inproj_kernel,
        out_shape=(jax.ShapeDtypeStruct((st.t, D_MODEL), F32), jax.ShapeDtypeStruct((st.t, n), BF16)),
        grid=(st.t // tm,),
        in_specs=[
            pl.BlockSpec((tm, D_MODEL), lambda i: (i, 0)),
            pl.BlockSpec((tm, D_MODEL), lambda i: (i, 0)),
            mspec, mspec,
            pl.BlockSpec((1, D_MODEL), lambda i: (0, 0)),
            pl.BlockSpec((D_MODEL, n), lambda i: (0, 0)),
        ],
        out_specs=(pl.BlockSpec((tm, D_MODEL), lambda i: (i, 0)), pl.BlockSpec((tm, n), lambda i: (i, 0))),
        compiler_params=_cparams("arbitrary"),
        name="hy_inproj",
    )(x, moe, mods_prev, mods, nw, w)


def _dft_tables(seq):
    n2 = 2 * seq
    assert n2 & (n2 - 1) == 0
    tr = min(seq, TM)

    def table_kernel(cos_ref, msin_ref, cosb_ref, msinb_ref):
        k = lax.broadcasted_iota(jnp.int32, (tr, seq), 0) + pl.program_id(0) * tr
        n = lax.broadcasted_iota(jnp.int32, (tr, seq), 1)
        ang = ((k * n) & (n2 - 1)).astype(F32) * (2.0 * math.pi / n2)
        c, s = jnp.cos(ang), -jnp.sin(ang)
        cos_ref[...] = c
        msin_ref[...] = s
        cosb_ref[...] = _bf(c)
        msinb_ref[...] = _bf(s)

    spec = pl.BlockSpec((tr, seq), lambda i: (i, 0))
    f32, b16 = jax.ShapeDtypeStruct((seq, seq), F32), jax.ShapeDtypeStruct((seq, seq), BF16)
    return pl.pallas_call(table_kernel, out_shape=(f32, f32, b16, b16), grid=(seq // tr,),
                          out_specs=(spec, spec, spec, spec), compiler_params=_cparams("arbitrary"),
                          name=f"dft_tables_{seq}")()


def _alternating(shape):
    return jnp.where(lax.broadcasted_iota(jnp.int32, shape, 0) % 2 == 0, 1.0, -1.0)


def _hy_filter_kernel(emb_ref, dec_ref, w1_ref, b1_ref, f1_ref, w2_ref, b2_ref, f2_ref, w3_ref,
                      cos_ref, msin_ref, kr_ref, ki_ref, kn_ref, *, seq):
    h = jnp.sin(f1_ref[...] * (_dot_hp(emb_ref[...], w1_ref[...]) + b1_ref[...]))
    h = jnp.sin(f2_ref[...] * (_dot_hp(h, w2_ref[...]) + b2_ref[...]))
    dec = dec_ref[...]
    row0 = lax.broadcasted_iota(jnp.int32, dec.shape, 0) == 0
    alt = _alternating(dec.shape)
    cos, msin = cos_ref[...], msin_ref[...]
    scale = jnp.where(row0, 1.0, 2.0) / (2 * seq)
    for o in range(HY_ORDER):
        h_f = _dot_hp(h, w3_ref[:, 2 * o, :]) * dec
        h_b = jnp.where(row0, 0.0, _dot_hp(h, w3_ref[:, 2 * o + 1, :]) * dec)
        kn_ref[o] = jnp.sum(alt * (h_f + h_b), axis=0, keepdims=True) / (2 * seq)
        kr_ref[o] = _dot_hp(cos, h_f + h_b) * scale
        ki_ref[o] = _dot_hp(msin, h_f - h_b) * scale


def _hy_filter(seq, w1, b1, f1, w2, b2, f2, w3, cos, msin):
    t = jnp.linspace(0.0, 1.0, seq, dtype=F32)[:, None]
    w = 2.0 * math.pi * jnp.arange(seq, dtype=F32)[:, None] / seq
    f = jnp.linspace(1e-4, HY_BANDS - 1, HY_BANDS, dtype=F32)[None, :]
    emb = jnp.concatenate([t, jnp.cos(f * w), -jnp.sin(f * w), jnp.zeros((seq, LANES - HY_EMB), F32)], axis=-1)
    decay = jnp.exp(-t * jnp.linspace(HY_MIN_DECAY, HY_MAX_DECAY, D_MODEL, dtype=F32)[None, :])
    w1p = jnp.concatenate([w1, jnp.zeros((LANES - HY_EMB, HY_FFN), F32)], axis=0)
    dblk = 256
    out = jax.ShapeDtypeStruct((HY_ORDER, seq, D_MODEL), F32)
    full = lambda shape: pl.BlockSpec(shape, lambda j: (0,) * len(shape))
    ospec = pl.BlockSpec((HY_ORDER, seq, dblk), lambda j: (0, 0, j))
    return pl.pallas_call(
        functools.partial(_hy_filter_kernel, seq=seq),
        out_shape=(out, out, jax.ShapeDtypeStruct((HY_ORDER, 1, D_MODEL), F32)),
        grid=(D_MODEL // dblk,),
        in_specs=[
            full((seq, LANES)),
            pl.BlockSpec((seq, dblk), lambda j: (0, j)),
            full((LANES, HY_FFN)), full((1, HY_FFN)), full((1, HY_FFN)),
            full((HY_FFN, HY_FFN)), full((1, HY_FFN)), full((1, HY_FFN)),
            pl.BlockSpec((HY_FFN, 2 * HY_ORDER, dblk), lambda j: (0, 0, j)),
            full((seq, seq)), full((seq, seq)),
        ],
        out_specs=(ospec, ospec, pl.BlockSpec((HY_ORDER, 1, dblk), lambda j: (0, 0, j))),
        compiler_params=_cparams("arbitrary"),
        name=f"hy_filter_{seq}",
    )(emb, decay, w1p, b1[None, :], f1[None, :], w2, b2[None, :], f2[None, :],
      w3.reshape(HY_FFN, 2 * HY_ORDER, D_MODEL), cos, msin)


def _hy_conv_kernel(*refs, seq, nseq):
    (zv_ref, z1_ref, z2_ref, cv_ref, c1_ref, c2_ref, kr_ref, ki_ref, kn_ref, bias_ref,
     cos_ref, msin_ref) = refs[:12]
    y_ref = refs[-1]
    dblk = y_ref.shape[1]
    t = lax.broadcasted_iota(jnp.int32, (seq, dblk), 0)
    first, last = t == 0, t == seq - 1
    alt = _alternating((seq, dblk))
    seqs = range(nseq)
    rows = [slice(q * seq, (q + 1) * seq) for q in seqs]

    def short_conv(z_ref, c_ref):
        out = []
        for q in seqs:
            z = z_ref[rows[q], :].astype(F32)
            prev = jnp.where(first, 0.0, pltpu.roll(z, 1, 0))
            nxt = jnp.where(last, 0.0, pltpu.roll(z, seq - 1, 0))
            out.append(c_ref[0:1, :] * prev + c_ref[1:2, :] * z + c_ref[2:3, :] * nxt)
        return out

    def long_conv(s, o):
        sb = [_bf(s[q]) for q in seqs]
        x_re = [_dot(cos_ref[...], sb[q]) for q in seqs]
        x_im = [_dot(msin_ref[...], sb[q]) for q in seqs]
        y_nyq = [jnp.sum(alt * s[q], axis=0, keepdims=True) * kn_ref[o] for q in seqs]
        k_re, k_im = kr_ref[o], ki_ref[o]
        y_re = [x_re[q] * k_re - x_im[q] * k_im for q in seqs]
        y_im = [x_re[q] * k_im + x_im[q] * k_re for q in seqs]
        y = [_dot(cos_ref[...], _bf(y_re[q])) + _dot(msin_ref[...], _bf(y_im[q])) for q in seqs]
        return [y[q] + alt * y_nyq[q] + bias_ref[o] * s[q] for q in seqs]

    v = short_conv(zv_ref, cv_ref)
    x1 = short_conv(z1_ref, c1_ref)
    x2 = short_conv(z2_ref, c2_ref)
    c1 = long_conv(v, 0)
    c2 = long_conv([x1[q] * c1[q] for q in seqs], 1)
    for q in seqs:
        y_ref[rows[q], :] = _bf(x2[q] * c2[q])


def _hy_conv(st, ctx, z, conv_w, filt, bias, tables, prev):
    seq = st.lp if ctx else st.ls
    nb = st.bp if ctx else st.bs
    nseq = max(1, min(nb, 4, 2048 // seq))
    assert nb % nseq == 0
    off = 0 if ctx else st.tp // (nseq * seq)
    dblk = 256
    nd = D_MODEL // dblk
    kr, ki, kn = filt
    cos, msin = tables
    vspec = pl.BlockSpec((HY_ORDER, 1, dblk), lambda j, b: (0, 0, j))
    full = lambda shape: pl.BlockSpec(shape, lambda j, b: (0,) * len(shape))
    kspec = pl.BlockSpec((HY_ORDER, seq, dblk), lambda j, b: (0, 0, j))
    in_specs = [
        pl.BlockSpec((nseq * seq, dblk), lambda j, b: (off + b, j)),
        pl.BlockSpec((nseq * seq, dblk), lambda j, b: (off + b, nd + j)),
        pl.BlockSpec((nseq * seq, dblk), lambda j, b: (off + b, 2 * nd + j)),
        pl.BlockSpec((3, dblk), lambda j, b: (0, j)),
        pl.BlockSpec((3, dblk), lambda j, b: (0, nd + j)),
        pl.BlockSpec((3, dblk), lambda j, b: (0, 2 * nd + j)),
        kspec, kspec, vspec, vspec,
        full((seq, seq)), full((seq, seq)),
    ]
    args = [z, z, z, conv_w, conv_w, conv_w, kr, ki, kn, bias.reshape(HY_ORDER, 1, D_MODEL), cos, msin]
    aliases = {}
    if prev is not None:
        aliases = {len(args): 0}
        in_specs.append(pl.BlockSpec(memory_space=pl.ANY))
        args.append(prev)
    return pl.pallas_call(
        functools.partial(_hy_conv_kernel, seq=seq, nseq=nseq),
        out_shape=jax.ShapeDtypeStruct((st.t, D_MODEL), BF16),
        grid=(nd, nb // nseq),
        in_specs=in_specs,
        out_specs=pl.BlockSpec((nseq * seq, dblk), lambda j, b: (off + b, j)),
        input_output_aliases=aliases,
        compiler_params=_cparams("arbitrary", "arbitrary"),
        name="hy_conv_ctx" if ctx else "hy_conv_lat",
    )(*args)


def _hy_outproj_kernel(y_ref, x_ref, mod_ref, w_ref, n2w_ref, wrt_ref, brt_ref, x3_ref, h2_ref, lg_ref):
    mod = _mod_slices(mod_ref[...])
    g1, sh2, sc2 = mod[2], mod[3], mod[4]
    nt = y_ref.shape[0] // TM
    tiles = range(nt)
    rows = [slice(b * TM, (b + 1) * TM) for b in tiles]
    mix = [_dot(y_ref[rows[b], :], w_ref[...]) for b in tiles]
    x3 = [x_ref[rows[b], :] + g1 * mix[b] for b in tiles]
    h2 = [_rms(x3[b], n2w_ref[...]) * (1.0 + sc2) + sh2 for b in tiles]
    lg = [_router_logits(h2[b], wrt_ref, brt_ref) for b in tiles]
    for b in tiles:
        x3_ref[rows[b], :] = x3[b]
        _store_rows_per_vreg(h2_ref.at[pl.ds(b * TM * _RPV, TM * _RPV)], h2[b])
        lg_ref[b] = lg[b]


def _hy_outproj(st, y, x, mods, w, n2w, wrt, brt):
    tm = TM_TOK
    return pl.pallas_call(
        _hy_outproj_kernel,
        out_shape=(jax.ShapeDtypeStruct((st.t, D_MODEL), F32),
                   jax.ShapeDtypeStruct((st.t * _RPV, LANES), F32),
                   jax.ShapeDtypeStruct((st.t // TM, LG_ROWS, TM), F32)),
        grid=(st.t // tm,),
        in_specs=[
            pl.BlockSpec((tm, D_MODEL), lambda i: (i, 0)),
            pl.BlockSpec((tm, D_MODEL), lambda i: (i, 0)),
            pl.BlockSpec((None, 1, N_MOD * D_MODEL), lambda i: (st.mod_row(i, tm), 0, 0)),
            pl.BlockSpec((D_MODEL, D_MODEL), lambda i: (0, 0)),
            pl.BlockSpec((1, D_MODEL), lambda i: (0, 0)),
            pl.BlockSpec((LG_ROWS, D_MODEL), lambda i: (0, 0)),
            pl.BlockSpec((LG_ROWS, 1), lambda i: (0, 0)),
        ],
        out_specs=(pl.BlockSpec((tm, D_MODEL), lambda i: (i, 0)),
                   pl.BlockSpec((tm * _RPV, LANES), lambda i: (i, 0)),
                   pl.BlockSpec((tm // TM, LG_ROWS, TM), lambda i: (i, 0, 0))),
        compiler_params=_cparams("arbitrary"),
        name="hy_outproj",
    )(y, x, mods, w, n2w, wrt, brt)


def _final_kernel(x_ref, m_ref, mod_ref, nf_ref, yp_ref, ys_ref, *, ncb):
    i = pl.program_id(0)
    g2 = _mod_slices(mod_ref[...])[5]
    y = _rms(x_ref[...] + g2 * m_ref[...].astype(F32), nf_ref[...])

    @pl.when(i < ncb)
    def _():
        yp_ref[...] = y

    @pl.when(i >= ncb)
    def _():
        ys_ref[...] = y


def _final(st, x, moe, mods, nf):
    tm = TM_TOK
    ncb = st.tp // tm
    return pl.pallas_call(
        functools.partial(_final_kernel, ncb=ncb),
        out_shape=(jax.ShapeDtypeStruct((st.tp, D_MODEL), F32), jax.ShapeDtypeStruct((st.ts, D_MODEL), F32)),
        grid=(st.t // tm,),
        in_specs=[
            pl.BlockSpec((tm, D_MODEL), lambda i: (i, 0)),
            pl.BlockSpec((tm, D_MODEL), lambda i: (i, 0)),
            pl.BlockSpec((None, 1, N_MOD * D_MODEL), lambda i: (st.mod_row(i, tm), 0, 0)),
            pl.BlockSpec((1, D_MODEL), lambda i: (0, 0)),
        ],
        out_specs=(pl.BlockSpec((tm, D_MODEL), lambda i: (jnp.minimum(i, ncb - 1), 0)),
                   pl.BlockSpec((tm, D_MODEL), lambda i: (jnp.maximum(i - ncb, 0), 0))),
        compiler_params=_cparams("arbitrary"),
        name="final_norm",
    )(x, moe, mods, nf)


def _grid_pos_table(seq):
    rows = seq // GRID_W
    r, cl = jnp.meshgrid(jnp.arange(rows, dtype=F32), jnp.arange(GRID_W, dtype=F32), indexing='ij')
    quarter = D_MODEL // 4
    omega = POS_THETA ** (-jnp.arange(quarter, dtype=F32) / quarter)

    def enc(pos):
        a = pos.reshape(-1, 1) * omega[None, :]
        return jnp.concatenate([jnp.sin(a), jnp.cos(a)], axis=-1)

    return jnp.concatenate([enc(r), enc(cl)], axis=-1)


def _router_weights(w_group, b_group, w_router, b_router):
    pad = LG_ROWS - N_EXPERTS - N_GROUPS
    w = jnp.concatenate([w_router.T, w_group.T, jnp.zeros((pad, D_MODEL), F32)], axis=0)
    b = jnp.concatenate([b_router, b_group, jnp.zeros((pad,), F32)])[:, None]
    return w, b


def kernel(x_prompt, x_sample, state_gla, state_mlstm_c, state_mlstm_n, state_mlstm_m, c, c_ctx, norm1_w, norm2_w, norm_f_w, ada_w, ada_b, rec_w_in, gla_gk_w, gla_gk_b, mlstm_gate_b, gla_norm_w, mlstm_norm_w, rec_w_out, hy_w_in, hy_conv_w, hy_f_w1, hy_f_b1, hy_f_freq1, hy_f_w2, hy_f_b2, hy_f_freq2, hy_f_w3, hy_f_bias, hy_w_out, moe_w_group, moe_b_group, moe_w_router, moe_b_router, moe_w_gate, moe_w_up, moe_w_down):
    bp, lp, _ = x_prompt.shape
    bs, ls, _ = x_sample.shape
    st = _Streams(bp, lp, bs, ls)
    xp = x_prompt.reshape(st.tp, D_MODEL)
    xs = x_sample.reshape(st.ts, D_MODEL)

    nrow = -(-(1 + bs) // 8) * 8
    cv = jnp.concatenate([c_ctx[None, :], c, jnp.zeros((nrow - 1 - bs, D_MODEL), F32)], axis=0)
    mods = _ada(cv, ada_w, ada_b)
    mods0 = mods[0].reshape(nrow, 1, N_MOD * D_MODEL)
    mods1 = mods[1].reshape(nrow, 1, N_MOD * D_MODEL)

    pos = _grid_pos_table(ls)

    w_in = rec_w_in[0]
    w_main = _bf(w_in[:, :_REC_MAIN])
    w_mg = w_in[:, _REC_MAIN + _MG0:].reshape(D_MODEL, 2, 2, H_B)
    w_fg = jnp.pad(w_mg[:, :, 1, :], ((0, 0), (0, 0), (0, 8 - H_B))).reshape(D_MODEL, 16)
    w_gate = jnp.concatenate([w_in[:, _REC_MAIN:], jnp.zeros((D_MODEL, LANES - _REC_GATE), F32),
                              jnp.zeros((D_MODEL, _MG0), F32), w_fg,
                              jnp.zeros((D_MODEL, LANES - _REC_GATE), F32)], axis=1)
    w_gate_t = w_in[:, _REC_MAIN + _MG0:].T
    x0, main, g, g2, mt = _inproj(st, xp, xs, pos, mods0, norm1_w[0][None, :], w_main, w_gate, w_gate_t)

    gkw = jnp.zeros((2, LANES, H_A * DK_A), F32)
    gkw = gkw.at[0, :GK_RANK].set(gla_gk_w[0, 0]).at[1, GK_RANK:2 * GK_RANK].set(gla_gk_w[0, 1])
    gkb = gla_gk_b[0][:, None, :]
    gb = mlstm_gate_b[0]
    gate_row = lambda b: jnp.pad(jnp.pad(b, ((0, 0), (0, 8 - H_B))).reshape(1, 16), ((0, 0), (_MG0, LANES - _REC_GATE)))
    gbcol = gb.reshape(16, 1)
    wr0, br0 = _router_weights(moe_w_group[0], moe_b_group[0], moe_w_router[0], moe_b_router[0])
    consts = [gkw, gkb, gate_row(gb[:, 0]), gate_row(gb[:, 1]), gbcol, gla_norm_w[0][None, :],
              mlstm_norm_w[0][None, :], _bf(rec_w_out[0]), norm2_w[0][None, :], wr0, br0]

    def scan_states(sg, sc, sn, sm):
        nb = sg.shape[0]
        return (jnp.swapaxes(sg, -1, -2), sc, sn.reshape(nb, 2 * H_B, DK_B), sm.reshape(nb, 1, 2 * H_B))

    zero = (jnp.zeros((bp, 2, H_A, DK_A, DV_A), F32), jnp.zeros((bp, 2, H_B, DK_B, DV_B), F32),
            jnp.zeros((bp, 2, H_B, DK_B), F32), jnp.zeros((bp, 2, H_B), F32))
    cached = (state_gla[:, 0], state_mlstm_c[:, 0], state_mlstm_n[:, 0], state_mlstm_m[:, 0])
    x1, h2, lg, fg, fc, fn, fm = _scan(st, True, main, g, g2, mt, x0, mods0, scan_states(*zero), consts, None)
    x1, h2, lg = _scan(st, False, main, g, g2, mt, x0, mods0, scan_states(*cached), consts, (x1, h2, lg))[:3]

    new_gla = jnp.swapaxes(fg, -1, -2)[:, None]
    new_c = fc[:, None]
    new_n = fn.reshape(bp, 1, 2, H_B, DK_B)
    new_m = fm.reshape(bp, 1, 2, H_B)

    wg, wu, wd = (_expert_weights_bf16(w) for w in (moe_w_gate, moe_w_up, moe_w_down))
    moe0 = _moe(st, 0, h2, _route(st, lg), wg, wu, wd)

    x2, z = _hy_inproj(st, x1, moe0, mods0, mods1, norm1_w[1][None, :], _bf(hy_w_in[0]))
    y = None
    for ctx in (True, False):
        seq = lp if ctx else ls
        cos, msin, cos_b, msin_b = _dft_tables(seq)
        filt = _hy_filter(seq, hy_f_w1[0], hy_f_b1[0], hy_f_freq1[0], hy_f_w2[0], hy_f_b2[0], hy_f_freq2[0],
                          hy_f_w3[0], cos, msin)
        y = _hy_conv(st, ctx, z, hy_conv_w[0], filt, hy_f_bias[0], (cos_b, msin_b), y)
    wr1, br1 = _router_weights(moe_w_group[1], moe_b_group[1], moe_w_router[1], moe_b_router[1])
    x3, h4, lg1 = _hy_outproj(st, y, x2, mods1, _bf(hy_w_out[0]), norm2_w[1][None, :], wr1, br1)

    moe1 = _moe(st, 1, h4, _route(st, lg1), wg, wu, wd)
    y_prompt, y_sample = _final(st, x3, moe1, mods1, norm_f_w[None, :])
    return (y_prompt.reshape(bp, lp, D_MODEL), y_sample.reshape(bs, ls, D_MODEL), new_gla, new_c, new_n, new_m)
```

```python
import functools
import math

import jax
import jax.numpy as jnp
from jax import lax
from jax.experimental import pallas as pl
from jax.experimental.pallas import tpu as pltpu

F32 = jnp.float32
BF16 = jnp.bfloat16

D_MODEL = 1024
GRID_W = 64
H_A = 4
DK_A = D_MODEL // 16
DV_A = D_MODEL // 8
GK_RANK = 16
GATE_TEMP = 16.0
H_B = 4
DK_B = D_MODEL // 16
DV_B = D_MODEL // 8
CHUNK = 64
HY_ORDER = 2
HY_EMB = 33
HY_BANDS = (HY_EMB - 1) // 2
HY_FFN = 64
HY_TARGET = 1e-2
HY_MAX_DECAY = abs(math.log(HY_TARGET)) / 0.3
HY_MIN_DECAY = abs(math.log(HY_TARGET)) / 1.5
N_GROUPS = 4
EXP_PER_GROUP = 4
N_EXPERTS = N_GROUPS * EXP_PER_GROUP
D_EXPERT = D_MODEL // 2
N_MOD = 6
POS_THETA = 10000.0
EPS = 1e-6

_QA, _KA, _VA, _RA = 0, 256, 512, 1024
_QB, _KB, _VB, _OB = 1536, 1792, 2048, 2560
_REC_MAIN = 3072
_REC_GATE = 48
_MG0 = 2 * GK_RANK

LANES = 128
SUBLANES = 8
_RPV = D_MODEL // LANES
assert _RPV == SUBLANES
TM = 256
TM_TOK = 1024
TM_IN = 512
SB_MOE = 2048
ROW_PAD = 16
EXPERTS_PER_STEP = 2
FFN_TILE = 288
LG_ROWS = 32
VMEM_LIMIT = 58 * 1024 * 1024


def _cparams(*sem):
    return pltpu.CompilerParams(dimension_semantics=sem, vmem_limit_bytes=VMEM_LIMIT)


def _bf(x):
    return x.astype(BF16)


def _dot(a, b):
    return jnp.dot(a, b, preferred_element_type=F32)


def _dot_nt(a, b):
    return lax.dot_general(a, b, (((1,), (1,)), ((), ())), preferred_element_type=F32)


def _dot_tn(a, b):
    return lax.dot_general(a, b, (((0,), (0,)), ((), ())), preferred_element_type=F32)


def _split2(x):
    hi = _bf(x)
    return hi, _bf(x - hi.astype(F32))


def _split3(x):
    hi = _bf(x)
    r = x - hi.astype(F32)
    mid = _bf(r)
    return hi, mid, _bf(r - mid.astype(F32))


def _dot_hp(a, b, dot=_dot):
    ah, al = _split2(a)
    bh, bl = _split2(b)
    return dot(ah, bh) + (dot(ah, bl) + dot(al, bh))


def _dot_mask_l(m, x, terms=3):
    if terms == 2:
        x1, x2 = _split2(x)
        return _dot(m, x1) + _dot(m, x2)
    x1, x2, x3 = _split3(x)
    return _dot(m, x1) + (_dot(m, x2) + _dot(m, x3))


def _dot_mask_r(x, m, terms=3):
    if terms == 2:
        x1, x2 = _split2(x)
        return _dot(x1, m) + _dot(x2, m)
    x1, x2, x3 = _split3(x)
    return _dot(x1, m) + (_dot(x2, m) + _dot(x3, m))


def _rms(x, w):
    return x * lax.rsqrt(jnp.mean(x * x, axis=-1, keepdims=True) + EPS) * w


def _mod_slices(mod):
    return [mod[:, k * D_MODEL:(k + 1) * D_MODEL] for k in range(N_MOD)]


def _store_rows_per_vreg(dst, h):
    rows = h.shape[0]
    for j in range(_RPV):
        dst[pl.ds(j, rows, stride=_RPV), :] = h[:, j * LANES:(j + 1) * LANES]


def _load_rows_per_vreg(src, rows):
    return jnp.concatenate([src[pl.ds(j, rows, stride=_RPV), :] for j in range(_RPV)], axis=-1)


def _router_logits(h, wrt_ref, brt_ref):
    return _dot_hp(wrt_ref[...], h, dot=_dot_nt) + brt_ref[...]


def _ada_kernel(cv_ref, w_ref, b_ref, o_ref):
    a = cv_ref[...]
    a = a * jax.nn.sigmoid(a)
    o_ref[...] = _dot_hp(a, w_ref[...]) + b_ref[...]


def _ada(cv, ada_w, ada_b):
    depth, d, n = ada_w.shape
    rows = cv.shape[0]
    tn = 1536
    return pl.pallas_call(
        _ada_kernel,
        out_shape=jax.ShapeDtypeStruct((depth, rows, n), F32),
        grid=(depth, n // tn),
        in_specs=[
            pl.BlockSpec((rows, d), lambda l, j: (0, 0)),
            pl.BlockSpec((None, d, tn), lambda l, j: (l, 0, j)),
            pl.BlockSpec((None, 1, tn), lambda l, j: (l, 0, j)),
        ],
        out_specs=pl.BlockSpec((None, rows, tn), lambda l, j: (l, 0, j)),
        compiler_params=_cparams("arbitrary", "arbitrary"),
        name="ada_mod",
    )(cv, ada_w, ada_b.reshape(depth, 1, n))


class _Streams:
    def __init__(self, bp, lp, bs, ls):
        self.bp, self.lp, self.bs, self.ls = bp, lp, bs, ls
        self.tp, self.ts = bp * lp, bs * ls
        self.t = self.tp + self.ts
        assert lp % TM == 0 and ls % TM == 0 and self.tp % ls == 0
        assert self.tp % TM_TOK == 0 and ls % TM_TOK == 0
        self.sb = min(SB_MOE, math.gcd(self.tp, self.ts))
        assert self.t % self.sb == 0 and self.sb % TM == 0

    def mod_row(self, i, tm):
        ncb = self.tp // tm
        return jnp.where(i < ncb, 0, 1 + (i - ncb) // (self.ls // tm))


def _inproj_kernel(xp_ref, xs_ref, pos_ref, mod_ref, nw_ref, w_ref, wg_ref, wgt_ref,
                   x0_ref, main_ref, g_ref, g2_ref, mt_ref, *, ncb):
    i = pl.program_id(0)
    x = jnp.where(i < ncb, xp_ref[...], xs_ref[...] + pos_ref[...])
    x0_ref[...] = x
    sh1, sc1 = _mod_slices(mod_ref[...])[:2]
    h = _rms(x, nw_ref[...]) * (1.0 + sc1) + sh1
    main_ref[...] = _bf(_dot(_bf(h), w_ref[...]))
    gates = _dot_hp(h, wg_ref[...])
    g_ref[...] = gates[:, :LANES]
    g2_ref[...] = gates[:, LANES:]
    mt = _dot_hp(wgt_ref[...], h, dot=_dot_nt)
    for c in range(mt.shape[1] // CHUNK):
        piece = mt[:, c * CHUNK:(c + 1) * CHUNK]
        mt_ref[c] = jnp.concatenate([piece, piece], axis=1)


def _inproj(st, xp, xs, pos, mods, nw, w_main, w_gate, w_gate_t):
    tm = TM_IN
    assert st.tp % tm == 0 and st.ls % tm == 0
    ncb = st.tp // tm
    bps = st.ls // tm
    t = st.t
    return pl.pallas_call(
        functools.partial(_inproj_kernel, ncb=ncb),
        out_shape=(jax.ShapeDtypeStruct((t, D_MODEL), F32),
                   jax.ShapeDtypeStruct((t, _REC_MAIN), BF16),
                   jax.ShapeDtypeStruct((t, LANES), F32),
                   jax.ShapeDtypeStruct((t, LANES), F32),
                   jax.ShapeDtypeStruct((t // CHUNK, 16, 2 * CHUNK), F32)),
        grid=(t // tm,),
        in_specs=[
            pl.BlockSpec((tm, D_MODEL), lambda i: (jnp.minimum(i, ncb - 1), 0)),
            pl.BlockSpec((tm, D_MODEL), lambda i: (jnp.maximum(i - ncb, 0), 0)),
            pl.BlockSpec((tm, D_MODEL), lambda i: (jnp.maximum(i - ncb, 0) % bps, 0)),
            pl.BlockSpec((None, 1, N_MOD * D_MODEL), lambda i: (st.mod_row(i, tm), 0, 0)),
            pl.BlockSpec((1, D_MODEL), lambda i: (0, 0)),
            pl.BlockSpec((D_MODEL, _REC_MAIN), lambda i: (0, 0)),
            pl.BlockSpec((D_MODEL, 2 * LANES), lambda i: (0, 0)),
            pl.BlockSpec((16, D_MODEL), lambda i: (0, 0)),
        ],
        out_specs=(pl.BlockSpec((tm, D_MODEL), lambda i: (i, 0)),
                   pl.BlockSpec((tm, _REC_MAIN), lambda i: (i, 0)),
                   pl.BlockSpec((tm, LANES), lambda i: (i, 0)),
                   pl.BlockSpec((tm, LANES), lambda i: (i, 0)),
                   pl.BlockSpec((tm // CHUNK, 16, 2 * CHUNK), lambda i: (i, 0, 0))),
        compiler_params=_cparams("arbitrary"),
        name="rec_inproj",
    )(xp, xs, pos, mods, nw, w_main, w_gate, w_gate_t)


def _block_diag(x):
    left = lax.broadcasted_iota(jnp.int32, (1, x.shape[1]), 1) < x.shape[1] // 2
    zero = jnp.zeros_like(x)
    return jnp.concatenate([jnp.where(left, x, zero), jnp.where(left, zero, x)], axis=0)


def _block_diag_mask(rows, width):
    r = lax.broadcasted_iota(jnp.int32, (rows, width), 0) < rows // 2
    l = lax.broadcasted_iota(jnp.int32, (rows, width), 1) < width // 2
    return r == l


def _running_max(x, reverse):
    n = x.shape[0]
    row = lax.broadcasted_iota(jnp.int32, x.shape, 0)
    sh = 1
    while sh < n:
        if reverse:
            y = jnp.where(row < n - sh, pltpu.roll(x, n - sh, 0), -jnp.inf)
        else:
            y = jnp.where(row >= sh, pltpu.roll(x, sh, 0), -jnp.inf)
        x = jnp.maximum(x, y)
        sh *= 2
    return x


def _scan_kernel(*refs, seq):
    (main_ref, g_ref, g2_ref, mt_ref, x_ref, mod_ref, sg_ref, sc_ref, sn_ref, sm_ref,
     gkw_ref, gkb_ref, gbi_ref, gbf_ref, gbcol_ref, gnw_ref, mnw_ref, wout_ref, n2w_ref, wrt_ref, brt_ref) = refs[:21]
    (x1_ref, h2_ref, lg_ref, og_ref, oc_ref, on_ref, om_ref,
     oa_scr, ob_scr, sbd_scr, cbd_scr, nbd_scr, mgl_scr) = refs[-13:]
    c = CHUNK
    nchunks = seq // c
    npair = H_A // 2
    assert H_A == H_B and DK_A == DK_B == c and DV_A == DV_B == LANES and 2 * DK_A == LANES

    lane = lax.broadcasted_iota(jnp.int32, (1, LANES), 1)
    gate_lane = lambda d, h: _MG0 + d * 8 + h
    used = tuple((lane >= gate_lane(d, 0)) & (lane < gate_lane(d, H_B)) for d in (0, 1))

    for d in (0, 1):
        for p in range(npair):
            k = d * npair + p
            z_s = jnp.zeros((DV_A, DK_A), F32)
            sbd_scr[k] = jnp.concatenate([jnp.concatenate([sg_ref[d, 2 * p], z_s], axis=1),
                                          jnp.concatenate([z_s, sg_ref[d, 2 * p + 1]], axis=1)], axis=0)
            z_c = jnp.zeros((DK_B, DV_B), F32)
            cbd_scr[k] = jnp.concatenate([jnp.concatenate([sc_ref[d, 2 * p], z_c], axis=1),
                                          jnp.concatenate([z_c, sc_ref[d, 2 * p + 1]], axis=1)], axis=0)
            n_rep = [jnp.broadcast_to(sn_ref[d * H_B + 2 * p + q:d * H_B + 2 * p + q + 1, :], (DV_B, DK_B)).T
                     for q in (0, 1)]
            nbd_scr[k] = jnp.concatenate([jnp.concatenate([n_rep[0], z_c], axis=1),
                                          jnp.concatenate([z_c, n_rep[1]], axis=1)], axis=0)
    m_gl = jnp.zeros((1, LANES), F32)
    for d in (0, 1):
        for h in range(H_B):
            r = d * H_B + h
            m_gl = jnp.where(lane == gate_lane(d, h), sm_ref[:, r:r + 1], m_gl)
    mgl_scr[...] = m_gl

    row_p = lax.broadcasted_iota(jnp.int32, (c, LANES), 0)
    s_p = lax.broadcasted_iota(jnp.int32, (c, LANES), 1) % c
    live_p = (s_p <= row_p, s_p >= row_p)
    row = lax.broadcasted_iota(jnp.int32, (c, c), 0)
    col = lax.broadcasted_iota(jnp.int32, (c, c), 1)
    tri = tuple(jnp.where(m, 1.0, 0.0).astype(BF16) for m in (col <= row, col >= row))
    tri_t2 = tuple(jnp.where(m, 1.0, 0.0).astype(BF16) for m in (row_p <= s_p, row_p >= s_p))
    grow = lax.broadcasted_iota(jnp.int32, (16, 1), 0)
    f_row = (grow % 8) >= 4
    scale_q = DK_A ** -0.5
    bd_val = _block_diag_mask(2 * DV_A, 2 * DK_A)
    bd_key = _block_diag_mask(2 * DK_B, 2 * DV_B)
    ones_bd = jnp.where(bd_key, 1.0, 0.0).astype(BF16)
    ones_cv = jnp.ones((c, 2 * DV_B), BF16)

    def replicate(d, width):
        r = lax.broadcasted_iota(jnp.int32, (LANES, H_B * width), 0)
        h = lax.broadcasted_iota(jnp.int32, (LANES, H_B * width), 1) // width
        return jnp.where(r == gate_lane(d, 0) + h, 1.0, 0.0).astype(BF16)

    rep_k = tuple(replicate(d, DK_B) for d in (0, 1))
    rep_v = tuple(replicate(d, DV_B) for d in (0, 1))

    def chunk_step(i, carry):
        rows, g_in, g2_in, mt_in, gla_in, mls_in = [], [], [], [], [], []
        for d in (0, 1):
            ci = i if d == 0 else nchunks - 1 - i
            rows.append(pl.ds(pl.multiple_of(ci * c, c), c))
            g_in.append(g_ref[rows[d], :])
            g2_in.append(g2_ref[rows[d], :])
            mt_in.append(mt_ref[ci])
            gla_in.append(main_ref[rows[d], _QA:_RA])
            mls_in.append(main_ref[rows[d], _QB:_OB])
        s_bd = [sbd_scr[k] for k in range(2 * npair)]
        c_bd = [cbd_scr[k] for k in range(2 * npair)]
        n_bd = [nbd_scr[k] for k in range(2 * npair)]
        m_gl = mgl_scr[...]
        dirs = (0, 1)
        ends = (c - 1, 0)
        pairs = [(d, p) for d in dirs for p in range(npair)]
        ks_of = lambda p: slice(p * 2 * DK_A, (p + 1) * 2 * DK_A)
        vs_of = lambda p: slice(p * 2 * DV_A, (p + 1) * 2 * DV_A)

        glin = [_dot_hp(g_in[d], gkw_ref[d]) + gkb_ref[d] for d in dirs]
        gi = [jnp.where(used[d], g_in[d] + gbi_ref[...], 0.0) for d in dirs]
        lf = [jnp.where(used[d], jax.nn.log_sigmoid(g2_in[d] + gbf_ref[...]), 0.0) for d in dirs]
        mt = [mt_in[d] + gbcol_ref[...] for d in dirs]
        mt = [jnp.where(f_row, jax.nn.log_sigmoid(mt[d]), mt[d]) for d in dirs]
        glog = [jax.nn.log_sigmoid(glin[d]) / GATE_TEMP for d in dirs]
        bc = [_dot_mask_l(tri[d], glog[d], 2) for d in dirs]
        cum = [_dot_mask_l(tri[d], lf[d], 2) for d in dirs]
        cum_t = [_dot_mask_r(mt[d][:, 0:c], tri_t2[d], 2) for d in dirs]
        b_end = [bc[d][ends[d]:ends[d] + 1, :] for d in dirs]
        qa = [gla_in[d][:, _QA:_QA + 256].astype(F32) * scale_q for d in dirs]
        ka = [gla_in[d][:, _KA:_KA + 256].astype(F32) for d in dirs]
        va = [gla_in[d][:, _VA:_VA + 512] for d in dirs]
        qe = [_bf(qa[d] * jnp.exp(bc[d])) for d in dirs]
        ke = [_bf(ka[d] * jnp.exp(-bc[d])) for d in dirs]
        kd = [_bf(ka[d] * jnp.exp(b_end[d] - bc[d])) for d in dirs]
        eb_end = [jnp.exp(b_end[d]) for d in dirs]
        qb = [mls_in[d][:, 0:256] * jnp.asarray(DK_B ** -0.5, BF16) for d in dirs]
        kb = [mls_in[d][:, _KB - _QB:_KB - _QB + 256] for d in dirs]
        vb = [mls_in[d][:, _VB - _QB:_VB - _QB + 512] for d in dirs]
        a_raw = {(d, p): _dot_nt(qe[d][:, ks_of(p)], _block_diag(ke[d][:, ks_of(p)])) for d, p in pairs}
        qk = {(d, p): _dot_nt(qb[d][:, ks_of(p)], _block_diag(kb[d][:, ks_of(p)])) for d, p in pairs}
        s_upd = {(d, p): _dot_tn(va[d][:, vs_of(p)], kd[d][:, ks_of(p)]) for d, p in pairs}
        o_car = {(d, p): _dot_nt(qe[d][:, ks_of(p)], _bf(s_bd[d * npair + p])) for d, p in pairs}
        m_loc = [cum[d] + _running_max(gi[d] - cum[d], reverse=(d == 1)) for d in dirs]
        inter = [cum[d] + m_gl for d in dirs]
        m_t = [jnp.maximum(inter[d], m_loc[d]) for d in dirs]
        b_last = [cum[d][ends[d]:ends[d] + 1, :] for d in dirs]
        dend = [b_last[d] - cum[d] + gi[d] for d in dirs]
        m_new = [jnp.maximum(b_last[d] + m_gl, jnp.max(dend[d], axis=0, keepdims=True)) for d in dirs]
        zero = jnp.zeros((c, LANES), F32)
        per_key = [jnp.concatenate([jnp.where(used[d], cum[d] - m_t[d], zero),
                                    jnp.where(used[d], jnp.exp(inter[d] - m_t[d]), zero),
                                    jnp.where(used[d], jnp.exp(dend[d] - m_new[d]), zero)], axis=0) for d in dirs]
        per_val = [jnp.concatenate([jnp.where(used[d], jnp.exp(-m_t[d]), zero),
                                    jnp.broadcast_to(jnp.where(used[d], jnp.exp(b_last[d] + m_gl - m_new[d]), 0.0),
                                                     (SUBLANES, LANES))], axis=0) for d in dirs]
        per_key = [_dot_mask_r(per_key[d], rep_k[d], 2) for d in dirs]
        per_val = [_dot_mask_r(per_val[d], rep_v[d], 2) for d in dirs]
        o_par = {(d, p): _dot(_bf(jnp.where(live_p[d], a_raw[d, p], 0.0)), _block_diag(va[d][:, vs_of(p)]))
                 for d, p in pairs}
        for d, p in pairs:
            k = d * npair + p
            s_bd[k] = s_bd[k] * eb_end[d][:, ks_of(p)] + jnp.where(bd_val, s_upd[d, p], 0.0)
        o_gla = [jnp.concatenate([o_par[d, p] + o_car[d, p] for p in range(npair)], axis=-1) for d in dirs]
        for d in dirs:
            m_gl = jnp.where(used[d], m_new[d], m_gl)

        w, qa2, kw = {}, {}, {}
        for d, p in pairs:
            ks = ks_of(p)
            r_i, r_f = d * 8 + 2 * p, d * 8 + 4 + 2 * p
            sub = [cum_t[d][r_f + q:r_f + q + 1, :] - mt[d][r_i + q:r_i + q + 1, :] for q in (0, 1)]
            sub = jnp.where(lane < DK_B, sub[0], sub[1])
            w[d, p] = jnp.exp(jnp.where(live_p[d], per_key[d][0:c, ks] - sub, -jnp.inf))
            qa2[d, p] = _bf(qb[d][:, ks].astype(F32) * per_key[d][c:2 * c, ks])
            kw[d, p] = _bf(kb[d][:, ks].astype(F32) * per_key[d][2 * c:3 * c, ks])
        carried = {(d, p): _dot(qa2[d, p], jnp.concatenate([_bf(c_bd[d * npair + p]), _bf(n_bd[d * npair + p])], axis=1))
                   for d, p in pairs}
        upd = {(d, p): _dot_tn(kw[d, p], jnp.concatenate([vb[d][:, vs_of(p)], ones_cv], axis=1)) for d, p in pairs}
        intra = {(d, p): _dot(_bf(qk[d, p] * w[d, p]),
                              jnp.concatenate([_block_diag(vb[d][:, vs_of(p)]), ones_bd], axis=1)) for d, p in pairs}
        outs = {}
        for d, p in pairs:
            k = d * npair + p
            vs = vs_of(p)
            num = intra[d, p][:, :2 * DV_B] + carried[d, p][:, :2 * DV_B]
            den = intra[d, p][:, 2 * DV_B:] + carried[d, p][:, 2 * DV_B:]
            outs[d, p] = num / jnp.maximum(jnp.abs(den), per_val[d][0:c, vs])
            decay = per_val[d][c:c + 1, vs]
            dec = jnp.concatenate([jnp.broadcast_to(jnp.concatenate([decay[:, q * DV_B:(q + 1) * DV_B]] * 2, axis=1),
                                                    (DK_B, 2 * DV_B)) for q in (0, 1)], axis=0)
            c_bd[k] = dec * c_bd[k] + jnp.where(bd_key, upd[d, p][:, :2 * DV_B], 0.0)
            n_bd[k] = dec * n_bd[k] + jnp.where(bd_key, upd[d, p][:, 2 * DV_B:], 0.0)
        o_mls = [jnp.concatenate([outs[d, p] for p in range(npair)], axis=-1) for d in dirs]

        for d in (0, 1):
            oa_scr[d, rows[d], :] = o_gla[d]
            ob_scr[d, rows[d], :] = o_mls[d]
        for k in range(2 * npair):
            sbd_scr[k] = s_bd[k]
            cbd_scr[k] = c_bd[k]
            nbd_scr[k] = n_bd[k]
        mgl_scr[...] = m_gl
        return carry

    lax.fori_loop(0, nchunks, chunk_step, 0)

    for d in (0, 1):
        for p in range(npair):
            k = d * npair + p
            s_t, c_f, n_f = sbd_scr[k], cbd_scr[k], nbd_scr[k]
            for q in (0, 1):
                h = 2 * p + q
                og_ref[d, h] = s_t[q * DV_A:(q + 1) * DV_A, q * DK_A:(q + 1) * DK_A]
                oc_ref[d, h] = c_f[q * DK_B:(q + 1) * DK_B, q * DV_B:(q + 1) * DV_B]
                n_t = n_f[q * DK_B:(q + 1) * DK_B, q * DV_B:(q + 1) * DV_B].T
                on_ref[d * H_B + h:d * H_B + h + 1, :] = n_t[0:1, :]
    m_gl = mgl_scr[...]
    om_ref[...] = jnp.concatenate([m_gl[:, gate_lane(d, h):gate_lane(d, h) + 1]
                                   for d in (0, 1) for h in range(H_B)], axis=1)

    mod = _mod_slices(mod_ref[...])
    g1, sh2, sc2 = mod[2], mod[3], mod[4]

    def out_step(j, carry):
        hm = TM // 2
        halves = (0, 1)
        r0 = [pl.multiple_of(j * TM + q * hm, hm) for q in halves]
        rows = [pl.ds(r0[q], hm) for q in halves]
        parts = [[], []]
        for h in range(H_A):
            vs = slice(h * DV_A, (h + 1) * DV_A)
            for q in halves:
                ra = main_ref[rows[q], _RA + h * DV_A:_RA + (h + 1) * DV_A].astype(F32)
                o = oa_scr[0, rows[q], vs] + oa_scr[1, rows[q], vs]
                parts[q].append(_rms(o, gnw_ref[...]) * (ra * jax.nn.sigmoid(ra)))
        for h in range(H_B):
            vs = slice(h * DV_B, (h + 1) * DV_B)
            for q in halves:
                ob = main_ref[rows[q], _OB + h * DV_B:_OB + (h + 1) * DV_B].astype(F32)
                o = ob_scr[0, rows[q], vs] + ob_scr[1, rows[q], vs]
                parts[q].append(_rms(o, mnw_ref[...]) * jax.nn.sigmoid(ob))
        mix = [_dot(_bf(jnp.concatenate(parts[q], axis=-1)), wout_ref[...]) for q in halves]
        x1 = [x_ref[rows[q], :] + g1 * mix[q] for q in halves]
        h2 = [_rms(x1[q], n2w_ref[...]) * (1.0 + sc2) + sh2 for q in halves]
        lg = [_router_logits(h2[q], wrt_ref, brt_ref) for q in halves]
        for q in halves:
            x1_ref[rows[q], :] = x1[q]
            _store_rows_per_vreg(h2_ref.at[pl.ds(pl.multiple_of(r0[q] * _RPV, hm * _RPV), hm * _RPV)], h2[q])
        lg_ref[j] = jnp.concatenate(lg, axis=1)
        return carry

    lax.fori_loop(0, seq // TM, out_step, 0)


def _scan(st, ctx, main, g, g2, mt, x, mods, states, consts, prev):
    seq = st.lp if ctx else st.ls
    nb = st.bp if ctx else st.bs
    off = 0 if ctx else st.tp // st.ls
    sg, sc, sn, sm = states
    full = lambda shape: pl.BlockSpec(shape, lambda b: (0,) * len(shape))
    in_specs = [
        pl.BlockSpec((seq, _REC_MAIN), lambda b: (off + b, 0)),
        pl.BlockSpec((seq, LANES), lambda b: (off + b, 0)),
        pl.BlockSpec((seq, LANES), lambda b: (off + b, 0)),
        pl.BlockSpec((seq // CHUNK, 16, 2 * CHUNK), lambda b: (off + b, 0, 0)),
        pl.BlockSpec((seq, D_MODEL), lambda b: (off + b, 0)),
    ]
    args = [main, g, g2, mt, x]
    mod_row = (lambda b: (0, 0, 0)) if ctx else (lambda b: (1 + b, 0, 0))
    in_specs += [
        pl.BlockSpec((None, 1, N_MOD * D_MODEL), mod_row),
        pl.BlockSpec((None, 2, H_A, DV_A, DK_A), lambda b: (b, 0, 0, 0, 0)),
        pl.BlockSpec((None, 2, H_B, DK_B, DV_B), lambda b: (b, 0, 0, 0, 0)),
        pl.BlockSpec((None, 2 * H_B, DK_B), lambda b: (b, 0, 0)),
        pl.BlockSpec((None, 1, 2 * H_B), lambda b: (b, 0, 0)),
    ]
    args += [mods, sg, sc, sn, sm]
    for a in consts:
        in_specs.append(full(a.shape))
        args.append(a)
    aliases = {}
    if prev is not None:
        aliases = {len(args) + k: k for k in range(3)}
        in_specs += [pl.BlockSpec(memory_space=pl.ANY)] * 3
        args += list(prev)
    out_shape = (jax.ShapeDtypeStruct((st.t, D_MODEL), F32),
                 jax.ShapeDtypeStruct((st.t * _RPV, LANES), F32),
                 jax.ShapeDtypeStruct((st.t // TM, LG_ROWS, TM), F32),
                 jax.ShapeDtypeStruct(sg.shape, F32), jax.ShapeDtypeStruct(sc.shape, F32),
                 jax.ShapeDtypeStruct(sn.shape, F32), jax.ShapeDtypeStruct(sm.shape, F32))
    out_specs = (pl.BlockSpec((seq, D_MODEL), lambda b: (off + b, 0)),
                 pl.BlockSpec((seq * _RPV, LANES), lambda b: (off + b, 0)),
                 pl.BlockSpec((seq // TM, LG_ROWS, TM), lambda b: (off + b, 0, 0)),
                 pl.BlockSpec((None, 2, H_A, DV_A, DK_A), lambda b: (b, 0, 0, 0, 0)),
                 pl.BlockSpec((None, 2, H_B, DK_B, DV_B), lambda b: (b, 0, 0, 0, 0)),
                 pl.BlockSpec((None, 2 * H_B, DK_B), lambda b: (b, 0, 0)),
                 pl.BlockSpec((None, 1, 2 * H_B), lambda b: (b, 0, 0)))
    return pl.pallas_call(
        functools.partial(_scan_kernel, seq=seq),
        out_shape=out_shape,
        grid=(nb,),
        in_specs=in_specs,
        out_specs=out_specs,
        scratch_shapes=[pltpu.VMEM((2, seq, H_A * DV_A), F32), pltpu.VMEM((2, seq, H_B * DV_B), F32),
                        pltpu.VMEM((H_A, 2 * DV_A, 2 * DK_A), F32), pltpu.VMEM((H_B, 2 * DK_B, 2 * DV_B), F32),
                        pltpu.VMEM((H_B, 2 * DK_B, 2 * DV_B), F32), pltpu.VMEM((1, LANES), F32)],
        input_output_aliases=aliases,
        compiler_params=_cparams("arbitrary"),
        name="rec_scan_ctx" if ctx else "rec_scan_lat",
    )(*args)


def _first_max(rows):
    m = rows[0]
    for r in rows[1:]:
        m = jnp.maximum(m, r)
    idx = jnp.full(m.shape, len(rows) - 1, jnp.int32)
    for k in range(len(rows) - 2, -1, -1):
        idx = jnp.where(rows[k] == m, k, idx)
    return m, idx


def _route_kernel(lg_ref, pos1_ref, pos2_ref, w1_ref, w2_ref, tab_ref, *, sb):
    lg = jnp.concatenate([lg_ref[b] for b in range(sb // TM)], axis=1)
    rows = [lg[k:k + 1, :] for k in range(N_EXPERTS + N_GROUPS)]
    grp = rows[N_EXPERTS:]
    gmax, gidx = _first_max(grp)
    p_group = 1.0 / sum(jnp.exp(r - gmax) for r in grp)
    e_in = []
    for k in range(EXP_PER_GROUP):
        v = rows[(N_GROUPS - 1) * EXP_PER_GROUP + k]
        for g in range(N_GROUPS - 2, -1, -1):
            v = jnp.where(gidx == g, rows[g * EXP_PER_GROUP + k], v)
        e_in.append(v)
    v1, i1 = _first_max(e_in)
    v2, i2 = _first_max([jnp.where(i1 == k, -jnp.inf, e_in[k]) for k in range(EXP_PER_GROUP)])
    ex = jnp.exp(v2 - v1)
    w1_ref[...] = p_group / (1.0 + ex)
    w2_ref[...] = p_group * ex / (1.0 + ex)
    x1 = gidx * EXP_PER_GROUP + i1
    x2 = gidx * EXP_PER_GROUP + i2

    eid = lax.broadcasted_iota(jnp.int32, (N_EXPERTS, sb), 0)
    sel = jnp.where((eid == x1) | (eid == x2), 1.0, 0.0)
    r_i = lax.broadcasted_iota(jnp.int32, (TM, TM), 0)
    c_i = lax.broadcasted_iota(jnp.int32, (TM, TM), 1)
    before = jnp.where(r_i < c_i, 1.0, 0.0).astype(BF16)
    carry = jnp.zeros((N_EXPERTS, 1), F32)
    ranks = []
    for b in range(sb // TM):
        s_b = sel[:, b * TM:(b + 1) * TM]
        ranks.append(_dot(_bf(s_b), before) + carry)
        carry = carry + jnp.sum(s_b, axis=1, keepdims=True)
    rank = jnp.concatenate(ranks, axis=1)
    shift = ROW_PAD.bit_length() - 1
    npad = jnp.left_shift(jnp.right_shift(carry.astype(jnp.int32) + (ROW_PAD - 1), shift), shift)

    lane = lax.broadcasted_iota(jnp.int32, (1, LANES), 1)
    tab = jnp.zeros((1, LANES), jnp.int32)
    pos1 = jnp.zeros((1, sb), F32)
    pos2 = jnp.zeros((1, sb), F32)
    off = jnp.zeros((1, 1), jnp.int32)
    for e in range(N_EXPERTS):
        n_e = npad[e:e + 1, :]
        tab = jnp.where(lane == e, off, tab)
        tab = jnp.where(lane == N_EXPERTS + e, n_e, tab)
        row = off.astype(F32) + rank[e:e + 1, :]
        pos1 = jnp.where(x1 == e, row, pos1)
        pos2 = jnp.where(x2 == e, row, pos2)
        off = off + n_e
    pos1_ref[...] = pos1.astype(jnp.int32) * _RPV
    pos2_ref[...] = pos2.astype(jnp.int32) * _RPV
    tab_ref[...] = tab


def _route(st, lg):
    sb = st.sb
    nsb = st.t // sb
    row_i = jax.ShapeDtypeStruct((nsb, 1, sb), jnp.int32)
    row_f = jax.ShapeDtypeStruct((nsb, 1, sb), F32)
    rspec = pl.BlockSpec((None, 1, sb), lambda s: (s, 0, 0))
    return pl.pallas_call(
        functools.partial(_route_kernel, sb=sb),
        out_shape=(row_i, row_i, row_f, row_f, jax.ShapeDtypeStruct((nsb, 1, LANES), jnp.int32)),
        grid=(nsb,),
        in_specs=[pl.BlockSpec((sb // TM, LG_ROWS, TM), lambda s: (s, 0, 0))],
        out_specs=(rspec, rspec, rspec, rspec, pl.BlockSpec((None, 1, LANES), lambda s: (s, 0, 0))),
        compiler_params=_cparams("arbitrary"),
        name="moe_route",
    )(lg)


def _moe_rows(sb):
    return 2 * sb + N_EXPERTS * ROW_PAD + FFN_TILE


def _moe_kernel(pos1_ref, pos2_ref, w1_ref, w2_ref, tab_ref, xr_ref, wg_ref, wu_ref, wd_ref, o_ref,
                rows_scr, stage_scr, *, sb):
    s = pl.program_id(0)
    e = pl.program_id(1)

    def tile_at(ref, r8):
        return ref.at[pl.ds(pl.multiple_of(r8, _RPV), _RPV)]

    @pl.when((s == 0) & (e == 0))
    def _():
        rows_scr[...] = jnp.zeros_like(rows_scr)

    @pl.when(e == 0)
    def _():
        def dispatch(t, carry):
            v = tile_at(xr_ref, t * _RPV)[...]
            tile_at(rows_scr, pos1_ref[0, t])[...] = v
            tile_at(rows_scr, pos2_ref[0, t])[...] = v
            return carry

        lax.fori_loop(0, sb, dispatch, 0, unroll=8)

    def expert(k):
        ex = e * EXPERTS_PER_STEP + k

        def ffn_tile(r0, m, valid=None):
            win = rows_scr.at[pl.ds(pl.multiple_of(r0 * _RPV, ROW_PAD * _RPV), m * _RPV)]
            x = _load_rows_per_vreg(win, m)
            xb = _bf(x)
            hg = _dot(xb, wg_ref[k])
            hu = _dot(xb, wu_ref[k])
            y = _dot(_bf(hg * jax.nn.sigmoid(hg) * hu), wd_ref[k])
            if valid is not None:
                y = jnp.where(lax.broadcasted_iota(jnp.int32, (m, 1), 0) < valid, y, xb.astype(F32))
            _store_rows_per_vreg(win, y)

        off = tab_ref[0, ex]
        npad = tab_ref[0, N_EXPERTS + ex]
        nfull = npad // FFN_TILE

        def full_tile(i, carry):
            ffn_tile(off + i * FFN_TILE, FFN_TILE)
            return carry

        lax.fori_loop(0, nfull, full_tile, 0)
        rem = npad - nfull * FFN_TILE
        last = off + nfull * FFN_TILE

        @pl.when((rem > 0) & (rem <= FFN_TILE // 2))
        def _():
            ffn_tile(last, FFN_TILE // 2, valid=rem)

        @pl.when(rem > FFN_TILE // 2)
        def _():
            ffn_tile(last, FFN_TILE, valid=rem)

    for k in range(EXPERTS_PER_STEP):
        expert(k)

    @pl.when(e == N_EXPERTS // EXPERTS_PER_STEP - 1)
    def _():
        for c in range(sb // TM):
            def combine(t, carry, c=c):
                tt = c * TM + t
                y = (w1_ref[0, tt] * tile_at(rows_scr, pos1_ref[0, tt])[...]
                     + w2_ref[0, tt] * tile_at(rows_scr, pos2_ref[0, tt])[...])
                tile_at(stage_scr, t * _RPV)[...] = y
                return carry

            lax.fori_loop(0, TM, combine, 0, unroll=8)
            o_ref[c * TM:(c + 1) * TM, :] = _bf(_load_rows_per_vreg(stage_scr, TM))


def _cast_kernel(x_ref, o_ref):
    o_ref[...] = _bf(x_ref[...])


def _expert_weights_bf16(w):
    depth, ne, a, b = w.shape
    per_step = 4
    assert (depth * ne) % per_step == 0
    spec = pl.BlockSpec((per_step, a, b), lambda i: (i, 0, 0))
    return pl.pallas_call(
        _cast_kernel,
        out_shape=jax.ShapeDtypeStruct((depth * ne, a, b), BF16),
        grid=(depth * ne // per_step,),
        in_specs=[spec],
        out_specs=spec,
        compiler_params=_cparams("arbitrary"),
        name="expert_weight_cast",
    )(w.reshape(depth * ne, a, b))


def _moe(st, layer, xr, route, wg, wu, wd):
    sb = st.sb
    eps = EXPERTS_PER_STEP
    b0 = layer * N_EXPERTS // eps
    smem = lambda n: pl.BlockSpec((None, 1, n), lambda s, e: (s, 0, 0), memory_space=pltpu.SMEM)
    return pl.pallas_call(
        functools.partial(_moe_kernel, sb=sb),
        out_shape=jax.ShapeDtypeStruct((st.t, D_MODEL), BF16),
        grid=(st.t // sb, N_EXPERTS // eps),
        in_specs=[
            smem(sb), smem(sb), smem(sb), smem(sb), smem(LANES),
            pl.BlockSpec((sb * _RPV, LANES), lambda s, e: (s, 0)),
            pl.BlockSpec((eps, D_MODEL, D_EXPERT), lambda s, e: (b0 + e, 0, 0)),
            pl.BlockSpec((eps, D_MODEL, D_EXPERT), lambda s, e: (b0 + e, 0, 0)),
            pl.BlockSpec((eps, D_EXPERT, D_MODEL), lambda s, e: (b0 + e, 0, 0)),
        ],
        out_specs=pl.BlockSpec((sb, D_MODEL), lambda s, e: (s, 0)),
        scratch_shapes=[pltpu.VMEM((_moe_rows(sb) * _RPV, LANES), F32), pltpu.VMEM((TM * _RPV, LANES), F32)],
        compiler_params=_cparams("arbitrary", "arbitrary"),
        name="moe_ffn",
    )(*route, xr, wg, wu, wd)


def _hy_inproj_kernel(x_ref, m_ref, mod0_ref, mod_ref, nw_ref, w_ref, x2_ref, z_ref):
    g2 = _mod_slices(mod0_ref[...])[5]
    x2 = x_ref[...] + g2 * m_ref[...].astype(F32)
    x2_ref[...] = x2
    sh1, sc1 = _mod_slices(mod_ref[...])[:2]
    h = _rms(x2, nw_ref[...]) * (1.0 + sc1) + sh1
    z_ref[...] = _bf(_dot(_bf(h), w_ref[...]))


def _hy_inproj(st, x, moe, mods_prev, mods, nw, w):
    n = w.shape[1]
    tm = TM_TOK
    mspec = pl.BlockSpec((None, 1, N_MOD * D_MODEL), lambda i: (st.mod_row(i, tm), 0, 0))
    return pl.pallas_call(
        _hy_inproj_kernel,
        out_shape=(jax.ShapeDtypeStruct((st.t, D_MODEL), F32), jax.ShapeDtypeStruct((st.t, n), BF16)),
        grid=(st.t // tm,),
        in_specs=[
            pl.BlockSpec((tm, D_MODEL), lambda i: (i, 0)),
            pl.BlockSpec((tm, D_MODEL), lambda i: (i, 0)),
            mspec, mspec,
            pl.BlockSpec((1, D_MODEL), lambda i: (0, 0)),
            pl.BlockSpec((D_MODEL, n), lambda i: (0, 0)),
        ],
        out_specs=(pl.BlockSpec((tm, D_MODEL), lambda i: (i, 0)), pl.BlockSpec((tm, n), lambda i: (i, 0))),
        compiler_params=_cparams("arbitrary"),
        name="hy_inproj",
    )(x, moe, mods_prev, mods, nw, w)


def _dft_tables(seq):
    n2 = 2 * seq
    assert n2 & (n2 - 1) == 0
    tr = min(seq, TM)

    def table_kernel(cos_ref, msin_ref, cosb_ref, msinb_ref):
        k = lax.broadcasted_iota(jnp.int32, (tr, seq), 0) + pl.program_id(0) * tr
        n = lax.broadcasted_iota(jnp.int32, (tr, seq), 1)
        ang = ((k * n) & (n2 - 1)).astype(F32) * (2.0 * math.pi / n2)
        c, s = jnp.cos(ang), -jnp.sin(ang)
        cos_ref[...] = c
        msin_ref[...] = s
        cosb_ref[...] = _bf(c)
        msinb_ref[...] = _bf(s)

    spec = pl.BlockSpec((tr, seq), lambda i: (i, 0))
    f32, b16 = jax.ShapeDtypeStruct((seq, seq), F32), jax.ShapeDtypeStruct((seq, seq), BF16)
    return pl.pallas_call(table_kernel, out_shape=(f32, f32, b16, b16), grid=(seq // tr,),
                          out_specs=(spec, spec, spec, spec), compiler_params=_cparams("arbitrary"),
                          name=f"dft_tables_{seq}")()


def _alternating(shape):
    return jnp.where(lax.broadcasted_iota(jnp.int32, shape, 0) % 2 == 0, 1.0, -1.0)


def _hy_filter_kernel(emb_ref, dec_ref, w1_ref, b1_ref, f1_ref, w2_ref, b2_ref, f2_ref, w3_ref,
                      cos_ref, msin_ref, kr_ref, ki_ref, kn_ref, *, seq):
    h = jnp.sin(f1_ref[...] * (_dot_hp(emb_ref[...], w1_ref[...]) + b1_ref[...]))
    h = jnp.sin(f2_ref[...] * (_dot_hp(h, w2_ref[...]) + b2_ref[...]))
    dec = dec_ref[...]
    row0 = lax.broadcasted_iota(jnp.int32, dec.shape, 0) == 0
    alt = _alternating(dec.shape)
    cos, msin = cos_ref[...], msin_ref[...]
    scale = jnp.where(row0, 1.0, 2.0) / (2 * seq)
    for o in range(HY_ORDER):
        h_f = _dot_hp(h, w3_ref[:, 2 * o, :]) * dec
        h_b = jnp.where(row0, 0.0, _dot_hp(h, w3_ref[:, 2 * o + 1, :]) * dec)
        kn_ref[o] = jnp.sum(alt * (h_f + h_b), axis=0, keepdims=True) / (2 * seq)
        kr_ref[o] = _dot_hp(cos, h_f + h_b) * scale
        ki_ref[o] = _dot_hp(msin, h_f - h_b) * scale


def _hy_filter(seq, w1, b1, f1, w2, b2, f2, w3, cos, msin):
    t = jnp.linspace(0.0, 1.0, seq, dtype=F32)[:, None]
    w = 2.0 * math.pi * jnp.arange(seq, dtype=F32)[:, None] / seq
    f = jnp.linspace(1e-4, HY_BANDS - 1, HY_BANDS, dtype=F32)[None, :]
    emb = jnp.concatenate([t, jnp.cos(f * w), -jnp.sin(f * w), jnp.zeros((seq, LANES - HY_EMB), F32)], axis=-1)
    decay = jnp.exp(-t * jnp.linspace(HY_MIN_DECAY, HY_MAX_DECAY, D_MODEL, dtype=F32)[None, :])
    w1p = jnp.concatenate([w1, jnp.zeros((LANES - HY_EMB, HY_FFN), F32)], axis=0)
    dblk = 256
    out = jax.ShapeDtypeStruct((HY_ORDER, seq, D_MODEL), F32)
    full = lambda shape: pl.BlockSpec(shape, lambda j: (0,) * len(shape))
    ospec = pl.BlockSpec((HY_ORDER, seq, dblk), lambda j: (0, 0, j))
    return pl.pallas_call(
        functools.partial(_hy_filter_kernel, seq=seq),
        out_shape=(out, out, jax.ShapeDtypeStruct((HY_ORDER, 1, D_MODEL), F32)),
        grid=(D_MODEL // dblk,),
        in_specs=[
            full((seq, LANES)),
            pl.BlockSpec((seq, dblk), lambda j: (0, j)),
            full((LANES, HY_FFN)), full((1, HY_FFN)), full((1, HY_FFN)),
            full((HY_FFN, HY_FFN)), full((1, HY_FFN)), full((1, HY_FFN)),
            pl.BlockSpec((HY_FFN, 2 * HY_ORDER, dblk), lambda j: (0, 0, j)),
            full((seq, seq)), full((seq, seq)),
        ],
        out_specs=(ospec, ospec, pl.BlockSpec((HY_ORDER, 1, dblk), lambda j: (0, 0, j))),
        compiler_params=_cparams("arbitrary"),
        name=f"hy_filter_{seq}",
    )(emb, decay, w1p, b1[None, :], f1[None, :], w2, b2[None, :], f2[None, :],
      w3.reshape(HY_FFN, 2 * HY_ORDER, D_MODEL), cos, msin)


def _hy_conv_kernel(*refs, seq, nseq):
    (zv_ref, z1_ref, z2_ref, cv_ref, c1_ref, c2_ref, kr_ref, ki_ref, kn_ref, bias_ref,
     cos_ref, msin_ref) = refs[:12]
    y_ref = refs[-1]
    dblk = y_ref.shape[1]
    t = lax.broadcasted_iota(jnp.int32, (seq, dblk), 0)
    first, last = t == 0, t == seq - 1
    alt = _alternating((seq, dblk))
    seqs = range(nseq)
    rows = [slice(q * seq, (q + 1) * seq) for q in seqs]

    def short_conv(z_ref, c_ref):
        out = []
        for q in seqs:
            z = z_ref[rows[q], :].astype(F32)
            prev = jnp.where(first, 0.0, pltpu.roll(z, 1, 0))
            nxt = jnp.where(last, 0.0, pltpu.roll(z, seq - 1, 0))
            out.append(c_ref[0:1, :] * prev + c_ref[1:2, :] * z + c_ref[2:3, :] * nxt)
        return out

    def long_conv(s, o):
        sb = [_bf(s[q]) for q in seqs]
        x_re = [_dot(cos_ref[...], sb[q]) for q in seqs]
        x_im = [_dot(msin_ref[...], sb[q]) for q in seqs]
        y_nyq = [jnp.sum(alt * s[q], axis=0, keepdims=True) * kn_ref[o] for q in seqs]
        k_re, k_im = kr_ref[o], ki_ref[o]
        y_re = [x_re[q] * k_re - x_im[q] * k_im for q in seqs]
        y_im = [x_re[q] * k_im + x_im[q] * k_re for q in seqs]
        y = [_dot(cos_ref[...], _bf(y_re[q])) + _dot(msin_ref[...], _bf(y_im[q])) for q in seqs]
        return [y[q] + alt * y_nyq[q] + bias_ref[o] * s[q] for q in seqs]

    v = short_conv(zv_ref, cv_ref)
    x1 = short_conv(z1_ref, c1_ref)
    x2 = short_conv(z2_ref, c2_ref)
    c1 = long_conv(v, 0)
    c2 = long_conv([x1[q] * c1[q] for q in seqs], 1)
    for q in seqs:
        y_ref[rows[q], :] = _bf(x2[q] * c2[q])


def _hy_conv(st, ctx, z, conv_w, filt, bias, tables, prev):
    seq = st.lp if ctx else st.ls
    nb = st.bp if ctx else st.bs
    nseq = max(1, min(nb, 4, 2048 // seq))
    assert nb % nseq == 0
    off = 0 if ctx else st.tp // (nseq * seq)
    dblk = 256
    nd = D_MODEL // dblk
    kr, ki, kn = filt
    cos, msin = tables
    vspec = pl.BlockSpec((HY_ORDER, 1, dblk), lambda j, b: (0, 0, j))
    full = lambda shape: pl.BlockSpec(shape, lambda j, b: (0,) * len(shape))
    kspec = pl.BlockSpec((HY_ORDER, seq, dblk), lambda j, b: (0, 0, j))
    in_specs = [
        pl.BlockSpec((nseq * seq, dblk), lambda j, b: (off + b, j)),
        pl.BlockSpec((nseq * seq, dblk), lambda j, b: (off + b, nd + j)),
        pl.BlockSpec((nseq * seq, dblk), lambda j, b: (off + b, 2 * nd + j)),
        pl.BlockSpec((3, dblk), lambda j, b: (0, j)),
        pl.BlockSpec((3, dblk), lambda j, b: (0, nd + j)),
        pl.BlockSpec((3, dblk), lambda j, b: (0, 2 * nd + j)),
        kspec, kspec, vspec, vspec,
        full((seq, seq)), full((seq, seq)),
    ]
    args = [z, z, z, conv_w, conv_w, conv_w, kr, ki, kn, bias.reshape(HY_ORDER, 1, D_MODEL), cos, msin]
    aliases = {}
    if prev is not None:
        aliases = {len(args): 0}
        in_specs.append(pl.BlockSpec(memory_space=pl.ANY))
        args.append(prev)
    return pl.pallas_call(
        functools.partial(_hy_conv_kernel, seq=seq, nseq=nseq),
        out_shape=jax.ShapeDtypeStruct((st.t, D_MODEL), BF16),
        grid=(nd, nb // nseq),
        in_specs=in_specs,
        out_specs=pl.BlockSpec((nseq * seq, dblk), lambda j, b: (off + b, j)),
        input_output_aliases=aliases,
        compiler_params=_cparams("arbitrary", "arbitrary"),
        name="hy_conv_ctx" if ctx else "hy_conv_lat",
    )(*args)


def _hy_outproj_kernel(y_ref, x_ref, mod_ref, w_ref, n2w_ref, wrt_ref, brt_ref, x3_ref, h2_ref, lg_ref):
    mod = _mod_slices(mod_ref[...])
    g1, sh2, sc2 = mod[2], mod[3], mod[4]
    nt = y_ref.shape[0] // TM
    tiles = range(nt)
    rows = [slice(b * TM, (b + 1) * TM) for b in tiles]
    mix = [_dot(y_ref[rows[b], :], w_ref[...]) for b in tiles]
    x3 = [x_ref[rows[b], :] + g1 * mix[b] for b in tiles]
    h2 = [_rms(x3[b], n2w_ref[...]) * (1.0 + sc2) + sh2 for b in tiles]
    lg = [_router_logits(h2[b], wrt_ref, brt_ref) for b in tiles]
    for b in tiles:
        x3_ref[rows[b], :] = x3[b]
        _store_rows_per_vreg(h2_ref.at[pl.ds(b * TM * _RPV, TM * _RPV)], h2[b])
        lg_ref[b] = lg[b]


def _hy_outproj(st, y, x, mods, w, n2w, wrt, brt):
    tm = TM_TOK
    return pl.pallas_call(
        _hy_outproj_kernel,
        out_shape=(jax.ShapeDtypeStruct((st.t, D_MODEL), F32),
                   jax.ShapeDtypeStruct((st.t * _RPV, LANES), F32),
                   jax.ShapeDtypeStruct((st.t // TM, LG_ROWS, TM), F32)),
        grid=(st.t // tm,),
        in_specs=[
            pl.BlockSpec((tm, D_MODEL), lambda i: (i, 0)),
            pl.BlockSpec((tm, D_MODEL), lambda i: (i, 0)),
            pl.BlockSpec((None, 1, N_MOD * D_MODEL), lambda i: (st.mod_row(i, tm), 0, 0)),
            pl.BlockSpec((D_MODEL, D_MODEL), lambda i: (0, 0)),
            pl.BlockSpec((1, D_MODEL), lambda i: (0, 0)),
            pl.BlockSpec((LG_ROWS, D_MODEL), lambda i: (0, 0)),
            pl.BlockSpec((LG_ROWS, 1), lambda i: (0, 0)),
        ],
        out_specs=(pl.BlockSpec((tm, D_MODEL), lambda i: (i, 0)),
                   pl.BlockSpec((tm * _RPV, LANES), lambda i: (i, 0)),
                   pl.BlockSpec((tm // TM, LG_ROWS, TM), lambda i: (i, 0, 0))),
        compiler_params=_cparams("arbitrary"),
        name="hy_outproj",
    )(y, x, mods, w, n2w, wrt, brt)


def _final_kernel(x_ref, m_ref, mod_ref, nf_ref, yp_ref, ys_ref, *, ncb):
    i = pl.program_id(0)
    g2 = _mod_slices(mod_ref[...])[5]
    y = _rms(x_ref[...] + g2 * m_ref[...].astype(F32), nf_ref[...])

    @pl.when(i < ncb)
    def _():
        yp_ref[...] = y

    @pl.when(i >= ncb)
    def _():
        ys_ref[...] = y


def _final(st, x, moe, mods, nf):
    tm = TM_TOK
    ncb = st.tp // tm
    return pl.pallas_call(
        functools.partial(_final_kernel, ncb=ncb),
        out_shape=(jax.ShapeDtypeStruct((st.tp, D_MODEL), F32), jax.ShapeDtypeStruct((st.ts, D_MODEL), F32)),
        grid=(st.t // tm,),
        in_specs=[
            pl.BlockSpec((tm, D_MODEL), lambda i: (i, 0)),
            pl.BlockSpec((tm, D_MODEL), lambda i: (i, 0)),
            pl.BlockSpec((None, 1, N_MOD * D_MODEL), lambda i: (st.mod_row(i, tm), 0, 0)),
            pl.BlockSpec((1, D_MODEL), lambda i: (0, 0)),
        ],
        out_specs=(pl.BlockSpec((tm, D_MODEL), lambda i: (jnp.minimum(i, ncb - 1), 0)),
                   pl.BlockSpec((tm, D_MODEL), lambda i: (jnp.maximum(i - ncb, 0), 0))),
        compiler_params=_cparams("arbitrary"),
        name="final_norm",
    )(x, moe, mods, nf)


def _grid_pos_table(seq):
    rows = seq // GRID_W
    r, cl = jnp.meshgrid(jnp.arange(rows, dtype=F32), jnp.arange(GRID_W, dtype=F32), indexing='ij')
    quarter = D_MODEL // 4
    omega = POS_THETA ** (-jnp.arange(quarter, dtype=F32) / quarter)

    def enc(pos):
        a = pos.reshape(-1, 1) * omega[None, :]
        return jnp.concatenate([jnp.sin(a), jnp.cos(a)], axis=-1)

    return jnp.concatenate([enc(r), enc(cl)], axis=-1)


def _router_weights(w_group, b_group, w_router, b_router):
    pad = LG_ROWS - N_EXPERTS - N_GROUPS
    w = jnp.concatenate([w_router.T, w_group.T, jnp.zeros((pad, D_MODEL), F32)], axis=0)
    b = jnp.concatenate([b_router, b_group, jnp.zeros((pad,), F32)])[:, None]
    return w, b


def kernel(x_prompt, x_sample, state_gla, state_mlstm_c, state_mlstm_n, state_mlstm_m, c, c_ctx, norm1_w, norm2_w, norm_f_w, ada_w, ada_b, rec_w_in, gla_gk_w, gla_gk_b, mlstm_gate_b, gla_norm_w, mlstm_norm_w, rec_w_out, hy_w_in, hy_conv_w, hy_f_w1, hy_f_b1, hy_f_freq1, hy_f_w2, hy_f_b2, hy_f_freq2, hy_f_w3, hy_f_bias, hy_w_out, moe_w_group, moe_b_group, moe_w_router, moe_b_router, moe_w_gate, moe_w_up, moe_w_down):
    bp, lp, _ = x_prompt.shape
    bs, ls, _ = x_sample.shape
    st = _Streams(bp, lp, bs, ls)
    xp = x_prompt.reshape(st.tp, D_MODEL)
    xs = x_sample.reshape(st.ts, D_MODEL)

    nrow = -(-(1 + bs) // 8) * 8
    cv = jnp.concatenate([c_ctx[None, :], c, jnp.zeros((nrow - 1 - bs, D_MODEL), F32)], axis=0)
    mods = _ada(cv, ada_w, ada_b)
    mods0 = mods[0].reshape(nrow, 1, N_MOD * D_MODEL)
    mods1 = mods[1].reshape(nrow, 1, N_MOD * D_MODEL)

    pos = _grid_pos_table(ls)

    w_in = rec_w_in[0]
    w_main = _bf(w_in[:, :_REC_MAIN])
    w_mg = w_in[:, _REC_MAIN + _MG0:].reshape(D_MODEL, 2, 2, H_B)
    w_fg = jnp.pad(w_mg[:, :, 1, :], ((0, 0), (0, 0), (0, 8 - H_B))).reshape(D_MODEL, 16)
    w_gate = jnp.concatenate([w_in[:, _REC_MAIN:], jnp.zeros((D_MODEL, LANES - _REC_GATE), F32),
                              jnp.zeros((D_MODEL, _MG0), F32), w_fg,
                              jnp.zeros((D_MODEL, LANES - _REC_GATE), F32)], axis=1)
    w_gate_t = w_in[:, _REC_MAIN + _MG0:].T
    x0, main, g, g2, mt = _inproj(st, xp, xs, pos, mods0, norm1_w[0][None, :], w_main, w_gate, w_gate_t)

    gkw = jnp.zeros((2, LANES, H_A * DK_A), F32)
    gkw = gkw.at[0, :GK_RANK].set(gla_gk_w[0, 0]).at[1, GK_RANK:2 * GK_RANK].set(gla_gk_w[0, 1])
    gkb = gla_gk_b[0][:, None, :]
    gb = mlstm_gate_b[0]
    gate_row = lambda b: jnp.pad(jnp.pad(b, ((0, 0), (0, 8 - H_B))).reshape(1, 16), ((0, 0), (_MG0, LANES - _REC_GATE)))
    gbcol = gb.reshape(16, 1)
    wr0, br0 = _router_weights(moe_w_group[0], moe_b_group[0], moe_w_router[0], moe_b_router[0])
    consts = [gkw, gkb, gate_row(gb[:, 0]), gate_row(gb[:, 1]), gbcol, gla_norm_w[0][None, :],
              mlstm_norm_w[0][None, :], _bf(rec_w_out[0]), norm2_w[0][None, :], wr0, br0]

    def scan_states(sg, sc, sn, sm):
        nb = sg.shape[0]
        return (jnp.swapaxes(sg, -1, -2), sc, sn.reshape(nb, 2 * H_B, DK_B), sm.reshape(nb, 1, 2 * H_B))

    zero = (jnp.zeros((bp, 2, H_A, DK_A, DV_A), F32), jnp.zeros((bp, 2, H_B, DK_B, DV_B), F32),
            jnp.zeros((bp, 2, H_B, DK_B), F32), jnp.zeros((bp, 2, H_B), F32))
    cached = (state_gla[:, 0], state_mlstm_c[:, 0], state_mlstm_n[:, 0], state_mlstm_m[:, 0])
    x1, h2, lg, fg, fc, fn, fm = _scan(st, True, main, g, g2, mt, x0, mods0, scan_states(*zero), consts, None)
    x1, h2, lg = _scan(st, False, main, g, g2, mt, x0, mods0, scan_states(*cached), consts, (x1, h2, lg))[:3]

    new_gla = jnp.swapaxes(fg, -1, -2)[:, None]
    new_c = fc[:, None]
    new_n = fn.reshape(bp, 1, 2, H_B, DK_B)
    new_m = fm.reshape(bp, 1, 2, H_B)

    wg, wu, wd = (_expert_weights_bf16(w) for w in (moe_w_gate, moe_w_up, moe_w_down))
    moe0 = _moe(st, 0, h2, _route(st, lg), wg, wu, wd)

    x2, z = _hy_inproj(st, x1, moe0, mods0, mods1, norm1_w[1][None, :], _bf(hy_w_in[0]))
    y = None
    for ctx in (True, False):
        seq = lp if ctx else ls
        cos, msin, cos_b, msin_b = _dft_tables(seq)
        filt = _hy_filter(seq, hy_f_w1[0], hy_f_b1[0], hy_f_freq1[0], hy_f_w2[0], hy_f_b2[0], hy_f_freq2[0],
                          hy_f_w3[0], cos, msin)
        y = _hy_conv(st, ctx, z, hy_conv_w[0], filt, hy_f_bias[0], (cos_b, msin_b), y)
    wr1, br1 = _router_weights(moe_w_group[1], moe_b_group[1], moe_w_router[1], moe_b_router[1])
    x3, h4, lg1 = _hy_outproj(st, y, x2, mods1, _bf(hy_w_out[0]), norm2_w[1][None, :], wr1, br1)

    moe1 = _moe(st, 1, h4, _route(st, lg1), wg, wu, wd)
    y_prompt, y_sample = _final(st, x3, moe1, mods1, norm_f_w[None, :])
    return (y_prompt.reshape(bp, lp, D_MODEL), y_sample.reshape(bs, ls, D_MODEL), new_gla, new_c, new_n, new_m)
```

```python
import functools
import math

import jax
import jax.numpy as jnp
from jax import lax
from jax.experimental import pallas as pl
from jax.experimental.pallas import tpu as pltpu

F32 = jnp.float32
BF16 = jnp.bfloat16

D_MODEL = 1024
GRID_W = 64
H_A = 4
DK_A = D_MODEL // 16
DV_A = D_MODEL // 8
GK_RANK = 16
GATE_TEMP = 16.0
H_B = 4
DK_B = D_MODEL // 16
DV_B = D_MODEL // 8
CHUNK = 64
HY_ORDER = 2
HY_EMB = 33
HY_BANDS = (HY_EMB - 1) // 2
HY_FFN = 64
HY_TARGET = 1e-2
HY_MAX_DECAY = abs(math.log(HY_TARGET)) / 0.3
HY_MIN_DECAY = abs(math.log(HY_TARGET)) / 1.5
N_GROUPS = 4
EXP_PER_GROUP = 4
N_EXPERTS = N_GROUPS * EXP_PER_GROUP
D_EXPERT = D_MODEL // 2
N_MOD = 6
POS_THETA = 10000.0
EPS = 1e-6

_QA, _KA, _VA, _RA = 0, 256, 512, 1024
_QB, _KB, _VB, _OB = 1536, 1792, 2048, 2560
_REC_MAIN = 3072
_REC_GATE = 48
_MG0 = 2 * GK_RANK

LANES = 128
SUBLANES = 8
_RPV = D_MODEL // LANES
assert _RPV == SUBLANES
TM = 256
TM_TOK = 1024
TM_IN = 512
SB_MOE = 2048
ROW_PAD = 16
EXPERTS_PER_STEP = 2
FFN_TILE = 256
LG_ROWS = 32
VMEM_LIMIT = 58 * 1024 * 1024


def _cparams(*sem):
    return pltpu.CompilerParams(dimension_semantics=sem, vmem_limit_bytes=VMEM_LIMIT)


def _bf(x):
    return x.astype(BF16)


def _dot(a, b):
    return jnp.dot(a, b, preferred_element_type=F32)


def _dot_nt(a, b):
    return lax.dot_general(a, b, (((1,), (1,)), ((), ())), preferred_element_type=F32)


def _dot_tn(a, b):
    return lax.dot_general(a, b, (((0,), (0,)), ((), ())), preferred_element_type=F32)


def _split2(x):
    hi = _bf(x)
    return hi, _bf(x - hi.astype(F32))


def _split3(x):
    hi = _bf(x)
    r = x - hi.astype(F32)
    mid = _bf(r)
    return hi, mid, _bf(r - mid.astype(F32))


def _dot_hp(a, b, dot=_dot):
    ah, al = _split2(a)
    bh, bl = _split2(b)
    return dot(ah, bh) + (dot(ah, bl) + dot(al, bh))


def _dot_mask_l(m, x, terms=3):
    if terms == 2:
        x1, x2 = _split2(x)
        return _dot(m, x1) + _dot(m, x2)
    x1, x2, x3 = _split3(x)
    return _dot(m, x1) + (_dot(m, x2) + _dot(m, x3))


def _dot_mask_r(x, m, terms=3):
    if terms == 2:
        x1, x2 = _split2(x)
        return _dot(x1, m) + _dot(x2, m)
    x1, x2, x3 = _split3(x)
    return _dot(x1, m) + (_dot(x2, m) + _dot(x3, m))


def _rms(x, w):
    return x * lax.rsqrt(jnp.mean(x * x, axis=-1, keepdims=True) + EPS) * w


def _mod_slices(mod):
    return [mod[:, k * D_MODEL:(k + 1) * D_MODEL] for k in range(N_MOD)]


def _store_rows_per_vreg(dst, h):
    rows = h.shape[0]
    for j in range(_RPV):
        dst[pl.ds(j, rows, stride=_RPV), :] = h[:, j * LANES:(j + 1) * LANES]


def _load_rows_per_vreg(src, rows):
    return jnp.concatenate([src[pl.ds(j, rows, stride=_RPV), :] for j in range(_RPV)], axis=-1)


def _router_logits(h, wrt_ref, brt_ref):
    return _dot_hp(wrt_ref[...], h, dot=_dot_nt) + brt_ref[...]


def _ada_kernel(cv_ref, w_ref, b_ref, o_ref):
    a = cv_ref[...]
    a = a * jax.nn.sigmoid(a)
    o_ref[...] = _dot_hp(a, w_ref[...]) + b_ref[...]


def _ada(cv, ada_w, ada_b):
    depth, d, n = ada_w.shape
    rows = cv.shape[0]
    tn = 768
    return pl.pallas_call(
        _ada_kernel,
        out_shape=jax.ShapeDtypeStruct((depth, rows, n), F32),
        grid=(depth, n // tn),
        in_specs=[
            pl.BlockSpec((rows, d), lambda l, j: (0, 0)),
            pl.BlockSpec((None, d, tn), lambda l, j: (l, 0, j)),
            pl.BlockSpec((None, 1, tn), lambda l, j: (l, 0, j)),
        ],
        out_specs=pl.BlockSpec((None, rows, tn), lambda l, j: (l, 0, j)),
        compiler_params=_cparams("arbitrary", "arbitrary"),
        name="ada_mod",
    )(cv, ada_w, ada_b.reshape(depth, 1, n))


class _Streams:
    def __init__(self, bp, lp, bs, ls):
        self.bp, self.lp, self.bs, self.ls = bp, lp, bs, ls
        self.tp, self.ts = bp * lp, bs * ls
        self.t = self.tp + self.ts
        assert lp % TM == 0 and ls % TM == 0 and self.tp % ls == 0
        assert self.tp % TM_TOK == 0 and ls % TM_TOK == 0
        self.sb = min(SB_MOE, math.gcd(self.tp, self.ts))
        assert self.t % self.sb == 0 and self.sb % TM == 0

    def mod_row(self, i, tm):
        ncb = self.tp // tm
        return jnp.where(i < ncb, 0, 1 + (i - ncb) // (self.ls // tm))


def _inproj_kernel(xp_ref, xs_ref, pos_ref, mod_ref, nw_ref, w_ref, wg_ref, wgt_ref,
                   x0_ref, main_ref, g_ref, g2_ref, mt_ref, *, ncb):
    i = pl.program_id(0)
    x = jnp.where(i < ncb, xp_ref[...], xs_ref[...] + pos_ref[...])
    x0_ref[...] = x
    sh1, sc1 = _mod_slices(mod_ref[...])[:2]
    h = _rms(x, nw_ref[...]) * (1.0 + sc1) + sh1
    main_ref[...] = _bf(_dot(_bf(h), w_ref[...]))
    gates = _dot_hp(h, wg_ref[...])
    g_ref[...] = gates[:, :LANES]
    g2_ref[...] = gates[:, LANES:]
    mt = _dot_hp(wgt_ref[...], h, dot=_dot_nt)
    for c in range(mt.shape[1] // CHUNK):
        piece = mt[:, c * CHUNK:(c + 1) * CHUNK]
        mt_ref[c] = jnp.concatenate([piece, piece], axis=1)


def _inproj(st, xp, xs, pos, mods, nw, w_main, w_gate, w_gate_t):
    tm = TM_IN
    assert st.tp % tm == 0 and st.ls % tm == 0
    ncb = st.tp // tm
    bps = st.ls // tm
    t = st.t
    return pl.pallas_call(
        functools.partial(_inproj_kernel, ncb=ncb),
        out_shape=(jax.ShapeDtypeStruct((t, D_MODEL), F32),
                   jax.ShapeDtypeStruct((t, _REC_MAIN), BF16),
                   jax.ShapeDtypeStruct((t, LANES), F32),
                   jax.ShapeDtypeStruct((t, LANES), F32),
                   jax.ShapeDtypeStruct((t // CHUNK, 16, 2 * CHUNK), F32)),
        grid=(t // tm,),
        in_specs=[
            pl.BlockSpec((tm, D_MODEL), lambda i: (jnp.minimum(i, ncb - 1), 0)),
            pl.BlockSpec((tm, D_MODEL), lambda i: (jnp.maximum(i - ncb, 0), 0)),
            pl.BlockSpec((tm, D_MODEL), lambda i: (jnp.maximum(i - ncb, 0) % bps, 0)),
            pl.BlockSpec((None, 1, N_MOD * D_MODEL), lambda i: (st.mod_row(i, tm), 0, 0)),
            pl.BlockSpec((1, D_MODEL), lambda i: (0, 0)),
            pl.BlockSpec((D_MODEL, _REC_MAIN), lambda i: (0, 0)),
            pl.BlockSpec((D_MODEL, 2 * LANES), lambda i: (0, 0)),
            pl.BlockSpec((16, D_MODEL), lambda i: (0, 0)),
        ],
        out_specs=(pl.BlockSpec((tm, D_MODEL), lambda i: (i, 0)),
                   pl.BlockSpec((tm, _REC_MAIN), lambda i: (i, 0)),
                   pl.BlockSpec((tm, LANES), lambda i: (i, 0)),
                   pl.BlockSpec((tm, LANES), lambda i: (i, 0)),
                   pl.BlockSpec((tm // CHUNK, 16, 2 * CHUNK), lambda i: (i, 0, 0))),
        compiler_params=_cparams("arbitrary"),
        name="rec_inproj",
    )(xp, xs, pos, mods, nw, w_main, w_gate, w_gate_t)


def _block_diag(x):
    left = lax.broadcasted_iota(jnp.int32, (1, x.shape[1]), 1) < x.shape[1] // 2
    zero = jnp.zeros_like(x)
    return jnp.concatenate([jnp.where(left, x, zero), jnp.where(left, zero, x)], axis=0)


def _block_diag_mask(rows, width):
    r = lax.broadcasted_iota(jnp.int32, (rows, width), 0) < rows // 2
    l = lax.broadcasted_iota(jnp.int32, (rows, width), 1) < width // 2
    return r == l


def _running_max(x, reverse):
    n = x.shape[0]
    row = lax.broadcasted_iota(jnp.int32, x.shape, 0)
    sh = 1
    while sh < n:
        if reverse:
            y = jnp.where(row < n - sh, pltpu.roll(x, n - sh, 0), -jnp.inf)
        else:
            y = jnp.where(row >= sh, pltpu.roll(x, sh, 0), -jnp.inf)
        x = jnp.maximum(x, y)
        sh *= 2
    return x


def _scan_kernel(*refs, seq):
    (main_ref, g_ref, g2_ref, mt_ref, x_ref, mod_ref, sg_ref, sc_ref, sn_ref, sm_ref,
     gkw_ref, gkb_ref, gbi_ref, gbf_ref, gbcol_ref, gnw_ref, mnw_ref, wout_ref, n2w_ref, wrt_ref, brt_ref) = refs[:21]
    (x1_ref, h2_ref, lg_ref, og_ref, oc_ref, on_ref, om_ref,
     oa_scr, ob_scr, sbd_scr, cbd_scr, nbd_scr, mgl_scr) = refs[-13:]
    c = CHUNK
    nchunks = seq // c
    npair = H_A // 2
    assert H_A == H_B and DK_A == DK_B == c and DV_A == DV_B == LANES and 2 * DK_A == LANES

    lane = lax.broadcasted_iota(jnp.int32, (1, LANES), 1)
    gate_lane = lambda d, h: _MG0 + d * 8 + h
    used = tuple((lane >= gate_lane(d, 0)) & (lane < gate_lane(d, H_B)) for d in (0, 1))

    for d in (0, 1):
        for p in range(npair):
            k = d * npair + p
            z_s = jnp.zeros((DV_A, DK_A), F32)
            sbd_scr[k] = jnp.concatenate([jnp.concatenate([sg_ref[d, 2 * p], z_s], axis=1),
                                          jnp.concatenate([z_s, sg_ref[d, 2 * p + 1]], axis=1)], axis=0)
            z_c = jnp.zeros((DK_B, DV_B), F32)
            cbd_scr[k] = jnp.concatenate([jnp.concatenate([sc_ref[d, 2 * p], z_c], axis=1),
                                          jnp.concatenate([z_c, sc_ref[d, 2 * p + 1]], axis=1)], axis=0)
            n_rep = [jnp.broadcast_to(sn_ref[d * H_B + 2 * p + q:d * H_B + 2 * p + q + 1, :], (DV_B, DK_B)).T
                     for q in (0, 1)]
            nbd_scr[k] = jnp.concatenate([jnp.concatenate([n_rep[0], z_c], axis=1),
                                          jnp.concatenate([z_c, n_rep[1]], axis=1)], axis=0)
    m_gl = jnp.zeros((1, LANES), F32)
    for d in (0, 1):
        for h in range(H_B):
            r = d * H_B + h
            m_gl = jnp.where(lane == gate_lane(d, h), sm_ref[:, r:r + 1], m_gl)
    mgl_scr[...] = m_gl

    row_p = lax.broadcasted_iota(jnp.int32, (c, LANES), 0)
    s_p = lax.broadcasted_iota(jnp.int32, (c, LANES), 1) % c
    live_p = (s_p <= row_p, s_p >= row_p)
    row = lax.broadcasted_iota(jnp.int32, (c, c), 0)
    col = lax.broadcasted_iota(jnp.int32, (c, c), 1)
    tri = tuple(jnp.where(m, 1.0, 0.0).astype(BF16) for m in (col <= row, col >= row))
    tri_t2 = tuple(jnp.where(m, 1.0, 0.0).astype(BF16) for m in (row_p <= s_p, row_p >= s_p))
    grow = lax.broadcasted_iota(jnp.int32, (16, 1), 0)
    f_row = (grow % 8) >= 4
    scale_q = DK_A ** -0.5
    bd_val = _block_diag_mask(2 * DV_A, 2 * DK_A)
    bd_key = _block_diag_mask(2 * DK_B, 2 * DV_B)
    ones_bd = jnp.where(bd_key, 1.0, 0.0).astype(BF16)
    ones_cv = jnp.ones((c, 2 * DV_B), BF16)

    def replicate(d, width):
        r = lax.broadcasted_iota(jnp.int32, (LANES, H_B * width), 0)
        h = lax.broadcasted_iota(jnp.int32, (LANES, H_B * width), 1) // width
        return jnp.where(r == gate_lane(d, 0) + h, 1.0, 0.0).astype(BF16)

    rep_k = tuple(replicate(d, DK_B) for d in (0, 1))
    rep_v = tuple(replicate(d, DV_B) for d in (0, 1))

    def chunk_step(i, carry):
        rows, g_in, g2_in, mt_in, gla_in, mls_in = [], [], [], [], [], []
        for d in (0, 1):
            ci = i if d == 0 else nchunks - 1 - i
            rows.append(pl.ds(pl.multiple_of(ci * c, c), c))
            g_in.append(g_ref[rows[d], :])
            g2_in.append(g2_ref[rows[d], :])
            mt_in.append(mt_ref[ci])
            gla_in.append(main_ref[rows[d], _QA:_RA])
            mls_in.append(main_ref[rows[d], _QB:_OB])
        s_bd = [sbd_scr[k] for k in range(2 * npair)]
        c_bd = [cbd_scr[k] for k in range(2 * npair)]
        n_bd = [nbd_scr[k] for k in range(2 * npair)]
        m_gl = mgl_scr[...]
        dirs = (0, 1)
        ends = (c - 1, 0)
        pairs = [(d, p) for d in dirs for p in range(npair)]
        ks_of = lambda p: slice(p * 2 * DK_A, (p + 1) * 2 * DK_A)
        vs_of = lambda p: slice(p * 2 * DV_A, (p + 1) * 2 * DV_A)

        glin = [_dot_hp(g_in[d], gkw_ref[d]) + gkb_ref[d] for d in dirs]
        gi = [jnp.where(used[d], g_in[d] + gbi_ref[...], 0.0) for d in dirs]
        lf = [jnp.where(used[d], jax.nn.log_sigmoid(g2_in[d] + gbf_ref[...]), 0.0) for d in dirs]
        mt = [mt_in[d] + gbcol_ref[...] for d in dirs]
        mt = [jnp.where(f_row, jax.nn.log_sigmoid(mt[d]), mt[d]) for d in dirs]
        glog = [jax.nn.log_sigmoid(glin[d]) / GATE_TEMP for d in dirs]
        bc = [_dot_mask_l(tri[d], glog[d], 2) for d in dirs]
        cum = [_dot_mask_l(tri[d], lf[d], 2) for d in dirs]
        cum_t = [_dot_mask_r(mt[d][:, 0:c], tri_t2[d], 2) for d in dirs]
        b_end = [bc[d][ends[d]:ends[d] + 1, :] for d in dirs]
        qa = [gla_in[d][:, _QA:_QA + 256].astype(F32) * scale_q for d in dirs]
        ka = [gla_in[d][:, _KA:_KA + 256].astype(F32) for d in dirs]
        va = [gla_in[d][:, _VA:_VA + 512] for d in dirs]
        qe = [_bf(qa[d] * jnp.exp(bc[d])) for d in dirs]
        ke = [_bf(ka[d] * jnp.exp(-bc[d])) for d in dirs]
        kd = [_bf(ka[d] * jnp.exp(b_end[d] - bc[d])) for d in dirs]
        eb_end = [jnp.exp(b_end[d]) for d in dirs]
        qb = [mls_in[d][:, 0:256] * jnp.asarray(DK_B ** -0.5, BF16) for d in dirs]
        kb = [mls_in[d][:, _KB - _QB:_KB - _QB + 256] for d in dirs]
        vb = [mls_in[d][:, _VB - _QB:_VB - _QB + 512] for d in dirs]
        a_raw = {(d, p): _dot_nt(qe[d][:, ks_of(p)], _block_diag(ke[d][:, ks_of(p)])) for d, p in pairs}
        qk = {(d, p): _dot_nt(qb[d][:, ks_of(p)], _block_diag(kb[d][:, ks_of(p)])) for d, p in pairs}
        s_upd = {(d, p): _dot_tn(va[d][:, vs_of(p)], kd[d][:, ks_of(p)]) for d, p in pairs}
        o_car = {(d, p): _dot_nt(qe[d][:, ks_of(p)], _bf(s_bd[d * npair + p])) for d, p in pairs}
        m_loc = [cum[d] + _running_max(gi[d] - cum[d], reverse=(d == 1)) for d in dirs]
        inter = [cum[d] + m_gl for d in dirs]
        m_t = [jnp.maximum(inter[d], m_loc[d]) for d in dirs]
        b_last = [cum[d][ends[d]:ends[d] + 1, :] for d in dirs]
        dend = [b_last[d] - cum[d] + gi[d] for d in dirs]
        m_new = [jnp.maximum(b_last[d] + m_gl, jnp.max(dend[d], axis=0, keepdims=True)) for d in dirs]
        zero = jnp.zeros((c, LANES), F32)
        per_key = [jnp.concatenate([jnp.where(used[d], cum[d] - m_t[d], zero),
                                    jnp.where(used[d], jnp.exp(inter[d] - m_t[d]), zero),
                                    jnp.where(used[d], jnp.exp(dend[d] - m_new[d]), zero)], axis=0) for d in dirs]
        per_val = [jnp.concatenate([jnp.where(used[d], jnp.exp(-m_t[d]), zero),
                                    jnp.broadcast_to(jnp.where(used[d], jnp.exp(b_last[d] + m_gl - m_new[d]), 0.0),
                                                     (SUBLANES, LANES))], axis=0) for d in dirs]
        per_key = [_dot_mask_r(per_key[d], rep_k[d], 2) for d in dirs]
        per_val = [_dot_mask_r(per_val[d], rep_v[d], 2) for d in dirs]
        o_par = {(d, p): _dot(_bf(jnp.where(live_p[d], a_raw[d, p], 0.0)), _block_diag(va[d][:, vs_of(p)]))
                 for d, p in pairs}
        for d, p in pairs:
            k = d * npair + p
            s_bd[k] = s_bd[k] * eb_end[d][:, ks_of(p)] + jnp.where(bd_val, s_upd[d, p], 0.0)
        o_gla = [jnp.concatenate([o_par[d, p] + o_car[d, p] for p in range(npair)], axis=-1) for d in dirs]
        for d in dirs:
            m_gl = jnp.where(used[d], m_new[d], m_gl)

        w, qa2, kw = {}, {}, {}
        for d, p in pairs:
            ks = ks_of(p)
            r_i, r_f = d * 8 + 2 * p, d * 8 + 4 + 2 * p
            sub = [cum_t[d][r_f + q:r_f + q + 1, :] - mt[d][r_i + q:r_i + q + 1, :] for q in (0, 1)]
            sub = jnp.where(lane < DK_B, sub[0], sub[1])
            w[d, p] = jnp.exp(jnp.where(live_p[d], per_key[d][0:c, ks] - sub, -jnp.inf))
            qa2[d, p] = _bf(qb[d][:, ks].astype(F32) * per_key[d][c:2 * c, ks])
            kw[d, p] = _bf(kb[d][:, ks].astype(F32) * per_key[d][2 * c:3 * c, ks])
        carried = {(d, p): _dot(qa2[d, p], jnp.concatenate([_bf(c_bd[d * npair + p]), _bf(n_bd[d * npair + p])], axis=1))
                   for d, p in pairs}
        upd = {(d, p): _dot_tn(kw[d, p], jnp.concatenate([vb[d][:, vs_of(p)], ones_cv], axis=1)) for d, p in pairs}
        intra = {(d, p): _dot(_bf(qk[d, p] * w[d, p]),
                              jnp.concatenate([_block_diag(vb[d][:, vs_of(p)]), ones_bd], axis=1)) for d, p in pairs}
        outs = {}
        for d, p in pairs:
            k = d * npair + p
            vs = vs_of(p)
            num = intra[d, p][:, :2 * DV_B] + carried[d, p][:, :2 * DV_B]
            den = intra[d, p][:, 2 * DV_B:] + carried[d, p][:, 2 * DV_B:]
            outs[d, p] = num / jnp.maximum(jnp.abs(den), per_val[d][0:c, vs])
            decay = per_val[d][c:c + 1, vs]
            dec = jnp.concatenate([jnp.broadcast_to(jnp.concatenate([decay[:, q * DV_B:(q + 1) * DV_B]] * 2, axis=1),
                                                    (DK_B, 2 * DV_B)) for q in (0, 1)], axis=0)
            c_bd[k] = dec * c_bd[k] + jnp.where(bd_key, upd[d, p][:, :2 * DV_B], 0.0)
            n_bd[k] = dec * n_bd[k] + jnp.where(bd_key, upd[d, p][:, 2 * DV_B:], 0.0)
        o_mls = [jnp.concatenate([outs[d, p] for p in range(npair)], axis=-1) for d in dirs]

        for d in (0, 1):
            oa_scr[d, rows[d], :] = o_gla[d]
            ob_scr[d, rows[d], :] = o_mls[d]
        for k in range(2 * npair):
            sbd_scr[k] = s_bd[k]
            cbd_scr[k] = c_bd[k]
            nbd_scr[k] = n_bd[k]
        mgl_scr[...] = m_gl
        return carry

    lax.fori_loop(0, nchunks, chunk_step, 0)

    for d in (0, 1):
        for p in range(npair):
            k = d * npair + p
            s_t, c_f, n_f = sbd_scr[k], cbd_scr[k], nbd_scr[k]
            for q in (0, 1):
                h = 2 * p + q
                og_ref[d, h] = s_t[q * DV_A:(q + 1) * DV_A, q * DK_A:(q + 1) * DK_A]
                oc_ref[d, h] = c_f[q * DK_B:(q + 1) * DK_B, q * DV_B:(q + 1) * DV_B]
                n_t = n_f[q * DK_B:(q + 1) * DK_B, q * DV_B:(q + 1) * DV_B].T
                on_ref[d * H_B + h:d * H_B + h + 1, :] = n_t[0:1, :]
    m_gl = mgl_scr[...]
    om_ref[...] = jnp.concatenate([m_gl[:, gate_lane(d, h):gate_lane(d, h) + 1]
                                   for d in (0, 1) for h in range(H_B)], axis=1)

    mod = _mod_slices(mod_ref[...])
    g1, sh2, sc2 = mod[2], mod[3], mod[4]

    def out_step(j, carry):
        hm = TM // 2
        halves = (0, 1)
        r0 = [pl.multiple_of(j * TM + q * hm, hm) for q in halves]
        rows = [pl.ds(r0[q], hm) for q in halves]
        parts = [[], []]
        for h in range(H_A):
            vs = slice(h * DV_A, (h + 1) * DV_A)
            for q in halves:
                ra = main_ref[rows[q], _RA + h * DV_A:_RA + (h + 1) * DV_A].astype(F32)
                o = oa_scr[0, rows[q], vs] + oa_scr[1, rows[q], vs]
                parts[q].append(_rms(o, gnw_ref[...]) * (ra * jax.nn.sigmoid(ra)))
        for h in range(H_B):
            vs = slice(h * DV_B, (h + 1) * DV_B)
            for q in halves:
                ob = main_ref[rows[q], _OB + h * DV_B:_OB + (h + 1) * DV_B].astype(F32)
                o = ob_scr[0, rows[q], vs] + ob_scr[1, rows[q], vs]
                parts[q].append(_rms(o, mnw_ref[...]) * jax.nn.sigmoid(ob))
        mix = [_dot(_bf(jnp.concatenate(parts[q], axis=-1)), wout_ref[...]) for q in halves]
        x1 = [x_ref[rows[q], :] + g1 * mix[q] for q in halves]
        h2 = [_rms(x1[q], n2w_ref[...]) * (1.0 + sc2) + sh2 for q in halves]
        lg = [_router_logits(h2[q], wrt_ref, brt_ref) for q in halves]
        for q in halves:
            x1_ref[rows[q], :] = x1[q]
            _store_rows_per_vreg(h2_ref.at[pl.ds(pl.multiple_of(r0[q] * _RPV, hm * _RPV), hm * _RPV)], h2[q])
        lg_ref[j] = jnp.concatenate(lg, axis=1)
        return carry

    lax.fori_loop(0, seq // TM, out_step, 0)


def _scan(st, ctx, main, g, g2, mt, x, mods, states, consts, prev):
    seq = st.lp if ctx else st.ls
    nb = st.bp if ctx else st.bs
    off = 0 if ctx else st.tp // st.ls
    sg, sc, sn, sm = states
    full = lambda shape: pl.BlockSpec(shape, lambda b: (0,) * len(shape))
    in_specs = [
        pl.BlockSpec((seq, _REC_MAIN), lambda b: (off + b, 0)),
        pl.BlockSpec((seq, LANES), lambda b: (off + b, 0)),
        pl.BlockSpec((seq, LANES), lambda b: (off + b, 0)),
        pl.BlockSpec((seq // CHUNK, 16, 2 * CHUNK), lambda b: (off + b, 0, 0)),
        pl.BlockSpec((seq, D_MODEL), lambda b: (off + b, 0)),
    ]
    args = [main, g, g2, mt, x]
    mod_row = (lambda b: (0, 0, 0)) if ctx else (lambda b: (1 + b, 0, 0))
    in_specs += [
        pl.BlockSpec((None, 1, N_MOD * D_MODEL), mod_row),
        pl.BlockSpec((None, 2, H_A, DV_A, DK_A), lambda b: (b, 0, 0, 0, 0)),
        pl.BlockSpec((None, 2, H_B, DK_B, DV_B), lambda b: (b, 0, 0, 0, 0)),
        pl.BlockSpec((None, 2 * H_B, DK_B), lambda b: (b, 0, 0)),
        pl.BlockSpec((None, 1, 2 * H_B), lambda b: (b, 0, 0)),
    ]
    args += [mods, sg, sc, sn, sm]
    for a in consts:
        in_specs.append(full(a.shape))
        args.append(a)
    aliases = {}
    if prev is not None:
        aliases = {len(args) + k: k for k in range(3)}
        in_specs += [pl.BlockSpec(memory_space=pl.ANY)] * 3
        args += list(prev)
    out_shape = (jax.ShapeDtypeStruct((st.t, D_MODEL), F32),
                 jax.ShapeDtypeStruct((st.t * _RPV, LANES), F32),
                 jax.ShapeDtypeStruct((st.t // TM, LG_ROWS, TM), F32),
                 jax.ShapeDtypeStruct(sg.shape, F32), jax.ShapeDtypeStruct(sc.shape, F32),
                 jax.ShapeDtypeStruct(sn.shape, F32), jax.ShapeDtypeStruct(sm.shape, F32))
    out_specs = (pl.BlockSpec((seq, D_MODEL), lambda b: (off + b, 0)),
                 pl.BlockSpec((seq * _RPV, LANES), lambda b: (off + b, 0)),
                 pl.BlockSpec((seq // TM, LG_ROWS, TM), lambda b: (off + b, 0, 0)),
                 pl.BlockSpec((None, 2, H_A, DV_A, DK_A), lambda b: (b, 0, 0, 0, 0)),
                 pl.BlockSpec((None, 2, H_B, DK_B, DV_B), lambda b: (b, 0, 0, 0, 0)),
                 pl.BlockSpec((None, 2 * H_B, DK_B), lambda b: (b, 0, 0)),
                 pl.BlockSpec((None, 1, 2 * H_B), lambda b: (b, 0, 0)))
    return pl.pallas_call(
        functools.partial(_scan_kernel, seq=seq),
        out_shape=out_shape,
        grid=(nb,),
        in_specs=in_specs,
        out_specs=out_specs,
        scratch_shapes=[pltpu.VMEM((2, seq, H_A * DV_A), F32), pltpu.VMEM((2, seq, H_B * DV_B), F32),
                        pltpu.VMEM((H_A, 2 * DV_A, 2 * DK_A), F32), pltpu.VMEM((H_B, 2 * DK_B, 2 * DV_B), F32),
                        pltpu.VMEM((H_B, 2 * DK_B, 2 * DV_B), F32), pltpu.VMEM((1, LANES), F32)],
        input_output_aliases=aliases,
        compiler_params=_cparams("arbitrary"),
        name="rec_scan_ctx" if ctx else "rec_scan_lat",
    )(*args)


def _first_max(rows):
    m = rows[0]
    for r in rows[1:]:
        m = jnp.maximum(m, r)
    idx = jnp.full(m.shape, len(rows) - 1, jnp.int32)
    for k in range(len(rows) - 2, -1, -1):
        idx = jnp.where(rows[k] == m, k, idx)
    return m, idx


def _route_kernel(lg_ref, pos1_ref, pos2_ref, w1_ref, w2_ref, tab_ref, *, sb):
    lg = jnp.concatenate([lg_ref[b] for b in range(sb // TM)], axis=1)
    rows = [lg[k:k + 1, :] for k in range(N_EXPERTS + N_GROUPS)]
    grp = rows[N_EXPERTS:]
    gmax, gidx = _first_max(grp)
    p_group = 1.0 / sum(jnp.exp(r - gmax) for r in grp)
    e_in = []
    for k in range(EXP_PER_GROUP):
        v = rows[(N_GROUPS - 1) * EXP_PER_GROUP + k]
        for g in range(N_GROUPS - 2, -1, -1):
            v = jnp.where(gidx == g, rows[g * EXP_PER_GROUP + k], v)
        e_in.append(v)
    v1, i1 = _first_max(e_in)
    v2, i2 = _first_max([jnp.where(i1 == k, -jnp.inf, e_in[k]) for k in range(EXP_PER_GROUP)])
    ex = jnp.exp(v2 - v1)
    w1_ref[...] = p_group / (1.0 + ex)
    w2_ref[...] = p_group * ex / (1.0 + ex)
    x1 = gidx * EXP_PER_GROUP + i1
    x2 = gidx * EXP_PER_GROUP + i2

    eid = lax.broadcasted_iota(jnp.int32, (N_EXPERTS, sb), 0)
    sel = jnp.where((eid == x1) | (eid == x2), 1.0, 0.0)
    r_i = lax.broadcasted_iota(jnp.int32, (TM, TM), 0)
    c_i = lax.broadcasted_iota(jnp.int32, (TM, TM), 1)
    before = jnp.where(r_i < c_i, 1.0, 0.0).astype(BF16)
    carry = jnp.zeros((N_EXPERTS, 1), F32)
    ranks = []
    for b in range(sb // TM):
        s_b = sel[:, b * TM:(b + 1) * TM]
        ranks.append(_dot(_bf(s_b), before) + carry)
        carry = carry + jnp.sum(s_b, axis=1, keepdims=True)
    rank = jnp.concatenate(ranks, axis=1)
    shift = ROW_PAD.bit_length() - 1
    npad = jnp.left_shift(jnp.right_shift(carry.astype(jnp.int32) + (ROW_PAD - 1), shift), shift)

    lane = lax.broadcasted_iota(jnp.int32, (1, LANES), 1)
    tab = jnp.zeros((1, LANES), jnp.int32)
    pos1 = jnp.zeros((1, sb), F32)
    pos2 = jnp.zeros((1, sb), F32)
    off = jnp.zeros((1, 1), jnp.int32)
    for e in range(N_EXPERTS):
        n_e = npad[e:e + 1, :]
        tab = jnp.where(lane == e, off, tab)
        tab = jnp.where(lane == N_EXPERTS + e, n_e, tab)
        row = off.astype(F32) + rank[e:e + 1, :]
        pos1 = jnp.where(x1 == e, row, pos1)
        pos2 = jnp.where(x2 == e, row, pos2)
        off = off + n_e
    pos1_ref[...] = pos1.astype(jnp.int32) * _RPV
    pos2_ref[...] = pos2.astype(jnp.int32) * _RPV
    tab_ref[...] = tab


def _route(st, lg):
    sb = st.sb
    nsb = st.t // sb
    row_i = jax.ShapeDtypeStruct((nsb, 1, sb), jnp.int32)
    row_f = jax.ShapeDtypeStruct((nsb, 1, sb), F32)
    rspec = pl.BlockSpec((None, 1, sb), lambda s: (s, 0, 0))
    return pl.pallas_call(
        functools.partial(_route_kernel, sb=sb),
        out_shape=(row_i, row_i, row_f, row_f, jax.ShapeDtypeStruct((nsb, 1, LANES), jnp.int32)),
        grid=(nsb,),
        in_specs=[pl.BlockSpec((sb // TM, LG_ROWS, TM), lambda s: (s, 0, 0))],
        out_specs=(rspec, rspec, rspec, rspec, pl.BlockSpec((None, 1, LANES), lambda s: (s, 0, 0))),
        compiler_params=_cparams("arbitrary"),
        name="moe_route",
    )(lg)


def _moe_rows(sb):
    return 2 * sb + N_EXPERTS * ROW_PAD + FFN_TILE


def _moe_kernel(pos1_ref, pos2_ref, w1_ref, w2_ref, tab_ref, xr_ref, wg_ref, wu_ref, wd_ref, o_ref,
                rows_scr, stage_scr, *, sb):
    s = pl.program_id(0)
    e = pl.program_id(1)

    def tile_at(ref, r8):
        return ref.at[pl.ds(pl.multiple_of(r8, _RPV), _RPV)]

    @pl.when((s == 0) & (e == 0))
    def _():
        rows_scr[...] = jnp.zeros_like(rows_scr)

    @pl.when(e == 0)
    def _():
        def dispatch(t, carry):
            v = tile_at(xr_ref, t * _RPV)[...]
            tile_at(rows_scr, pos1_ref[0, t])[...] = v
            tile_at(rows_scr, pos2_ref[0, t])[...] = v
            return carry

        lax.fori_loop(0, sb, dispatch, 0, unroll=8)

    def expert(k):
        ex = e * EXPERTS_PER_STEP + k

        def ffn_tile(r0, m, valid=None):
            win = rows_scr.at[pl.ds(pl.multiple_of(r0 * _RPV, ROW_PAD * _RPV), m * _RPV)]
            x = _load_rows_per_vreg(win, m)
            xb = _bf(x)
            hg = _dot(xb, wg_ref[k])
            hu = _dot(xb, wu_ref[k])
            y = _dot(_bf(hg * jax.nn.sigmoid(hg) * hu), wd_ref[k])
            if valid is not None:
                y = jnp.where(lax.broadcasted_iota(jnp.int32, (m, 1), 0) < valid, y, x)
            _store_rows_per_vreg(win, y)

        off = tab_ref[0, ex]
        npad = tab_ref[0, N_EXPERTS + ex]
        nfull = npad // FFN_TILE

        def full_tile(i, carry):
            ffn_tile(off + i * FFN_TILE, FFN_TILE)
            return carry

        lax.fori_loop(0, nfull, full_tile, 0)
        rem = npad - nfull * FFN_TILE
        last = off + nfull * FFN_TILE

        @pl.when((rem > 0) & (rem <= FFN_TILE // 2))
        def _():
            ffn_tile(last, FFN_TILE // 2, valid=rem)

        @pl.when(rem > FFN_TILE // 2)
        def _():
            ffn_tile(last, FFN_TILE, valid=rem)

    for k in range(EXPERTS_PER_STEP):
        expert(k)

    @pl.when(e == N_EXPERTS // EXPERTS_PER_STEP - 1)
    def _():
        for c in range(sb // TM):
            def combine(t, carry, c=c):
                tt = c * TM + t
                y = (w1_ref[0, tt] * tile_at(rows_scr, pos1_ref[0, tt])[...]
                     + w2_ref[0, tt] * tile_at(rows_scr, pos2_ref[0, tt])[...])
                tile_at(stage_scr, t * _RPV)[...] = y
                return carry

            lax.fori_loop(0, TM, combine, 0, unroll=8)
            o_ref[c * TM:(c + 1) * TM, :] = _bf(_load_rows_per_vreg(stage_scr, TM))


def _cast_kernel(x_ref, o_ref):
    o_ref[...] = _bf(x_ref[...])


def _expert_weights_bf16(w):
    depth, ne, a, b = w.shape
    per_step = 4
    assert (depth * ne) % per_step == 0
    spec = pl.BlockSpec((per_step, a, b), lambda i: (i, 0, 0))
    return pl.pallas_call(
        _cast_kernel,
        out_shape=jax.ShapeDtypeStruct((depth * ne, a, b), BF16),
        grid=(depth * ne // per_step,),
        in_specs=[spec],
        out_specs=spec,
        compiler_params=_cparams("arbitrary"),
        name="expert_weight_cast",
    )(w.reshape(depth * ne, a, b))


def _moe(st, layer, xr, route, wg, wu, wd):
    sb = st.sb
    eps = EXPERTS_PER_STEP
    b0 = layer * N_EXPERTS // eps
    smem = lambda n: pl.BlockSpec((None, 1, n), lambda s, e: (s, 0, 0), memory_space=pltpu.SMEM)
    return pl.pallas_call(
        functools.partial(_moe_kernel, sb=sb),
        out_shape=jax.ShapeDtypeStruct((st.t, D_MODEL), BF16),
        grid=(st.t // sb, N_EXPERTS // eps),
        in_specs=[
            smem(sb), smem(sb), smem(sb), smem(sb), smem(LANES),
            pl.BlockSpec((sb * _RPV, LANES), lambda s, e: (s, 0)),
            pl.BlockSpec((eps, D_MODEL, D_EXPERT), lambda s, e: (b0 + e, 0, 0)),
            pl.BlockSpec((eps, D_MODEL, D_EXPERT), lambda s, e: (b0 + e, 0, 0)),
            pl.BlockSpec((eps, D_EXPERT, D_MODEL), lambda s, e: (b0 + e, 0, 0)),
        ],
        out_specs=pl.BlockSpec((sb, D_MODEL), lambda s, e: (s, 0)),
        scratch_shapes=[pltpu.VMEM((_moe_rows(sb) * _RPV, LANES), F32), pltpu.VMEM((TM * _RPV, LANES), F32)],
        compiler_params=_cparams("arbitrary", "arbitrary"),
        name="moe_ffn",
    )(*route, xr, wg, wu, wd)


def _hy_inproj_kernel(x_ref, m_ref, mod0_ref, mod_ref, nw_ref, w_ref, x2_ref, z_ref):
    g2 = _mod_slices(mod0_ref[...])[5]
    x2 = x_ref[...] + g2 * m_ref[...].astype(F32)
    x2_ref[...] = x2
    sh1, sc1 = _mod_slices(mod_ref[...])[:2]
    h = _rms(x2, nw_ref[...]) * (1.0 + sc1) + sh1
    z_ref[...] = _bf(_dot(_bf(h), w_ref[...]))


def _hy_inproj(st, x, moe, mods_prev, mods, nw, w):
    n = w.shape[1]
    tm = TM_TOK
    mspec = pl.BlockSpec((None, 1, N_MOD * D_MODEL), lambda i: (st.mod_row(i, tm), 0, 0))
    return pl.pallas_call(
        _hy_inproj_kernel,
        out_shape=(jax.ShapeDtypeStruct((st.t, D_MODEL), F32), jax.ShapeDtypeStruct((st.t, n), BF16)),
        grid=(st.t // tm,),
        in_specs=[
            pl.BlockSpec((tm, D_MODEL), lambda i: (i, 0)),
            pl.BlockSpec((tm, D_MODEL), lambda i: (i, 0)),
            mspec, mspec,
            pl.BlockSpec((1, D_MODEL), lambda i: (0, 0)),
            pl.BlockSpec((D_MODEL, n), lambda i: (0, 0)),
        ],
        out_specs=(pl.BlockSpec((tm, D_MODEL), lambda i: (i, 0)), pl.BlockSpec((tm, n), lambda i: (i, 0))),
        compiler_params=_cparams("arbitrary"),
        name="hy_inproj",
    )(x, moe, mods_prev, mods, nw, w)


def _dft_tables(seq):
    n2 = 2 * seq
    assert n2 & (n2 - 1) == 0
    tr = min(seq, TM)

    def table_kernel(cos_ref, msin_ref, cosb_ref, msinb_ref):
        k = lax.broadcasted_iota(jnp.int32, (tr, seq), 0) + pl.program_id(0) * tr
        n = lax.broadcasted_iota(jnp.int32, (tr, seq), 1)
        ang = ((k * n) & (n2 - 1)).astype(F32) * (2.0 * math.pi / n2)
        c, s = jnp.cos(ang), -jnp.sin(ang)
        cos_ref[...] = c
        msin_ref[...] = s
        cosb_ref[...] = _bf(c)
        msinb_ref[...] = _bf(s)

    spec = pl.BlockSpec((tr, seq), lambda i: (i, 0))
    f32, b16 = jax.ShapeDtypeStruct((seq, seq), F32), jax.ShapeDtypeStruct((seq, seq), BF16)
    return pl.pallas_call(table_kernel, out_shape=(f32, f32, b16, b16), grid=(seq // tr,),
                          out_specs=(spec, spec, spec, spec), compiler_params=_cparams("arbitrary"),
                          name=f"dft_tables_{seq}")()


def _alternating(shape):
    return jnp.where(lax.broadcasted_iota(jnp.int32, shape, 0) % 2 == 0, 1.0, -1.0)


def _hy_filter_kernel(emb_ref, dec_ref, w1_ref, b1_ref, f1_ref, w2_ref, b2_ref, f2_ref, w3_ref,
                      cos_ref, msin_ref, kr_ref, ki_ref, kn_ref, *, seq):
    h = jnp.sin(f1_ref[...] * (_dot_hp(emb_ref[...], w1_ref[...]) + b1_ref[...]))
    h = jnp.sin(f2_ref[...] * (_dot_hp(h, w2_ref[...]) + b2_ref[...]))
    dec = dec_ref[...]
    row0 = lax.broadcasted_iota(jnp.int32, dec.shape, 0) == 0
    alt = _alternating(dec.shape)
    cos, msin = cos_ref[...], msin_ref[...]
    scale = jnp.where(row0, 1.0, 2.0) / (2 * seq)
    for o in range(HY_ORDER):
        h_f = _dot_hp(h, w3_ref[:, 2 * o, :]) * dec
        h_b = jnp.where(row0, 0.0, _dot_hp(h, w3_ref[:, 2 * o + 1, :]) * dec)
        kn_ref[o] = jnp.sum(alt * (h_f + h_b), axis=0, keepdims=True) / (2 * seq)
        kr_ref[o] = _dot_hp(cos, h_f + h_b) * scale
        ki_ref[o] = _dot_hp(msin, h_f - h_b) * scale


def _hy_filter(seq, w1, b1, f1, w2, b2, f2, w3, cos, msin):
    t = jnp.linspace(0.0, 1.0, seq, dtype=F32)[:, None]
    w = 2.0 * math.pi * jnp.arange(seq, dtype=F32)[:, None] / seq
    f = jnp.linspace(1e-4, HY_BANDS - 1, HY_BANDS, dtype=F32)[None, :]
    emb = jnp.concatenate([t, jnp.cos(f * w), -jnp.sin(f * w), jnp.zeros((seq, LANES - HY_EMB), F32)], axis=-1)
    decay = jnp.exp(-t * jnp.linspace(HY_MIN_DECAY, HY_MAX_DECAY, D_MODEL, dtype=F32)[None, :])
    w1p = jnp.concatenate([w1, jnp.zeros((LANES - HY_EMB, HY_FFN), F32)], axis=0)
    dblk = 256
    out = jax.ShapeDtypeStruct((HY_ORDER, seq, D_MODEL), F32)
    full = lambda shape: pl.BlockSpec(shape, lambda j: (0,) * len(shape))
    ospec = pl.BlockSpec((HY_ORDER, seq, dblk), lambda j: (0, 0, j))
    return pl.pallas_call(
        functools.partial(_hy_filter_kernel, seq=seq),
        out_shape=(out, out, jax.ShapeDtypeStruct((HY_ORDER, 1, D_MODEL), F32)),
        grid=(D_MODEL // dblk,),
        in_specs=[
            full((seq, LANES)),
            pl.BlockSpec((seq, dblk), lambda j: (0, j)),
            full((LANES, HY_FFN)), full((1, HY_FFN)), full((1, HY_FFN)),
            full((HY_FFN, HY_FFN)), full((1, HY_FFN)), full((1, HY_FFN)),
            pl.BlockSpec((HY_FFN, 2 * HY_ORDER, dblk), lambda j: (0, 0, j)),
            full((seq, seq)), full((seq, seq)),
        ],
        out_specs=(ospec, ospec, pl.BlockSpec((HY_ORDER, 1, dblk), lambda j: (0, 0, j))),
        compiler_params=_cparams("arbitrary"),
        name=f"hy_filter_{seq}",
    )(emb, decay, w1p, b1[None, :], f1[None, :], w2, b2[None, :], f2[None, :],
      w3.reshape(HY_FFN, 2 * HY_ORDER, D_MODEL), cos, msin)


def _hy_conv_kernel(*refs, seq, nseq):
    (zv_ref, z1_ref, z2_ref, cv_ref, c1_ref, c2_ref, kr_ref, ki_ref, kn_ref, bias_ref,
     cos_ref, msin_ref) = refs[:12]
    y_ref = refs[-1]
    dblk = y_ref.shape[1]
    t = lax.broadcasted_iota(jnp.int32, (seq, dblk), 0)
    first, last = t == 0, t == seq - 1
    alt = _alternating((seq, dblk))
    seqs = range(nseq)
    rows = [slice(q * seq, (q + 1) * seq) for q in seqs]

    def short_conv(z_ref, c_ref):
        out = []
        for q in seqs:
            z = z_ref[rows[q], :].astype(F32)
            prev = jnp.where(first, 0.0, pltpu.roll(z, 1, 0))
            nxt = jnp.where(last, 0.0, pltpu.roll(z, seq - 1, 0))
            out.append(c_ref[0:1, :] * prev + c_ref[1:2, :] * z + c_ref[2:3, :] * nxt)
        return out

    def long_conv(s, o):
        sb = [_bf(s[q]) for q in seqs]
        x_re = [_dot(cos_ref[...], sb[q]) for q in seqs]
        x_im = [_dot(msin_ref[...], sb[q]) for q in seqs]
        y_nyq = [jnp.sum(alt * s[q], axis=0, keepdims=True) * kn_ref[o] for q in seqs]
        k_re, k_im = kr_ref[o], ki_ref[o]
        y_re = [x_re[q] * k_re - x_im[q] * k_im for q in seqs]
        y_im = [x_re[q] * k_im + x_im[q] * k_re for q in seqs]
        y = [_dot(cos_ref[...], _bf(y_re[q])) + _dot(msin_ref[...], _bf(y_im[q])) for q in seqs]
        return [y[q] + alt * y_nyq[q] + bias_ref[o] * s[q] for q in seqs]

    v = short_conv(zv_ref, cv_ref)
    x1 = short_conv(z1_ref, c1_ref)
    x2 = short_conv(z2_ref, c2_ref)
    c1 = long_conv(v, 0)
    c2 = long_conv([x1[q] * c1[q] for q in seqs], 1)
    for q in seqs:
        y_ref[rows[q], :] = _bf(x2[q] * c2[q])


def _hy_conv(st, ctx, z, conv_w, filt, bias, tables, prev):
    seq = st.lp if ctx else st.ls
    nb = st.bp if ctx else st.bs
    nseq = max(1, min(nb, 4, 2048 // seq))
    assert nb % nseq == 0
    off = 0 if ctx else st.tp // (nseq * seq)
    dblk = 256
    nd = D_MODEL // dblk
    kr, ki, kn = filt
    cos, msin = tables
    vspec = pl.BlockSpec((HY_ORDER, 1, dblk), lambda j, b: (0, 0, j))
    full = lambda shape: pl.BlockSpec(shape, lambda j, b: (0,) * len(shape))
    kspec = pl.BlockSpec((HY_ORDER, seq, dblk), lambda j, b: (0, 0, j))
    in_specs = [
        pl.BlockSpec((nseq * seq, dblk), lambda j, b: (off + b, j)),
        pl.BlockSpec((nseq * seq, dblk), lambda j, b: (off + b, nd + j)),
        pl.BlockSpec((nseq * seq, dblk), lambda j, b: (off + b, 2 * nd + j)),
        pl.BlockSpec((3, dblk), lambda j, b: (0, j)),
        pl.BlockSpec((3, dblk), lambda j, b: (0, nd + j)),
        pl.BlockSpec((3, dblk), lambda j, b: (0, 2 * nd + j)),
        kspec, kspec, vspec, vspec,
        full((seq, seq)), full((seq, seq)),
    ]
    args = [z, z, z, conv_w, conv_w, conv_w, kr, ki, kn, bias.reshape(HY_ORDER, 1, D_MODEL), cos, msin]
    aliases = {}
    if prev is not None:
        aliases = {len(args): 0}
        in_specs.append(pl.BlockSpec(memory_space=pl.ANY))
        args.append(prev)
    return pl.pallas_call(
        functools.partial(_hy_conv_kernel, seq=seq, nseq=nseq),
        out_shape=jax.ShapeDtypeStruct((st.t, D_MODEL), BF16),
        grid=(nd, nb // nseq),
        in_specs=in_specs,
        out_specs=pl.BlockSpec((nseq * seq, dblk), lambda j, b: (off + b, j)),
        input_output_aliases=aliases,
        compiler_params=_cparams("arbitrary", "arbitrary"),
        name="hy_conv_ctx" if ctx else "hy_conv_lat",
    )(*args)


def _hy_outproj_kernel(y_ref, x_ref, mod_ref, w_ref, n2w_ref, wrt_ref, brt_ref, x3_ref, h2_ref, lg_ref):
    mod = _mod_slices(mod_ref[...])
    g1, sh2, sc2 = mod[2], mod[3], mod[4]
    nt = y_ref.shape[0] // TM
    tiles = range(nt)
    rows = [slice(b * TM, (b + 1) * TM) for b in tiles]
    mix = [_dot(y_ref[rows[b], :], w_ref[...]) for b in tiles]
    x3 = [x_ref[rows[b], :] + g1 * mix[b] for b in tiles]
    h2 = [_rms(x3[b], n2w_ref[...]) * (1.0 + sc2) + sh2 for b in tiles]
    lg = [_router_logits(h2[b], wrt_ref, brt_ref) for b in tiles]
    for b in tiles:
        x3_ref[rows[b], :] = x3[b]
        _store_rows_per_vreg(h2_ref.at[pl.ds(b * TM * _RPV, TM * _RPV)], h2[b])
        lg_ref[b] = lg[b]


def _hy_outproj(st, y, x, mods, w, n2w, wrt, brt):
    tm = TM_TOK
    return pl.pallas_call(
        _hy_outproj_kernel,
        out_shape=(jax.ShapeDtypeStruct((st.t, D_MODEL), F32),
                   jax.ShapeDtypeStruct((st.t * _RPV, LANES), F32),
                   jax.ShapeDtypeStruct((st.t // TM, LG_ROWS, TM), F32)),
        grid=(st.t // tm,),
        in_specs=[
            pl.BlockSpec((tm, D_MODEL), lambda i: (i, 0)),
            pl.BlockSpec((tm, D_MODEL), lambda i: (i, 0)),
            pl.BlockSpec((None, 1, N_MOD * D_MODEL), lambda i: (st.mod_row(i, tm), 0, 0)),
            pl.BlockSpec((D_MODEL, D_MODEL), lambda i: (0, 0)),
            pl.BlockSpec((1, D_MODEL), lambda i: (0, 0)),
            pl.BlockSpec((LG_ROWS, D_MODEL), lambda i: (0, 0)),
            pl.BlockSpec((LG_ROWS, 1), lambda i: (0, 0)),
        ],
        out_specs=(pl.BlockSpec((tm, D_MODEL), lambda i: (i, 0)),
                   pl.BlockSpec((tm * _RPV, LANES), lambda i: (i, 0)),
                   pl.BlockSpec((tm // TM, LG_ROWS, TM), lambda i: (i, 0, 0))),
        compiler_params=_cparams("arbitrary"),
        name="hy_outproj",
    )(y, x, mods, w, n2w, wrt, brt)


def _final_kernel(x_ref, m_ref, mod_ref, nf_ref, yp_ref, ys_ref, *, ncb):
    i = pl.program_id(0)
    g2 = _mod_slices(mod_ref[...])[5]
    y = _rms(x_ref[...] + g2 * m_ref[...].astype(F32), nf_ref[...])

    @pl.when(i < ncb)
    def _():
        yp_ref[...] = y

    @pl.when(i >= ncb)
    def _():
        ys_ref[...] = y


def _final(st, x, moe, mods, nf):
    tm = TM_TOK
    ncb = st.tp // tm
    return pl.pallas_call(
        functools.partial(_final_kernel, ncb=ncb),
        out_shape=(jax.ShapeDtypeStruct((st.tp, D_MODEL), F32), jax.ShapeDtypeStruct((st.ts, D_MODEL), F32)),
        grid=(st.t // tm,),
        in_specs=[
            pl.BlockSpec((tm, D_MODEL), lambda i: (i, 0)),
            pl.BlockSpec((tm, D_MODEL), lambda i: (i, 0)),
            pl.BlockSpec((None, 1, N_MOD * D_MODEL), lambda i: (st.mod_row(i, tm), 0, 0)),
            pl.BlockSpec((1, D_MODEL), lambda i: (0, 0)),
        ],
        out_specs=(pl.BlockSpec((tm, D_MODEL), lambda i: (jnp.minimum(i, ncb - 1), 0)),
                   pl.BlockSpec((tm, D_MODEL), lambda i: (jnp.maximum(i - ncb, 0), 0))),
        compiler_params=_cparams("arbitrary"),
        name="final_norm",
    )(x, moe, mods, nf)


def _grid_pos_table(seq):
    rows = seq // GRID_W
    r, cl = jnp.meshgrid(jnp.arange(rows, dtype=F32), jnp.arange(GRID_W, dtype=F32), indexing='ij')
    quarter = D_MODEL // 4
    omega = POS_THETA ** (-jnp.arange(quarter, dtype=F32) / quarter)

    def enc(pos):
        a = pos.reshape(-1, 1) * omega[None, :]
        return jnp.concatenate([jnp.sin(a), jnp.cos(a)], axis=-1)

    return jnp.concatenate([enc(r), enc(cl)], axis=-1)


def _router_weights(w_group, b_group, w_router, b_router):
    pad = LG_ROWS - N_EXPERTS - N_GROUPS
    w = jnp.concatenate([w_router.T, w_group.T, jnp.zeros((pad, D_MODEL), F32)], axis=0)
    b = jnp.concatenate([b_router, b_group, jnp.zeros((pad,), F32)])[:, None]
    return w, b


def kernel(x_prompt, x_sample, state_gla, state_mlstm_c, state_mlstm_n, state_mlstm_m, c, c_ctx, norm1_w, norm2_w, norm_f_w, ada_w, ada_b, rec_w_in, gla_gk_w, gla_gk_b, mlstm_gate_b, gla_norm_w, mlstm_norm_w, rec_w_out, hy_w_in, hy_conv_w, hy_f_w1, hy_f_b1, hy_f_freq1, hy_f_w2, hy_f_b2, hy_f_freq2, hy_f_w3, hy_f_bias, hy_w_out, moe_w_group, moe_b_group, moe_w_router, moe_b_router, moe_w_gate, moe_w_up, moe_w_down):
    bp, lp, _ = x_prompt.shape
    bs, ls, _ = x_sample.shape
    st = _Streams(bp, lp, bs, ls)
    xp = x_prompt.reshape(st.tp, D_MODEL)
    xs = x_sample.reshape(st.ts, D_MODEL)

    nrow = -(-(1 + bs) // 8) * 8
    cv = jnp.concatenate([c_ctx[None, :], c, jnp.zeros((nrow - 1 - bs, D_MODEL), F32)], axis=0)
    mods = _ada(cv, ada_w, ada_b)
    mods0 = mods[0].reshape(nrow, 1, N_MOD * D_MODEL)
    mods1 = mods[1].reshape(nrow, 1, N_MOD * D_MODEL)

    pos = _grid_pos_table(ls)

    w_in = rec_w_in[0]
    w_main = _bf(w_in[:, :_REC_MAIN])
    w_mg = w_in[:, _REC_MAIN + _MG0:].reshape(D_MODEL, 2, 2, H_B)
    w_fg = jnp.pad(w_mg[:, :, 1, :], ((0, 0), (0, 0), (0, 8 - H_B))).reshape(D_MODEL, 16)
    w_gate = jnp.concatenate([w_in[:, _REC_MAIN:], jnp.zeros((D_MODEL, LANES - _REC_GATE), F32),
                              jnp.zeros((D_MODEL, _MG0), F32), w_fg,
                              jnp.zeros((D_MODEL, LANES - _REC_GATE), F32)], axis=1)
    w_gate_t = w_in[:, _REC_MAIN + _MG0:].T
    x0, main, g, g2, mt = _inproj(st, xp, xs, pos, mods0, norm1_w[0][None, :], w_main, w_gate, w_gate_t)

    gkw = jnp.zeros((2, LANES, H_A * DK_A), F32)
    gkw = gkw.at[0, :GK_RANK].set(gla_gk_w[0, 0]).at[1, GK_RANK:2 * GK_RANK].set(gla_gk_w[0, 1])
    gkb = gla_gk_b[0][:, None, :]
    gb = mlstm_gate_b[0]
    gate_row = lambda b: jnp.pad(jnp.pad(b, ((0, 0), (0, 8 - H_B))).reshape(1, 16), ((0, 0), (_MG0, LANES - _REC_GATE)))
    gbcol = gb.reshape(16, 1)
    wr0, br0 = _router_weights(moe_w_group[0], moe_b_group[0], moe_w_router[0], moe_b_router[0])
    consts = [gkw, gkb, gate_row(gb[:, 0]), gate_row(gb[:, 1]), gbcol, gla_norm_w[0][None, :],
              mlstm_norm_w[0][None, :], _bf(rec_w_out[0]), norm2_w[0][None, :], wr0, br0]

    def scan_states(sg, sc, sn, sm):
        nb = sg.shape[0]
        return (jnp.swapaxes(sg, -1, -2), sc, sn.reshape(nb, 2 * H_B, DK_B), sm.reshape(nb, 1, 2 * H_B))

    zero = (jnp.zeros((bp, 2, H_A, DK_A, DV_A), F32), jnp.zeros((bp, 2, H_B, DK_B, DV_B), F32),
            jnp.zeros((bp, 2, H_B, DK_B), F32), jnp.zeros((bp, 2, H_B), F32))
    cached = (state_gla[:, 0], state_mlstm_c[:, 0], state_mlstm_n[:, 0], state_mlstm_m[:, 0])
    x1, h2, lg, fg, fc, fn, fm = _scan(st, True, main, g, g2, mt, x0, mods0, scan_states(*zero), consts, None)
    x1, h2, lg = _scan(st, False, main, g, g2, mt, x0, mods0, scan_states(*cached), consts, (x1, h2, lg))[:3]

    new_gla = jnp.swapaxes(fg, -1, -2)[:, None]
    new_c = fc[:, None]
    new_n = fn.reshape(bp, 1, 2, H_B, DK_B)
    new_m = fm.reshape(bp, 1, 2, H_B)

    wg, wu, wd = (_expert_weights_bf16(w) for w in (moe_w_gate, moe_w_up, moe_w_down))
    moe0 = _moe(st, 0, h2, _route(st, lg), wg, wu, wd)

    x2, z = _hy_inproj(st, x1, moe0, mods0, mods1, norm1_w[1][None, :], _bf(hy_w_in[0]))
    y = None
    for ctx in (True, False):
        seq = lp if ctx else ls
        cos, msin, cos_b, msin_b = _dft_tables(seq)
        filt = _hy_filter(seq, hy_f_w1[0], hy_f_b1[0], hy_f_freq1[0], hy_f_w2[0], hy_f_b2[0], hy_f_freq2[0],
                          hy_f_w3[0], cos, msin)
        y = _hy_conv(st, ctx, z, hy_conv_w[0], filt, hy_f_bias[0], (cos_b, msin_b), y)
    wr1, br1 = _router_weights(moe_w_group[1], moe_b_group[1], moe_w_router[1], moe_b_router[1])
    x3, h4, lg1 = _hy_outproj(st, y, x2, mods1, _bf(hy_w_out[0]), norm2_w[1][None, :], wr1, br1)

    moe1 = _moe(st, 1, h4, _route(st, lg1), wg, wu, wd)
    y_prompt, y_sample = _final(st, x3, moe1, mods1, norm_f_w[None, :])
    return (y_prompt.reshape(bp, lp, D_MODEL), y_sample.reshape(bs, ls, D_MODEL), new_gla, new_c, new_n, new_m)
```

```python
import functools
import math

import jax
import jax.numpy as jnp
from jax import lax
from jax.experimental import pallas as pl
from jax.experimental.pallas import tpu as pltpu

F32 = jnp.float32
BF16 = jnp.bfloat16

D_MODEL = 1024
GRID_W = 64
H_A = 4
DK_A = D_MODEL // 16
DV_A = D_MODEL // 8
GK_RANK = 16
GATE_TEMP = 16.0
H_B = 4
DK_B = D_MODEL // 16
DV_B = D_MODEL // 8
CHUNK = 64
HY_ORDER = 2
HY_EMB = 33
HY_BANDS = (HY_EMB - 1) // 2
HY_FFN = 64
HY_TARGET = 1e-2
HY_MAX_DECAY = abs(math.log(HY_TARGET)) / 0.3
HY_MIN_DECAY = abs(math.log(HY_TARGET)) / 1.5
N_GROUPS = 4
EXP_PER_GROUP = 4
N_EXPERTS = N_GROUPS * EXP_PER_GROUP
D_EXPERT = D_MODEL // 2
N_MOD = 6
POS_THETA = 10000.0
EPS = 1e-6

_QA, _KA, _VA, _RA = 0, 256, 512, 1024
_QB, _KB, _VB, _OB = 1536, 1792, 2048, 2560
_REC_MAIN = 3072
_REC_GATE = 48
_MG0 = 2 * GK_RANK

LANES = 128
SUBLANES = 8
_RPV = D_MODEL // LANES
assert _RPV == SUBLANES
TM = 256
TM_TOK = 1024
TM_IN = 512
SB_MOE = 2048
ROW_PAD = 16
EXPERTS_PER_STEP = 2
FFN_TILE = 288
LG_ROWS = 32
VMEM_LIMIT = 58 * 1024 * 1024


def _cparams(*sem):
    return pltpu.CompilerParams(dimension_semantics=sem, vmem_limit_bytes=VMEM_LIMIT)


def _bf(x):
    return x.astype(BF16)


def _dot(a, b):
    return jnp.dot(a, b, preferred_element_type=F32)


def _dot_nt(a, b):
    return lax.dot_general(a, b, (((1,), (1,)), ((), ())), preferred_element_type=F32)


def _dot_tn(a, b):
    return lax.dot_general(a, b, (((0,), (0,)), ((), ())), preferred_element_type=F32)


def _split2(x):
    hi = _bf(x)
    return hi, _bf(x - hi.astype(F32))


def _split3(x):
    hi = _bf(x)
    r = x - hi.astype(F32)
    mid = _bf(r)
    return hi, mid, _bf(r - mid.astype(F32))


def _dot_hp(a, b, dot=_dot):
    ah, al = _split2(a)
    bh, bl = _split2(b)
    return dot(ah, bh) + (dot(ah, bl) + dot(al, bh))


def _dot_mask_l(m, x, terms=3):
    if terms == 2:
        x1, x2 = _split2(x)
        return _dot(m, x1) + _dot(m, x2)
    x1, x2, x3 = _split3(x)
    return _dot(m, x1) + (_dot(m, x2) + _dot(m, x3))


def _dot_mask_r(x, m, terms=3):
    if terms == 2:
        x1, x2 = _split2(x)
        return _dot(x1, m) + _dot(x2, m)
    x1, x2, x3 = _split3(x)
    return _dot(x1, m) + (_dot(x2, m) + _dot(x3, m))


def _rms(x, w):
    return x * lax.rsqrt(jnp.mean(x * x, axis=-1, keepdims=True) + EPS) * w


def _mod_slices(mod):
    return [mod[:, k * D_MODEL:(k + 1) * D_MODEL] for k in range(N_MOD)]


def _store_rows_per_vreg(dst, h):
    rows = h.shape[0]
    for j in range(_RPV):
        dst[pl.ds(j, rows, stride=_RPV), :] = h[:, j * LANES:(j + 1) * LANES]


def _load_rows_per_vreg(src, rows):
    return jnp.concatenate([src[pl.ds(j, rows, stride=_RPV), :] for j in range(_RPV)], axis=-1)


def _router_logits(h, wrt_ref, brt_ref):
    return _dot_hp(wrt_ref[...], h, dot=_dot_nt) + brt_ref[...]


def _ada_kernel(cv_ref, w_ref, b_ref, o_ref):
    a = cv_ref[...]
    a = a * jax.nn.sigmoid(a)
    o_ref[...] = _dot_hp(a, w_ref[...]) + b_ref[...]


def _ada(cv, ada_w, ada_b):
    depth, d, n = ada_w.shape
    rows = cv.shape[0]
    tn = 1536
    return pl.pallas_call(
        _ada_kernel,
        out_shape=jax.ShapeDtypeStruct((depth, rows, n), F32),
        grid=(depth, n // tn),
        in_specs=[
            pl.BlockSpec((rows, d), lambda l, j: (0, 0)),
            pl.BlockSpec((None, d, tn), lambda l, j: (l, 0, j)),
            pl.BlockSpec((None, 1, tn), lambda l, j: (l, 0, j)),
        ],
        out_specs=pl.BlockSpec((None, rows, tn), lambda l, j: (l, 0, j)),
        compiler_params=_cparams("arbitrary", "arbitrary"),
        name="ada_mod",
    )(cv, ada_w, ada_b.reshape(depth, 1, n))


class _Streams:
    def __init__(self, bp, lp, bs, ls):
        self.bp, self.lp, self.bs, self.ls = bp, lp, bs, ls
        self.tp, self.ts = bp * lp, bs * ls
        self.t = self.tp + self.ts
        assert lp % TM == 0 and ls % TM == 0 and self.tp % ls == 0
        assert self.tp % TM_TOK == 0 and ls % TM_TOK == 0
        self.sb = min(SB_MOE, math.gcd(self.tp, self.ts))
        assert self.t % self.sb == 0 and self.sb % TM == 0

    def mod_row(self, i, tm):
        ncb = self.tp // tm
        return jnp.where(i < ncb, 0, 1 + (i - ncb) // (self.ls // tm))


def _inproj_kernel(xp_ref, xs_ref, pos_ref, mod_ref, nw_ref, w_ref, wg_ref, wgt_ref,
                   x0_ref, main_ref, g_ref, g2_ref, mt_ref, *, ncb):
    i = pl.program_id(0)
    x = jnp.where(i < ncb, xp_ref[...], xs_ref[...] + pos_ref[...])
    x0_ref[...] = x
    sh1, sc1 = _mod_slices(mod_ref[...])[:2]
    h = _rms(x, nw_ref[...]) * (1.0 + sc1) + sh1
    main_ref[...] = _bf(_dot(_bf(h), w_ref[...]))
    h_hi, h_lo = _split2(h)
    wg_hi = _bf(wg_ref[...])
    gates = _dot(h_hi, wg_hi) + _dot(h_lo, wg_hi)
    g_ref[...] = gates[:, :LANES]
    g2_ref[...] = gates[:, LANES:]
    mt = _dot_hp(wgt_ref[...], h, dot=_dot_nt)
    for c in range(mt.shape[1] // CHUNK):
        piece = mt[:, c * CHUNK:(c + 1) * CHUNK]
        mt_ref[c] = jnp.concatenate([piece, piece], axis=1)


def _inproj(st, xp, xs, pos, mods, nw, w_main, w_gate, w_gate_t):
    tm = TM_IN
    assert st.tp % tm == 0 and st.ls % tm == 0
    ncb = st.tp // tm
    bps = st.ls // tm
    t = st.t
    return pl.pallas_call(
        functools.partial(_inproj_kernel, ncb=ncb),
        out_shape=(jax.ShapeDtypeStruct((t, D_MODEL), F32),
                   jax.ShapeDtypeStruct((t, _REC_MAIN), BF16),
                   jax.ShapeDtypeStruct((t, LANES), F32),
                   jax.ShapeDtypeStruct((t, LANES), F32),
                   jax.ShapeDtypeStruct((t // CHUNK, 16, 2 * CHUNK), F32)),
        grid=(t // tm,),
        in_specs=[
            pl.BlockSpec((tm, D_MODEL), lambda i: (jnp.minimum(i, ncb - 1), 0)),
            pl.BlockSpec((tm, D_MODEL), lambda i: (jnp.maximum(i - ncb, 0), 0)),
            pl.BlockSpec((tm, D_MODEL), lambda i: (jnp.maximum(i - ncb, 0) % bps, 0)),
            pl.BlockSpec((None, 1, N_MOD * D_MODEL), lambda i: (st.mod_row(i, tm), 0, 0)),
            pl.BlockSpec((1, D_MODEL), lambda i: (0, 0)),
            pl.BlockSpec((D_MODEL, _REC_MAIN), lambda i: (0, 0)),
            pl.BlockSpec((D_MODEL, 2 * LANES), lambda i: (0, 0)),
            pl.BlockSpec((16, D_MODEL), lambda i: (0, 0)),
        ],
        out_specs=(pl.BlockSpec((tm, D_MODEL), lambda i: (i, 0)),
                   pl.BlockSpec((tm, _REC_MAIN), lambda i: (i, 0)),
                   pl.BlockSpec((tm, LANES), lambda i: (i, 0)),
                   pl.BlockSpec((tm, LANES), lambda i: (i, 0)),
                   pl.BlockSpec((tm // CHUNK, 16, 2 * CHUNK), lambda i: (i, 0, 0))),
        compiler_params=_cparams("arbitrary"),
        name="rec_inproj",
    )(xp, xs, pos, mods, nw, w_main, w_gate, w_gate_t)


def _block_diag(x):
    left = lax.broadcasted_iota(jnp.int32, (1, x.shape[1]), 1) < x.shape[1] // 2
    zero = jnp.zeros_like(x)
    return jnp.concatenate([jnp.where(left, x, zero), jnp.where(left, zero, x)], axis=0)


def _block_diag_mask(rows, width):
    r = lax.broadcasted_iota(jnp.int32, (rows, width), 0) < rows // 2
    l = lax.broadcasted_iota(jnp.int32, (rows, width), 1) < width // 2
    return r == l


def _running_max(x, reverse):
    n = x.shape[0]
    row = lax.broadcasted_iota(jnp.int32, x.shape, 0)
    sh = 1
    while sh < n:
        if reverse:
            y = jnp.where(row < n - sh, pltpu.roll(x, n - sh, 0), -jnp.inf)
        else:
            y = jnp.where(row >= sh, pltpu.roll(x, sh, 0), -jnp.inf)
        x = jnp.maximum(x, y)
        sh *= 2
    return x


def _scan_kernel(*refs, seq):
    (main_ref, g_ref, g2_ref, mt_ref, x_ref, mod_ref, sg_ref, sc_ref, sn_ref, sm_ref,
     gkw_ref, gkb_ref, gbi_ref, gbf_ref, gbcol_ref, gnw_ref, mnw_ref, wout_ref, n2w_ref, wrt_ref, brt_ref) = refs[:21]
    (x1_ref, h2_ref, lg_ref, og_ref, oc_ref, on_ref, om_ref,
     oa_scr, ob_scr, sbd_scr, cbd_scr, nbd_scr, mgl_scr) = refs[-13:]
    c = CHUNK
    nchunks = seq // c
    npair = H_A // 2
    assert H_A == H_B and DK_A == DK_B == c and DV_A == DV_B == LANES and 2 * DK_A == LANES

    lane = lax.broadcasted_iota(jnp.int32, (1, LANES), 1)
    gate_lane = lambda d, h: _MG0 + d * 8 + h
    used = tuple((lane >= gate_lane(d, 0)) & (lane < gate_lane(d, H_B)) for d in (0, 1))

    for d in (0, 1):
        for p in range(npair):
            k = d * npair + p
            z_s = jnp.zeros((DV_A, DK_A), F32)
            sbd_scr[k] = jnp.concatenate([jnp.concatenate([sg_ref[d, 2 * p], z_s], axis=1),
                                          jnp.concatenate([z_s, sg_ref[d, 2 * p + 1]], axis=1)], axis=0)
            z_c = jnp.zeros((DK_B, DV_B), F32)
            cbd_scr[k] = jnp.concatenate([jnp.concatenate([sc_ref[d, 2 * p], z_c], axis=1),
                                          jnp.concatenate([z_c, sc_ref[d, 2 * p + 1]], axis=1)], axis=0)
            n_rep = [jnp.broadcast_to(sn_ref[d * H_B + 2 * p + q:d * H_B + 2 * p + q + 1, :], (DV_B, DK_B)).T
                     for q in (0, 1)]
            nbd_scr[k] = jnp.concatenate([jnp.concatenate([n_rep[0], z_c], axis=1),
                                          jnp.concatenate([z_c, n_rep[1]], axis=1)], axis=0)
    m_gl = jnp.zeros((1, LANES), F32)
    for d in (0, 1):
        for h in range(H_B):
            r = d * H_B + h
            m_gl = jnp.where(lane == gate_lane(d, h), sm_ref[:, r:r + 1], m_gl)
    mgl_scr[...] = m_gl

    row_p = lax.broadcasted_iota(jnp.int32, (c, LANES), 0)
    s_p = lax.broadcasted_iota(jnp.int32, (c, LANES), 1) % c
    live_p = (s_p <= row_p, s_p >= row_p)
    row = lax.broadcasted_iota(jnp.int32, (c, c), 0)
    col = lax.broadcasted_iota(jnp.int32, (c, c), 1)
    tri = tuple(jnp.where(m, 1.0, 0.0).astype(BF16) for m in (col <= row, col >= row))
    tri_t2 = tuple(jnp.where(m, 1.0, 0.0).astype(BF16) for m in (row_p <= s_p, row_p >= s_p))
    grow = lax.broadcasted_iota(jnp.int32, (16, 1), 0)
    f_row = (grow % 8) >= 4
    scale_q = DK_A ** -0.5
    bd_val = _block_diag_mask(2 * DV_A, 2 * DK_A)
    bd_key = _block_diag_mask(2 * DK_B, 2 * DV_B)
    ones_bd = jnp.where(bd_key, 1.0, 0.0).astype(BF16)
    ones_cv = jnp.ones((c, 2 * DV_B), BF16)

    def replicate(d, width):
        r = lax.broadcasted_iota(jnp.int32, (LANES, H_B * width), 0)
        h = lax.broadcasted_iota(jnp.int32, (LANES, H_B * width), 1) // width
        return jnp.where(r == gate_lane(d, 0) + h, 1.0, 0.0).astype(BF16)

    rep_k = tuple(replicate(d, DK_B) for d in (0, 1))
    rep_v = tuple(replicate(d, DV_B) for d in (0, 1))

    def chunk_step(i, carry):
        rows, g_in, g2_in, mt_in, gla_in, mls_in = [], [], [], [], [], []
        for d in (0, 1):
            ci = i if d == 0 else nchunks - 1 - i
            rows.append(pl.ds(pl.multiple_of(ci * c, c), c))
            g_in.append(g_ref[rows[d], :])
            g2_in.append(g2_ref[rows[d], :])
            mt_in.append(mt_ref[ci])
            gla_in.append(main_ref[rows[d], _QA:_RA])
            mls_in.append(main_ref[rows[d], _QB:_OB])
        s_bd = [sbd_scr[k] for k in range(2 * npair)]
        c_bd = [cbd_scr[k] for k in range(2 * npair)]
        n_bd = [nbd_scr[k] for k in range(2 * npair)]
        m_gl = mgl_scr[...]
        dirs = (0, 1)
        ends = (c - 1, 0)
        pairs = [(d, p) for d in dirs for p in range(npair)]
        ks_of = lambda p: slice(p * 2 * DK_A, (p + 1) * 2 * DK_A)
        vs_of = lambda p: slice(p * 2 * DV_A, (p + 1) * 2 * DV_A)

        glin = [_dot_hp(g_in[d], gkw_ref[d]) + gkb_ref[d] for d in dirs]
        gi = [jnp.where(used[d], g_in[d] + gbi_ref[...], 0.0) for d in dirs]
        lf = [jnp.where(used[d], jax.nn.log_sigmoid(g2_in[d] + gbf_ref[...]), 0.0) for d in dirs]
        mt = [mt_in[d] + gbcol_ref[...] for d in dirs]
        mt = [jnp.where(f_row, jax.nn.log_sigmoid(mt[d]), mt[d]) for d in dirs]
        glog = [jax.nn.log_sigmoid(glin[d]) / GATE_TEMP for d in dirs]
        bc = [_dot_mask_l(tri[d], glog[d], 2) for d in dirs]
        cum = [_dot_mask_l(tri[d], lf[d], 2) for d in dirs]
        cum_t = [_dot_mask_r(mt[d][:, 0:c], tri_t2[d], 2) for d in dirs]
        b_end = [bc[d][ends[d]:ends[d] + 1, :] for d in dirs]
        qa = [gla_in[d][:, _QA:_QA + 256].astype(F32) * scale_q for d in dirs]
        ka = [gla_in[d][:, _KA:_KA + 256].astype(F32) for d in dirs]
        va = [gla_in[d][:, _VA:_VA + 512] for d in dirs]
        qe = [_bf(qa[d] * jnp.exp(bc[d])) for d in dirs]
        ke = [_bf(ka[d] * jnp.exp(-bc[d])) for d in dirs]
        kd = [_bf(ka[d] * jnp.exp(b_end[d] - bc[d])) for d in dirs]
        eb_end = [jnp.exp(b_end[d]) for d in dirs]
        qb = [mls_in[d][:, 0:256] * jnp.asarray(DK_B ** -0.5, BF16) for d in dirs]
        kb = [mls_in[d][:, _KB - _QB:_KB - _QB + 256] for d in dirs]
        vb = [mls_in[d][:, _VB - _QB:_VB - _QB + 512] for d in dirs]
        a_raw = {(d, p): _dot_nt(qe[d][:, ks_of(p)], _block_diag(ke[d][:, ks_of(p)])) for d, p in pairs}
        qk = {(d, p): _dot_nt(qb[d][:, ks_of(p)], _block_diag(kb[d][:, ks_of(p)])) for d, p in pairs}
        s_upd = {(d, p): _dot_tn(va[d][:, vs_of(p)], kd[d][:, ks_of(p)]) for d, p in pairs}
        o_car = {(d, p): _dot_nt(qe[d][:, ks_of(p)], _bf(s_bd[d * npair + p])) for d, p in pairs}
        m_loc = [cum[d] + _running_max(gi[d] - cum[d], reverse=(d == 1)) for d in dirs]
        inter = [cum[d] + m_gl for d in dirs]
        m_t = [jnp.maximum(inter[d], m_loc[d]) for d in dirs]
        b_last = [cum[d][ends[d]:ends[d] + 1, :] for d in dirs]
        dend = [b_last[d] - cum[d] + gi[d] for d in dirs]
        m_new = [jnp.maximum(b_last[d] + m_gl, jnp.max(dend[d], axis=0, keepdims=True)) for d in dirs]
        zero = jnp.zeros((c, LANES), F32)
        per_key = [jnp.concatenate([jnp.where(used[d], cum[d] - m_t[d], zero),
                                    jnp.where(used[d], jnp.exp(inter[d] - m_t[d]), zero),
                                    jnp.where(used[d], jnp.exp(dend[d] - m_new[d]), zero)], axis=0) for d in dirs]
        per_val = [jnp.concatenate([jnp.where(used[d], jnp.exp(-m_t[d]), zero),
                                    jnp.broadcast_to(jnp.where(used[d], jnp.exp(b_last[d] + m_gl - m_new[d]), 0.0),
                                                     (SUBLANES, LANES))], axis=0) for d in dirs]
        per_key = [_dot_mask_r(per_key[d], rep_k[d], 2) for d in dirs]
        per_val = [_dot_mask_r(per_val[d], rep_v[d], 2) for d in dirs]
        o_par = {(d, p): _dot(_bf(jnp.where(live_p[d], a_raw[d, p], 0.0)), _block_diag(va[d][:, vs_of(p)]))
                 for d, p in pairs}
        for d, p in pairs:
            k = d * npair + p
            s_bd[k] = s_bd[k] * eb_end[d][:, ks_of(p)] + jnp.where(bd_val, s_upd[d, p], 0.0)
        o_gla = [jnp.concatenate([o_par[d, p] + o_car[d, p] for p in range(npair)], axis=-1) for d in dirs]
        for d in dirs:
            m_gl = jnp.where(used[d], m_new[d], m_gl)

        w, qa2, kw = {}, {}, {}
        for d, p in pairs:
            ks = ks_of(p)
            r_i, r_f = d * 8 + 2 * p, d * 8 + 4 + 2 * p
            sub = [cum_t[d][r_f + q:r_f + q + 1, :] - mt[d][r_i + q:r_i + q + 1, :] for q in (0, 1)]
            sub = jnp.where(lane < DK_B, sub[0], sub[1])
            w[d, p] = jnp.exp(jnp.where(live_p[d], per_key[d][0:c, ks] - sub, -jnp.inf))
            qa2[d, p] = _bf(qb[d][:, ks].astype(F32) * per_key[d][c:2 * c, ks])
            kw[d, p] = _bf(kb[d][:, ks].astype(F32) * per_key[d][2 * c:3 * c, ks])
        carried = {(d, p): _dot(qa2[d, p], jnp.concatenate([_bf(c_bd[d * npair + p]), _bf(n_bd[d * npair + p])], axis=1))
                   for d, p in pairs}
        upd = {(d, p): _dot_tn(kw[d, p], jnp.concatenate([vb[d][:, vs_of(p)], ones_cv], axis=1)) for d, p in pairs}
        intra = {(d, p): _dot(_bf(qk[d, p] * w[d, p]),
                              jnp.concatenate([_block_diag(vb[d][:, vs_of(p)]), ones_bd], axis=1)) for d, p in pairs}
        outs = {}
        for d, p in pairs:
            k = d * npair + p
            vs = vs_of(p)
            num = intra[d, p][:, :2 * DV_B] + carried[d, p][:, :2 * DV_B]
            den = intra[d, p][:, 2 * DV_B:] + carried[d, p][:, 2 * DV_B:]
            outs[d, p] = num / jnp.maximum(jnp.abs(den), per_val[d][0:c, vs])
            decay = per_val[d][c:c + 1, vs]
            dec = jnp.concatenate([jnp.broadcast_to(jnp.concatenate([decay[:, q * DV_B:(q + 1) * DV_B]] * 2, axis=1),
                                                    (DK_B, 2 * DV_B)) for q in (0, 1)], axis=0)
            c_bd[k] = dec * c_bd[k] + jnp.where(bd_key, upd[d, p][:, :2 * DV_B], 0.0)
            n_bd[k] = dec * n_bd[k] + jnp.where(bd_key, upd[d, p][:, 2 * DV_B:], 0.0)
        o_mls = [jnp.concatenate([outs[d, p] for p in range(npair)], axis=-1) for d in dirs]

        for d in (0, 1):
            oa_scr[d, rows[d], :] = o_gla[d]
            ob_scr[d, rows[d], :] = o_mls[d]
        for k in range(2 * npair):
            sbd_scr[k] = s_bd[k]
            cbd_scr[k] = c_bd[k]
            nbd_scr[k] = n_bd[k]
        mgl_scr[...] = m_gl
        return carry

    lax.fori_loop(0, nchunks, chunk_step, 0)

    for d in (0, 1):
        for p in range(npair):
            k = d * npair + p
            s_t, c_f, n_f = sbd_scr[k], cbd_scr[k], nbd_scr[k]
            for q in (0, 1):
                h = 2 * p + q
                og_ref[d, h] = s_t[q * DV_A:(q + 1) * DV_A, q * DK_A:(q + 1) * DK_A]
                oc_ref[d, h] = c_f[q * DK_B:(q + 1) * DK_B, q * DV_B:(q + 1) * DV_B]
                n_t = n_f[q * DK_B:(q + 1) * DK_B, q * DV_B:(q + 1) * DV_B].T
                on_ref[d * H_B + h:d * H_B + h + 1, :] = n_t[0:1, :]
    m_gl = mgl_scr[...]
    om_ref[...] = jnp.concatenate([m_gl[:, gate_lane(d, h):gate_lane(d, h) + 1]
                                   for d in (0, 1) for h in range(H_B)], axis=1)

    mod = _mod_slices(mod_ref[...])
    g1, sh2, sc2 = mod[2], mod[3], mod[4]

    def out_step(j, carry):
        hm = TM // 2
        halves = (0, 1)
        r0 = [pl.multiple_of(j * TM + q * hm, hm) for q in halves]
        rows = [pl.ds(r0[q], hm) for q in halves]
        parts = [[], []]
        for h in range(H_A):
            vs = slice(h * DV_A, (h + 1) * DV_A)
            for q in halves:
                ra = main_ref[rows[q], _RA + h * DV_A:_RA + (h + 1) * DV_A].astype(F32)
                o = oa_scr[0, rows[q], vs] + oa_scr[1, rows[q], vs]
                parts[q].append(_rms(o, gnw_ref[...]) * (ra * jax.nn.sigmoid(ra)))
        for h in range(H_B):
            vs = slice(h * DV_B, (h + 1) * DV_B)
            for q in halves:
                ob = main_ref[rows[q], _OB + h * DV_B:_OB + (h + 1) * DV_B].astype(F32)
                o = ob_scr[0, rows[q], vs] + ob_scr[1, rows[q], vs]
                parts[q].append(_rms(o, mnw_ref[...]) * jax.nn.sigmoid(ob))
        mix = [_dot(_bf(jnp.concatenate(parts[q], axis=-1)), wout_ref[...]) for q in halves]
        x1 = [x_ref[rows[q], :] + g1 * mix[q] for q in halves]
        h2 = [_rms(x1[q], n2w_ref[...]) * (1.0 + sc2) + sh2 for q in halves]
        lg = [_router_logits(h2[q], wrt_ref, brt_ref) for q in halves]
        for q in halves:
            x1_ref[rows[q], :] = x1[q]
            _store_rows_per_vreg(h2_ref.at[pl.ds(pl.multiple_of(r0[q] * _RPV, hm * _RPV), hm * _RPV)], h2[q])
        lg_ref[j] = jnp.concatenate(lg, axis=1)
        return carry

    lax.fori_loop(0, seq // TM, out_step, 0)


def _scan(st, ctx, main, g, g2, mt, x, mods, states, consts, prev):
    seq = st.lp if ctx else st.ls
    nb = st.bp if ctx else st.bs
    off = 0 if ctx else st.tp // st.ls
    sg, sc, sn, sm = states
    full = lambda shape: pl.BlockSpec(shape, lambda b: (0,) * len(shape))
    in_specs = [
        pl.BlockSpec((seq, _REC_MAIN), lambda b: (off + b, 0)),
        pl.BlockSpec((seq, LANES), lambda b: (off + b, 0)),
        pl.BlockSpec((seq, LANES), lambda b: (off + b, 0)),
        pl.BlockSpec((seq // CHUNK, 16, 2 * CHUNK), lambda b: (off + b, 0, 0)),
        pl.BlockSpec((seq, D_MODEL), lambda b: (off + b, 0)),
    ]
    args = [main, g, g2, mt, x]
    mod_row = (lambda b: (0, 0, 0)) if ctx else (lambda b: (1 + b, 0, 0))
    in_specs += [
        pl.BlockSpec((None, 1, N_MOD * D_MODEL), mod_row),
        pl.BlockSpec((None, 2, H_A, DV_A, DK_A), lambda b: (b, 0, 0, 0, 0)),
        pl.BlockSpec((None, 2, H_B, DK_B, DV_B), lambda b: (b, 0, 0, 0, 0)),
        pl.BlockSpec((None, 2 * H_B, DK_B), lambda b: (b, 0, 0)),
        pl.BlockSpec((None, 1, 2 * H_B), lambda b: (b, 0, 0)),
    ]
    args += [mods, sg, sc, sn, sm]
    for a in consts:
        in_specs.append(full(a.shape))
        args.append(a)
    aliases = {}
    if prev is not None:
        aliases = {len(args) + k: k for k in range(3)}
        in_specs += [pl.BlockSpec(memory_space=pl.ANY)] * 3
        args += list(prev)
    out_shape = (jax.ShapeDtypeStruct((st.t, D_MODEL), F32),
                 jax.ShapeDtypeStruct((st.t * _RPV, LANES), F32),
                 jax.ShapeDtypeStruct((st.t // TM, LG_ROWS, TM), F32),
                 jax.ShapeDtypeStruct(sg.shape, F32), jax.ShapeDtypeStruct(sc.shape, F32),
                 jax.ShapeDtypeStruct(sn.shape, F32), jax.ShapeDtypeStruct(sm.shape, F32))
    out_specs = (pl.BlockSpec((seq, D_MODEL), lambda b: (off + b, 0)),
                 pl.BlockSpec((seq * _RPV, LANES), lambda b: (off + b, 0)),
                 pl.BlockSpec((seq // TM, LG_ROWS, TM), lambda b: (off + b, 0, 0)),
                 pl.BlockSpec((None, 2, H_A, DV_A, DK_A), lambda b: (b, 0, 0, 0, 0)),
                 pl.BlockSpec((None, 2, H_B, DK_B, DV_B), lambda b: (b, 0, 0, 0, 0)),
                 pl.BlockSpec((None, 2 * H_B, DK_B), lambda b: (b, 0, 0)),
                 pl.BlockSpec((None, 1, 2 * H_B), lambda b: (b, 0, 0)))
    return pl.pallas_call(
        functools.partial(_scan_kernel, seq=seq),
        out_shape=out_shape,
        grid=(nb,),
        in_specs=in_specs,
        out_specs=out_specs,
        scratch_shapes=[pltpu.VMEM((2, seq, H_A * DV_A), F32), pltpu.VMEM((2, seq, H_B * DV_B), F32),
                        pltpu.VMEM((H_A, 2 * DV_A, 2 * DK_A), F32), pltpu.VMEM((H_B, 2 * DK_B, 2 * DV_B), F32),
                        pltpu.VMEM((H_B, 2 * DK_B, 2 * DV_B), F32), pltpu.VMEM((1, LANES), F32)],
        input_output_aliases=aliases,
        compiler_params=_cparams("arbitrary"),
        name="rec_scan_ctx" if ctx else "rec_scan_lat",
    )(*args)


def _first_max(rows):
    m = rows[0]
    for r in rows[1:]:
        m = jnp.maximum(m, r)
    idx = jnp.full(m.shape, len(rows) - 1, jnp.int32)
    for k in range(len(rows) - 2, -1, -1):
        idx = jnp.where(rows[k] == m, k, idx)
    return m, idx


def _route_kernel(lg_ref, pos1_ref, pos2_ref, w1_ref, w2_ref, tab_ref, *, sb):
    lg = jnp.concatenate([lg_ref[b] for b in range(sb // TM)], axis=1)
    rows = [lg[k:k + 1, :] for k in range(N_EXPERTS + N_GROUPS)]
    grp = rows[N_EXPERTS:]
    gmax, gidx = _first_max(grp)
    p_group = 1.0 / sum(jnp.exp(r - gmax) for r in grp)
    e_in = []
    for k in range(EXP_PER_GROUP):
        v = rows[(N_GROUPS - 1) * EXP_PER_GROUP + k]
        for g in range(N_GROUPS - 2, -1, -1):
            v = jnp.where(gidx == g, rows[g * EXP_PER_GROUP + k], v)
        e_in.append(v)
    v1, i1 = _first_max(e_in)
    v2, i2 = _first_max([jnp.where(i1 == k, -jnp.inf, e_in[k]) for k in range(EXP_PER_GROUP)])
    ex = jnp.exp(v2 - v1)
    w1_ref[...] = p_group / (1.0 + ex)
    w2_ref[...] = p_group * ex / (1.0 + ex)
    x1 = gidx * EXP_PER_GROUP + i1
    x2 = gidx * EXP_PER_GROUP + i2

    eid = lax.broadcasted_iota(jnp.int32, (N_EXPERTS, sb), 0)
    sel = jnp.where((eid == x1) | (eid == x2), 1.0, 0.0)
    r_i = lax.broadcasted_iota(jnp.int32, (TM, TM), 0)
    c_i = lax.broadcasted_iota(jnp.int32, (TM, TM), 1)
    before = jnp.where(r_i < c_i, 1.0, 0.0).astype(BF16)
    carry = jnp.zeros((N_EXPERTS, 1), F32)
    ranks = []
    for b in range(sb // TM):
        s_b = sel[:, b * TM:(b + 1) * TM]
        ranks.append(_dot(_bf(s_b), before) + carry)
        carry = carry + jnp.sum(s_b, axis=1, keepdims=True)
    rank = jnp.concatenate(ranks, axis=1)
    shift = ROW_PAD.bit_length() - 1
    npad = jnp.left_shift(jnp.right_shift(carry.astype(jnp.int32) + (ROW_PAD - 1), shift), shift)

    lane = lax.broadcasted_iota(jnp.int32, (1, LANES), 1)
    tab = jnp.zeros((1, LANES), jnp.int32)
    pos1 = jnp.zeros((1, sb), F32)
    pos2 = jnp.zeros((1, sb), F32)
    off = jnp.zeros((1, 1), jnp.int32)
    for e in range(N_EXPERTS):
        n_e = npad[e:e + 1, :]
        tab = jnp.where(lane == e, off, tab)
        tab = jnp.where(lane == N_EXPERTS + e, n_e, tab)
        row = off.astype(F32) + rank[e:e + 1, :]
        pos1 = jnp.where(x1 == e, row, pos1)
        pos2 = jnp.where(x2 == e, row, pos2)
        off = off + n_e
    pos1_ref[...] = pos1.astype(jnp.int32) * _RPV
    pos2_ref[...] = pos2.astype(jnp.int32) * _RPV
    tab_ref[...] = tab


def _route(st, lg):
    sb = st.sb
    nsb = st.t // sb
    row_i = jax.ShapeDtypeStruct((nsb, 1, sb), jnp.int32)
    row_f = jax.ShapeDtypeStruct((nsb, 1, sb), F32)
    rspec = pl.BlockSpec((None, 1, sb), lambda s: (s, 0, 0))
    return pl.pallas_call(
        functools.partial(_route_kernel, sb=sb),
        out_shape=(row_i, row_i, row_f, row_f, jax.ShapeDtypeStruct((nsb, 1, LANES), jnp.int32)),
        grid=(nsb,),
        in_specs=[pl.BlockSpec((sb // TM, LG_ROWS, TM), lambda s: (s, 0, 0))],
        out_specs=(rspec, rspec, rspec, rspec, pl.BlockSpec((None, 1, LANES), lambda s: (s, 0, 0))),
        compiler_params=_cparams("arbitrary"),
        name="moe_route",
    )(lg)


def _moe_rows(sb):
    return 2 * sb + N_EXPERTS * ROW_PAD + FFN_TILE


def _moe_kernel(pos1_ref, pos2_ref, w1_ref, w2_ref, tab_ref, xr_ref, wg_ref, wu_ref, wd_ref, o_ref,
                rows_scr, stage_scr, *, sb):
    s = pl.program_id(0)
    e = pl.program_id(1)

    def tile_at(ref, r8):
        return ref.at[pl.ds(pl.multiple_of(r8, _RPV), _RPV)]

    @pl.when((s == 0) & (e == 0))
    def _():
        rows_scr[...] = jnp.zeros_like(rows_scr)

    @pl.when(e == 0)
    def _():
        def dispatch(t, carry):
            v = tile_at(xr_ref, t * _RPV)[...]
            tile_at(rows_scr, pos1_ref[0, t])[...] = v
            tile_at(rows_scr, pos2_ref[0, t])[...] = v
            return carry

        lax.fori_loop(0, sb, dispatch, 0, unroll=8)

    def expert(k):
        ex = e * EXPERTS_PER_STEP + k

        def ffn_tile(r0, m, valid=None):
            win = rows_scr.at[pl.ds(pl.multiple_of(r0 * _RPV, ROW_PAD * _RPV), m * _RPV)]
            x = _load_rows_per_vreg(win, m)
            xb = _bf(x)
            hg = _dot(xb, wg_ref[k])
            hu = _dot(xb, wu_ref[k])
            y = _dot(_bf(hg * jax.nn.sigmoid(hg) * hu), wd_ref[k])
            if valid is not None:
                y = jnp.where(lax.broadcasted_iota(jnp.int32, (m, 1), 0) < valid, y, x)
            _store_rows_per_vreg(win, y)

        off = tab_ref[0, ex]
        npad = tab_ref[0, N_EXPERTS + ex]
        nfull = npad // FFN_TILE

        def full_tile(i, carry):
            ffn_tile(off + i * FFN_TILE, FFN_TILE)
            return carry

        lax.fori_loop(0, nfull, full_tile, 0)
        rem = npad - nfull * FFN_TILE
        last = off + nfull * FFN_TILE

        @pl.when((rem > 0) & (rem <= FFN_TILE // 2))
        def _():
            ffn_tile(last, FFN_TILE // 2, valid=rem)

        @pl.when(rem > FFN_TILE // 2)
        def _():
            ffn_tile(last, FFN_TILE, valid=rem)

    for k in range(EXPERTS_PER_STEP):
        expert(k)

    @pl.when(e == N_EXPERTS // EXPERTS_PER_STEP - 1)
    def _():
        for c in range(sb // TM):
            def combine(t, carry, c=c):
                tt = c * TM + t
                y = (w1_ref[0, tt] * tile_at(rows_scr, pos1_ref[0, tt])[...]
                     + w2_ref[0, tt] * tile_at(rows_scr, pos2_ref[0, tt])[...])
                tile_at(stage_scr, t * _RPV)[...] = y
                return carry

            lax.fori_loop(0, TM, combine, 0, unroll=8)
            o_ref[c * TM:(c + 1) * TM, :] = _bf(_load_rows_per_vreg(stage_scr, TM))


def _cast_kernel(x_ref, o_ref):
    o_ref[...] = _bf(x_ref[...])


def _expert_weights_bf16(w):
    depth, ne, a, b = w.shape
    per_step = 4
    assert (depth * ne) % per_step == 0
    spec = pl.BlockSpec((per_step, a, b), lambda i: (i, 0, 0))
    return pl.pallas_call(
        _cast_kernel,
        out_shape=jax.ShapeDtypeStruct((depth * ne, a, b), BF16),
        grid=(depth * ne // per_step,),
        in_specs=[spec],
        out_specs=spec,
        compiler_params=_cparams("arbitrary"),
        name="expert_weight_cast",
    )(w.reshape(depth * ne, a, b))


def _moe(st, layer, xr, route, wg, wu, wd):
    sb = st.sb
    eps = EXPERTS_PER_STEP
    b0 = layer * N_EXPERTS // eps
    smem = lambda n: pl.BlockSpec((None, 1, n), lambda s, e: (s, 0, 0), memory_space=pltpu.SMEM)
    return pl.pallas_call(
        functools.partial(_moe_kernel, sb=sb),
        out_shape=jax.ShapeDtypeStruct((st.t, D_MODEL), BF16),
        grid=(st.t // sb, N_EXPERTS // eps),
        in_specs=[
            smem(sb), smem(sb), smem(sb), smem(sb), smem(LANES),
            pl.BlockSpec((sb * _RPV, LANES), lambda s, e: (s, 0)),
            pl.BlockSpec((eps, D_MODEL, D_EXPERT), lambda s, e: (b0 + e, 0, 0)),
            pl.BlockSpec((eps, D_MODEL, D_EXPERT), lambda s, e: (b0 + e, 0, 0)),
            pl.BlockSpec((eps, D_EXPERT, D_MODEL), lambda s, e: (b0 + e, 0, 0)),
        ],
        out_specs=pl.BlockSpec((sb, D_MODEL), lambda s, e: (s, 0)),
        scratch_shapes=[pltpu.VMEM((_moe_rows(sb) * _RPV, LANES), F32), pltpu.VMEM((TM * _RPV, LANES), F32)],
        compiler_params=_cparams("arbitrary", "arbitrary"),
        name="moe_ffn",
    )(*route, xr, wg, wu, wd)


def _hy_inproj_kernel(x_ref, m_ref, mod0_ref, mod_ref, nw_ref, w_ref, x2_ref, z_ref):
    g2 = _mod_slices(mod0_ref[...])[5]
    x2 = x_ref[...] + g2 * m_ref[...].astype(F32)
    x2_ref[...] = x2
    sh1, sc1 = _mod_slices(mod_ref[...])[:2]
    h = _rms(x2, nw_ref[...]) * (1.0 + sc1) + sh1
    z_ref[...] = _bf(_dot(_bf(h), w_ref[...]))


def _hy_inproj(st, x, moe, mods_prev, mods, nw, w):
    n = w.shape[1]
    tm = TM_TOK
    mspec = pl.BlockSpec((None, 1, N_MOD * D_MODEL), lambda i: (st.mod_row(i, tm), 0, 0))
    return pl.pallas_call(
        _hy_inproj_kernel,
        out_shape=(jax.ShapeDtypeStruct((st.t, D_MODEL), F32), jax.ShapeDtypeStruct((st.t, n), BF16)),
        grid=(st.t // tm,),
        in_specs=[
            pl.BlockSpec((tm, D_MODEL), lambda i: (i, 0)),
            pl.BlockSpec((tm, D_MODEL), lambda i: (i, 0)),
            mspec, mspec,
            pl.BlockSpec((1, D_MODEL), lambda i: (0, 0)),
            pl.BlockSpec((D_MODEL, n), lambda i: (0, 0)),
        ],
        out_specs=(pl.BlockSpec((tm, D_MODEL), lambda i: (i, 0)), pl.BlockSpec((tm, n), lambda i: (i, 0))),
        compiler_params=_cparams("arbitrary"),
        name="hy_inproj",
    )(x, moe, mods_prev, mods, nw, w)


def _dft_tables(seq):
    n2 = 2 * seq
    assert n2 & (n2 - 1) == 0
    tr = min(seq, TM)

    def table_kernel(cos_ref, msin_ref, cosb_ref, msinb_ref):
        k = lax.broadcasted_iota(jnp.int32, (tr, seq), 0) + pl.program_id(0) * tr
        n = lax.broadcasted_iota(jnp.int32, (tr, seq), 1)
        ang = ((k * n) & (n2 - 1)).astype(F32) * (2.0 * math.pi / n2)
        c, s = jnp.cos(ang), -jnp.sin(ang)
        cos_ref[...] = c
        msin_ref[...] = s
        cosb_ref[...] = _bf(c)
        msinb_ref[...] = _bf(s)

    spec = pl.BlockSpec((tr, seq), lambda i: (i, 0))
    f32, b16 = jax.ShapeDtypeStruct((seq, seq), F32), jax.ShapeDtypeStruct((seq, seq), BF16)
    return pl.pallas_call(table_kernel, out_shape=(f32, f32, b16, b16), grid=(seq // tr,),
                          out_specs=(spec, spec, spec, spec), compiler_params=_cparams("arbitrary"),
                          name=f"dft_tables_{seq}")()


def _alternating(shape):
    return jnp.where(lax.broadcasted_iota(jnp.int32, shape, 0) % 2 == 0, 1.0, -1.0)


def _hy_filter_kernel(emb_ref, dec_ref, w1_ref, b1_ref, f1_ref, w2_ref, b2_ref, f2_ref, w3_ref,
                      cos_ref, msin_ref, kr_ref, ki_ref, kn_ref, *, seq):
    h = jnp.sin(f1_ref[...] * (_dot_hp(emb_ref[...], w1_ref[...]) + b1_ref[...]))
    h = jnp.sin(f2_ref[...] * (_dot_hp(h, w2_ref[...]) + b2_ref[...]))
    dec = dec_ref[...]
    row0 = lax.broadcasted_iota(jnp.int32, dec.shape, 0) == 0
    alt = _alternating(dec.shape)
    cos, msin = cos_ref[...], msin_ref[...]
    scale = jnp.where(row0, 1.0, 2.0) / (2 * seq)
    for o in range(HY_ORDER):
        h_f = _dot_hp(h, w3_ref[:, 2 * o, :]) * dec
        h_b = jnp.where(row0, 0.0, _dot_hp(h, w3_ref[:, 2 * o + 1, :]) * dec)
        kn_ref[o] = jnp.sum(alt * (h_f + h_b), axis=0, keepdims=True) / (2 * seq)
        kr_ref[o] = _dot_hp(cos, h_f + h_b) * scale
        ki_ref[o] = _dot_hp(msin, h_f - h_b) * scale


def _hy_filter(seq, w1, b1, f1, w2, b2, f2, w3, cos, msin):
    t = jnp.linspace(0.0, 1.0, seq, dtype=F32)[:, None]
    w = 2.0 * math.pi * jnp.arange(seq, dtype=F32)[:, None] / seq
    f = jnp.linspace(1e-4, HY_BANDS - 1, HY_BANDS, dtype=F32)[None, :]
    emb = jnp.concatenate([t, jnp.cos(f * w), -jnp.sin(f * w), jnp.zeros((seq, LANES - HY_EMB), F32)], axis=-1)
    decay = jnp.exp(-t * jnp.linspace(HY_MIN_DECAY, HY_MAX_DECAY, D_MODEL, dtype=F32)[None, :])
    w1p = jnp.concatenate([w1, jnp.zeros((LANES - HY_EMB, HY_FFN), F32)], axis=0)
    dblk = 256
    out = jax.ShapeDtypeStruct((HY_ORDER, seq, D_MODEL), F32)
    full = lambda shape: pl.BlockSpec(shape, lambda j: (0,) * len(shape))
    ospec = pl.BlockSpec((HY_ORDER, seq, dblk), lambda j: (0, 0, j))
    return pl.pallas_call(
        functools.partial(_hy_filter_kernel, seq=seq),
        out_shape=(out, out, jax.ShapeDtypeStruct((HY_ORDER, 1, D_MODEL), F32)),
        grid=(D_MODEL // dblk,),
        in_specs=[
            full((seq, LANES)),
            pl.BlockSpec((seq, dblk), lambda j: (0, j)),
            full((LANES, HY_FFN)), full((1, HY_FFN)), full((1, HY_FFN)),
            full((HY_FFN, HY_FFN)), full((1, HY_FFN)), full((1, HY_FFN)),
            pl.BlockSpec((HY_FFN, 2 * HY_ORDER, dblk), lambda j: (0, 0, j)),
            full((seq, seq)), full((seq, seq)),
        ],
        out_specs=(ospec, ospec, pl.BlockSpec((HY_ORDER, 1, dblk), lambda j: (0, 0, j))),
        compiler_params=_cparams("arbitrary"),
        name=f"hy_filter_{seq}",
    )(emb, decay, w1p, b1[None, :], f1[None, :], w2, b2[None, :], f2[None, :],
      w3.reshape(HY_FFN, 2 * HY_ORDER, D_MODEL), cos, msin)


def _hy_conv_kernel(*refs, seq, nseq):
    (zv_ref, z1_ref, z2_ref, cv_ref, c1_ref, c2_ref, kr_ref, ki_ref, kn_ref, bias_ref,
     cos_ref, msin_ref) = refs[:12]
    y_ref = refs[-1]
    dblk = y_ref.shape[1]
    t = lax.broadcasted_iota(jnp.int32, (seq, dblk), 0)
    first, last = t == 0, t == seq - 1
    alt = _alternating((seq, dblk))
    seqs = range(nseq)
    rows = [slice(q * seq, (q + 1) * seq) for q in seqs]

    def short_conv(z_ref, c_ref):
        out = []
        for q in seqs:
            z = z_ref[rows[q], :].astype(F32)
            prev = jnp.where(first, 0.0, pltpu.roll(z, 1, 0))
            nxt = jnp.where(last, 0.0, pltpu.roll(z, seq - 1, 0))
            out.append(c_ref[0:1, :] * prev + c_ref[1:2, :] * z + c_ref[2:3, :] * nxt)
        return out

    def long_conv(s, o):
        sb = [_bf(s[q]) for q in seqs]
        x_re = [_dot(cos_ref[...], sb[q]) for q in seqs]
        x_im = [_dot(msin_ref[...], sb[q]) for q in seqs]
        y_nyq = [jnp.sum(alt * s[q], axis=0, keepdims=True) * kn_ref[o] for q in seqs]
        k_re, k_im = kr_ref[o], ki_ref[o]
        y_re = [x_re[q] * k_re - x_im[q] * k_im for q in seqs]
        y_im = [x_re[q] * k_im + x_im[q] * k_re for q in seqs]
        y = [_dot(cos_ref[...], _bf(y_re[q])) + _dot(msin_ref[...], _bf(y_im[q])) for q in seqs]
        return [y[q] + alt * y_nyq[q] + bias_ref[o] * s[q] for q in seqs]

    v = short_conv(zv_ref, cv_ref)
    x1 = short_conv(z1_ref, c1_ref)
    x2 = short_conv(z2_ref, c2_ref)
    c1 = long_conv(v, 0)
    c2 = long_conv([x1[q] * c1[q] for q in seqs], 1)
    for q in seqs:
        y_ref[rows[q], :] = _bf(x2[q] * c2[q])


def _hy_conv(st, ctx, z, conv_w, filt, bias, tables, prev):
    seq = st.lp if ctx else st.ls
    nb = st.bp if ctx else st.bs
    nseq = max(1, min(nb, 4))
    assert nb % nseq == 0
    off = 0 if ctx else st.tp // (nseq * seq)
    dblk = 256
    nd = D_MODEL // dblk
    kr, ki, kn = filt
    cos, msin = tables
    vspec = pl.BlockSpec((HY_ORDER, 1, dblk), lambda j, b: (0, 0, j))
    full = lambda shape: pl.BlockSpec(shape, lambda j, b: (0,) * len(shape))
    kspec = pl.BlockSpec((HY_ORDER, seq, dblk), lambda j, b: (0, 0, j))
    in_specs = [
        pl.BlockSpec((nseq * seq, dblk), lambda j, b: (off + b, j)),
        pl.BlockSpec((nseq * seq, dblk), lambda j, b: (off + b, nd + j)),
        pl.BlockSpec((nseq * seq, dblk), lambda j, b: (off + b, 2 * nd + j)),
        pl.BlockSpec((3, dblk), lambda j, b: (0, j)),
        pl.BlockSpec((3, dblk), lambda j, b: (0, nd + j)),
        pl.BlockSpec((3, dblk), lambda j, b: (0, 2 * nd + j)),
        kspec, kspec, vspec, vspec,
        full((seq, seq)), full((seq, seq)),
    ]
    args = [z, z, z, conv_w, conv_w, conv_w, kr, ki, kn, bias.reshape(HY_ORDER, 1, D_MODEL), cos, msin]
    aliases = {}
    if prev is not None:
        aliases = {len(args): 0}
        in_specs.append(pl.BlockSpec(memory_space=pl.ANY))
        args.append(prev)
    return pl.pallas_call(
        functools.partial(_hy_conv_kernel, seq=seq, nseq=nseq),
        out_shape=jax.ShapeDtypeStruct((st.t, D_MODEL), BF16),
        grid=(nd, nb // nseq),
        in_specs=in_specs,
        out_specs=pl.BlockSpec((nseq * seq, dblk), lambda j, b: (off + b, j)),
        input_output_aliases=aliases,
        compiler_params=_cparams("arbitrary", "arbitrary"),
        name="hy_conv_ctx" if ctx else "hy_conv_lat",
    )(*args)


def _hy_outproj_kernel(y_ref, x_ref, mod_ref, w_ref, n2w_ref, wrt_ref, brt_ref, x3_ref, h2_ref, lg_ref):
    mod = _mod_slices(mod_ref[...])
    g1, sh2, sc2 = mod[2], mod[3], mod[4]
    nt = y_ref.shape[0] // TM
    tiles = range(nt)
    rows = [slice(b * TM, (b + 1) * TM) for b in tiles]
    mix = [_dot(y_ref[rows[b], :], w_ref[...]) for b in tiles]
    x3 = [x_ref[rows[b], :] + g1 * mix[b] for b in tiles]
    h2 = [_rms(x3[b], n2w_ref[...]) * (1.0 + sc2) + sh2 for b in tiles]
    lg = [_router_logits(h2[b], wrt_ref, brt_ref) for b in tiles]
    for b in tiles:
        x3_ref[rows[b], :] = x3[b]
        _store_rows_per_vreg(h2_ref.at[pl.ds(b * TM * _RPV, TM * _RPV)], h2[b])
        lg_ref[b] = lg[b]


def _hy_outproj(st, y, x, mods, w, n2w, wrt, brt):
    tm = TM_TOK
    return pl.pallas_call(
        _hy_outproj_kernel,
        out_shape=(jax.ShapeDtypeStruct((st.t, D_MODEL), F32),
                   jax.ShapeDtypeStruct((st.t * _RPV, LANES), F32),
                   jax.ShapeDtypeStruct((st.t // TM, LG_ROWS, TM), F32)),
        grid=(st.t // tm,),
        in_specs=[
            pl.BlockSpec((tm, D_MODEL), lambda i: (i, 0)),
            pl.BlockSpec((tm, D_MODEL), lambda i: (i, 0)),
            pl.BlockSpec((None, 1, N_MOD * D_MODEL), lambda i: (st.mod_row(i, tm), 0, 0)),
            pl.BlockSpec((D_MODEL, D_MODEL), lambda i: (0, 0)),
            pl.BlockSpec((1, D_MODEL), lambda i: (0, 0)),
            pl.BlockSpec((LG_ROWS, D_MODEL), lambda i: (0, 0)),
            pl.BlockSpec((LG_ROWS, 1), lambda i: (0, 0)),
        ],
        out_specs=(pl.BlockSpec((tm, D_MODEL), lambda i: (i, 0)),
                   pl.BlockSpec((tm * _RPV, LANES), lambda i: (i, 0)),
                   pl.BlockSpec((tm // TM, LG_ROWS, TM), lambda i: (i, 0, 0))),
        compiler_params=_cparams("arbitrary"),
        name="hy_outproj",
    )(y, x, mods, w, n2w, wrt, brt)


def _final_kernel(x_ref, m_ref, mod_ref, nf_ref, yp_ref, ys_ref, *, ncb):
    i = pl.program_id(0)
    g2 = _mod_slices(mod_ref[...])[5]
    y = _rms(x_ref[...] + g2 * m_ref[...].astype(F32), nf_ref[...])

    @pl.when(i < ncb)
    def _():
        yp_ref[...] = y

    @pl.when(i >= ncb)
    def _():
        ys_ref[...] = y


def _final(st, x, moe, mods, nf):
    tm = TM_TOK
    ncb = st.tp // tm
    return pl.pallas_call(
        functools.partial(_final_kernel, ncb=ncb),
        out_shape=(jax.ShapeDtypeStruct((st.tp, D_MODEL), F32), jax.ShapeDtypeStruct((st.ts, D_MODEL), F32)),
        grid=(st.t // tm,),
        in_specs=[
            pl.BlockSpec((tm, D_MODEL), lambda i: (i, 0)),
            pl.BlockSpec((tm, D_MODEL), lambda i: (i, 0)),
            pl.BlockSpec((None, 1, N_MOD * D_MODEL), lambda i: (st.mod_row(i, tm), 0, 0)),
            pl.BlockSpec((1, D_MODEL), lambda i: (0, 0)),
        ],
        out_specs=(pl.BlockSpec((tm, D_MODEL), lambda i: (jnp.minimum(i, ncb - 1), 0)),
                   pl.BlockSpec((tm, D_MODEL), lambda i: (jnp.maximum(i - ncb, 0), 0))),
        compiler_params=_cparams("arbitrary"),
        name="final_norm",
    )(x, moe, mods, nf)


def _grid_pos_table(seq):
    rows = seq // GRID_W
    r, cl = jnp.meshgrid(jnp.arange(rows, dtype=F32), jnp.arange(GRID_W, dtype=F32), indexing='ij')
    quarter = D_MODEL // 4
    omega = POS_THETA ** (-jnp.arange(quarter, dtype=F32) / quarter)

    def enc(pos):
        a = pos.reshape(-1, 1) * omega[None, :]
        return jnp.concatenate([jnp.sin(a), jnp.cos(a)], axis=-1)

    return jnp.concatenate([enc(r), enc(cl)], axis=-1)


def _router_weights(w_group, b_group, w_router, b_router):
    pad = LG_ROWS - N_EXPERTS - N_GROUPS
    w = jnp.concatenate([w_router.T, w_group.T, jnp.zeros((pad, D_MODEL), F32)], axis=0)
    b = jnp.concatenate([b_router, b_group, jnp.zeros((pad,), F32)])[:, None]
    return w, b


def kernel(x_prompt, x_sample, state_gla, state_mlstm_c, state_mlstm_n, state_mlstm_m, c, c_ctx, norm1_w, norm2_w, norm_f_w, ada_w, ada_b, rec_w_in, gla_gk_w, gla_gk_b, mlstm_gate_b, gla_norm_w, mlstm_norm_w, rec_w_out, hy_w_in, hy_conv_w, hy_f_w1, hy_f_b1, hy_f_freq1, hy_f_w2, hy_f_b2, hy_f_freq2, hy_f_w3, hy_f_bias, hy_w_out, moe_w_group, moe_b_group, moe_w_router, moe_b_router, moe_w_gate, moe_w_up, moe_w_down):
    bp, lp, _ = x_prompt.shape
    bs, ls, _ = x_sample.shape
    st = _Streams(bp, lp, bs, ls)
    xp = x_prompt.reshape(st.tp, D_MODEL)
    xs = x_sample.reshape(st.ts, D_MODEL)

    nrow = -(-(1 + bs) // 8) * 8
    cv = jnp.concatenate([c_ctx[None, :], c, jnp.zeros((nrow - 1 - bs, D_MODEL), F32)], axis=0)
    mods = _ada(cv, ada_w, ada_b)
    mods0 = mods[0].reshape(nrow, 1, N_MOD * D_MODEL)
    mods1 = mods[1].reshape(nrow, 1, N_MOD * D_MODEL)

    pos = _grid_pos_table(ls)

    w_in = rec_w_in[0]
    w_main = _bf(w_in[:, :_REC_MAIN])
    w_mg = w_in[:, _REC_MAIN + _MG0:].reshape(D_MODEL, 2, 2, H_B)
    w_fg = jnp.pad(w_mg[:, :, 1, :], ((0, 0), (0, 0), (0, 8 - H_B))).reshape(D_MODEL, 16)
    w_gate = jnp.concatenate([w_in[:, _REC_MAIN:], jnp.zeros((D_MODEL, LANES - _REC_GATE), F32),
                              jnp.zeros((D_MODEL, _MG0), F32), w_fg,
                              jnp.zeros((D_MODEL, LANES - _REC_GATE), F32)], axis=1)
    w_gate_t = w_in[:, _REC_MAIN + _MG0:].T
    x0, main, g, g2, mt = _inproj(st, xp, xs, pos, mods0, norm1_w[0][None, :], w_main, w_gate, w_gate_t)

    gkw = jnp.zeros((2, LANES, H_A * DK_A), F32)
    gkw = gkw.at[0, :GK_RANK].set(gla_gk_w[0, 0]).at[1, GK_RANK:2 * GK_RANK].set(gla_gk_w[0, 1])
    gkb = gla_gk_b[0][:, None, :]
    gb = mlstm_gate_b[0]
    gate_row = lambda b: jnp.pad(jnp.pad(b, ((0, 0), (0, 8 - H_B))).reshape(1, 16), ((0, 0), (_MG0, LANES - _REC_GATE)))
    gbcol = gb.reshape(16, 1)
    wr0, br0 = _router_weights(moe_w_group[0], moe_b_group[0], moe_w_router[0], moe_b_router[0])
    consts = [gkw, gkb, gate_row(gb[:, 0]), gate_row(gb[:, 1]), gbcol, gla_norm_w[0][None, :],
              mlstm_norm_w[0][None, :], _bf(rec_w_out[0]), norm2_w[0][None, :], wr0, br0]

    def scan_states(sg, sc, sn, sm):
        nb = sg.shape[0]
        return (jnp.swapaxes(sg, -1, -2), sc, sn.reshape(nb, 2 * H_B, DK_B), sm.reshape(nb, 1, 2 * H_B))

    zero = (jnp.zeros((bp, 2, H_A, DK_A, DV_A), F32), jnp.zeros((bp, 2, H_B, DK_B, DV_B), F32),
            jnp.zeros((bp, 2, H_B, DK_B), F32), jnp.zeros((bp, 2, H_B), F32))
    cached = (state_gla[:, 0], state_mlstm_c[:, 0], state_mlstm_n[:, 0], state_mlstm_m[:, 0])
    x1, h2, lg, fg, fc, fn, fm = _scan(st, True, main, g, g2, mt, x0, mods0, scan_states(*zero), consts, None)
    x1, h2, lg = _scan(st, False, main, g, g2, mt, x0, mods0, scan_states(*cached), consts, (x1, h2, lg))[:3]

    new_gla = jnp.swapaxes(fg, -1, -2)[:, None]
    new_c = fc[:, None]
    new_n = fn.reshape(bp, 1, 2, H_B, DK_B)
    new_m = fm.reshape(bp, 1, 2, H_B)

    wg, wu, wd = (_expert_weights_bf16(w) for w in (moe_w_gate, moe_w_up, moe_w_down))
    moe0 = _moe(st, 0, h2, _route(st, lg), wg, wu, wd)

    x2, z = _hy_inproj(st, x1, moe0, mods0, mods1, norm1_w[1][None, :], _bf(hy_w_in[0]))
    y = None
    for ctx in (True, False):
        seq = lp if ctx else ls
        cos, msin, cos_b, msin_b = _dft_tables(seq)
        filt = _hy_filter(seq, hy_f_w1[0], hy_f_b1[0], hy_f_freq1[0], hy_f_w2[0], hy_f_b2[0], hy_f_freq2[0],
                          hy_f_w3[0], cos, msin)
        y = _hy_conv(st, ctx, z, hy_conv_w[0], filt, hy_f_bias[0], (cos_b, msin_b), y)
    wr1, br1 = _router_weights(moe_w_group[1], moe_b_group[1], moe_w_router[1], moe_b_router[1])
    x3, h4, lg1 = _hy_outproj(st, y, x2, mods1, _bf(hy_w_out[0]), norm2_w[1][None, :], wr1, br1)

    moe1 = _moe(st, 1, h4, _route(st, lg1), wg, wu, wd)
    y_prompt, y_sample = _final(st, x3, moe1, mods1, norm_f_w[None, :])
    return (y_prompt.reshape(bp, lp, D_MODEL), y_sample.reshape(bs, ls, D_MODEL), new_gla, new_c, new_n, new_m)
```

```python
import functools
import math

import jax
import jax.numpy as jnp
from jax import lax
from jax.experimental import pallas as pl
from jax.experimental.pallas import tpu as pltpu

F32 = jnp.float32
BF16 = jnp.bfloat16

D_MODEL = 1024
GRID_W = 64
H_A = 4
DK_A = D_MODEL // 16
DV_A = D_MODEL // 8
GK_RANK = 16
GATE_TEMP = 16.0
H_B = 4
DK_B = D_MODEL // 16
DV_B = D_MODEL // 8
CHUNK = 64
HY_ORDER = 2
HY_EMB = 33
HY_BANDS = (HY_EMB - 1) // 2
HY_FFN = 64
HY_TARGET = 1e-2
HY_MAX_DECAY = abs(math.log(HY_TARGET)) / 0.3
HY_MIN_DECAY = abs(math.log(HY_TARGET)) / 1.5
N_GROUPS = 4
EXP_PER_GROUP = 4
N_EXPERTS = N_GROUPS * EXP_PER_GROUP
D_EXPERT = D_MODEL // 2
N_MOD = 6
POS_THETA = 10000.0
EPS = 1e-6

_QA, _KA, _VA, _RA = 0, 256, 512, 1024
_QB, _KB, _VB, _OB = 1536, 1792, 2048, 2560
_REC_MAIN = 3072
_REC_GATE = 48
_MG0 = 2 * GK_RANK

LANES = 128
SUBLANES = 8
_RPV = D_MODEL // LANES
assert _RPV == SUBLANES
TM = 256
TM_TOK = 1024
TM_IN = 512
SB_MOE = 2048
ROW_PAD = 16
EXPERTS_PER_STEP = 2
FFN_TILE = 288
LG_ROWS = 32
VMEM_LIMIT = 58 * 1024 * 1024


def _cparams(*sem):
    return pltpu.CompilerParams(dimension_semantics=sem, vmem_limit_bytes=VMEM_LIMIT)


def _bf(x):
    return x.astype(BF16)


def _dot(a, b):
    return jnp.dot(a, b, preferred_element_type=F32)


def _dot_nt(a, b):
    return lax.dot_general(a, b, (((1,), (1,)), ((), ())), preferred_element_type=F32)


def _dot_tn(a, b):
    return lax.dot_general(a, b, (((0,), (0,)), ((), ())), preferred_element_type=F32)


def _split2(x):
    hi = _bf(x)
    return hi, _bf(x - hi.astype(F32))


def _split3(x):
    hi = _bf(x)
    r = x - hi.astype(F32)
    mid = _bf(r)
    return hi, mid, _bf(r - mid.astype(F32))


def _dot_hp(a, b, dot=_dot):
    ah, al = _split2(a)
    bh, bl = _split2(b)
    return dot(ah, bh) + (dot(ah, bl) + dot(al, bh))


def _dot_mask_l(m, x, terms=3):
    if terms == 2:
        x1, x2 = _split2(x)
        return _dot(m, x1) + _dot(m, x2)
    x1, x2, x3 = _split3(x)
    return _dot(m, x1) + (_dot(m, x2) + _dot(m, x3))


def _dot_mask_r(x, m, terms=3):
    if terms == 2:
        x1, x2 = _split2(x)
        return _dot(x1, m) + _dot(x2, m)
    x1, x2, x3 = _split3(x)
    return _dot(x1, m) + (_dot(x2, m) + _dot(x3, m))


def _rms(x, w):
    return x * lax.rsqrt(jnp.mean(x * x, axis=-1, keepdims=True) + EPS) * w


def _mod_slices(mod):
    return [mod[:, k * D_MODEL:(k + 1) * D_MODEL] for k in range(N_MOD)]


def _store_rows_per_vreg(dst, h):
    rows = h.shape[0]
    for j in range(_RPV):
        dst[pl.ds(j, rows, stride=_RPV), :] = h[:, j * LANES:(j + 1) * LANES]


def _load_rows_per_vreg(src, rows):
    return jnp.concatenate([src[pl.ds(j, rows, stride=_RPV), :] for j in range(_RPV)], axis=-1)


def _router_logits(h, wrt_ref, brt_ref):
    return _dot_hp(wrt_ref[...], h, dot=_dot_nt) + brt_ref[...]


def _ada_kernel(cv_ref, w_ref, b_ref, o_ref):
    a = cv_ref[...]
    a = a * jax.nn.sigmoid(a)
    o_ref[...] = _dot_hp(a, w_ref[...]) + b_ref[...]


def _ada(cv, ada_w, ada_b):
    depth, d, n = ada_w.shape
    rows = cv.shape[0]
    tn = 1536
    return pl.pallas_call(
        _ada_kernel,
        out_shape=jax.ShapeDtypeStruct((depth, rows, n), F32),
        grid=(depth, n // tn),
        in_specs=[
            pl.BlockSpec((rows, d), lambda l, j: (0, 0)),
            pl.BlockSpec((None, d, tn), lambda l, j: (l, 0, j)),
            pl.BlockSpec((None, 1, tn), lambda l, j: (l, 0, j)),
        ],
        out_specs=pl.BlockSpec((None, rows, tn), lambda l, j: (l, 0, j)),
        compiler_params=_cparams("arbitrary", "arbitrary"),
        name="ada_mod",
    )(cv, ada_w, ada_b.reshape(depth, 1, n))


class _Streams:
    def __init__(self, bp, lp, bs, ls):
        self.bp, self.lp, self.bs, self.ls = bp, lp, bs, ls
        self.tp, self.ts = bp * lp, bs * ls
        self.t = self.tp + self.ts
        assert lp % TM == 0 and ls % TM == 0 and self.tp % ls == 0
        assert self.tp % TM_TOK == 0 and ls % TM_TOK == 0
        self.sb = min(SB_MOE, math.gcd(self.tp, self.ts))
        assert self.t % self.sb == 0 and self.sb % TM == 0

    def mod_row(self, i, tm):
        ncb = self.tp // tm
        return jnp.where(i < ncb, 0, 1 + (i - ncb) // (self.ls // tm))


def _inproj_kernel(xp_ref, xs_ref, pos_ref, mod_ref, nw_ref, w_ref, wg_ref, wgt_ref,
                   x0_ref, main_ref, g_ref, g2_ref, mt_ref, *, ncb):
    i = pl.program_id(0)
    x = jnp.where(i < ncb, xp_ref[...], xs_ref[...] + pos_ref[...])
    x0_ref[...] = x
    sh1, sc1 = _mod_slices(mod_ref[...])[:2]
    h = _rms(x, nw_ref[...]) * (1.0 + sc1) + sh1
    main_ref[...] = _bf(_dot(_bf(h), w_ref[...]))
    gates = _dot_hp(h, wg_ref[...])
    g_ref[...] = gates[:, :LANES]
    g2_ref[...] = gates[:, LANES:]
    mt = _dot_hp(wgt_ref[...], h, dot=_dot_nt)
    for c in range(mt.shape[1] // CHUNK):
        piece = mt[:, c * CHUNK:(c + 1) * CHUNK]
        mt_ref[c] = jnp.concatenate([piece, piece], axis=1)


def _inproj(st, xp, xs, pos, mods, nw, w_main, w_gate, w_gate_t):
    tm = TM_IN
    assert st.tp % tm == 0 and st.ls % tm == 0
    ncb = st.tp // tm
    bps = st.ls // tm
    t = st.t
    return pl.pallas_call(
        functools.partial(_inproj_kernel, ncb=ncb),
        out_shape=(jax.ShapeDtypeStruct((t, D_MODEL), F32),
                   jax.ShapeDtypeStruct((t, _REC_MAIN), BF16),
                   jax.ShapeDtypeStruct((t, LANES), F32),
                   jax.ShapeDtypeStruct((t, LANES), F32),
                   jax.ShapeDtypeStruct((t // CHUNK, 16, 2 * CHUNK), F32)),
        grid=(t // tm,),
        in_specs=[
            pl.BlockSpec((tm, D_MODEL), lambda i: (jnp.minimum(i, ncb - 1), 0)),
            pl.BlockSpec((tm, D_MODEL), lambda i: (jnp.maximum(i - ncb, 0), 0)),
            pl.BlockSpec((tm, D_MODEL), lambda i: (jnp.maximum(i - ncb, 0) % bps, 0)),
            pl.BlockSpec((None, 1, N_MOD * D_MODEL), lambda i: (st.mod_row(i, tm), 0, 0)),
            pl.BlockSpec((1, D_MODEL), lambda i: (0, 0)),
            pl.BlockSpec((D_MODEL, _REC_MAIN), lambda i: (0, 0)),
            pl.BlockSpec((D_MODEL, 2 * LANES), lambda i: (0, 0)),
            pl.BlockSpec((16, D_MODEL), lambda i: (0, 0)),
        ],
        out_specs=(pl.BlockSpec((tm, D_MODEL), lambda i: (i, 0)),
                   pl.BlockSpec((tm, _REC_MAIN), lambda i: (i, 0)),
                   pl.BlockSpec((tm, LANES), lambda i: (i, 0)),
                   pl.BlockSpec((tm, LANES), lambda i: (i, 0)),
                   pl.BlockSpec((tm // CHUNK, 16, 2 * CHUNK), lambda i: (i, 0, 0))),
        compiler_params=_cparams("arbitrary"),
        name="rec_inproj",
    )(xp, xs, pos, mods, nw, w_main, w_gate, w_gate_t)


def _block_diag(x):
    left = lax.broadcasted_iota(jnp.int32, (1, x.shape[1]), 1) < x.shape[1] // 2
    zero = jnp.zeros_like(x)
    return jnp.concatenate([jnp.where(left, x, zero), jnp.where(left, zero, x)], axis=0)


def _block_diag_mask(rows, width):
    r = lax.broadcasted_iota(jnp.int32, (rows, width), 0) < rows // 2
    l = lax.broadcasted_iota(jnp.int32, (rows, width), 1) < width // 2
    return r == l


def _running_max(x, reverse):
    n = x.shape[0]
    row = lax.broadcasted_iota(jnp.int32, x.shape, 0)
    sh = 1
    while sh < n:
        if reverse:
            y = jnp.where(row < n - sh, pltpu.roll(x, n - sh, 0), -jnp.inf)
        else:
            y = jnp.where(row >= sh, pltpu.roll(x, sh, 0), -jnp.inf)
        x = jnp.maximum(x, y)
        sh *= 2
    return x


def _scan_kernel(*refs, seq):
    (main_ref, g_ref, g2_ref, mt_ref, x_ref, mod_ref, sg_ref, sc_ref, sn_ref, sm_ref,
     gkw_ref, gkb_ref, gbi_ref, gbf_ref, gbcol_ref, gnw_ref, mnw_ref, wout_ref, n2w_ref, wrt_ref, brt_ref) = refs[:21]
    (x1_ref, h2_ref, lg_ref, og_ref, oc_ref, on_ref, om_ref,
     oa_scr, ob_scr, sbd_scr, cbd_scr, nbd_scr, mgl_scr) = refs[-13:]
    c = CHUNK
    nchunks = seq // c
    npair = H_A // 2
    assert H_A == H_B and DK_A == DK_B == c and DV_A == DV_B == LANES and 2 * DK_A == LANES

    lane = lax.broadcasted_iota(jnp.int32, (1, LANES), 1)
    gate_lane = lambda d, h: _MG0 + d * 8 + h
    used = tuple((lane >= gate_lane(d, 0)) & (lane < gate_lane(d, H_B)) for d in (0, 1))

    for d in (0, 1):
        for p in range(npair):
            k = d * npair + p
            z_s = jnp.zeros((DV_A, DK_A), F32)
            sbd_scr[k] = jnp.concatenate([jnp.concatenate([sg_ref[d, 2 * p], z_s], axis=1),
                                          jnp.concatenate([z_s, sg_ref[d, 2 * p + 1]], axis=1)], axis=0)
            z_c = jnp.zeros((DK_B, DV_B), F32)
            cbd_scr[k] = jnp.concatenate([jnp.concatenate([sc_ref[d, 2 * p], z_c], axis=1),
                                          jnp.concatenate([z_c, sc_ref[d, 2 * p + 1]], axis=1)], axis=0)
            n_rep = [jnp.broadcast_to(sn_ref[d * H_B + 2 * p + q:d * H_B + 2 * p + q + 1, :], (DV_B, DK_B)).T
                     for q in (0, 1)]
            nbd_scr[k] = jnp.concatenate([jnp.concatenate([n_rep[0], z_c], axis=1),
                                          jnp.concatenate([z_c, n_rep[1]], axis=1)], axis=0)
    m_gl = jnp.zeros((1, LANES), F32)
    for d in (0, 1):
        for h in range(H_B):
            r = d * H_B + h
            m_gl = jnp.where(lane == gate_lane(d, h), sm_ref[:, r:r + 1], m_gl)
    mgl_scr[...] = m_gl

    row_p = lax.broadcasted_iota(jnp.int32, (c, LANES), 0)
    s_p = lax.broadcasted_iota(jnp.int32, (c, LANES), 1) % c
    live_p = (s_p <= row_p, s_p >= row_p)
    row = lax.broadcasted_iota(jnp.int32, (c, c), 0)
    col = lax.broadcasted_iota(jnp.int32, (c, c), 1)
    tri = tuple(jnp.where(m, 1.0, 0.0).astype(BF16) for m in (col <= row, col >= row))
    tri_t2 = tuple(jnp.where(m, 1.0, 0.0).astype(BF16) for m in (row_p <= s_p, row_p >= s_p))
    grow = lax.broadcasted_iota(jnp.int32, (16, 1), 0)
    f_row = (grow % 8) >= 4
    scale_q = DK_A ** -0.5
    bd_val = _block_diag_mask(2 * DV_A, 2 * DK_A)
    bd_key = _block_diag_mask(2 * DK_B, 2 * DV_B)
    ones_bd = jnp.where(bd_key, 1.0, 0.0).astype(BF16)
    ones_cv = jnp.ones((c, 2 * DV_B), BF16)

    def replicate(d, width):
        r = lax.broadcasted_iota(jnp.int32, (LANES, H_B * width), 0)
        h = lax.broadcasted_iota(jnp.int32, (LANES, H_B * width), 1) // width
        return jnp.where(r == gate_lane(d, 0) + h, 1.0, 0.0).astype(BF16)

    rep_k = tuple(replicate(d, DK_B) for d in (0, 1))
    rep_v = tuple(replicate(d, DV_B) for d in (0, 1))

    def chunk_step(i, carry):
        rows, g_in, g2_in, mt_in, gla_in, mls_in = [], [], [], [], [], []
        for d in (0, 1):
            ci = i if d == 0 else nchunks - 1 - i
            rows.append(pl.ds(pl.multiple_of(ci * c, c), c))
            g_in.append(g_ref[rows[d], :])
            g2_in.append(g2_ref[rows[d], :])
            mt_in.append(mt_ref[ci])
            gla_in.append(main_ref[rows[d], _QA:_RA])
            mls_in.append(main_ref[rows[d], _QB:_OB])
        s_bd = [sbd_scr[k] for k in range(2 * npair)]
        c_bd = [cbd_scr[k] for k in range(2 * npair)]
        n_bd = [nbd_scr[k] for k in range(2 * npair)]
        m_gl = mgl_scr[...]
        dirs = (0, 1)
        ends = (c - 1, 0)
        pairs = [(d, p) for d in dirs for p in range(npair)]
        ks_of = lambda p: slice(p * 2 * DK_A, (p + 1) * 2 * DK_A)
        vs_of = lambda p: slice(p * 2 * DV_A, (p + 1) * 2 * DV_A)

        glin = [_dot_hp(g_in[d], gkw_ref[d]) + gkb_ref[d] for d in dirs]
        gi = [jnp.where(used[d], g_in[d] + gbi_ref[...], 0.0) for d in dirs]
        lf = [jnp.where(used[d], jax.nn.log_sigmoid(g2_in[d] + gbf_ref[...]), 0.0) for d in dirs]
        mt = [mt_in[d] + gbcol_ref[...] for d in dirs]
        mt = [jnp.where(f_row, jax.nn.log_sigmoid(mt[d]), mt[d]) for d in dirs]
        glog = [jax.nn.log_sigmoid(glin[d]) / GATE_TEMP for d in dirs]
        bc = [_dot_mask_l(tri[d], glog[d], 2) for d in dirs]
        cum = [_dot_mask_l(tri[d], lf[d], 2) for d in dirs]
        cum_t = [_dot_mask_r(mt[d][:, 0:c], tri_t2[d], 2) for d in dirs]
        b_end = [bc[d][ends[d]:ends[d] + 1, :] for d in dirs]
        qa = [gla_in[d][:, _QA:_QA + 256].astype(F32) * scale_q for d in dirs]
        ka = [gla_in[d][:, _KA:_KA + 256].astype(F32) for d in dirs]
        va = [gla_in[d][:, _VA:_VA + 512] for d in dirs]
        qe = [_bf(qa[d] * jnp.exp(bc[d])) for d in dirs]
        ke = [_bf(ka[d] * jnp.exp(-bc[d])) for d in dirs]
        kd = [_bf(ka[d] * jnp.exp(b_end[d] - bc[d])) for d in dirs]
        eb_end = [jnp.exp(b_end[d]) for d in dirs]
        qb = [mls_in[d][:, 0:256] * jnp.asarray(DK_B ** -0.5, BF16) for d in dirs]
        kb = [mls_in[d][:, _KB - _QB:_KB - _QB + 256] for d in dirs]
        vb = [mls_in[d][:, _VB - _QB:_VB - _QB + 512] for d in dirs]
        a_raw = {(d, p): _dot_nt(qe[d][:, ks_of(p)], _block_diag(ke[d][:, ks_of(p)])) for d, p in pairs}
        qk = {(d, p): _dot_nt(qb[d][:, ks_of(p)], _block_diag(kb[d][:, ks_of(p)])) for d, p in pairs}
        s_upd = {(d, p): _dot_tn(va[d][:, vs_of(p)], kd[d][:, ks_of(p)]) for d, p in pairs}
        o_car = {(d, p): _dot_nt(qe[d][:, ks_of(p)], _bf(s_bd[d * npair + p])) for d, p in pairs}
        m_loc = [cum[d] + _running_max(gi[d] - cum[d], reverse=(d == 1)) for d in dirs]
        inter = [cum[d] + m_gl for d in dirs]
        m_t = [jnp.maximum(inter[d], m_loc[d]) for d in dirs]
        b_last = [cum[d][ends[d]:ends[d] + 1, :] for d in dirs]
        dend = [b_last[d] - cum[d] + gi[d] for d in dirs]
        m_new = [jnp.maximum(b_last[d] + m_gl, jnp.max(dend[d], axis=0, keepdims=True)) for d in dirs]
        zero = jnp.zeros((c, LANES), F32)
        per_key = [jnp.concatenate([jnp.where(used[d], cum[d] - m_t[d], zero),
                                    jnp.where(used[d], jnp.exp(inter[d] - m_t[d]), zero),
                                    jnp.where(used[d], jnp.exp(dend[d] - m_new[d]), zero)], axis=0) for d in dirs]
        per_val = [jnp.concatenate([jnp.where(used[d], jnp.exp(-m_t[d]), zero),
                                    jnp.broadcast_to(jnp.where(used[d], jnp.exp(b_last[d] + m_gl - m_new[d]), 0.0),
                                                     (SUBLANES, LANES))], axis=0) for d in dirs]
        per_key = [_dot_mask_r(per_key[d], rep_k[d], 2) for d in dirs]
        per_val = [_dot_mask_r(per_val[d], rep_v[d], 2) for d in dirs]
        o_par = {(d, p): _dot(_bf(jnp.where(live_p[d], a_raw[d, p], 0.0)), _block_diag(va[d][:, vs_of(p)]))
                 for d, p in pairs}
        for d, p in pairs:
            k = d * npair + p
            s_bd[k] = s_bd[k] * eb_end[d][:, ks_of(p)] + jnp.where(bd_val, s_upd[d, p], 0.0)
        o_gla = [jnp.concatenate([o_par[d, p] + o_car[d, p] for p in range(npair)], axis=-1) for d in dirs]
        for d in dirs:
            m_gl = jnp.where(used[d], m_new[d], m_gl)

        w, qa2, kw = {}, {}, {}
        for d, p in pairs:
            ks = ks_of(p)
            r_i, r_f = d * 8 + 2 * p, d * 8 + 4 + 2 * p
            sub = [cum_t[d][r_f + q:r_f + q + 1, :] - mt[d][r_i + q:r_i + q + 1, :] for q in (0, 1)]
            sub = jnp.where(lane < DK_B, sub[0], sub[1])
            w[d, p] = jnp.exp(jnp.where(live_p[d], per_key[d][0:c, ks] - sub, -jnp.inf))
            qa2[d, p] = _bf(qb[d][:, ks].astype(F32) * per_key[d][c:2 * c, ks])
            kw[d, p] = _bf(kb[d][:, ks].astype(F32) * per_key[d][2 * c:3 * c, ks])
        carried = {(d, p): _dot(qa2[d, p], jnp.concatenate([_bf(c_bd[d * npair + p]), _bf(n_bd[d * npair + p])], axis=1))
                   for d, p in pairs}
        upd = {(d, p): _dot_tn(kw[d, p], jnp.concatenate([vb[d][:, vs_of(p)], ones_cv], axis=1)) for d, p in pairs}
        intra = {(d, p): _dot(_bf(qk[d, p] * w[d, p]),
                              jnp.concatenate([_block_diag(vb[d][:, vs_of(p)]), ones_bd], axis=1)) for d, p in pairs}
        outs = {}
        for d, p in pairs:
            k = d * npair + p
            vs = vs_of(p)
            num = intra[d, p][:, :2 * DV_B] + carried[d, p][:, :2 * DV_B]
            den = intra[d, p][:, 2 * DV_B:] + carried[d, p][:, 2 * DV_B:]
            outs[d, p] = num / jnp.maximum(jnp.abs(den), per_val[d][0:c, vs])
            decay = per_val[d][c:c + 1, vs]
            dec = jnp.concatenate([jnp.broadcast_to(jnp.concatenate([decay[:, q * DV_B:(q + 1) * DV_B]] * 2, axis=1),
                                                    (DK_B, 2 * DV_B)) for q in (0, 1)], axis=0)
            c_bd[k] = dec * c_bd[k] + jnp.where(bd_key, upd[d, p][:, :2 * DV_B], 0.0)
            n_bd[k] = dec * n_bd[k] + jnp.where(bd_key, upd[d, p][:, 2 * DV_B:], 0.0)
        o_mls = [jnp.concatenate([outs[d, p] for p in range(npair)], axis=-1) for d in dirs]

        for d in (0, 1):
            oa_scr[d, rows[d], :] = o_gla[d]
            ob_scr[d, rows[d], :] = o_mls[d]
        for k in range(2 * npair):
            sbd_scr[k] = s_bd[k]
            cbd_scr[k] = c_bd[k]
            nbd_scr[k] = n_bd[k]
        mgl_scr[...] = m_gl
        return carry

    lax.fori_loop(0, nchunks, chunk_step, 0)

    for d in (0, 1):
        for p in range(npair):
            k = d * npair + p
            s_t, c_f, n_f = sbd_scr[k], cbd_scr[k], nbd_scr[k]
            for q in (0, 1):
                h = 2 * p + q
                og_ref[d, h] = s_t[q * DV_A:(q + 1) * DV_A, q * DK_A:(q + 1) * DK_A]
                oc_ref[d, h] = c_f[q * DK_B:(q + 1) * DK_B, q * DV_B:(q + 1) * DV_B]
                n_t = n_f[q * DK_B:(q + 1) * DK_B, q * DV_B:(q + 1) * DV_B].T
                on_ref[d * H_B + h:d * H_B + h + 1, :] = n_t[0:1, :]
    m_gl = mgl_scr[...]
    om_ref[...] = jnp.concatenate([m_gl[:, gate_lane(d, h):gate_lane(d, h) + 1]
                                   for d in (0, 1) for h in range(H_B)], axis=1)

    mod = _mod_slices(mod_ref[...])
    g1, sh2, sc2 = mod[2], mod[3], mod[4]

    def out_step(j, carry):
        hm = TM // 2
        halves = (0, 1)
        r0 = [pl.multiple_of(j * TM + q * hm, hm) for q in halves]
        rows = [pl.ds(r0[q], hm) for q in halves]
        parts = [[], []]
        for h in range(H_A):
            vs = slice(h * DV_A, (h + 1) * DV_A)
            for q in halves:
                ra = main_ref[rows[q], _RA + h * DV_A:_RA + (h + 1) * DV_A].astype(F32)
                o = oa_scr[0, rows[q], vs] + oa_scr[1, rows[q], vs]
                parts[q].append(_rms(o, gnw_ref[...]) * (ra * jax.nn.sigmoid(ra)))
        for h in range(H_B):
            vs = slice(h * DV_B, (h + 1) * DV_B)
            for q in halves:
                ob = main_ref[rows[q], _OB + h * DV_B:_OB + (h + 1) * DV_B].astype(F32)
                o = ob_scr[0, rows[q], vs] + ob_scr[1, rows[q], vs]
                parts[q].append(_rms(o, mnw_ref[...]) * jax.nn.sigmoid(ob))
        mix = [_dot(_bf(jnp.concatenate(parts[q], axis=-1)), wout_ref[...]) for q in halves]
        x1 = [x_ref[rows[q], :] + g1 * mix[q] for q in halves]
        h2 = [_rms(x1[q], n2w_ref[...]) * (1.0 + sc2) + sh2 for q in halves]
        lg = [_router_logits(h2[q], wrt_ref, brt_ref) for q in halves]
        for q in halves:
            x1_ref[rows[q], :] = x1[q]
            _store_rows_per_vreg(h2_ref.at[pl.ds(pl.multiple_of(r0[q] * _RPV, hm * _RPV), hm * _RPV)], h2[q])
        lg_ref[j] = jnp.concatenate(lg, axis=1)
        return carry

    lax.fori_loop(0, seq // TM, out_step, 0)


def _scan(st, ctx, main, g, g2, mt, x, mods, states, consts, prev):
    seq = st.lp if ctx else st.ls
    nb = st.bp if ctx else st.bs
    off = 0 if ctx else st.tp // st.ls
    sg, sc, sn, sm = states
    full = lambda shape: pl.BlockSpec(shape, lambda b: (0,) * len(shape))
    in_specs = [
        pl.BlockSpec((seq, _REC_MAIN), lambda b: (off + b, 0)),
        pl.BlockSpec((seq, LANES), lambda b: (off + b, 0)),
        pl.BlockSpec((seq, LANES), lambda b: (off + b, 0)),
        pl.BlockSpec((seq // CHUNK, 16, 2 * CHUNK), lambda b: (off + b, 0, 0)),
        pl.BlockSpec((seq, D_MODEL), lambda b: (off + b, 0)),
    ]
    args = [main, g, g2, mt, x]
    mod_row = (lambda b: (0, 0, 0)) if ctx else (lambda b: (1 + b, 0, 0))
    in_specs += [
        pl.BlockSpec((None, 1, N_MOD * D_MODEL), mod_row),
        pl.BlockSpec((None, 2, H_A, DV_A, DK_A), lambda b: (b, 0, 0, 0, 0)),
        pl.BlockSpec((None, 2, H_B, DK_B, DV_B), lambda b: (b, 0, 0, 0, 0)),
        pl.BlockSpec((None, 2 * H_B, DK_B), lambda b: (b, 0, 0)),
        pl.BlockSpec((None, 1, 2 * H_B), lambda b: (b, 0, 0)),
    ]
    args += [mods, sg, sc, sn, sm]
    for a in consts:
        in_specs.append(full(a.shape))
        args.append(a)
    aliases = {}
    if prev is not None:
        aliases = {len(args) + k: k for k in range(3)}
        in_specs += [pl.BlockSpec(memory_space=pl.ANY)] * 3
        args += list(prev)
    out_shape = (jax.ShapeDtypeStruct((st.t, D_MODEL), F32),
                 jax.ShapeDtypeStruct((st.t * _RPV, LANES), F32),
                 jax.ShapeDtypeStruct((st.t // TM, LG_ROWS, TM), F32),
                 jax.ShapeDtypeStruct(sg.shape, F32), jax.ShapeDtypeStruct(sc.shape, F32),
                 jax.ShapeDtypeStruct(sn.shape, F32), jax.ShapeDtypeStruct(sm.shape, F32))
    out_specs = (pl.BlockSpec((seq, D_MODEL), lambda b: (off + b, 0)),
                 pl.BlockSpec((seq * _RPV, LANES), lambda b: (off + b, 0)),
                 pl.BlockSpec((seq // TM, LG_ROWS, TM), lambda b: (off + b, 0, 0)),
                 pl.BlockSpec((None, 2, H_A, DV_A, DK_A), lambda b: (b, 0, 0, 0, 0)),
                 pl.BlockSpec((None, 2, H_B, DK_B, DV_B), lambda b: (b, 0, 0, 0, 0)),
                 pl.BlockSpec((None, 2 * H_B, DK_B), lambda b: (b, 0, 0)),
                 pl.BlockSpec((None, 1, 2 * H_B), lambda b: (b, 0, 0)))
    return pl.pallas_call(
        functools.partial(_scan_kernel, seq=seq),
        out_shape=out_shape,
        grid=(nb,),
        in_specs=in_specs,
        out_specs=out_specs,
        scratch_shapes=[pltpu.VMEM((2, seq, H_A * DV_A), F32), pltpu.VMEM((2, seq, H_B * DV_B), F32),
                        pltpu.VMEM((H_A, 2 * DV_A, 2 * DK_A), F32), pltpu.VMEM((H_B, 2 * DK_B, 2 * DV_B), F32),
                        pltpu.VMEM((H_B, 2 * DK_B, 2 * DV_B), F32), pltpu.VMEM((1, LANES), F32)],
        input_output_aliases=aliases,
        compiler_params=_cparams("arbitrary"),
        name="rec_scan_ctx" if ctx else "rec_scan_lat",
    )(*args)


def _first_max(rows):
    m = rows[0]
    for r in rows[1:]:
        m = jnp.maximum(m, r)
    idx = jnp.full(m.shape, len(rows) - 1, jnp.int32)
    for k in range(len(rows) - 2, -1, -1):
        idx = jnp.where(rows[k] == m, k, idx)
    return m, idx


def _route_kernel(lg_ref, pos1_ref, pos2_ref, w1_ref, w2_ref, tab_ref, *, sb):
    lg = jnp.concatenate([lg_ref[b] for b in range(sb // TM)], axis=1)
    rows = [lg[k:k + 1, :] for k in range(N_EXPERTS + N_GROUPS)]
    grp = rows[N_EXPERTS:]
    gmax, gidx = _first_max(grp)
    p_group = 1.0 / sum(jnp.exp(r - gmax) for r in grp)
    e_in = []
    for k in range(EXP_PER_GROUP):
        v = rows[(N_GROUPS - 1) * EXP_PER_GROUP + k]
        for g in range(N_GROUPS - 2, -1, -1):
            v = jnp.where(gidx == g, rows[g * EXP_PER_GROUP + k], v)
        e_in.append(v)
    v1, i1 = _first_max(e_in)
    v2, i2 = _first_max([jnp.where(i1 == k, -jnp.inf, e_in[k]) for k in range(EXP_PER_GROUP)])
    ex = jnp.exp(v2 - v1)
    w1_ref[...] = p_group / (1.0 + ex)
    w2_ref[...] = p_group * ex / (1.0 + ex)
    x1 = gidx * EXP_PER_GROUP + i1
    x2 = gidx * EXP_PER_GROUP + i2

    eid = lax.broadcasted_iota(jnp.int32, (N_EXPERTS, sb), 0)
    sel = jnp.where((eid == x1) | (eid == x2), 1.0, 0.0)
    r_i = lax.broadcasted_iota(jnp.int32, (TM, TM), 0)
    c_i = lax.broadcasted_iota(jnp.int32, (TM, TM), 1)
    before = jnp.where(r_i < c_i, 1.0, 0.0).astype(BF16)
    carry = jnp.zeros((N_EXPERTS, 1), F32)
    ranks = []
    for b in range(sb // TM):
        s_b = sel[:, b * TM:(b + 1) * TM]
        ranks.append(_dot(_bf(s_b), before) + carry)
        carry = carry + jnp.sum(s_b, axis=1, keepdims=True)
    rank = jnp.concatenate(ranks, axis=1)
    shift = ROW_PAD.bit_length() - 1
    npad = jnp.left_shift(jnp.right_shift(carry.astype(jnp.int32) + (ROW_PAD - 1), shift), shift)

    lane = lax.broadcasted_iota(jnp.int32, (1, LANES), 1)
    tab = jnp.zeros((1, LANES), jnp.int32)
    pos1 = jnp.zeros((1, sb), F32)
    pos2 = jnp.zeros((1, sb), F32)
    off = jnp.zeros((1, 1), jnp.int32)
    for e in range(N_EXPERTS):
        n_e = npad[e:e + 1, :]
        tab = jnp.where(lane == e, off, tab)
        tab = jnp.where(lane == N_EXPERTS + e, n_e, tab)
        row = off.astype(F32) + rank[e:e + 1, :]
        pos1 = jnp.where(x1 == e, row, pos1)
        pos2 = jnp.where(x2 == e, row, pos2)
        off = off + n_e
    pos1_ref[...] = pos1.astype(jnp.int32) * _RPV
    pos2_ref[...] = pos2.astype(jnp.int32) * _RPV
    tab_ref[...] = tab


def _route(st, lg):
    sb = st.sb
    nsb = st.t // sb
    row_i = jax.ShapeDtypeStruct((nsb, 1, sb), jnp.int32)
    row_f = jax.ShapeDtypeStruct((nsb, 1, sb), F32)
    rspec = pl.BlockSpec((None, 1, sb), lambda s: (s, 0, 0))
    return pl.pallas_call(
        functools.partial(_route_kernel, sb=sb),
        out_shape=(row_i, row_i, row_f, row_f, jax.ShapeDtypeStruct((nsb, 1, LANES), jnp.int32)),
        grid=(nsb,),
        in_specs=[pl.BlockSpec((sb // TM, LG_ROWS, TM), lambda s: (s, 0, 0))],
        out_specs=(rspec, rspec, rspec, rspec, pl.BlockSpec((None, 1, LANES), lambda s: (s, 0, 0))),
        compiler_params=_cparams("arbitrary"),
        name="moe_route",
    )(lg)


def _moe_rows(sb):
    return 2 * sb + N_EXPERTS * ROW_PAD + FFN_TILE


def _moe_kernel(pos1_ref, pos2_ref, w1_ref, w2_ref, tab_ref, xr_ref, wg_ref, wu_ref, wd_ref, o_ref,
                rows_scr, stage_scr, *, sb):
    s = pl.program_id(0)
    e = pl.program_id(1)

    def tile_at(ref, r8):
        return ref.at[pl.ds(pl.multiple_of(r8, _RPV), _RPV)]

    @pl.when((s == 0) & (e == 0))
    def _():
        rows_scr[...] = jnp.zeros_like(rows_scr)

    @pl.when(e == 0)
    def _():
        def dispatch(t, carry):
            v = tile_at(xr_ref, t * _RPV)[...]
            tile_at(rows_scr, pos1_ref[0, t])[...] = v
            tile_at(rows_scr, pos2_ref[0, t])[...] = v
            return carry

        lax.fori_loop(0, sb, dispatch, 0, unroll=8)

    def expert(k):
        ex = e * EXPERTS_PER_STEP + k

        def ffn_tile(r0, m, valid=None):
            win = rows_scr.at[pl.ds(pl.multiple_of(r0 * _RPV, ROW_PAD * _RPV), m * _RPV)]
            x = _load_rows_per_vreg(win, m)
            xb = _bf(x)
            hg = _dot(xb, wg_ref[k])
            hu = _dot(xb, wu_ref[k])
            y = _dot(_bf(hg * jax.nn.sigmoid(hg) * hu), wd_ref[k])
            if valid is not None:
                y = jnp.where(lax.broadcasted_iota(jnp.int32, (m, 1), 0) < valid, y, x)
            _store_rows_per_vreg(win, y)

        off = tab_ref[0, ex]
        npad = tab_ref[0, N_EXPERTS + ex]
        nfull = npad // FFN_TILE

        def full_tile(i, carry):
            ffn_tile(off + i * FFN_TILE, FFN_TILE)
            return carry

        lax.fori_loop(0, nfull, full_tile, 0)
        rem = npad - nfull * FFN_TILE
        last = off + nfull * FFN_TILE

        @pl.when((rem > 0) & (rem <= FFN_TILE // 2))
        def _():
            ffn_tile(last, FFN_TILE // 2, valid=rem)

        @pl.when(rem > FFN_TILE // 2)
        def _():
            ffn_tile(last, FFN_TILE, valid=rem)

    for k in range(EXPERTS_PER_STEP):
        expert(k)

    @pl.when(e == N_EXPERTS // EXPERTS_PER_STEP - 1)
    def _():
        for c in range(sb // TM):
            def combine(t, carry, c=c):
                tt = c * TM + t
                y = (w1_ref[0, tt] * tile_at(rows_scr, pos1_ref[0, tt])[...]
                     + w2_ref[0, tt] * tile_at(rows_scr, pos2_ref[0, tt])[...])
                tile_at(stage_scr, t * _RPV)[...] = y
                return carry

            lax.fori_loop(0, TM, combine, 0, unroll=8)
            o_ref[c * TM:(c + 1) * TM, :] = _bf(_load_rows_per_vreg(stage_scr, TM))


def _cast_kernel(x_ref, o_ref):
    o_ref[...] = _bf(x_ref[...])


def _expert_weights_bf16(w):
    depth, ne, a, b = w.shape
    per_step = 4
    assert (depth * ne) % per_step == 0
    spec = pl.BlockSpec((per_step, a, b), lambda i: (i, 0, 0))
    return pl.pallas_call(
        _cast_kernel,
        out_shape=jax.ShapeDtypeStruct((depth * ne, a, b), BF16),
        grid=(depth * ne // per_step,),
        in_specs=[spec],
        out_specs=spec,
        compiler_params=_cparams("arbitrary"),
        name="expert_weight_cast",
    )(w.reshape(depth * ne, a, b))


def _moe(st, layer, xr, route, wg, wu, wd):
    sb = st.sb
    eps = EXPERTS_PER_STEP
    b0 = layer * N_EXPERTS // eps
    smem = lambda n: pl.BlockSpec((None, 1, n), lambda s, e: (s, 0, 0), memory_space=pltpu.SMEM)
    return pl.pallas_call(
        functools.partial(_moe_kernel, sb=sb),
        out_shape=jax.ShapeDtypeStruct((st.t, D_MODEL), BF16),
        grid=(st.t // sb, N_EXPERTS // eps),
        in_specs=[
            smem(sb), smem(sb), smem(sb), smem(sb), smem(LANES),
            pl.BlockSpec((sb * _RPV, LANES), lambda s, e: (s, 0)),
            pl.BlockSpec((eps, D_MODEL, D_EXPERT), lambda s, e: (b0 + e, 0, 0)),
            pl.BlockSpec((eps, D_MODEL, D_EXPERT), lambda s, e: (b0 + e, 0, 0)),
            pl.BlockSpec((eps, D_EXPERT, D_MODEL), lambda s, e: (b0 + e, 0, 0)),
        ],
        out_specs=pl.BlockSpec((sb, D_MODEL), lambda s, e: (s, 0)),
        scratch_shapes=[pltpu.VMEM((_moe_rows(sb) * _RPV, LANES), F32), pltpu.VMEM((TM * _RPV, LANES), F32)],
        compiler_params=_cparams("arbitrary", "arbitrary"),
        name="moe_ffn",
    )(*route, xr, wg, wu, wd)


def _hy_inproj_kernel(x_ref, m_ref, mod0_ref, mod_ref, nw_ref, w_ref, x2_ref, z_ref):
    g2 = _mod_slices(mod0_ref[...])[5]
    x2 = x_ref[...] + g2 * m_ref[...].astype(F32)
    x2_ref[...] = x2
    sh1, sc1 = _mod_slices(mod_ref[...])[:2]
    h = _rms(x2, nw_ref[...]) * (1.0 + sc1) + sh1
    z_ref[...] = _bf(_dot(_bf(h), w_ref[...]))


def _hy_inproj(st, x, moe, mods_prev, mods, nw, w):
    n = w.shape[1]
    tm = TM_TOK
    mspec = pl.BlockSpec((None, 1, N_MOD * D_MODEL), lambda i: (st.mod_row(i, tm), 0, 0))
    return pl.pallas_call(
        _hy_inproj_kernel,
        out_shape=(jax.ShapeDtypeStruct((st.t, D_MODEL), F32), jax.ShapeDtypeStruct((st.t, n), BF16)),
        grid=(st.t // tm,),
        in_specs=[
            pl.BlockSpec((tm, D_MODEL), lambda i: (i, 0)),
            pl.BlockSpec((tm, D_MODEL), lambda i: (i, 0)),
            mspec, mspec,
            pl.BlockSpec((1, D_MODEL), lambda i: (0, 0)),
            pl.BlockSpec((D_MODEL, n), lambda i: (0, 0)),
        ],
        out_specs=(pl.BlockSpec((tm, D_MODEL), lambda i: (i, 0)), pl.BlockSpec((tm, n), lambda i: (i, 0))),
        compiler_params=_cparams("arbitrary"),
        name="hy_inproj",
    )(x, moe, mods_prev, mods, nw, w)


def _dft_tables(seq):
    n2 = 2 * seq
    assert n2 & (n2 - 1) == 0
    tr = min(seq, TM)

    def table_kernel(cos_ref, msin_ref, cosb_ref, msinb_ref):
        k = lax.broadcasted_iota(jnp.int32, (tr, seq), 0) + pl.program_id(0) * tr
        n = lax.broadcasted_iota(jnp.int32, (tr, seq), 1)
        ang = ((k * n) & (n2 - 1)).astype(F32) * (2.0 * math.pi / n2)
        c, s = jnp.cos(ang), -jnp.sin(ang)
        cos_ref[...] = c
        msin_ref[...] = s
        cosb_ref[...] = _bf(c)
        msinb_ref[...] = _bf(s)

    spec = pl.BlockSpec((tr, seq), lambda i: (i, 0))
    f32, b16 = jax.ShapeDtypeStruct((seq, seq), F32), jax.ShapeDtypeStruct((seq, seq), BF16)
    return pl.pallas_call(table_kernel, out_shape=(f32, f32, b16, b16), grid=(seq // tr,),
                          out_specs=(spec, spec, spec, spec), compiler_params=_cparams("arbitrary"),
                          name=f"dft_tables_{seq}")()


def _alternating(shape):
    return jnp.where(lax.broadcasted_iota(jnp.int32, shape, 0) % 2 == 0, 1.0, -1.0)


def _hy_filter_kernel(emb_ref, dec_ref, w1_ref, b1_ref, f1_ref, w2_ref, b2_ref, f2_ref, w3_ref,
                      cos_ref, msin_ref, kr_ref, ki_ref, kn_ref, *, seq):
    h = jnp.sin(f1_ref[...] * (_dot_hp(emb_ref[...], w1_ref[...]) + b1_ref[...]))
    h = jnp.sin(f2_ref[...] * (_dot_hp(h, w2_ref[...]) + b2_ref[...]))
    dec = dec_ref[...]
    row0 = lax.broadcasted_iota(jnp.int32, dec.shape, 0) == 0
    alt = _alternating(dec.shape)
    cos, msin = cos_ref[...], msin_ref[...]
    scale = jnp.where(row0, 1.0, 2.0) / (2 * seq)
    for o in range(HY_ORDER):
        h_f = _dot_hp(h, w3_ref[:, 2 * o, :]) * dec
        h_b = jnp.where(row0, 0.0, _dot_hp(h, w3_ref[:, 2 * o + 1, :]) * dec)
        kn_ref[o] = jnp.sum(alt * (h_f + h_b), axis=0, keepdims=True) / (2 * seq)
        kr_ref[o] = _dot_hp(cos, h_f + h_b) * scale
        ki_ref[o] = _dot_hp(msin, h_f - h_b) * scale


def _hy_filter(seq, w1, b1, f1, w2, b2, f2, w3, cos, msin):
    t = jnp.linspace(0.0, 1.0, seq, dtype=F32)[:, None]
    w = 2.0 * math.pi * jnp.arange(seq, dtype=F32)[:, None] / seq
    f = jnp.linspace(1e-4, HY_BANDS - 1, HY_BANDS, dtype=F32)[None, :]
    emb = jnp.concatenate([t, jnp.cos(f * w), -jnp.sin(f * w), jnp.zeros((seq, LANES - HY_EMB), F32)], axis=-1)
    decay = jnp.exp(-t * jnp.linspace(HY_MIN_DECAY, HY_MAX_DECAY, D_MODEL, dtype=F32)[None, :])
    w1p = jnp.concatenate([w1, jnp.zeros((LANES - HY_EMB, HY_FFN), F32)], axis=0)
    dblk = 256
    out = jax.ShapeDtypeStruct((HY_ORDER, seq, D_MODEL), F32)
    full = lambda shape: pl.BlockSpec(shape, lambda j: (0,) * len(shape))
    ospec = pl.BlockSpec((HY_ORDER, seq, dblk), lambda j: (0, 0, j))
    return pl.pallas_call(
        functools.partial(_hy_filter_kernel, seq=seq),
        out_shape=(out, out, jax.ShapeDtypeStruct((HY_ORDER, 1, D_MODEL), F32)),
        grid=(D_MODEL // dblk,),
        in_specs=[
            full((seq, LANES)),
            pl.BlockSpec((seq, dblk), lambda j: (0, j)),
            full((LANES, HY_FFN)), full((1, HY_FFN)), full((1, HY_FFN)),
            full((HY_FFN, HY_FFN)), full((1, HY_FFN)), full((1, HY_FFN)),
            pl.BlockSpec((HY_FFN, 2 * HY_ORDER, dblk), lambda j: (0, 0, j)),
            full((seq, seq)), full((seq, seq)),
        ],
        out_specs=(ospec, ospec, pl.BlockSpec((HY_ORDER, 1, dblk), lambda j: (0, 0, j))),
        compiler_params=_cparams("arbitrary"),
        name=f"hy_filter_{seq}",
    )(emb, decay, w1p, b1[None, :], f1[None, :], w2, b2[None, :], f2[None, :],
      w3.reshape(HY_FFN, 2 * HY_ORDER, D_MODEL), cos, msin)


def _hy_conv_kernel(*refs, seq, nseq):
    (zv_ref, z1_ref, z2_ref, cv_ref, c1_ref, c2_ref, kr_ref, ki_ref, kn_ref, bias_ref,
     cos_ref, msin_ref) = refs[:12]
    y_ref = refs[-1]
    dblk = y_ref.shape[1]
    t = lax.broadcasted_iota(jnp.int32, (seq, dblk), 0)
    first, last = t == 0, t == seq - 1
    alt = _alternating((seq, dblk))
    seqs = range(nseq)
    rows = [slice(q * seq, (q + 1) * seq) for q in seqs]

    def short_conv(z_ref, c_ref):
        out = []
        for q in seqs:
            z = z_ref[rows[q], :].astype(F32)
            prev = jnp.where(first, 0.0, pltpu.roll(z, 1, 0))
            nxt = jnp.where(last, 0.0, pltpu.roll(z, seq - 1, 0))
            out.append(c_ref[0:1, :] * prev + c_ref[1:2, :] * z + c_ref[2:3, :] * nxt)
        return out

    def long_conv(s, o):
        sb = [_bf(s[q]) for q in seqs]
        x_re = [_dot(cos_ref[...], sb[q]) for q in seqs]
        x_im = [_dot(msin_ref[...], sb[q]) for q in seqs]
        y_nyq = [jnp.sum(alt * s[q], axis=0, keepdims=True) * kn_ref[o] for q in seqs]
        k_re, k_im = kr_ref[o], ki_ref[o]
        y_re = [x_re[q] * k_re - x_im[q] * k_im for q in seqs]
        y_im = [x_re[q] * k_im + x_im[q] * k_re for q in seqs]
        y = [_dot(cos_ref[...], _bf(y_re[q])) + _dot(msin_ref[...], _bf(y_im[q])) for q in seqs]
        return [y[q] + alt * y_nyq[q] + bias_ref[o] * s[q] for q in seqs]

    v = short_conv(zv_ref, cv_ref)
    x1 = short_conv(z1_ref, c1_ref)
    x2 = short_conv(z2_ref, c2_ref)
    c1 = long_conv(v, 0)
    c2 = long_conv([x1[q] * c1[q] for q in seqs], 1)
    for q in seqs:
        y_ref[rows[q], :] = _bf(x2[q] * c2[q])


def _hy_conv(st, ctx, z, conv_w, filt, bias, tables, prev):
    seq = st.lp if ctx else st.ls
    nb = st.bp if ctx else st.bs
    nseq = max(1, min(nb, 4))
    assert nb % nseq == 0
    off = 0 if ctx else st.tp // (nseq * seq)
    dblk = 256
    nd = D_MODEL // dblk
    kr, ki, kn = filt
    cos, msin = tables
    vspec = pl.BlockSpec((HY_ORDER, 1, dblk), lambda j, b: (0, 0, j))
    full = lambda shape: pl.BlockSpec(shape, lambda j, b: (0,) * len(shape))
    kspec = pl.BlockSpec((HY_ORDER, seq, dblk), lambda j, b: (0, 0, j))
    in_specs = [
        pl.BlockSpec((nseq * seq, dblk), lambda j, b: (off + b, j)),
        pl.BlockSpec((nseq * seq, dblk), lambda j, b: (off + b, nd + j)),
        pl.BlockSpec((nseq * seq, dblk), lambda j, b: (off + b, 2 * nd + j)),
        pl.BlockSpec((3, dblk), lambda j, b: (0, j)),
        pl.BlockSpec((3, dblk), lambda j, b: (0, nd + j)),
        pl.BlockSpec((3, dblk), lambda j, b: (0, 2 * nd + j)),
        kspec, kspec, vspec, vspec,
        full((seq, seq)), full((seq, seq)),
    ]
    args = [z, z, z, conv_w, conv_w, conv_w, kr, ki, kn, bias.reshape(HY_ORDER, 1, D_MODEL), cos, msin]
    aliases = {}
    if prev is not None:
        aliases = {len(args): 0}
        in_specs.append(pl.BlockSpec(memory_space=pl.ANY))
        args.append(prev)
    return pl.pallas_call(
        functools.partial(_hy_conv_kernel, seq=seq, nseq=nseq),
        out_shape=jax.ShapeDtypeStruct((st.t, D_MODEL), BF16),
        grid=(nd, nb // nseq),
        in_specs=in_specs,
        out_specs=pl.BlockSpec((nseq * seq, dblk), lambda j, b: (off + b, j)),
        input_output_aliases=aliases,
        compiler_params=_cparams("arbitrary", "arbitrary"),
        name="hy_conv_ctx" if ctx else "hy_conv_lat",
    )(*args)


def _hy_outproj_kernel(y_ref, x_ref, mod_ref, w_ref, n2w_ref, wrt_ref, brt_ref, x3_ref, h2_ref, lg_ref):
    mod = _mod_slices(mod_ref[...])
    g1, sh2, sc2 = mod[2], mod[3], mod[4]
    nt = y_ref.shape[0] // TM
    tiles = range(nt)
    rows = [slice(b * TM, (b + 1) * TM) for b in tiles]
    mix = [_dot(y_ref[rows[b], :], w_ref[...]) for b in tiles]
    x3 = [x_ref[rows[b], :] + g1 * mix[b] for b in tiles]
    h2 = [_rms(x3[b], n2w_ref[...]) * (1.0 + sc2) + sh2 for b in tiles]
    lg = [_router_logits(h2[b], wrt_ref, brt_ref) for b in tiles]
    for b in tiles:
        x3_ref[rows[b], :] = x3[b]
        _store_rows_per_vreg(h2_ref.at[pl.ds(b * TM * _RPV, TM * _RPV)], h2[b])
        lg_ref[b] = lg[b]


def _hy_outproj(st, y, x, mods, w, n2w, wrt, brt):
    tm = TM_TOK
    return pl.pallas_call(
        _hy_outproj_kernel,
        out_shape=(jax.ShapeDtypeStruct((st.t, D_MODEL), F32),
                   jax.ShapeDtypeStruct((st.t * _RPV, LANES), F32),
                   jax.ShapeDtypeStruct((st.t // TM, LG_ROWS, TM), F32)),
        grid=(st.t // tm,),
        in_specs=[
            pl.BlockSpec((tm, D_MODEL), lambda i: (i, 0)),
            pl.BlockSpec((tm, D_MODEL), lambda i: (i, 0)),
            pl.BlockSpec((None, 1, N_MOD * D_MODEL), lambda i: (st.mod_row(i, tm), 0, 0)),
            pl.BlockSpec((D_MODEL, D_MODEL), lambda i: (0, 0)),
            pl.BlockSpec((1, D_MODEL), lambda i: (0, 0)),
            pl.BlockSpec((LG_ROWS, D_MODEL), lambda i: (0, 0)),
            pl.BlockSpec((LG_ROWS, 1), lambda i: (0, 0)),
        ],
        out_specs=(pl.BlockSpec((tm, D_MODEL), lambda i: (i, 0)),
                   pl.BlockSpec((tm * _RPV, LANES), lambda i: (i, 0)),
                   pl.BlockSpec((tm // TM, LG_ROWS, TM), lambda i: (i, 0, 0))),
        compiler_params=_cparams("arbitrary"),
        name="hy_outproj",
    )(y, x, mods, w, n2w, wrt, brt)


def _final_kernel(x_ref, m_ref, mod_ref, nf_ref, yp_ref, ys_ref, *, ncb):
    i = pl.program_id(0)
    g2 = _mod_slices(mod_ref[...])[5]
    y = _rms(x_ref[...] + g2 * m_ref[...].astype(F32), nf_ref[...])

    @pl.when(i < ncb)
    def _():
        yp_ref[...] = y

    @pl.when(i >= ncb)
    def _():
        ys_ref[...] = y


def _final(st, x, moe, mods, nf):
    tm = TM_TOK
    ncb = st.tp // tm
    return pl.pallas_call(
        functools.partial(_final_kernel, ncb=ncb),
        out_shape=(jax.ShapeDtypeStruct((st.tp, D_MODEL), F32), jax.ShapeDtypeStruct((st.ts, D_MODEL), F32)),
        grid=(st.t // tm,),
        in_specs=[
            pl.BlockSpec((tm, D_MODEL), lambda i: (i, 0)),
            pl.BlockSpec((tm, D_MODEL), lambda i: (i, 0)),
            pl.BlockSpec((None, 1, N_MOD * D_MODEL), lambda i: (st.mod_row(i, tm), 0, 0)),
            pl.BlockSpec((1, D_MODEL), lambda i: (0, 0)),
        ],
        out_specs=(pl.BlockSpec((tm, D_MODEL), lambda i: (jnp.minimum(i, ncb - 1), 0)),
                   pl.BlockSpec((tm, D_MODEL), lambda i: (jnp.maximum(i - ncb, 0), 0))),
        compiler_params=_cparams("arbitrary"),
        name="final_norm",
    )(x, moe, mods, nf)


def _grid_pos_table(seq):
    rows = seq // GRID_W
    r, cl = jnp.meshgrid(jnp.arange(rows, dtype=F32), jnp.arange(GRID_W, dtype=F32), indexing='ij')
    quarter = D_MODEL // 4
    omega = POS_THETA ** (-jnp.arange(quarter, dtype=F32) / quarter)

    def enc(pos):
        a = pos.reshape(-1, 1) * omega[None, :]
        return jnp.concatenate([jnp.sin(a), jnp.cos(a)], axis=-1)

    return jnp.concatenate([enc(r), enc(cl)], axis=-1)


def _router_weights(w_group, b_group, w_router, b_router):
    pad = LG_ROWS - N_EXPERTS - N_GROUPS
    w = jnp.concatenate([w_router.T, w_group.T, jnp.zeros((pad, D_MODEL), F32)], axis=0)
    b = jnp.concatenate([b_router, b_group, jnp.zeros((pad,), F32)])[:, None]
    return w, b


def kernel(x_prompt, x_sample, state_gla, state_mlstm_c, state_mlstm_n, state_mlstm_m, c, c_ctx, norm1_w, norm2_w, norm_f_w, ada_w, ada_b, rec_w_in, gla_gk_w, gla_gk_b, mlstm_gate_b, gla_norm_w, mlstm_norm_w, rec_w_out, hy_w_in, hy_conv_w, hy_f_w1, hy_f_b1, hy_f_freq1, hy_f_w2, hy_f_b2, hy_f_freq2, hy_f_w3, hy_f_bias, hy_w_out, moe_w_group, moe_b_group, moe_w_router, moe_b_router, moe_w_gate, moe_w_up, moe_w_down):
    bp, lp, _ = x_prompt.shape
    bs, ls, _ = x_sample.shape
    st = _Streams(bp, lp, bs, ls)
    xp = x_prompt.reshape(st.tp, D_MODEL)
    xs = x_sample.reshape(st.ts, D_MODEL)

    nrow = -(-(1 + bs) // 8) * 8
    cv = jnp.concatenate([c_ctx[None, :], c, jnp.zeros((nrow - 1 - bs, D_MODEL), F32)], axis=0)
    mods = _ada(cv, ada_w, ada_b)
    mods0 = mods[0].reshape(nrow, 1, N_MOD * D_MODEL)
    mods1 = mods[1].reshape(nrow, 1, N_MOD * D_MODEL)

    pos = _grid_pos_table(ls)

    w_in = rec_w_in[0]
    w_main = _bf(w_in[:, :_REC_MAIN])
    w_mg = w_in[:, _REC_MAIN + _MG0:].reshape(D_MODEL, 2, 2, H_B)
    w_fg = jnp.pad(w_mg[:, :, 1, :], ((0, 0), (0, 0), (0, 8 - H_B))).reshape(D_MODEL, 16)
    w_gate = jnp.concatenate([w_in[:, _REC_MAIN:], jnp.zeros((D_MODEL, LANES - _REC_GATE), F32),
                              jnp.zeros((D_MODEL, _MG0), F32), w_fg,
                              jnp.zeros((D_MODEL, LANES - _REC_GATE), F32)], axis=1)
    w_gate_t = w_in[:, _REC_MAIN + _MG0:].T
    x0, main, g, g2, mt = _inproj(st, xp, xs, pos, mods0, norm1_w[0][None, :], w_main, w_gate, w_gate_t)

    gkw = jnp.zeros((2, LANES, H_A * DK_A), F32)
    gkw = gkw.at[0, :GK_RANK].set(gla_gk_w[0, 0]).at[1, GK_RANK:2 * GK_RANK].set(gla_gk_w[0, 1])
    gkb = gla_gk_b[0][:, None, :]
    gb = mlstm_gate_b[0]
    gate_row = lambda b: jnp.pad(jnp.pad(b, ((0, 0), (0, 8 - H_B))).reshape(1, 16), ((0, 0), (_MG0, LANES - _REC_GATE)))
    gbcol = gb.reshape(16, 1)
    wr0, br0 = _router_weights(moe_w_group[0], moe_b_group[0], moe_w_router[0], moe_b_router[0])
    consts = [gkw, gkb, gate_row(gb[:, 0]), gate_row(gb[:, 1]), gbcol, gla_norm_w[0][None, :],
              mlstm_norm_w[0][None, :], _bf(rec_w_out[0]), norm2_w[0][None, :], wr0, br0]

    def scan_states(sg, sc, sn, sm):
        nb = sg.shape[0]
        return (jnp.swapaxes(sg, -1, -2), sc, sn.reshape(nb, 2 * H_B, DK_B), sm.reshape(nb, 1, 2 * H_B))

    zero = (jnp.zeros((bp, 2, H_A, DK_A, DV_A), F32), jnp.zeros((bp, 2, H_B, DK_B, DV_B), F32),
            jnp.zeros((bp, 2, H_B, DK_B), F32), jnp.zeros((bp, 2, H_B), F32))
    cached = (state_gla[:, 0], state_mlstm_c[:, 0], state_mlstm_n[:, 0], state_mlstm_m[:, 0])
    x1, h2, lg, fg, fc, fn, fm = _scan(st, True, main, g, g2, mt, x0, mods0, scan_states(*zero), consts, None)
    x1, h2, lg = _scan(st, False, main, g, g2, mt, x0, mods0, scan_states(*cached), consts, (x1, h2, lg))[:3]

    new_gla = jnp.swapaxes(fg, -1, -2)[:, None]
    new_c = fc[:, None]
    new_n = fn.reshape(bp, 1, 2, H_B, DK_B)
    new_m = fm.reshape(bp, 1, 2, H_B)

    wg, wu, wd = (_expert_weights_bf16(w) for w in (moe_w_gate, moe_w_up, moe_w_down))
    moe0 = _moe(st, 0, h2, _route(st, lg), wg, wu, wd)

    x2, z = _hy_inproj(st, x1, moe0, mods0, mods1, norm1_w[1][None, :], _bf(hy_w_in[0]))
    y = None
    for ctx in (True, False):
        seq = lp if ctx else ls
        cos, msin, cos_b, msin_b = _dft_tables(seq)
        filt = _hy_filter(seq, hy_f_w1[0], hy_f_b1[0], hy_f_freq1[0], hy_f_w2[0], hy_f_b2[0], hy_f_freq2[0],
                          hy_f_w3[0], cos, msin)
        y = _hy_conv(st, ctx, z, hy_conv_w[0], filt, hy_f_bias[0], (cos_b, msin_b), y)
    wr1, br1 = _router_weights(moe_w_group[1], moe_b_group[1], moe_w_router[1], moe_b_router[1])
    x3, h4, lg1 = _hy_outproj(st, y, x2, mods1, _bf(hy_w_out[0]), norm2_w[1][None, :], wr1, br1)

    moe1 = _moe(st, 1, h4, _route(st, lg1), wg, wu, wd)
    y_prompt, y_sample = _final(st, x3, moe1, mods1, norm_f_w[None, :])
    return (y_prompt.reshape(bp, lp, D_MODEL), y_sample.reshape(bs, ls, D_MODEL), new_gla, new_c, new_n, new_m)
```

```python
import functools
import math

import jax
import jax.numpy as jnp
from jax import lax
from jax.experimental import pallas as pl
from jax.experimental.pallas import tpu as pltpu

F32 = jnp.float32
BF16 = jnp.bfloat16

D_MODEL = 1024
GRID_W = 64
H_A = 4
DK_A = D_MODEL // 16
DV_A = D_MODEL // 8
GK_RANK = 16
GATE_TEMP = 16.0
H_B = 4
DK_B = D_MODEL // 16
DV_B = D_MODEL // 8
CHUNK = 64
HY_ORDER = 2
HY_EMB = 33
HY_BANDS = (HY_EMB - 1) // 2
HY_FFN = 64
HY_TARGET = 1e-2
HY_MAX_DECAY = abs(math.log(HY_TARGET)) / 0.3
HY_MIN_DECAY = abs(math.log(HY_TARGET)) / 1.5
N_GROUPS = 4
EXP_PER_GROUP = 4
N_EXPERTS = N_GROUPS * EXP_PER_GROUP
D_EXPERT = D_MODEL // 2
N_MOD = 6
POS_THETA = 10000.0
EPS = 1e-6

_QA, _KA, _VA, _RA = 0, 256, 512, 1024
_QB, _KB, _VB, _OB = 1536, 1792, 2048, 2560
_REC_MAIN = 3072
_REC_GATE = 48
_MG0 = 2 * GK_RANK

LANES = 128
SUBLANES = 8
_RPV = D_MODEL // LANES
assert _RPV == SUBLANES
TM = 256
TM_TOK = 1024
TM_IN = 512
SB_MOE = 2048
ROW_PAD = 16
EXPERTS_PER_STEP = 2
FFN_TILE = 288
LG_ROWS = 32
VMEM_LIMIT = 58 * 1024 * 1024


def _cparams(*sem):
    return pltpu.CompilerParams(dimension_semantics=sem, vmem_limit_bytes=VMEM_LIMIT)


def _bf(x):
    return x.astype(BF16)


def _dot(a, b):
    return jnp.dot(a, b, preferred_element_type=F32)


def _dot_nt(a, b):
    return lax.dot_general(a, b, (((1,), (1,)), ((), ())), preferred_element_type=F32)


def _dot_tn(a, b):
    return lax.dot_general(a, b, (((0,), (0,)), ((), ())), preferred_element_type=F32)


def _split2(x):
    hi = _bf(x)
    return hi, _bf(x - hi.astype(F32))


def _split3(x):
    hi = _bf(x)
    r = x - hi.astype(F32)
    mid = _bf(r)
    return hi, mid, _bf(r - mid.astype(F32))


def _dot_hp(a, b, dot=_dot):
    ah, al = _split2(a)
    bh, bl = _split2(b)
    return dot(ah, bh) + (dot(ah, bl) + dot(al, bh))


def _dot_mask_l(m, x, terms=3):
    if terms == 2:
        x1, x2 = _split2(x)
        return _dot(m, x1) + _dot(m, x2)
    x1, x2, x3 = _split3(x)
    return _dot(m, x1) + (_dot(m, x2) + _dot(m, x3))


def _dot_mask_r(x, m, terms=3):
    if terms == 2:
        x1, x2 = _split2(x)
        return _dot(x1, m) + _dot(x2, m)
    x1, x2, x3 = _split3(x)
    return _dot(x1, m) + (_dot(x2, m) + _dot(x3, m))


def _rms(x, w):
    return x * lax.rsqrt(jnp.mean(x * x, axis=-1, keepdims=True) + EPS) * w


def _mod_slices(mod):
    return [mod[:, k * D_MODEL:(k + 1) * D_MODEL] for k in range(N_MOD)]


def _store_rows_per_vreg(dst, h):
    rows = h.shape[0]
    for j in range(_RPV):
        dst[pl.ds(j, rows, stride=_RPV), :] = h[:, j * LANES:(j + 1) * LANES]


def _load_rows_per_vreg(src, rows):
    return jnp.concatenate([src[pl.ds(j, rows, stride=_RPV), :] for j in range(_RPV)], axis=-1)


def _router_logits(h, wrt_ref, brt_ref):
    return _dot_hp(wrt_ref[...], h, dot=_dot_nt) + brt_ref[...]


def _ada_kernel(cv_ref, w_ref, b_ref, o_ref):
    a = cv_ref[...]
    a = a * jax.nn.sigmoid(a)
    o_ref[...] = _dot_hp(a, w_ref[...]) + b_ref[...]


def _ada(cv, ada_w, ada_b):
    depth, d, n = ada_w.shape
    rows = cv.shape[0]
    tn = 1536
    return pl.pallas_call(
        _ada_kernel,
        out_shape=jax.ShapeDtypeStruct((depth, rows, n), F32),
        grid=(depth, n // tn),
        in_specs=[
            pl.BlockSpec((rows, d), lambda l, j: (0, 0)),
            pl.BlockSpec((None, d, tn), lambda l, j: (l, 0, j)),
            pl.BlockSpec((None, 1, tn), lambda l, j: (l, 0, j)),
        ],
        out_specs=pl.BlockSpec((None, rows, tn), lambda l, j: (l, 0, j)),
        compiler_params=_cparams("arbitrary", "arbitrary"),
        name="ada_mod",
    )(cv, ada_w, ada_b.reshape(depth, 1, n))


class _Streams:
    def __init__(self, bp, lp, bs, ls):
        self.bp, self.lp, self.bs, self.ls = bp, lp, bs, ls
        self.tp, self.ts = bp * lp, bs * ls
        self.t = self.tp + self.ts
        assert lp % TM == 0 and ls % TM == 0 and self.tp % ls == 0
        assert self.tp % TM_TOK == 0 and ls % TM_TOK == 0
        self.sb = min(SB_MOE, math.gcd(self.tp, self.ts))
        assert self.t % self.sb == 0 and self.sb % TM == 0

    def mod_row(self, i, tm):
        ncb = self.tp // tm
        return jnp.where(i < ncb, 0, 1 + (i - ncb) // (self.ls // tm))


def _inproj_kernel(xp_ref, xs_ref, pos_ref, mod_ref, nw_ref, w_ref, wg_ref, wgt_ref,
                   x0_ref, main_ref, g_ref, g2_ref, mt_ref, *, ncb):
    i = pl.program_id(0)
    x = jnp.where(i < ncb, xp_ref[...], xs_ref[...] + pos_ref[...])
    x0_ref[...] = x
    sh1, sc1 = _mod_slices(mod_ref[...])[:2]
    h = _rms(x, nw_ref[...]) * (1.0 + sc1) + sh1
    main_ref[...] = _bf(_dot(_bf(h), w_ref[...]))
    gates = _dot_hp(h, wg_ref[...])
    g_ref[...] = gates[:, :LANES]
    g2_ref[...] = gates[:, LANES:]
    mt = _dot_hp(wgt_ref[...], h, dot=_dot_nt)
    for c in range(mt.shape[1] // CHUNK):
        piece = mt[:, c * CHUNK:(c + 1) * CHUNK]
        mt_ref[c] = jnp.concatenate([piece, piece], axis=1)


def _inproj(st, xp, xs, pos, mods, nw, w_main, w_gate, w_gate_t):
    tm = TM_IN
    assert st.tp % tm == 0 and st.ls % tm == 0
    ncb = st.tp // tm
    bps = st.ls // tm
    t = st.t
    return pl.pallas_call(
        functools.partial(_inproj_kernel, ncb=ncb),
        out_shape=(jax.ShapeDtypeStruct((t, D_MODEL), F32),
                   jax.ShapeDtypeStruct((t, _REC_MAIN), BF16),
                   jax.ShapeDtypeStruct((t, LANES), F32),
                   jax.ShapeDtypeStruct((t, LANES), F32),
                   jax.ShapeDtypeStruct((t // CHUNK, 16, 2 * CHUNK), F32)),
        grid=(t // tm,),
        in_specs=[
            pl.BlockSpec((tm, D_MODEL), lambda i: (jnp.minimum(i, ncb - 1), 0)),
            pl.BlockSpec((tm, D_MODEL), lambda i: (jnp.maximum(i - ncb, 0), 0)),
            pl.BlockSpec((tm, D_MODEL), lambda i: (jnp.maximum(i - ncb, 0) % bps, 0)),
            pl.BlockSpec((None, 1, N_MOD * D_MODEL), lambda i: (st.mod_row(i, tm), 0, 0)),
            pl.BlockSpec((1, D_MODEL), lambda i: (0, 0)),
            pl.BlockSpec((D_MODEL, _REC_MAIN), lambda i: (0, 0)),
            pl.BlockSpec((D_MODEL, 2 * LANES), lambda i: (0, 0)),
            pl.BlockSpec((16, D_MODEL), lambda i: (0, 0)),
        ],
        out_specs=(pl.BlockSpec((tm, D_MODEL), lambda i: (i, 0)),
                   pl.BlockSpec((tm, _REC_MAIN), lambda i: (i, 0)),
                   pl.BlockSpec((tm, LANES), lambda i: (i, 0)),
                   pl.BlockSpec((tm, LANES), lambda i: (i, 0)),
                   pl.BlockSpec((tm // CHUNK, 16, 2 * CHUNK), lambda i: (i, 0, 0))),
        compiler_params=_cparams("arbitrary"),
        name="rec_inproj",
    )(xp, xs, pos, mods, nw, w_main, w_gate, w_gate_t)


def _block_diag(x):
    left = lax.broadcasted_iota(jnp.int32, (1, x.shape[1]), 1) < x.shape[1] // 2
    zero = jnp.zeros_like(x)
    return jnp.concatenate([jnp.where(left, x, zero), jnp.where(left, zero, x)], axis=0)


def _block_diag_mask(rows, width):
    r = lax.broadcasted_iota(jnp.int32, (rows, width), 0) < rows // 2
    l = lax.broadcasted_iota(jnp.int32, (rows, width), 1) < width // 2
    return r == l


def _running_max(x, reverse):
    n = x.shape[0]
    row = lax.broadcasted_iota(jnp.int32, x.shape, 0)
    sh = 1
    while sh < n:
        if reverse:
            y = jnp.where(row < n - sh, pltpu.roll(x, n - sh, 0), -jnp.inf)
        else:
            y = jnp.where(row >= sh, pltpu.roll(x, sh, 0), -jnp.inf)
        x = jnp.maximum(x, y)
        sh *= 2
    return x


def _scan_kernel(*refs, seq):
    (main_ref, g_ref, g2_ref, mt_ref, x_ref, mod_ref, sg_ref, sc_ref, sn_ref, sm_ref,
     gkw_ref, gkb_ref, gbi_ref, gbf_ref, gbcol_ref, gnw_ref, mnw_ref, wout_ref, n2w_ref, wrt_ref, brt_ref) = refs[:21]
    (x1_ref, h2_ref, lg_ref, og_ref, oc_ref, on_ref, om_ref,
     oa_scr, ob_scr, sbd_scr, cbd_scr, nbd_scr, mgl_scr) = refs[-13:]
    c = CHUNK
    nchunks = seq // c
    npair = H_A // 2
    assert H_A == H_B and DK_A == DK_B == c and DV_A == DV_B == LANES and 2 * DK_A == LANES

    lane = lax.broadcasted_iota(jnp.int32, (1, LANES), 1)
    gate_lane = lambda d, h: _MG0 + d * 8 + h
    used = tuple((lane >= gate_lane(d, 0)) & (lane < gate_lane(d, H_B)) for d in (0, 1))

    for d in (0, 1):
        for p in range(npair):
            k = d * npair + p
            z_s = jnp.zeros((DV_A, DK_A), F32)
            sbd_scr[k] = jnp.concatenate([jnp.concatenate([sg_ref[d, 2 * p], z_s], axis=1),
                                          jnp.concatenate([z_s, sg_ref[d, 2 * p + 1]], axis=1)], axis=0)
            z_c = jnp.zeros((DK_B, DV_B), F32)
            cbd_scr[k] = jnp.concatenate([jnp.concatenate([sc_ref[d, 2 * p], z_c], axis=1),
                                          jnp.concatenate([z_c, sc_ref[d, 2 * p + 1]], axis=1)], axis=0)
            n_rep = [jnp.broadcast_to(sn_ref[d * H_B + 2 * p + q:d * H_B + 2 * p + q + 1, :], (DV_B, DK_B)).T
                     for q in (0, 1)]
            nbd_scr[k] = jnp.concatenate([jnp.concatenate([n_rep[0], z_c], axis=1),
                                          jnp.concatenate([z_c, n_rep[1]], axis=1)], axis=0)
    m_gl = jnp.zeros((1, LANES), F32)
    for d in (0, 1):
        for h in range(H_B):
            r = d * H_B + h
            m_gl = jnp.where(lane == gate_lane(d, h), sm_ref[:, r:r + 1], m_gl)
    mgl_scr[...] = m_gl

    row_p = lax.broadcasted_iota(jnp.int32, (c, LANES), 0)
    s_p = lax.broadcasted_iota(jnp.int32, (c, LANES), 1) % c
    live_p = (s_p <= row_p, s_p >= row_p)
    row = lax.broadcasted_iota(jnp.int32, (c, c), 0)
    col = lax.broadcasted_iota(jnp.int32, (c, c), 1)
    tri = tuple(jnp.where(m, 1.0, 0.0).astype(BF16) for m in (col <= row, col >= row))
    tri_t2 = tuple(jnp.where(m, 1.0, 0.0).astype(BF16) for m in (row_p <= s_p, row_p >= s_p))
    grow = lax.broadcasted_iota(jnp.int32, (16, 1), 0)
    f_row = (grow % 8) >= 4
    scale_q = DK_A ** -0.5
    bd_val = _block_diag_mask(2 * DV_A, 2 * DK_A)
    bd_key = _block_diag_mask(2 * DK_B, 2 * DV_B)
    ones_bd = jnp.where(bd_key, 1.0, 0.0).astype(BF16)
    ones_cv = jnp.ones((c, 2 * DV_B), BF16)

    def replicate(d, width):
        r = lax.broadcasted_iota(jnp.int32, (LANES, H_B * width), 0)
        h = lax.broadcasted_iota(jnp.int32, (LANES, H_B * width), 1) // width
        return jnp.where(r == gate_lane(d, 0) + h, 1.0, 0.0).astype(BF16)

    rep_k = tuple(replicate(d, DK_B) for d in (0, 1))
    rep_v = tuple(replicate(d, DV_B) for d in (0, 1))

    def chunk_step(i, carry):
        rows, g_in, g2_in, mt_in, gla_in, mls_in = [], [], [], [], [], []
        for d in (0, 1):
            ci = i if d == 0 else nchunks - 1 - i
            rows.append(pl.ds(pl.multiple_of(ci * c, c), c))
            g_in.append(g_ref[rows[d], :])
            g2_in.append(g2_ref[rows[d], :])
            mt_in.append(mt_ref[ci])
            gla_in.append(main_ref[rows[d], _QA:_RA])
            mls_in.append(main_ref[rows[d], _QB:_OB])
        s_bd = [sbd_scr[k] for k in range(2 * npair)]
        c_bd = [cbd_scr[k] for k in range(2 * npair)]
        n_bd = [nbd_scr[k] for k in range(2 * npair)]
        m_gl = mgl_scr[...]
        dirs = (0, 1)
        ends = (c - 1, 0)
        pairs = [(d, p) for d in dirs for p in range(npair)]
        ks_of = lambda p: slice(p * 2 * DK_A, (p + 1) * 2 * DK_A)
        vs_of = lambda p: slice(p * 2 * DV_A, (p + 1) * 2 * DV_A)

        glin = [_dot_hp(g_in[d], gkw_ref[d]) + gkb_ref[d] for d in dirs]
        gi = [jnp.where(used[d], g_in[d] + gbi_ref[...], 0.0) for d in dirs]
        lf = [jnp.where(used[d], jax.nn.log_sigmoid(g2_in[d] + gbf_ref[...]), 0.0) for d in dirs]
        mt = [mt_in[d] + gbcol_ref[...] for d in dirs]
        mt = [jnp.where(f_row, jax.nn.log_sigmoid(mt[d]), mt[d]) for d in dirs]
        glog = [jax.nn.log_sigmoid(glin[d]) / GATE_TEMP for d in dirs]
        bc = [_dot_mask_l(tri[d], glog[d], 2) for d in dirs]
        cum = [_dot_mask_l(tri[d], lf[d], 2) for d in dirs]
        cum_t = [_dot_mask_r(mt[d][:, 0:c], tri_t2[d], 2) for d in dirs]
        b_end = [bc[d][ends[d]:ends[d] + 1, :] for d in dirs]
        qa = [gla_in[d][:, _QA:_QA + 256].astype(F32) * scale_q for d in dirs]
        ka = [gla_in[d][:, _KA:_KA + 256].astype(F32) for d in dirs]
        va = [gla_in[d][:, _VA:_VA + 512] for d in dirs]
        qe = [_bf(qa[d] * jnp.exp(bc[d])) for d in dirs]
        ke = [_bf(ka[d] * jnp.exp(-bc[d])) for d in dirs]
        kd = [_bf(ka[d] * jnp.exp(b_end[d] - bc[d])) for d in dirs]
        eb_end = [jnp.exp(b_end[d]) for d in dirs]
        qb = [mls_in[d][:, 0:256] * jnp.asarray(DK_B ** -0.5, BF16) for d in dirs]
        kb = [mls_in[d][:, _KB - _QB:_KB - _QB + 256] for d in dirs]
        vb = [mls_in[d][:, _VB - _QB:_VB - _QB + 512] for d in dirs]
        a_raw = {(d, p): _dot_nt(qe[d][:, ks_of(p)], _block_diag(ke[d][:, ks_of(p)])) for d, p in pairs}
        qk = {(d, p): _dot_nt(qb[d][:, ks_of(p)], _block_diag(kb[d][:, ks_of(p)])) for d, p in pairs}
        s_upd = {(d, p): _dot_tn(va[d][:, vs_of(p)], kd[d][:, ks_of(p)]) for d, p in pairs}
        o_car = {(d, p): _dot_nt(qe[d][:, ks_of(p)], _bf(s_bd[d * npair + p])) for d, p in pairs}
        m_loc = [cum[d] + _running_max(gi[d] - cum[d], reverse=(d == 1)) for d in dirs]
        inter = [cum[d] + m_gl for d in dirs]
        m_t = [jnp.maximum(inter[d], m_loc[d]) for d in dirs]
        b_last = [cum[d][ends[d]:ends[d] + 1, :] for d in dirs]
        dend = [b_last[d] - cum[d] + gi[d] for d in dirs]
        m_new = [jnp.maximum(b_last[d] + m_gl, jnp.max(dend[d], axis=0, keepdims=True)) for d in dirs]
        zero = jnp.zeros((c, LANES), F32)
        per_key = [jnp.concatenate([jnp.where(used[d], cum[d] - m_t[d], zero),
                                    jnp.where(used[d], jnp.exp(inter[d] - m_t[d]), zero),
                                    jnp.where(used[d], jnp.exp(dend[d] - m_new[d]), zero)], axis=0) for d in dirs]
        per_val = [jnp.concatenate([jnp.where(used[d], jnp.exp(-m_t[d]), zero),
                                    jnp.broadcast_to(jnp.where(used[d], jnp.exp(b_last[d] + m_gl - m_new[d]), 0.0),
                                                     (SUBLANES, LANES))], axis=0) for d in dirs]
        per_key = [_dot_mask_r(per_key[d], rep_k[d], 2) for d in dirs]
        per_val = [_dot_mask_r(per_val[d], rep_v[d], 2) for d in dirs]
        o_par = {(d, p): _dot(_bf(jnp.where(live_p[d], a_raw[d, p], 0.0)), _block_diag(va[d][:, vs_of(p)]))
                 for d, p in pairs}
        for d, p in pairs:
            k = d * npair + p
            s_bd[k] = s_bd[k] * eb_end[d][:, ks_of(p)] + jnp.where(bd_val, s_upd[d, p], 0.0)
        o_gla = [jnp.concatenate([o_par[d, p] + o_car[d, p] for p in range(npair)], axis=-1) for d in dirs]
        for d in dirs:
            m_gl = jnp.where(used[d], m_new[d], m_gl)

        w, qa2, kw = {}, {}, {}
        for d, p in pairs:
            ks = ks_of(p)
            r_i, r_f = d * 8 + 2 * p, d * 8 + 4 + 2 * p
            sub = [cum_t[d][r_f + q:r_f + q + 1, :] - mt[d][r_i + q:r_i + q + 1, :] for q in (0, 1)]
            sub = jnp.where(lane < DK_B, sub[0], sub[1])
            w[d, p] = jnp.exp(jnp.where(live_p[d], per_key[d][0:c, ks] - sub, -jnp.inf))
            qa2[d, p] = _bf(qb[d][:, ks].astype(F32) * per_key[d][c:2 * c, ks])
            kw[d, p] = _bf(kb[d][:, ks].astype(F32) * per_key[d][2 * c:3 * c, ks])
        carried = {(d, p): _dot(qa2[d, p], jnp.concatenate([_bf(c_bd[d * npair + p]), _bf(n_bd[d * npair + p])], axis=1))
                   for d, p in pairs}
        upd = {(d, p): _dot_tn(kw[d, p], jnp.concatenate([vb[d][:, vs_of(p)], ones_cv], axis=1)) for d, p in pairs}
        intra = {(d, p): _dot(_bf(qk[d, p] * w[d, p]),
                              jnp.concatenate([_block_diag(vb[d][:, vs_of(p)]), ones_bd], axis=1)) for d, p in pairs}
        outs = {}
        for d, p in pairs:
            k = d * npair + p
            vs = vs_of(p)
            num = intra[d, p][:, :2 * DV_B] + carried[d, p][:, :2 * DV_B]
            den = intra[d, p][:, 2 * DV_B:] + carried[d, p][:, 2 * DV_B:]
            outs[d, p] = num / jnp.maximum(jnp.abs(den), per_val[d][0:c, vs])
            decay = per_val[d][c:c + 1, vs]
            dec = jnp.concatenate([jnp.broadcast_to(jnp.concatenate([decay[:, q * DV_B:(q + 1) * DV_B]] * 2, axis=1),
                                                    (DK_B, 2 * DV_B)) for q in (0, 1)], axis=0)
            c_bd[k] = dec * c_bd[k] + jnp.where(bd_key, upd[d, p][:, :2 * DV_B], 0.0)
            n_bd[k] = dec * n_bd[k] + jnp.where(bd_key, upd[d, p][:, 2 * DV_B:], 0.0)
        o_mls = [jnp.concatenate([outs[d, p] for p in range(npair)], axis=-1) for d in dirs]

        for d in (0, 1):
            oa_scr[d, rows[d], :] = o_gla[d]
            ob_scr[d, rows[d], :] = o_mls[d]
        for k in range(2 * npair):
            sbd_scr[k] = s_bd[k]
            cbd_scr[k] = c_bd[k]
            nbd_scr[k] = n_bd[k]
        mgl_scr[...] = m_gl
        return carry

    lax.fori_loop(0, nchunks, chunk_step, 0)

    for d in (0, 1):
        for p in range(npair):
            k = d * npair + p
            s_t, c_f, n_f = sbd_scr[k], cbd_scr[k], nbd_scr[k]
            for q in (0, 1):
                h = 2 * p + q
                og_ref[d, h] = s_t[q * DV_A:(q + 1) * DV_A, q * DK_A:(q + 1) * DK_A]
                oc_ref[d, h] = c_f[q * DK_B:(q + 1) * DK_B, q * DV_B:(q + 1) * DV_B]
                n_t = n_f[q * DK_B:(q + 1) * DK_B, q * DV_B:(q + 1) * DV_B].T
                on_ref[d * H_B + h:d * H_B + h + 1, :] = n_t[0:1, :]
    m_gl = mgl_scr[...]
    om_ref[...] = jnp.concatenate([m_gl[:, gate_lane(d, h):gate_lane(d, h) + 1]
                                   for d in (0, 1) for h in range(H_B)], axis=1)

    mod = _mod_slices(mod_ref[...])
    g1, sh2, sc2 = mod[2], mod[3], mod[4]

    def out_step(j, carry):
        hm = TM // 2
        halves = (0, 1)
        r0 = [pl.multiple_of(j * TM + q * hm, hm) for q in halves]
        rows = [pl.ds(r0[q], hm) for q in halves]
        parts = [[], []]
        for h in range(H_A):
            vs = slice(h * DV_A, (h + 1) * DV_A)
            for q in halves:
                ra = main_ref[rows[q], _RA + h * DV_A:_RA + (h + 1) * DV_A].astype(F32)
                o = oa_scr[0, rows[q], vs] + oa_scr[1, rows[q], vs]
                parts[q].append(_rms(o, gnw_ref[...]) * (ra * jax.nn.sigmoid(ra)))
        for h in range(H_B):
            vs = slice(h * DV_B, (h + 1) * DV_B)
            for q in halves:
                ob = main_ref[rows[q], _OB + h * DV_B:_OB + (h + 1) * DV_B].astype(F32)
                o = ob_scr[0, rows[q], vs] + ob_scr[1, rows[q], vs]
                parts[q].append(_rms(o, mnw_ref[...]) * jax.nn.sigmoid(ob))
        mix = [_dot(_bf(jnp.concatenate(parts[q], axis=-1)), wout_ref[...]) for q in halves]
        x1 = [x_ref[rows[q], :] + g1 * mix[q] for q in halves]
        h2 = [_rms(x1[q], n2w_ref[...]) * (1.0 + sc2) + sh2 for q in halves]
        lg = [_router_logits(h2[q], wrt_ref, brt_ref) for q in halves]
        for q in halves:
            x1_ref[rows[q], :] = x1[q]
            _store_rows_per_vreg(h2_ref.at[pl.ds(pl.multiple_of(r0[q] * _RPV, hm * _RPV), hm * _RPV)], h2[q])
        lg_ref[j] = jnp.concatenate(lg, axis=1)
        return carry

    lax.fori_loop(0, seq // TM, out_step, 0)


def _scan(st, ctx, main, g, g2, mt, x, mods, states, consts, prev):
    seq = st.lp if ctx else st.ls
    nb = st.bp if ctx else st.bs
    off = 0 if ctx else st.tp // st.ls
    sg, sc, sn, sm = states
    full = lambda shape: pl.BlockSpec(shape, lambda b: (0,) * len(shape))
    in_specs = [
        pl.BlockSpec((seq, _REC_MAIN), lambda b: (off + b, 0)),
        pl.BlockSpec((seq, LANES), lambda b: (off + b, 0)),
        pl.BlockSpec((seq, LANES), lambda b: (off + b, 0)),
        pl.BlockSpec((seq // CHUNK, 16, 2 * CHUNK), lambda b: (off + b, 0, 0)),
        pl.BlockSpec((seq, D_MODEL), lambda b: (off + b, 0)),
    ]
    args = [main, g, g2, mt, x]
    mod_row = (lambda b: (0, 0, 0)) if ctx else (lambda b: (1 + b, 0, 0))
    in_specs += [
        pl.BlockSpec((None, 1, N_MOD * D_MODEL), mod_row),
        pl.BlockSpec((None, 2, H_A, DV_A, DK_A), lambda b: (b, 0, 0, 0, 0)),
        pl.BlockSpec((None, 2, H_B, DK_B, DV_B), lambda b: (b, 0, 0, 0, 0)),
        pl.BlockSpec((None, 2 * H_B, DK_B), lambda b: (b, 0, 0)),
        pl.BlockSpec((None, 1, 2 * H_B), lambda b: (b, 0, 0)),
    ]
    args += [mods, sg, sc, sn, sm]
    for a in consts:
        in_specs.append(full(a.shape))
        args.append(a)
    aliases = {}
    if prev is not None:
        aliases = {len(args) + k: k for k in range(3)}
        in_specs += [pl.BlockSpec(memory_space=pl.ANY)] * 3
        args += list(prev)
    out_shape = (jax.ShapeDtypeStruct((st.t, D_MODEL), F32),
                 jax.ShapeDtypeStruct((st.t * _RPV, LANES), F32),
                 jax.ShapeDtypeStruct((st.t // TM, LG_ROWS, TM), F32),
                 jax.ShapeDtypeStruct(sg.shape, F32), jax.ShapeDtypeStruct(sc.shape, F32),
                 jax.ShapeDtypeStruct(sn.shape, F32), jax.ShapeDtypeStruct(sm.shape, F32))
    out_specs = (pl.BlockSpec((seq, D_MODEL), lambda b: (off + b, 0)),
                 pl.BlockSpec((seq * _RPV, LANES), lambda b: (off + b, 0)),
                 pl.BlockSpec((seq // TM, LG_ROWS, TM), lambda b: (off + b, 0, 0)),
                 pl.BlockSpec((None, 2, H_A, DV_A, DK_A), lambda b: (b, 0, 0, 0, 0)),
                 pl.BlockSpec((None, 2, H_B, DK_B, DV_B), lambda b: (b, 0, 0, 0, 0)),
                 pl.BlockSpec((None, 2 * H_B, DK_B), lambda b: (b, 0, 0)),
                 pl.BlockSpec((None, 1, 2 * H_B), lambda b: (b, 0, 0)))
    return pl.pallas_call(
        functools.partial(_scan_kernel, seq=seq),
        out_shape=out_shape,
        grid=(nb,),
        in_specs=in_specs,
        out_specs=out_specs,
        scratch_shapes=[pltpu.VMEM((2, seq, H_A * DV_A), F32), pltpu.VMEM((2, seq, H_B * DV_B), F32),
                        pltpu.VMEM((H_A, 2 * DV_A, 2 * DK_A), F32), pltpu.VMEM((H_B, 2 * DK_B, 2 * DV_B), F32),
                        pltpu.VMEM((H_B, 2 * DK_B, 2 * DV_B), F32), pltpu.VMEM((1, LANES), F32)],
        input_output_aliases=aliases,
        compiler_params=_cparams("arbitrary"),
        name="rec_scan_ctx" if ctx else "rec_scan_lat",
    )(*args)


def _first_max(rows):
    m = rows[0]
    for r in rows[1:]:
        m = jnp.maximum(m, r)
    idx = jnp.full(m.shape, len(rows) - 1, jnp.int32)
    for k in range(len(rows) - 2, -1, -1):
        idx = jnp.where(rows[k] == m, k, idx)
    return m, idx


def _route_kernel(lg_ref, pos1_ref, pos2_ref, w1_ref, w2_ref, tab_ref, *, sb):
    lg = jnp.concatenate([lg_ref[b] for b in range(sb // TM)], axis=1)
    rows = [lg[k:k + 1, :] for k in range(N_EXPERTS + N_GROUPS)]
    grp = rows[N_EXPERTS:]
    gmax, gidx = _first_max(grp)
    p_group = 1.0 / sum(jnp.exp(r - gmax) for r in grp)
    e_in = []
    for k in range(EXP_PER_GROUP):
        v = rows[(N_GROUPS - 1) * EXP_PER_GROUP + k]
        for g in range(N_GROUPS - 2, -1, -1):
            v = jnp.where(gidx == g, rows[g * EXP_PER_GROUP + k], v)
        e_in.append(v)
    v1, i1 = _first_max(e_in)
    v2, i2 = _first_max([jnp.where(i1 == k, -jnp.inf, e_in[k]) for k in range(EXP_PER_GROUP)])
    ex = jnp.exp(v2 - v1)
    w1_ref[...] = p_group / (1.0 + ex)
    w2_ref[...] = p_group * ex / (1.0 + ex)
    x1 = gidx * EXP_PER_GROUP + i1
    x2 = gidx * EXP_PER_GROUP + i2

    eid = lax.broadcasted_iota(jnp.int32, (N_EXPERTS, sb), 0)
    sel = jnp.where((eid == x1) | (eid == x2), 1.0, 0.0)
    r_i = lax.broadcasted_iota(jnp.int32, (TM, TM), 0)
    c_i = lax.broadcasted_iota(jnp.int32, (TM, TM), 1)
    before = jnp.where(r_i < c_i, 1.0, 0.0).astype(BF16)
    carry = jnp.zeros((N_EXPERTS, 1), F32)
    ranks = []
    for b in range(sb // TM):
        s_b = sel[:, b * TM:(b + 1) * TM]
        ranks.append(_dot(_bf(s_b), before) + carry)
        carry = carry + jnp.sum(s_b, axis=1, keepdims=True)
    rank = jnp.concatenate(ranks, axis=1)
    shift = ROW_PAD.bit_length() - 1
    npad = jnp.left_shift(jnp.right_shift(carry.astype(jnp.int32) + (ROW_PAD - 1), shift), shift)

    lane = lax.broadcasted_iota(jnp.int32, (1, LANES), 1)
    tab = jnp.zeros((1, LANES), jnp.int32)
    pos1 = jnp.zeros((1, sb), F32)
    pos2 = jnp.zeros((1, sb), F32)
    off = jnp.zeros((1, 1), jnp.int32)
    for e in range(N_EXPERTS):
        n_e = npad[e:e + 1, :]
        tab = jnp.where(lane == e, off, tab)
        tab = jnp.where(lane == N_EXPERTS + e, n_e, tab)
        row = off.astype(F32) + rank[e:e + 1, :]
        pos1 = jnp.where(x1 == e, row, pos1)
        pos2 = jnp.where(x2 == e, row, pos2)
        off = off + n_e
    pos1_ref[...] = pos1.astype(jnp.int32) * _RPV
    pos2_ref[...] = pos2.astype(jnp.int32) * _RPV
    tab_ref[...] = tab


def _route(st, lg):
    sb = st.sb
    nsb = st.t // sb
    row_i = jax.ShapeDtypeStruct((nsb, 1, sb), jnp.int32)
    row_f = jax.ShapeDtypeStruct((nsb, 1, sb), F32)
    rspec = pl.BlockSpec((None, 1, sb), lambda s: (s, 0, 0))
    return pl.pallas_call(
        functools.partial(_route_kernel, sb=sb),
        out_shape=(row_i, row_i, row_f, row_f, jax.ShapeDtypeStruct((nsb, 1, LANES), jnp.int32)),
        grid=(nsb,),
        in_specs=[pl.BlockSpec((sb // TM, LG_ROWS, TM), lambda s: (s, 0, 0))],
        out_specs=(rspec, rspec, rspec, rspec, pl.BlockSpec((None, 1, LANES), lambda s: (s, 0, 0))),
        compiler_params=_cparams("arbitrary"),
        name="moe_route",
    )(lg)


def _moe_rows(sb):
    return 2 * sb + N_EXPERTS * ROW_PAD + FFN_TILE


def _moe_kernel(pos1_ref, pos2_ref, w1_ref, w2_ref, tab_ref, xr_ref, wg_ref, wu_ref, wd_ref, o_ref,
                rows_scr, stage_scr, *, sb):
    s = pl.program_id(0)
    e = pl.program_id(1)

    def tile_at(ref, r8):
        return ref.at[pl.ds(pl.multiple_of(r8, _RPV), _RPV)]

    @pl.when((s == 0) & (e == 0))
    def _():
        rows_scr[...] = jnp.zeros_like(rows_scr)

    @pl.when(e == 0)
    def _():
        def dispatch(t, carry):
            v = tile_at(xr_ref, t * _RPV)[...]
            tile_at(rows_scr, pos1_ref[0, t])[...] = v
            tile_at(rows_scr, pos2_ref[0, t])[...] = v
            return carry

        lax.fori_loop(0, sb, dispatch, 0, unroll=8)

    def expert(k):
        ex = e * EXPERTS_PER_STEP + k

        def ffn_tile(r0, m, valid=None):
            win = rows_scr.at[pl.ds(pl.multiple_of(r0 * _RPV, ROW_PAD * _RPV), m * _RPV)]
            x = _load_rows_per_vreg(win, m)
            xb = _bf(x)
            y = None
            for c0 in range(0, D_EXPERT, D_EXPERT // 2):
                cs = slice(c0, c0 + D_EXPERT // 2)
                hg = _dot(xb, wg_ref[k, :, cs])
                hu = _dot(xb, wu_ref[k, :, cs])
                part = _dot(_bf(hg * jax.nn.sigmoid(hg) * hu), wd_ref[k, cs, :])
                y = part if y is None else y + part
            if valid is not None:
                y = jnp.where(lax.broadcasted_iota(jnp.int32, (m, 1), 0) < valid, y, x)
            _store_rows_per_vreg(win, y)

        off = tab_ref[0, ex]
        npad = tab_ref[0, N_EXPERTS + ex]
        nfull = npad // FFN_TILE

        def full_tile(i, carry):
            ffn_tile(off + i * FFN_TILE, FFN_TILE)
            return carry

        lax.fori_loop(0, nfull, full_tile, 0)
        rem = npad - nfull * FFN_TILE
        last = off + nfull * FFN_TILE

        @pl.when((rem > 0) & (rem <= FFN_TILE // 2))
        def _():
            ffn_tile(last, FFN_TILE // 2, valid=rem)

        @pl.when(rem > FFN_TILE // 2)
        def _():
            ffn_tile(last, FFN_TILE, valid=rem)

    for k in range(EXPERTS_PER_STEP):
        expert(k)

    @pl.when(e == N_EXPERTS // EXPERTS_PER_STEP - 1)
    def _():
        for c in range(sb // TM):
            def combine(t, carry, c=c):
                tt = c * TM + t
                y = (w1_ref[0, tt] * tile_at(rows_scr, pos1_ref[0, tt])[...]
                     + w2_ref[0, tt] * tile_at(rows_scr, pos2_ref[0, tt])[...])
                tile_at(stage_scr, t * _RPV)[...] = y
                return carry

            lax.fori_loop(0, TM, combine, 0, unroll=8)
            o_ref[c * TM:(c + 1) * TM, :] = _bf(_load_rows_per_vreg(stage_scr, TM))


def _cast_kernel(x_ref, o_ref):
    o_ref[...] = _bf(x_ref[...])


def _expert_weights_bf16(w):
    depth, ne, a, b = w.shape
    per_step = 4
    assert (depth * ne) % per_step == 0
    spec = pl.BlockSpec((per_step, a, b), lambda i: (i, 0, 0))
    return pl.pallas_call(
        _cast_kernel,
        out_shape=jax.ShapeDtypeStruct((depth * ne, a, b), BF16),
        grid=(depth * ne // per_step,),
        in_specs=[spec],
        out_specs=spec,
        compiler_params=_cparams("arbitrary"),
        name="expert_weight_cast",
    )(w.reshape(depth * ne, a, b))


def _moe(st, layer, xr, route, wg, wu, wd):
    sb = st.sb
    eps = EXPERTS_PER_STEP
    b0 = layer * N_EXPERTS // eps
    smem = lambda n: pl.BlockSpec((None, 1, n), lambda s, e: (s, 0, 0), memory_space=pltpu.SMEM)
    return pl.pallas_call(
        functools.partial(_moe_kernel, sb=sb),
        out_shape=jax.ShapeDtypeStruct((st.t, D_MODEL), BF16),
        grid=(st.t // sb, N_EXPERTS // eps),
        in_specs=[
            smem(sb), smem(sb), smem(sb), smem(sb), smem(LANES),
            pl.BlockSpec((sb * _RPV, LANES), lambda s, e: (s, 0)),
            pl.BlockSpec((eps, D_MODEL, D_EXPERT), lambda s, e: (b0 + e, 0, 0)),
            pl.BlockSpec((eps, D_MODEL, D_EXPERT), lambda s, e: (b0 + e, 0, 0)),
            pl.BlockSpec((eps, D_EXPERT, D_MODEL), lambda s, e: (b0 + e, 0, 0)),
        ],
        out_specs=pl.BlockSpec((sb, D_MODEL), lambda s, e: (s, 0)),
        scratch_shapes=[pltpu.VMEM((_moe_rows(sb) * _RPV, LANES), F32), pltpu.VMEM((TM * _RPV, LANES), F32)],
        compiler_params=_cparams("arbitrary", "arbitrary"),
        name="moe_ffn",
    )(*route, xr, wg, wu, wd)


def _hy_inproj_kernel(x_ref, m_ref, mod0_ref, mod_ref, nw_ref, w_ref, x2_ref, z_ref):
    g2 = _mod_slices(mod0_ref[...])[5]
    x2 = x_ref[...] + g2 * m_ref[...].astype(F32)
    x2_ref[...] = x2
    sh1, sc1 = _mod_slices(mod_ref[...])[:2]
    h = _rms(x2, nw_ref[...]) * (1.0 + sc1) + sh1
    z_ref[...] = _bf(_dot(_bf(h), w_ref[...]))


def _hy_inproj(st, x, moe, mods_prev, mods, nw, w):
    n = w.shape[1]
    tm = TM_TOK
    mspec = pl.BlockSpec((None, 1, N_MOD * D_MODEL), lambda i: (st.mod_row(i, tm), 0, 0))
    return pl.pallas_call(
        _hy_inproj_kernel,
        out_shape=(jax.ShapeDtypeStruct((st.t, D_MODEL), F32), jax.ShapeDtypeStruct((st.t, n), BF16)),
        grid=(st.t // tm,),
        in_specs=[
            pl.BlockSpec((tm, D_MODEL), lambda i: (i, 0)),
            pl.BlockSpec((tm, D_MODEL), lambda i: (i, 0)),
            mspec, mspec,
            pl.BlockSpec((1, D_MODEL), lambda i: (0, 0)),
            pl.BlockSpec((D_MODEL, n), lambda i: (0, 0)),
        ],
        out_specs=(pl.BlockSpec((tm, D_MODEL), lambda i: (i, 0)), pl.BlockSpec((tm, n), lambda i: (i, 0))),
        compiler_params=_cparams("arbitrary"),
        name="hy_inproj",
    )(x, moe, mods_prev, mods, nw, w)


def _dft_tables(seq):
    n2 = 2 * seq
    assert n2 & (n2 - 1) == 0
    tr = min(seq, TM)

    def table_kernel(cos_ref, msin_ref, cosb_ref, msinb_ref):
        k = lax.broadcasted_iota(jnp.int32, (tr, seq), 0) + pl.program_id(0) * tr
        n = lax.broadcasted_iota(jnp.int32, (tr, seq), 1)
        ang = ((k * n) & (n2 - 1)).astype(F32) * (2.0 * math.pi / n2)
        c, s = jnp.cos(ang), -jnp.sin(ang)
        cos_ref[...] = c
        msin_ref[...] = s
        cosb_ref[...] = _bf(c)
        msinb_ref[...] = _bf(s)

    spec = pl.BlockSpec((tr, seq), lambda i: (i, 0))
    f32, b16 = jax.ShapeDtypeStruct((seq, seq), F32), jax.ShapeDtypeStruct((seq, seq), BF16)
    return pl.pallas_call(table_kernel, out_shape=(f32, f32, b16, b16), grid=(seq // tr,),
                          out_specs=(spec, spec, spec, spec), compiler_params=_cparams("arbitrary"),
                          name=f"dft_tables_{seq}")()


def _alternating(shape):
    return jnp.where(lax.broadcasted_iota(jnp.int32, shape, 0) % 2 == 0, 1.0, -1.0)


def _hy_filter_kernel(emb_ref, dec_ref, w1_ref, b1_ref, f1_ref, w2_ref, b2_ref, f2_ref, w3_ref,
                      cos_ref, msin_ref, kr_ref, ki_ref, kn_ref, *, seq):
    h = jnp.sin(f1_ref[...] * (_dot_hp(emb_ref[...], w1_ref[...]) + b1_ref[...]))
    h = jnp.sin(f2_ref[...] * (_dot_hp(h, w2_ref[...]) + b2_ref[...]))
    dec = dec_ref[...]
    row0 = lax.broadcasted_iota(jnp.int32, dec.shape, 0) == 0
    alt = _alternating(dec.shape)
    cos, msin = cos_ref[...], msin_ref[...]
    scale = jnp.where(row0, 1.0, 2.0) / (2 * seq)
    for o in range(HY_ORDER):
        h_f = _dot_hp(h, w3_ref[:, 2 * o, :]) * dec
        h_b = jnp.where(row0, 0.0, _dot_hp(h, w3_ref[:, 2 * o + 1, :]) * dec)
        kn_ref[o] = jnp.sum(alt * (h_f + h_b), axis=0, keepdims=True) / (2 * seq)
        kr_ref[o] = _dot_hp(cos, h_f + h_b) * scale
        ki_ref[o] = _dot_hp(msin, h_f - h_b) * scale


def _hy_filter(seq, w1, b1, f1, w2, b2, f2, w3, cos, msin):
    t = jnp.linspace(0.0, 1.0, seq, dtype=F32)[:, None]
    w = 2.0 * math.pi * jnp.arange(seq, dtype=F32)[:, None] / seq
    f = jnp.linspace(1e-4, HY_BANDS - 1, HY_BANDS, dtype=F32)[None, :]
    emb = jnp.concatenate([t, jnp.cos(f * w), -jnp.sin(f * w), jnp.zeros((seq, LANES - HY_EMB), F32)], axis=-1)
    decay = jnp.exp(-t * jnp.linspace(HY_MIN_DECAY, HY_MAX_DECAY, D_MODEL, dtype=F32)[None, :])
    w1p = jnp.concatenate([w1, jnp.zeros((LANES - HY_EMB, HY_FFN), F32)], axis=0)
    dblk = 256
    out = jax.ShapeDtypeStruct((HY_ORDER, seq, D_MODEL), F32)
    full = lambda shape: pl.BlockSpec(shape, lambda j: (0,) * len(shape))
    ospec = pl.BlockSpec((HY_ORDER, seq, dblk), lambda j: (0, 0, j))
    return pl.pallas_call(
        functools.partial(_hy_filter_kernel, seq=seq),
        out_shape=(out, out, jax.ShapeDtypeStruct((HY_ORDER, 1, D_MODEL), F32)),
        grid=(D_MODEL // dblk,),
        in_specs=[
            full((seq, LANES)),
            pl.BlockSpec((seq, dblk), lambda j: (0, j)),
            full((LANES, HY_FFN)), full((1, HY_FFN)), full((1, HY_FFN)),
            full((HY_FFN, HY_FFN)), full((1, HY_FFN)), full((1, HY_FFN)),
            pl.BlockSpec((HY_FFN, 2 * HY_ORDER, dblk), lambda j: (0, 0, j)),
            full((seq, seq)), full((seq, seq)),
        ],
        out_specs=(ospec, ospec, pl.BlockSpec((HY_ORDER, 1, dblk), lambda j: (0, 0, j))),
        compiler_params=_cparams("arbitrary"),
        name=f"hy_filter_{seq}",
    )(emb, decay, w1p, b1[None, :], f1[None, :], w2, b2[None, :], f2[None, :],
      w3.reshape(HY_FFN, 2 * HY_ORDER, D_MODEL), cos, msin)


def _hy_conv_kernel(*refs, seq, nseq):
    (zv_ref, z1_ref, z2_ref, cv_ref, c1_ref, c2_ref, kr_ref, ki_ref, kn_ref, bias_ref,
     cos_ref, msin_ref) = refs[:12]
    y_ref = refs[-1]
    dblk = y_ref.shape[1]
    t = lax.broadcasted_iota(jnp.int32, (seq, dblk), 0)
    first, last = t == 0, t == seq - 1
    alt = _alternating((seq, dblk))
    seqs = range(nseq)
    rows = [slice(q * seq, (q + 1) * seq) for q in seqs]

    def short_conv(z_ref, c_ref):
        out = []
        for q in seqs:
            z = z_ref[rows[q], :].astype(F32)
            prev = jnp.where(first, 0.0, pltpu.roll(z, 1, 0))
            nxt = jnp.where(last, 0.0, pltpu.roll(z, seq - 1, 0))
            out.append(c_ref[0:1, :] * prev + c_ref[1:2, :] * z + c_ref[2:3, :] * nxt)
        return out

    def long_conv(s, o):
        sb = [_bf(s[q]) for q in seqs]
        x_re = [_dot(cos_ref[...], sb[q]) for q in seqs]
        x_im = [_dot(msin_ref[...], sb[q]) for q in seqs]
        y_nyq = [jnp.sum(alt * s[q], axis=0, keepdims=True) * kn_ref[o] for q in seqs]
        k_re, k_im = kr_ref[o], ki_ref[o]
        y_re = [x_re[q] * k_re - x_im[q] * k_im for q in seqs]
        y_im = [x_re[q] * k_im + x_im[q] * k_re for q in seqs]
        y = [_dot(cos_ref[...], _bf(y_re[q])) + _dot(msin_ref[...], _bf(y_im[q])) for q in seqs]
        return [y[q] + alt * y_nyq[q] + bias_ref[o] * s[q] for q in seqs]

    v = short_conv(zv_ref, cv_ref)
    x1 = short_conv(z1_ref, c1_ref)
    x2 = short_conv(z2_ref, c2_ref)
    c1 = long_conv(v, 0)
    c2 = long_conv([x1[q] * c1[q] for q in seqs], 1)
    for q in seqs:
        y_ref[rows[q], :] = _bf(x2[q] * c2[q])


def _hy_conv(st, ctx, z, conv_w, filt, bias, tables, prev):
    seq = st.lp if ctx else st.ls
    nb = st.bp if ctx else st.bs
    nseq = max(1, min(nb, 4))
    assert nb % nseq == 0
    off = 0 if ctx else st.tp // (nseq * seq)
    dblk = 256
    nd = D_MODEL // dblk
    kr, ki, kn = filt
    cos, msin = tables
    vspec = pl.BlockSpec((HY_ORDER, 1, dblk), lambda j, b: (0, 0, j))
    full = lambda shape: pl.BlockSpec(shape, lambda j, b: (0,) * len(shape))
    kspec = pl.BlockSpec((HY_ORDER, seq, dblk), lambda j, b: (0, 0, j))
    in_specs = [
        pl.BlockSpec((nseq * seq, dblk), lambda j, b: (off + b, j)),
        pl.BlockSpec((nseq * seq, dblk), lambda j, b: (off + b, nd + j)),
        pl.BlockSpec((nseq * seq, dblk), lambda j, b: (off + b, 2 * nd + j)),
        pl.BlockSpec((3, dblk), lambda j, b: (0, j)),
        pl.BlockSpec((3, dblk), lambda j, b: (0, nd + j)),
        pl.BlockSpec((3, dblk), lambda j, b: (0, 2 * nd + j)),
        kspec, kspec, vspec, vspec,
        full((seq, seq)), full((seq, seq)),
    ]
    args = [z, z, z, conv_w, conv_w, conv_w, kr, ki, kn, bias.reshape(HY_ORDER, 1, D_MODEL), cos, msin]
    aliases = {}
    if prev is not None:
        aliases = {len(args): 0}
        in_specs.append(pl.BlockSpec(memory_space=pl.ANY))
        args.append(prev)
    return pl.pallas_call(
        functools.partial(_hy_conv_kernel, seq=seq, nseq=nseq),
        out_shape=jax.ShapeDtypeStruct((st.t, D_MODEL), BF16),
        grid=(nd, nb // nseq),
        in_specs=in_specs,
        out_specs=pl.BlockSpec((nseq * seq, dblk), lambda j, b: (off + b, j)),
        input_output_aliases=aliases,
        compiler_params=_cparams("arbitrary", "arbitrary"),
        name="hy_conv_ctx" if ctx else "hy_conv_lat",
    )(*args)


def _hy_outproj_kernel(y_ref, x_ref, mod_ref, w_ref, n2w_ref, wrt_ref, brt_ref, x3_ref, h2_ref, lg_ref):
    mod = _mod_slices(mod_ref[...])
    g1, sh2, sc2 = mod[2], mod[3], mod[4]
    nt = y_ref.shape[0] // TM
    tiles = range(nt)
    rows = [slice(b * TM, (b + 1) * TM) for b in tiles]
    mix = [_dot(y_ref[rows[b], :], w_ref[...]) for b in tiles]
    x3 = [x_ref[rows[b], :] + g1 * mix[b] for b in tiles]
    h2 = [_rms(x3[b], n2w_ref[...]) * (1.0 + sc2) + sh2 for b in tiles]
    lg = [_router_logits(h2[b], wrt_ref, brt_ref) for b in tiles]
    for b in tiles:
        x3_ref[rows[b], :] = x3[b]
        _store_rows_per_vreg(h2_ref.at[pl.ds(b * TM * _RPV, TM * _RPV)], h2[b])
        lg_ref[b] = lg[b]


def _hy_outproj(st, y, x, mods, w, n2w, wrt, brt):
    tm = TM_TOK
    return pl.pallas_call(
        _hy_outproj_kernel,
        out_shape=(jax.ShapeDtypeStruct((st.t, D_MODEL), F32),
                   jax.ShapeDtypeStruct((st.t * _RPV, LANES), F32),
                   jax.ShapeDtypeStruct((st.t // TM, LG_ROWS, TM), F32)),
        grid=(st.t // tm,),
        in_specs=[
            pl.BlockSpec((tm, D_MODEL), lambda i: (i, 0)),
            pl.BlockSpec((tm, D_MODEL), lambda i: (i, 0)),
            pl.BlockSpec((None, 1, N_MOD * D_MODEL), lambda i: (st.mod_row(i, tm), 0, 0)),
            pl.BlockSpec((D_MODEL, D_MODEL), lambda i: (0, 0)),
            pl.BlockSpec((1, D_MODEL), lambda i: (0, 0)),
            pl.BlockSpec((LG_ROWS, D_MODEL), lambda i: (0, 0)),
            pl.BlockSpec((LG_ROWS, 1), lambda i: (0, 0)),
        ],
        out_specs=(pl.BlockSpec((tm, D_MODEL), lambda i: (i, 0)),
                   pl.BlockSpec((tm * _RPV, LANES), lambda i: (i, 0)),
                   pl.BlockSpec((tm // TM, LG_ROWS, TM), lambda i: (i, 0, 0))),
        compiler_params=_cparams("arbitrary"),
        name="hy_outproj",
    )(y, x, mods, w, n2w, wrt, brt)


def _final_kernel(x_ref, m_ref, mod_ref, nf_ref, yp_ref, ys_ref, *, ncb):
    i = pl.program_id(0)
    g2 = _mod_slices(mod_ref[...])[5]
    y = _rms(x_ref[...] + g2 * m_ref[...].astype(F32), nf_ref[...])

    @pl.when(i < ncb)
    def _():
        yp_ref[...] = y

    @pl.when(i >= ncb)
    def _():
        ys_ref[...] = y


def _final(st, x, moe, mods, nf):
    tm = TM_TOK
    ncb = st.tp // tm
    return pl.pallas_call(
        functools.partial(_final_kernel, ncb=ncb),
        out_shape=(jax.ShapeDtypeStruct((st.tp, D_MODEL), F32), jax.ShapeDtypeStruct((st.ts, D_MODEL), F32)),
        grid=(st.t // tm,),
        in_specs=[
            pl.BlockSpec((tm, D_MODEL), lambda i: (i, 0)),
            pl.BlockSpec((tm, D_MODEL), lambda i: (i, 0)),
            pl.BlockSpec((None, 1, N_MOD * D_MODEL), lambda i: (st.mod_row(i, tm), 0, 0)),
            pl.BlockSpec((1, D_MODEL), lambda i: (0, 0)),
        ],
        out_specs=(pl.BlockSpec((tm, D_MODEL), lambda i: (jnp.minimum(i, ncb - 1), 0)),
                   pl.BlockSpec((tm, D_MODEL), lambda i: (jnp.maximum(i - ncb, 0), 0))),
        compiler_params=_cparams("arbitrary"),
        name="final_norm",
    )(x, moe, mods, nf)


def _grid_pos_table(seq):
    rows = seq // GRID_W
    r, cl = jnp.meshgrid(jnp.arange(rows, dtype=F32), jnp.arange(GRID_W, dtype=F32), indexing='ij')
    quarter = D_MODEL // 4
    omega = POS_THETA ** (-jnp.arange(quarter, dtype=F32) / quarter)

    def enc(pos):
        a = pos.reshape(-1, 1) * omega[None, :]
        return jnp.concatenate([jnp.sin(a), jnp.cos(a)], axis=-1)

    return jnp.concatenate([enc(r), enc(cl)], axis=-1)


def _router_weights(w_group, b_group, w_router, b_router):
    pad = LG_ROWS - N_EXPERTS - N_GROUPS
    w = jnp.concatenate([w_router.T, w_group.T, jnp.zeros((pad, D_MODEL), F32)], axis=0)
    b = jnp.concatenate([b_router, b_group, jnp.zeros((pad,), F32)])[:, None]
    return w, b


def kernel(x_prompt, x_sample, state_gla, state_mlstm_c, state_mlstm_n, state_mlstm_m, c, c_ctx, norm1_w, norm2_w, norm_f_w, ada_w, ada_b, rec_w_in, gla_gk_w, gla_gk_b, mlstm_gate_b, gla_norm_w, mlstm_norm_w, rec_w_out, hy_w_in, hy_conv_w, hy_f_w1, hy_f_b1, hy_f_freq1, hy_f_w2, hy_f_b2, hy_f_freq2, hy_f_w3, hy_f_bias, hy_w_out, moe_w_group, moe_b_group, moe_w_router, moe_b_router, moe_w_gate, moe_w_up, moe_w_down):
    bp, lp, _ = x_prompt.shape
    bs, ls, _ = x_sample.shape
    st = _Streams(bp, lp, bs, ls)
    xp = x_prompt.reshape(st.tp, D_MODEL)
    xs = x_sample.reshape(st.ts, D_MODEL)

    nrow = -(-(1 + bs) // 8) * 8
    cv = jnp.concatenate([c_ctx[None, :], c, jnp.zeros((nrow - 1 - bs, D_MODEL), F32)], axis=0)
    mods = _ada(cv, ada_w, ada_b)
    mods0 = mods[0].reshape(nrow, 1, N_MOD * D_MODEL)
    mods1 = mods[1].reshape(nrow, 1, N_MOD * D_MODEL)

    pos = _grid_pos_table(ls)

    w_in = rec_w_in[0]
    w_main = _bf(w_in[:, :_REC_MAIN])
    w_mg = w_in[:, _REC_MAIN + _MG0:].reshape(D_MODEL, 2, 2, H_B)
    w_fg = jnp.pad(w_mg[:, :, 1, :], ((0, 0), (0, 0), (0, 8 - H_B))).reshape(D_MODEL, 16)
    w_gate = jnp.concatenate([w_in[:, _REC_MAIN:], jnp.zeros((D_MODEL, LANES - _REC_GATE), F32),
                              jnp.zeros((D_MODEL, _MG0), F32), w_fg,
                              jnp.zeros((D_MODEL, LANES - _REC_GATE), F32)], axis=1)
    w_gate_t = w_in[:, _REC_MAIN + _MG0:].T
    x0, main, g, g2, mt = _inproj(st, xp, xs, pos, mods0, norm1_w[0][None, :], w_main, w_gate, w_gate_t)

    gkw = jnp.zeros((2, LANES, H_A * DK_A), F32)
    gkw = gkw.at[0, :GK_RANK].set(gla_gk_w[0, 0]).at[1, GK_RANK:2 * GK_RANK].set(gla_gk_w[0, 1])
    gkb = gla_gk_b[0][:, None, :]
    gb = mlstm_gate_b[0]
    gate_row = lambda b: jnp.pad(jnp.pad(b, ((0, 0), (0, 8 - H_B))).reshape(1, 16), ((0, 0), (_MG0, LANES - _REC_GATE)))
    gbcol = gb.reshape(16, 1)
    wr0, br0 = _router_weights(moe_w_group[0], moe_b_group[0], moe_w_router[0], moe_b_router[0])
    consts = [gkw, gkb, gate_row(gb[:, 0]), gate_row(gb[:, 1]), gbcol, gla_norm_w[0][None, :],
              mlstm_norm_w[0][None, :], _bf(rec_w_out[0]), norm2_w[0][None, :], wr0, br0]

    def scan_states(sg, sc, sn, sm):
        nb = sg.shape[0]
        return (jnp.swapaxes(sg, -1, -2), sc, sn.reshape(nb, 2 * H_B, DK_B), sm.reshape(nb, 1, 2 * H_B))

    zero = (jnp.zeros((bp, 2, H_A, DK_A, DV_A), F32), jnp.zeros((bp, 2, H_B, DK_B, DV_B), F32),
            jnp.zeros((bp, 2, H_B, DK_B), F32), jnp.zeros((bp, 2, H_B), F32))
    cached = (state_gla[:, 0], state_mlstm_c[:, 0], state_mlstm_n[:, 0], state_mlstm_m[:, 0])
    x1, h2, lg, fg, fc, fn, fm = _scan(st, True, main, g, g2, mt, x0, mods0, scan_states(*zero), consts, None)
    x1, h2, lg = _scan(st, False, main, g, g2, mt, x0, mods0, scan_states(*cached), consts, (x1, h2, lg))[:3]

    new_gla = jnp.swapaxes(fg, -1, -2)[:, None]
    new_c = fc[:, None]
    new_n = fn.reshape(bp, 1, 2, H_B, DK_B)
    new_m = fm.reshape(bp, 1, 2, H_B)

    wg, wu, wd = (_expert_weights_bf16(w) for w in (moe_w_gate, moe_w_up, moe_w_down))
    moe0 = _moe(st, 0, h2, _route(st, lg), wg, wu, wd)

    x2, z = _hy_inproj(st, x1, moe0, mods0, mods1, norm1_w[1][None, :], _bf(hy_w_in[0]))
    y = None
    for ctx in (True, False):
        seq = lp if ctx else ls
        cos, msin, cos_b, msin_b = _dft_tables(seq)
        filt = _hy_filter(seq, hy_f_w1[0], hy_f_b1[0], hy_f_freq1[0], hy_f_w2[0], hy_f_b2[0], hy_f_freq2[0],
                          hy_f_w3[0], cos, msin)
        y = _hy_conv(st, ctx, z, hy_conv_w[0], filt, hy_f_bias[0], (cos_b, msin_b), y)
    wr1, br1 = _router_weights(moe_w_group[1], moe_b_group[1], moe_w_router[1], moe_b_router[1])
    x3, h4, lg1 = _hy_outproj(st, y, x2, mods1, _bf(hy_w_out[0]), norm2_w[1][None, :], wr1, br1)

    moe1 = _moe(st, 1, h4, _route(st, lg1), wg, wu, wd)
    y_prompt, y_sample = _final(st, x3, moe1, mods1, norm_f_w[None, :])
    return (y_prompt.reshape(bp, lp, D_MODEL), y_sample.reshape(bs, ls, D_MODEL), new_gla, new_c, new_n, new_m)
```
